```python
import jax, jax.numpy as jnp
from jax import lax
import numpy as np

D_MODEL = 2048
BATCH = 4
SEQ = 2048
DEPTH = 2

GRID_W = 64
CTX_LEN = 256
N_MIXERS = 2
EXPAND = 2
INNER = EXPAND * D_MODEL
MLSTM_HEADS = 4
MLSTM_HEAD_DIM = INNER // MLSTM_HEADS
QKV_BLOCK = 4
CONV_K = 3
CHUNK = 128
SGU_GROUPS = 8
N_EXPERTS = 32
TOP_K = 4
D_EXPERT = D_MODEL
SWIGLU_LIMIT = 7.0
SWIGLU_ALPHA = 1.702
MOE_BLOCK = 256
DEEPNORM_ALPHA = (2 * DEPTH) ** 0.25
DEEPNORM_BETA = (8 * DEPTH) ** -0.25
LN_EPS = 1e-5
N_A = (DEPTH + N_MIXERS - 1) // N_MIXERS
N_B = DEPTH // N_MIXERS

kernel_name = 'hybrid_mlstm_chunkmlp_moe_prefix_dit'


def layer_norm(a, g, b):
    af = a.astype(jnp.float32)
    mu = af.mean(-1, keepdims=True)
    var = jnp.mean(jnp.square(af - mu), -1, keepdims=True)
    return ((af - mu) * lax.rsqrt(var + LN_EPS) * g + b).astype(a.dtype)


def modulate(h, shift, scale):
    return h * (1 + scale) + shift


def headwise(a, w):
    B, T, E = a.shape
    return jnp.einsum('btgi,gio->btgo', a.reshape(B, T, E // QKV_BLOCK, QKV_BLOCK), w).reshape(B, T, E)


def grid_conv(a, w, b):
    B, T, E = a.shape
    rows = T // GRID_W
    img = a.reshape(B, rows, GRID_W, E)
    out = lax.conv_general_dilated(img, w[:, :, None, :], (1, 1), 'SAME',
                                   dimension_numbers=('NHWC', 'HWIO', 'NHWC'), feature_group_count=E)
    return out.reshape(B, T, E) + b


def seq_conv(a, w_row, b):
    out = lax.conv_general_dilated(a, w_row[:, None, :], (1,), 'SAME',
                                   dimension_numbers=('NWC', 'WIO', 'NWC'), feature_group_count=a.shape[-1])
    return out + b


def to_heads(a):
    B, T, _ = a.shape
    return a.reshape(B, T, MLSTM_HEADS, MLSTM_HEAD_DIM).transpose(0, 2, 1, 3).astype(jnp.float32)


def mlstm_chunked(q, k, v, ig, lf, state, with_out):
    B, H, T, dh = q.shape
    nc = T // CHUNK

    def to_chunks(a):
        return jnp.moveaxis(a.reshape(a.shape[:2] + (nc, CHUNK) + a.shape[3:]), 2, 0)

    k = k * (dh ** -0.5)
    tril = jnp.tril(jnp.ones((CHUNK, CHUNK), dtype=bool))

    def step(carry, xs):
        C, n, m = carry
        qc, kc, vc, igc, lfc = xs
        b = jnp.cumsum(lfc, axis=-1)
        b_end = b[..., -1]
        w_end = b_end[..., None] - b + igc
        m_new = jnp.maximum(b_end + m, w_end.max(-1))
        decay = jnp.exp(b_end + m - m_new)
        wts = jnp.exp(w_end - m_new[..., None])
        C_new = decay[..., None, None] * C + jnp.einsum('bhs,bhsv,bhsk->bhvk', wts, vc, kc)
        n_new = decay[..., None] * n + jnp.einsum('bhs,bhsk->bhk', wts, kc)
        if not with_out:
            return (C_new, n_new, m_new), None
        log_d = jnp.where(tril, b[..., :, None] - b[..., None, :] + igc[..., None, :], -jnp.inf)
        g = b + m[..., None]
        m_t = jnp.maximum(g, log_d.max(-1))
        dw = jnp.exp(log_d - m_t[..., None])
        inter = jnp.exp(g - m_t)
        s = jnp.einsum('bhtd,bhsd->bhts', qc, kc) * dw
        num = jnp.einsum('bhts,bhsv->bhtv', s, vc) + inter[..., None] * jnp.einsum('bhvk,bhtk->bhtv', C, qc)
        den = s.sum(-1) + inter * jnp.einsum('bhk,bhtk->bht', n, qc)
        h = num / jnp.maximum(jnp.abs(den), jnp.exp(-m_t))[..., None]
        return (C_new, n_new, m_new), h

    xs = (to_chunks(q), to_chunks(k), to_chunks(v), to_chunks(ig), to_chunks(lf))
    state_out, hs = lax.scan(step, state, xs)
    if not with_out:
        return state_out, None
    return state_out, jnp.moveaxis(hs, 0, 2).reshape(B, H, T, dh)


def mlstm_mixer(hx, hc, w_in, conv_w, conv_b, w_q, w_k, w_v, w_gate, b_gate, norm_w, skip, w_out, ctx_out):
    NH, DH = MLSTM_HEADS, MLSTM_HEAD_DIM
    dt = hx.dtype

    def branch(h, conv_fn):
        xm, z = jnp.split(h @ w_in, 2, axis=-1)
        xc = jax.nn.silu(conv_fn(xm))
        q, k, v = headwise(xc, w_q), headwise(xc, w_k), headwise(xm, w_v)
        gates = jnp.concatenate([q, k, v], axis=-1) @ w_gate + b_gate
        return xc, z, to_heads(q), to_heads(k), to_heads(v), jnp.moveaxis(gates.astype(jnp.float32), -1, 1)

    def finish(h, xc, z):
        B, H, T, _ = h.shape
        mu = h.mean(-1, keepdims=True)
        var = jnp.mean(jnp.square(h - mu), -1, keepdims=True)
        hn = ((h - mu) * lax.rsqrt(var + LN_EPS)).transpose(0, 2, 1, 3).reshape(B, T, INNER).astype(dt)
        return ((hn * norm_w + skip * xc) * jax.nn.silu(z)) @ w_out

    xc_x, z_x, qx, kx, vx, gx = branch(hx, lambda a: grid_conv(a, conv_w, conv_b))
    xc_c, z_c, qc, kc, vc, gc = branch(hc, lambda a: seq_conv(a, conv_w[CONV_K // 2], conv_b))
    B = hx.shape[0]
    outs_x, outs_c = [], []
    for d in range(2):
        flip = (lambda a: jnp.flip(a, axis=2)) if d == 1 else (lambda a: a)
        i_sl = slice(2 * d * NH, (2 * d + 1) * NH)
        f_sl = slice((2 * d + 1) * NH, (2 * d + 2) * NH)
        state0 = (jnp.zeros((B, NH, DH, DH), jnp.float32), jnp.zeros((B, NH, DH), jnp.float32),
                  jnp.full((B, NH), -jnp.inf, jnp.float32))
        state_c, h_c = mlstm_chunked(flip(qc), flip(kc), flip(vc), flip(gc[:, i_sl]),
                                     flip(jax.nn.log_sigmoid(gc[:, f_sl])), state0, ctx_out)
        _, h_x = mlstm_chunked(flip(qx), flip(kx), flip(vx), flip(gx[:, i_sl]),
                               flip(jax.nn.log_sigmoid(gx[:, f_sl])), state_c, True)
        outs_x.append(flip(h_x))
        if ctx_out:
            outs_c.append(flip(h_c))
    ox = finish(outs_x[0] + outs_x[1], xc_x, z_x)
    oc = finish(outs_c[0] + outs_c[1], xc_c, z_c) if ctx_out else None
    return ox, oc


def chunk_mlp(h, w_in, ln_g, ln_b, w_s, b_s, w_out):
    B, T, _ = h.shape
    u, v = jnp.split(jax.nn.gelu(h @ w_in, approximate=False), 2, axis=-1)
    v = layer_norm(v, ln_g, ln_b)
    vg = v.reshape(B, T // CHUNK, CHUNK, SGU_GROUPS, INNER // SGU_GROUPS)
    mixed = jnp.einsum('gts,bcsgd->bctgd', w_s, vg) + b_s.T[:, :, None]
    return (u * mixed.reshape(B, T, INNER)) @ w_out


def moe(tok, r_w, r_b, w1, b1, w2, b2):
    T, D = tok.shape
    logits = (tok @ r_w + r_b).astype(jnp.float32)
    top_v, top_i = lax.top_k(logits, TOP_K)
    gate = jax.nn.softmax(top_v, axis=-1)
    flat_e = top_i.reshape(-1)
    n_assign = T * TOP_K
    order = jnp.argsort(flat_e)
    sorted_e = flat_e[order]
    counts = jnp.bincount(flat_e, length=N_EXPERTS)
    starts = jnp.cumsum(counts) - counts
    pcounts = (counts + MOE_BLOCK - 1) // MOE_BLOCK * MOE_BLOCK
    pends = jnp.cumsum(pcounts)
    pstarts = pends - pcounts
    dest_sorted = pstarts[sorted_e] + (jnp.arange(n_assign) - starts[sorted_e])
    n_blocks = -(-n_assign // MOE_BLOCK) + N_EXPERTS
    slot_tok = jnp.zeros((n_blocks * MOE_BLOCK,), jnp.int32).at[dest_sorted].set((order // TOP_K).astype(jnp.int32))
    block_e = jnp.clip(jnp.searchsorted(pends, jnp.arange(n_blocks) * MOE_BLOCK, side='right'), 0, N_EXPERTS - 1)

    def block_fn(args):
        toks, e = args
        hcat = tok[toks] @ w1[e] + b1[e]
        glu = jnp.minimum(hcat[:, :D_EXPERT], SWIGLU_LIMIT)
        lin = jnp.clip(hcat[:, D_EXPERT:], -SWIGLU_LIMIT, SWIGLU_LIMIT)
        act = glu * jax.nn.sigmoid(SWIGLU_ALPHA * glu) * (lin + 1)
        return act @ w2[e] + b2[e]

    y_slots = lax.map(block_fn, (slot_tok.reshape(n_blocks, MOE_BLOCK), block_e))
    dest_flat = jnp.zeros((n_assign,), dest_sorted.dtype).at[order].set(dest_sorted)
    y = y_slots.reshape(n_blocks * MOE_BLOCK, D)[dest_flat].reshape(T, TOP_K, D)
    return jnp.einsum('tk,tkd->td', gate.astype(y.dtype), y)


def setup_inputs(seed: int = 0) -> dict:
    key = jax.random.key(seed)
    ks = iter(jax.random.split(key, 48))

    def nrm(shape, scale=1.0):
        return jax.random.normal(next(ks), shape, jnp.float32) * scale

    D, E, F, NH, NE = D_MODEL, INNER, D_EXPERT, MLSTM_HEADS, N_EXPERTS
    f_bias = jnp.linspace(3.0, 6.0, NH, dtype=jnp.float32)
    return {
        'x': nrm((BATCH, SEQ, D)),
        'c': nrm((BATCH, D)),
        'ctx': nrm((BATCH, CTX_LEN, D)),
        'c_ctx': nrm((D,)),
        'mod_w': nrm((DEPTH, D, 6 * D), 0.5 * D ** -0.5),
        'mod_b': nrm((DEPTH, 6 * D), 0.02),
        'ln1_g': 1.0 + nrm((DEPTH, D), 0.02),
        'ln1_b': nrm((DEPTH, D), 0.02),
        'ln2_g': 1.0 + nrm((DEPTH, D), 0.02),
        'ln2_b': nrm((DEPTH, D), 0.02),
        'a_w_in': nrm((N_A, D, 2 * E), D ** -0.5),
        'a_conv_w': nrm((N_A, CONV_K, CONV_K, E), 1.0 / CONV_K),
        'a_conv_b': nrm((N_A, E), 0.02),
        'a_w_q': nrm((N_A, E // QKV_BLOCK, QKV_BLOCK, QKV_BLOCK), QKV_BLOCK ** -0.5),
        'a_w_k': nrm((N_A, E // QKV_BLOCK, QKV_BLOCK, QKV_BLOCK), QKV_BLOCK ** -0.5),
        'a_w_v': nrm((N_A, E // QKV_BLOCK, QKV_BLOCK, QKV_BLOCK), QKV_BLOCK ** -0.5),
        'a_w_gate': nrm((N_A, 3 * E, 4 * NH), (3 * E) ** -0.5),
        'a_b_gate': jnp.concatenate([nrm((N_A, NH), 0.1), f_bias + nrm((N_A, NH), 0.1),
                                     nrm((N_A, NH), 0.1), f_bias + nrm((N_A, NH), 0.1)], axis=-1),
        'a_norm_w': 1.0 + nrm((N_A, E), 0.02),
        'a_skip': 1.0 + nrm((N_A, E), 0.02),
        'a_w_out': nrm((N_A, E, D), E ** -0.5 * DEEPNORM_BETA),
        'b_w_in': nrm((N_B, D, 2 * E), D ** -0.5),
        'b_ln_g': 1.0 + nrm((N_B, E), 0.02),
        'b_ln_b': nrm((N_B, E), 0.02),
        'b_w_s': nrm((N_B, SGU_GROUPS, CHUNK, CHUNK), CHUNK ** -0.5),
        'b_b_s': 1.0 + nrm((N_B, SGU_GROUPS, CHUNK), 0.02),
        'b_w_out': nrm((N_B, E, D), E ** -0.5 * DEEPNORM_BETA),
        'r_w': nrm((DEPTH, D, NE), D ** -0.5),
        'r_b': nrm((DEPTH, NE), 0.01),
        'e_w1': nrm((DEPTH, NE, D, 2 * F), D ** -0.5),
        'e_b1': nrm((DEPTH, NE, 2 * F), 0.02),
        'e_w2': nrm((DEPTH, NE, F, D), F ** -0.5 * DEEPNORM_BETA),
        'e_b2': nrm((DEPTH, NE, D), 0.02),
    }


def reference(x, c, ctx, c_ctx, mod_w, mod_b, ln1_g, ln1_b, ln2_g, ln2_b,
              a_w_in, a_conv_w, a_conv_b, a_w_q, a_w_k, a_w_v, a_w_gate, a_b_gate, a_norm_w, a_skip, a_w_out,
              b_w_in, b_ln_g, b_ln_b, b_w_s, b_b_s, b_w_out,
              r_w, r_b, e_w1, e_b1, e_w2, e_b2):
    B, S, D = x.shape
    Lc = ctx.shape[1]
    hs, cs = x, ctx
    for layer in range(DEPTH):
        mixer = layer % N_MIXERS
        j = layer // N_MIXERS
        ctx_out = layer < DEPTH - 1
        ctx_in = ctx_out or mixer == 0
        sh1, sc1, g1, sh2, sc2, g2 = [m[:, None, :] for m in
                                      jnp.split(jax.nn.silu(c) @ mod_w[layer] + mod_b[layer], 6, axis=-1)]
        hx = modulate(hs, sh1, sc1)
        hc = None
        if ctx_in:
            csh1, csc1, cg1, csh2, csc2, cg2 = jnp.split(jax.nn.silu(c_ctx) @ mod_w[layer] + mod_b[layer], 6, axis=-1)
            hc = modulate(cs, csh1, csc1)
        if mixer == 0:
            ox, oc = mlstm_mixer(hx, hc, a_w_in[j], a_conv_w[j], a_conv_b[j], a_w_q[j], a_w_k[j], a_w_v[j],
                                 a_w_gate[j], a_b_gate[j], a_norm_w[j], a_skip[j], a_w_out[j], ctx_out)
        else:
            ox = chunk_mlp(hx, b_w_in[j], b_ln_g[j], b_ln_b[j], b_w_s[j], b_b_s[j], b_w_out[j])
            oc = chunk_mlp(hc, b_w_in[j], b_ln_g[j], b_ln_b[j], b_w_s[j], b_b_s[j], b_w_out[j]) if ctx_out else None
        hs = layer_norm(DEEPNORM_ALPHA * hs + g1 * ox, ln1_g[layer], ln1_b[layer])
        if ctx_out:
            cs = layer_norm(DEEPNORM_ALPHA * cs + cg1 * oc, ln1_g[layer], ln1_b[layer])
            tok = jnp.concatenate([modulate(hs, sh2, sc2).reshape(B * S, D),
                                   modulate(cs, csh2, csc2).reshape(B * Lc, D)], axis=0)
        else:
            tok = modulate(hs, sh2, sc2).reshape(B * S, D)
        y = moe(tok, r_w[layer], r_b[layer], e_w1[layer], e_b1[layer], e_w2[layer], e_b2[layer])
        hs = layer_norm(DEEPNORM_ALPHA * hs + g2 * y[:B * S].reshape(B, S, D), ln2_g[layer], ln2_b[layer])
        if ctx_out:
            cs = layer_norm(DEEPNORM_ALPHA * cs + cg2 * y[B * S:].reshape(B, Lc, D), ln2_g[layer], ln2_b[layer])
    return hs
```

```python
import functools

import jax
import jax.numpy as jnp
from jax import lax
from jax.experimental import pallas as pl
from jax.experimental.pallas import tpu as pltpu

D = 2048
BATCH = 4
SEQ = 2048
DEPTH = 2
GRID_W = 64
LC = 256
U = LC + SEQ
E = 2 * D
NH = 4
DH = E // NH
QKV_BLOCK = 4
SGU_GROUPS = 8
SGU_CHUNK = 128
NE = 32
TOP_K = 4
F = D
SWIGLU_LIMIT = 7.0
SWIGLU_ALPHA = 1.702
ALPHA = (2 * DEPTH) ** 0.25
LN_EPS = 1e-5

ROW_TILE = 256
MLSTM_CHUNK = 256
MOE_TM = 512
BD_TILE = 256
VMEM_LIMIT = 56 * 1024 * 1024

f32 = jnp.float32
bf16 = jnp.bfloat16


def _cparams(n_axes):
    return pltpu.CompilerParams(dimension_semantics=("arbitrary",) * n_axes,
                                vmem_limit_bytes=VMEM_LIMIT)


def _dot(a, b):
    return jnp.dot(a, b, preferred_element_type=f32)


def _sigmoid(x):
    return 1.0 / (1.0 + jnp.exp(-x))


def _mod_kernel(c_ref, w_ref, b_ref, o_ref):
    c = c_ref[...]
    a = (c * _sigmoid(c)).astype(bf16)
    o_ref[0] = _dot(a, w_ref[0].astype(bf16)) + b_ref[0]


def _mod_rows(cvec, mod_w, mod_b):
    tn = 1024
    return pl.pallas_call(
        _mod_kernel,
        grid=(DEPTH, 6 * D // tn),
        in_specs=[pl.BlockSpec((8, D), lambda l, j: (0, 0)),
                  pl.BlockSpec((1, D, tn), lambda l, j: (l, 0, j)),
                  pl.BlockSpec((1, 1, tn), lambda l, j: (l, 0, j))],
        out_specs=pl.BlockSpec((1, 8, tn), lambda l, j: (l, 0, j)),
        out_shape=jax.ShapeDtypeStruct((DEPTH, 8, 6 * D), f32),
        compiler_params=_cparams(2),
        name="adaln_rows",
    )(cvec, mod_w, mod_b.reshape(DEPTH, 1, 6 * D))


def _group_joint(i):
    tiles = U // ROW_TILE
    return 2 * (i // tiles) + jnp.minimum(i % tiles, 1)


def _group_latent(i):
    return 2 * (i // (SEQ // ROW_TILE)) + 1


def _modulate_kernel(x_ref, m_ref, o_ref):
    m = m_ref[0]
    o_ref[...] = (x_ref[...] * (1.0 + m[1:2]) + m[0:1]).astype(o_ref.dtype)


def _modulate(x2d, mt, group):
    m = x2d.shape[0]
    return pl.pallas_call(
        _modulate_kernel,
        grid=(m // ROW_TILE,),
        in_specs=[pl.BlockSpec((ROW_TILE, D), lambda i: (i, 0)),
                  pl.BlockSpec((1, 6, D), lambda i: (group(i), 0, 0))],
        out_specs=pl.BlockSpec((ROW_TILE, D), lambda i: (i, 0)),
        out_shape=jax.ShapeDtypeStruct((m, D), bf16),
        compiler_params=_cparams(1),
        name="modulate",
    )(x2d, mt)


def _erf(x):
    return lax.erf(x)


def _mm_kernel(x_ref, w_ref, o_ref, wb_ref, *, act):
    @pl.when(pl.program_id(1) == 0)
    def _():
        wb_ref[...] = w_ref[...].astype(bf16)

    acc = _dot(x_ref[...], wb_ref[...])
    if act == "gelu":
        acc = 0.5 * acc * (1.0 + _erf(acc * (2.0 ** -0.5)))
    o_ref[...] = acc.astype(o_ref.dtype)


def _matmul(x, w, *, tm, tn, act=None, name):
    m, k = x.shape
    n = w.shape[1]
    return pl.pallas_call(
        functools.partial(_mm_kernel, act=act),
        grid=(n // tn, m // tm),
        in_specs=[pl.BlockSpec((tm, k), lambda j, i: (i, 0)),
                  pl.BlockSpec((k, tn), lambda j, i: (0, j))],
        out_specs=pl.BlockSpec((tm, tn), lambda j, i: (i, j)),
        out_shape=jax.ShapeDtypeStruct((m, n), bf16),
        scratch_shapes=[pltpu.VMEM((k, tn), bf16)],
        compiler_params=_cparams(2),
        name=name,
    )(x, w)


def _conv_qkv_kernel(xm_ref, cw_ref, cb_ref, wq_ref, wk_ref, wv_ref, wg_ref, bg_ref,
                     xc_ref, q_ref, k_ref, v_ref, g_ref):
    ct = pl.program_id(1)
    a = xm_ref[0].astype(f32)
    c = a.shape[1]
    r = lax.broadcasted_iota(jnp.int32, (U, c), 0)
    latent = r >= LC
    p = r - LC
    col = jnp.where(latent, p & (GRID_W - 1), r)
    last = jnp.where(latent, GRID_W - 1, LC - 1)
    a_l = jnp.where(col > 0, pltpu.roll(a, 1, 0), 0.0)
    a_r = jnp.where(col < last, pltpu.roll(a, U - 1, 0), 0.0)
    w = cw_ref[...]
    rows = [w[3 * i:3 * i + 1] * a_l + w[3 * i + 1:3 * i + 2] * a + w[3 * i + 2:3 * i + 3] * a_r
            for i in range(3)]
    up = jnp.where(p >= GRID_W, pltpu.roll(rows[0], GRID_W, 0), 0.0)
    down = jnp.where(latent & (p < SEQ - GRID_W), pltpu.roll(rows[2], U - GRID_W, 0), 0.0)
    pre = rows[1] + up + down + cb_ref[...]
    xc = (pre * _sigmoid(pre)).astype(bf16)
    xc_ref[0] = xc
    xm = xm_ref[0]
    q = _dot(xc, wq_ref[0]).astype(bf16)
    k = _dot(xc, wk_ref[0]).astype(bf16)
    v = _dot(xm, wv_ref[0]).astype(bf16)
    q_ref[0] = q
    k_ref[0] = k
    v_ref[0] = v
    g = (_dot(q, wg_ref[0].astype(bf16)) + _dot(k, wg_ref[1].astype(bf16))
         + _dot(v, wg_ref[2].astype(bf16)))

    @pl.when(ct == 0)
    def _():
        g_ref[0] = g + bg_ref[...]

    @pl.when(ct > 0)
    def _():
        g_ref[0] += g


def _block_diag(w):
    per = BD_TILE // QKV_BLOCK
    wt = w.reshape(E // BD_TILE, per, QKV_BLOCK, QKV_BLOCK)
    eye = jnp.eye(per, dtype=w.dtype)
    full = jnp.einsum("tgio,gh->tgiho", wt, eye)
    return full.reshape(E // BD_TILE, BD_TILE, BD_TILE).astype(bf16)


def _conv_qkv(xz, conv_w, conv_b, w_q, w_k, w_v, w_gate, b_gate):
    c = BD_TILE
    n_gate = 4 * NH
    act = jax.ShapeDtypeStruct((BATCH, U, E), bf16)
    tile = pl.BlockSpec((1, U, c), lambda b, t: (b, 0, t))
    bd = pl.BlockSpec((1, c, c), lambda b, t: (t, 0, 0))
    return pl.pallas_call(
        _conv_qkv_kernel,
        grid=(BATCH, E // c),
        in_specs=[tile,
                  pl.BlockSpec((9, c), lambda b, t: (0, t)),
                  pl.BlockSpec((1, c), lambda b, t: (0, t)),
                  bd, bd, bd,
                  pl.BlockSpec((3, c, n_gate), lambda b, t: (0, t, 0)),
                  pl.BlockSpec((1, n_gate), lambda b, t: (0, 0))],
        out_specs=[tile, tile, tile, tile,
                   pl.BlockSpec((1, U, n_gate), lambda b, t: (b, 0, 0))],
        out_shape=[act, act, act, act, jax.ShapeDtypeStruct((BATCH, U, n_gate), f32)],
        compiler_params=_cparams(2),
        name="conv_qkv_gates",
    )(xz.reshape(BATCH, U, 2 * E), conv_w.reshape(9, E), conv_b.reshape(1, E),
      _block_diag(w_q), _block_diag(w_k), _block_diag(w_v),
      w_gate.reshape(3, E, n_gate), b_gate.reshape(1, n_gate))


def _log_sigmoid(x):
    return jnp.minimum(x, 0.0) - jnp.log1p(jnp.exp(-jnp.abs(x)))


def _mlstm_kernel(*refs, direction, finish):
    if finish:
        (q_ref, k_ref, v_ref, gc_ref, gr_ref, hf_ref, xc_ref, z_ref, nw_ref, sk_ref,
         o_ref, ct_ref, ctb_ref, n_ref, m_ref) = refs
    else:
        q_ref, k_ref, v_ref, gc_ref, gr_ref, o_ref, ct_ref, ctb_ref, n_ref, m_ref = refs
    L = MLSTM_CHUNK
    h = pl.program_id(1)
    step = pl.program_id(2)

    @pl.when(step == 0)
    def _():
        ct_ref[...] = jnp.zeros_like(ct_ref)
        ctb_ref[...] = jnp.zeros_like(ctb_ref)
        n_ref[...] = jnp.zeros_like(n_ref)
        m_ref[...] = jnp.full_like(m_ref, -jnp.inf)

    i_idx = 2 * direction * NH + h
    f_idx = (2 * direction + 1) * NH + h
    gc = gc_ref[0]
    gr = gr_ref[0]
    lane = lax.broadcasted_iota(jnp.int32, gc.shape, 1)
    sub = lax.broadcasted_iota(jnp.int32, gr.shape, 0)
    ig_col = jnp.sum(jnp.where(lane == i_idx, gc, 0.0), axis=1, keepdims=True)
    f_col = jnp.sum(jnp.where(lane == f_idx, gc, 0.0), axis=1, keepdims=True)
    ig_row = jnp.sum(jnp.where(sub == i_idx, gr, 0.0), axis=0, keepdims=True)
    f_row = jnp.sum(jnp.where(sub == f_idx, gr, 0.0), axis=0, keepdims=True)
    lf_col = _log_sigmoid(f_col)
    lf_row = _log_sigmoid(f_row)

    t_i = lax.broadcasted_iota(jnp.int32, (L, L), 0)
    s_i = lax.broadcasted_iota(jnp.int32, (L, L), 1)
    seen = (s_i <= t_i) if direction == 0 else (s_i >= t_i)
    b_col = jnp.sum(jnp.where(seen, lf_row, 0.0), axis=1, keepdims=True)
    seen_t = (t_i <= s_i) if direction == 0 else (t_i >= s_i)
    b_row = jnp.sum(jnp.where(seen_t, lf_col, 0.0), axis=0, keepdims=True)
    b_end = jnp.sum(lf_row, axis=1, keepdims=True)

    m_prev = m_ref[:, 0:1]
    log_d = jnp.where(seen, b_col - b_row + ig_row, -jnp.inf)
    g_col = b_col + m_prev
    m_t = jnp.maximum(g_col, jnp.max(log_d, axis=1, keepdims=True))
    dw = jnp.exp(log_d - m_t)
    inter = jnp.exp(g_col - m_t)

    q = q_ref[0]
    k = k_ref[0] * (DH ** -0.5)
    v = v_ref[0]
    s = lax.dot_general(q, k, (((1,), (1,)), ((), ())), preferred_element_type=f32) * dw
    num = _dot(s.astype(bf16), v) + inter * _dot(q, ctb_ref[...])
    qn = jnp.sum(q.astype(f32) * n_ref[...], axis=1, keepdims=True)
    den = jnp.sum(s, axis=1, keepdims=True) + inter * qn
    hout = num / jnp.maximum(jnp.abs(den), jnp.exp(-m_t))

    w_end = b_end - b_col + ig_col
    m_new = jnp.maximum(b_end + m_prev, jnp.max(w_end, axis=0, keepdims=True))
    decay = jnp.exp(b_end + m_prev - m_new)
    kw = k.astype(f32) * jnp.exp(w_end - m_new)
    upd = lax.dot_general(kw.astype(bf16), v, (((0,), (0,)), ((), ())), preferred_element_type=f32)
    c_new = decay * ct_ref[...] + upd
    ct_ref[...] = c_new
    ctb_ref[...] = c_new.astype(bf16)
    n_ref[...] = decay * n_ref[...] + jnp.sum(kw, axis=0, keepdims=True)
    m_ref[...] = jnp.broadcast_to(m_new, m_ref.shape)

    if not finish:
        o_ref[0] = hout.astype(o_ref.dtype)
    else:
        hs = hout + hf_ref[0].astype(f32)
        mu = jnp.mean(hs, axis=1, keepdims=True)
        xc_ = hs - mu
        var = jnp.mean(xc_ * xc_, axis=1, keepdims=True)
        hn = xc_ * lax.rsqrt(var + LN_EPS)
        z = z_ref[0].astype(f32)
        o_ref[0] = ((hn * nw_ref[...] + sk_ref[...] * xc_ref[0].astype(f32))
                    * (z * _sigmoid(z))).astype(o_ref.dtype)


def _mlstm(q, k, v, gates, gates_t, direction, finish_args=None):
    L = MLSTM_CHUNK
    n_chunks = U // L

    if direction == 0:
        def chunk(s):
            return s
    else:
        def chunk(s):
            return jnp.where(s == 0, 0, n_chunks - s)

    tile = pl.BlockSpec((1, L, DH), lambda b, h, s: (b, chunk(s), h))
    in_specs = [tile, tile, tile,
                pl.BlockSpec((1, L, 4 * NH), lambda b, h, s: (b, chunk(s), 0)),
                pl.BlockSpec((1, 4 * NH, L), lambda b, h, s: (b, 0, chunk(s)))]
    args = [q, k, v, gates, gates_t]
    finish = finish_args is not None
    if finish:
        h_fwd, xc, xz, norm_w, skip = finish_args
        vec = pl.BlockSpec((1, DH), lambda b, h, s: (0, h))
        in_specs += [tile, tile,
                     pl.BlockSpec((1, L, DH), lambda b, h, s: (b, chunk(s), NH + h)),
                     vec, vec]
        args += [h_fwd, xc, xz.reshape(BATCH, U, 2 * E), norm_w.reshape(1, E), skip.reshape(1, E)]
    return pl.pallas_call(
        functools.partial(_mlstm_kernel, direction=direction, finish=finish),
        grid=(BATCH, NH, n_chunks),
        in_specs=in_specs,
        out_specs=tile,
        out_shape=jax.ShapeDtypeStruct((BATCH, U, E), bf16),
        scratch_shapes=[pltpu.VMEM((DH, DH), f32), pltpu.VMEM((DH, DH), bf16),
                        pltpu.VMEM((1, DH), f32), pltpu.VMEM((1, 128), f32)],
        compiler_params=_cparams(3),
        name="mlstm_bwd_finish" if finish else "mlstm_fwd",
    )(*args)


def _ln_rows(r, g, b):
    mu = jnp.mean(r, axis=1, keepdims=True)
    rc = r - mu
    var = jnp.mean(rc * rc, axis=1, keepdims=True)
    return rc * lax.rsqrt(var + LN_EPS) * g + b


def _ln_router_kernel(hs_ref, o_ref, m_ref, g_ref, b_ref, rw_ref, rb_ref,
                      hs1_ref, tok_ref, lg_ref):
    m = m_ref[0]
    r = ALPHA * hs_ref[...] + m[2:3] * o_ref[...].astype(f32)
    hs1 = _ln_rows(r, g_ref[...], b_ref[...])
    hs1_ref[...] = hs1
    tok = hs1 * (1.0 + m[4:5]) + m[3:4]
    t_hi = tok.astype(bf16)
    tok_ref[...] = t_hi
    t_lo = (tok - t_hi.astype(f32)).astype(bf16)
    w = rw_ref[...]
    w_hi = w.astype(bf16)
    w_lo = (w - w_hi.astype(f32)).astype(bf16)
    lg_ref[...] = _dot(t_hi, w_hi) + _dot(t_hi, w_lo) + _dot(t_lo, w_hi) + rb_ref[...]


def _ln_router(hs, o, mt, group, ln_g, ln_b, r_w, r_b):
    m = hs.shape[0]
    row = pl.BlockSpec((ROW_TILE, D), lambda i: (i, 0))
    vec = pl.BlockSpec((1, D), lambda i: (0, 0))
    return pl.pallas_call(
        _ln_router_kernel,
        grid=(m // ROW_TILE,),
        in_specs=[row, row,
                  pl.BlockSpec((1, 6, D), lambda i: (group(i), 0, 0)),
                  vec, vec,
                  pl.BlockSpec((D, NE), lambda i: (0, 0)),
                  pl.BlockSpec((1, NE), lambda i: (0, 0))],
        out_specs=[row, row, pl.BlockSpec((ROW_TILE, NE), lambda i: (i, 0))],
        out_shape=[jax.ShapeDtypeStruct((m, D), f32), jax.ShapeDtypeStruct((m, D), bf16),
                   jax.ShapeDtypeStruct((m, NE), f32)],
        compiler_params=_cparams(1),
        name="ln_router",
    )(hs, o, mt, ln_g.reshape(1, D), ln_b.reshape(1, D), r_w, r_b.reshape(1, NE))


def _first_of_expert(be_ref, i):
    return jnp.logical_or(i == 0, be_ref[i] != be_ref[jnp.maximum(i - 1, 0)])


def _moe_up_kernel(be_ref, nu_ref, x_ref, wg_ref, wl_ref, bg_ref, bl_ref, o_ref, wgb_ref, wlb_ref):
    i = pl.program_id(1)
    valid = i < nu_ref[0]

    @pl.when(jnp.logical_and(valid, _first_of_expert(be_ref, i)))
    def _():
        wgb_ref[...] = wg_ref[0].astype(bf16)
        wlb_ref[...] = wl_ref[0].astype(bf16)

    @pl.when(valid)
    def _():
        x = x_ref[...]
        glu = jnp.minimum(_dot(x, wgb_ref[...]) + bg_ref[0], SWIGLU_LIMIT)
        lin = jnp.clip(_dot(x, wlb_ref[...]) + bl_ref[0], -SWIGLU_LIMIT, SWIGLU_LIMIT)
        o_ref[...] = (glu * _sigmoid(SWIGLU_ALPHA * glu) * (lin + 1.0)).astype(o_ref.dtype)

    @pl.when(jnp.logical_not(valid))
    def _():
        o_ref[...] = jnp.zeros_like(o_ref)


def _moe_down_kernel(be_ref, nu_ref, a_ref, w_ref, b_ref, o_ref, wb_ref):
    i = pl.program_id(1)
    valid = i < nu_ref[0]

    @pl.when(jnp.logical_and(valid, _first_of_expert(be_ref, i)))
    def _():
        wb_ref[...] = w_ref[0].astype(bf16)

    @pl.when(valid)
    def _():
        o_ref[...] = (_dot(a_ref[...], wb_ref[...]) + b_ref[0]).astype(o_ref.dtype)

    @pl.when(jnp.logical_not(valid))
    def _():
        o_ref[...] = jnp.zeros_like(o_ref)


def _moe_experts(xs, block_e, n_used, w1, b1, w2, b2):
    ns = xs.shape[0]
    nb = ns // MOE_TM
    tf = 512
    tn = 1024
    lin0 = F // tf

    def row_blk(j, i, be, nu):
        return (jnp.minimum(i, nu[0] - 1), 0)

    act = pl.pallas_call(
        _moe_up_kernel,
        grid_spec=pltpu.PrefetchScalarGridSpec(
            num_scalar_prefetch=2,
            grid=(F // tf, nb),
            in_specs=[pl.BlockSpec((MOE_TM, D), row_blk),
                      pl.BlockSpec((1, D, tf), lambda j, i, be, nu: (be[i], 0, j)),
                      pl.BlockSpec((1, D, tf), lambda j, i, be, nu: (be[i], 0, lin0 + j)),
                      pl.BlockSpec((1, 1, tf), lambda j, i, be, nu: (be[i], 0, j)),
                      pl.BlockSpec((1, 1, tf), lambda j, i, be, nu: (be[i], 0, lin0 + j))],
            out_specs=pl.BlockSpec((MOE_TM, tf), lambda j, i, be, nu: (i, j)),
            scratch_shapes=[pltpu.VMEM((D, tf), bf16), pltpu.VMEM((D, tf), bf16)]),
        out_shape=jax.ShapeDtypeStruct((ns, F), bf16),
        compiler_params=_cparams(2),
        name="moe_up",
    )(block_e, n_used, xs, w1, w1, b1.reshape(NE, 1, 2 * F), b1.reshape(NE, 1, 2 * F))

    return pl.pallas_call(
        _moe_down_kernel,
        grid_spec=pltpu.PrefetchScalarGridSpec(
            num_scalar_prefetch=2,
            grid=(D // tn, nb),
            in_specs=[pl.BlockSpec((MOE_TM, F), row_blk),
                      pl.BlockSpec((1, F, tn), lambda j, i, be, nu: (be[i], 0, j)),
                      pl.BlockSpec((1, 1, tn), lambda j, i, be, nu: (be[i], 0, j))],
            out_specs=pl.BlockSpec((MOE_TM, tn), lambda j, i, be, nu: (i, j)),
            scratch_shapes=[pltpu.VMEM((F, tn), bf16)]),
        out_shape=jax.ShapeDtypeStruct((ns, D), bf16),
        compiler_params=_cparams(2),
        name="moe_down",
    )(block_e, n_used, act, w2, b2.reshape(NE, 1, D))


def _route(logits):
    t = logits.shape[0]
    n_assign = t * TOP_K
    nb = -(-n_assign // MOE_TM) + NE
    top_v, top_i = lax.top_k(logits, TOP_K)
    gate = jax.nn.softmax(top_v, axis=-1)
    flat_e = top_i.reshape(-1)
    onehot = (flat_e[:, None] == jnp.arange(NE, dtype=flat_e.dtype)[None, :]).astype(jnp.int32)
    csum = jnp.cumsum(onehot, axis=0)
    counts = csum[-1]
    rank = jnp.take_along_axis(csum, flat_e[:, None], axis=1)[:, 0] - 1
    pcounts = (counts + MOE_TM - 1) // MOE_TM * MOE_TM
    pends = jnp.cumsum(pcounts)
    pstarts = pends - pcounts
    dest = (pstarts[flat_e] + rank).astype(jnp.int32)
    slot_tok = jnp.zeros((nb * MOE_TM,), jnp.int32).at[dest].set(
        jnp.arange(n_assign, dtype=jnp.int32) // TOP_K)
    n_used = (pends[-1] // MOE_TM).astype(jnp.int32)
    blk = jnp.arange(nb, dtype=jnp.int32)
    block_e = jnp.clip(jnp.searchsorted(pends, blk * MOE_TM, side="right"), 0, NE - 1).astype(jnp.int32)
    block_e = jnp.where(blk < n_used, block_e, block_e[n_used - 1])
    return gate, dest, slot_tok, block_e, n_used.reshape(1)


def _combine_kernel(hs_ref, y_ref, gate_ref, m_ref, g_ref, b_ref, *rest, with_next):
    if with_next:
        mn_ref, hs2_ref, nxt_ref = rest
    else:
        (hs2_ref,) = rest
    m = m_ref[0]
    gate = gate_ref[...]
    y = gate[:, 0:1] * y_ref[:, 0:D].astype(f32)
    for k in range(1, TOP_K):
        y = y + gate[:, k:k + 1] * y_ref[:, k * D:(k + 1) * D].astype(f32)
    hs2 = _ln_rows(ALPHA * hs_ref[...] + m[5:6] * y, g_ref[...], b_ref[...])
    hs2_ref[...] = hs2
    if with_next:
        mn = mn_ref[0]
        nxt_ref[...] = (hs2 * (1.0 + mn[1:2]) + mn[0:1]).astype(nxt_ref.dtype)


def _combine(hs1, yg, gate, mt, group, ln_g, ln_b, mt_next=None):
    m = hs1.shape[0]
    row = pl.BlockSpec((ROW_TILE, D), lambda i: (i, 0))
    vec = pl.BlockSpec((1, D), lambda i: (0, 0))
    mod = pl.BlockSpec((1, 6, D), lambda i: (group(i), 0, 0))
    in_specs = [row, pl.BlockSpec((ROW_TILE, TOP_K * D), lambda i: (i, 0)),
                pl.BlockSpec((ROW_TILE, TOP_K), lambda i: (i, 0)), mod, vec, vec]
    args = [hs1, yg, gate, mt, ln_g.reshape(1, D), ln_b.reshape(1, D)]
    out_specs = [row]
    out_shape = [jax.ShapeDtypeStruct((m, D), f32)]
    with_next = mt_next is not None
    if with_next:
        in_specs.append(mod)
        args.append(mt_next)
        out_specs.append(row)
        out_shape.append(jax.ShapeDtypeStruct((m, D), bf16))
    return pl.pallas_call(
        functools.partial(_combine_kernel, with_next=with_next),
        grid=(m // ROW_TILE,),
        in_specs=in_specs, out_specs=out_specs, out_shape=out_shape,
        compiler_params=_cparams(1),
        name="moe_combine_ln",
    )(*args)


def _moe_layer(hs, o, mt, group, ln1_g, ln1_b, ln2_g, ln2_b, r_w, r_b, w1, b1, w2, b2, mt_next=None):
    hs1, tok, logits = _ln_router(hs, o, mt, group, ln1_g, ln1_b, r_w, r_b)
    gate, dest, slot_tok, block_e, n_used = _route(logits)
    xs = jnp.take(tok, slot_tok, axis=0)
    y = _moe_experts(xs, block_e, n_used, w1, b1, w2, b2)
    yg = jnp.take(y, dest, axis=0).reshape(hs.shape[0], TOP_K * D)
    return _combine(hs1, yg, gate, mt, group, ln2_g, ln2_b, mt_next)


def _sgu_kernel(g_ref, v_ref, lg_ref, lb_ref, ws_ref, bs_ref, o_ref):
    v = v_ref[...].astype(f32)
    vn = _ln_rows(v, lg_ref[...], lb_ref[...]).astype(bf16)
    gw = E // SGU_GROUPS
    for g in range(SGU_GROUPS):
        mixed = _dot(ws_ref[g].astype(bf16), vn[:, g * gw:(g + 1) * gw]) + bs_ref[:, g:g + 1]
        o_ref[:, g * gw:(g + 1) * gw] = (g_ref[:, g * gw:(g + 1) * gw].astype(f32) * mixed).astype(o_ref.dtype)


def _sgu(uv, ln_g, ln_b, w_s, b_s):
    m = uv.shape[0]
    c = SGU_CHUNK
    vec = pl.BlockSpec((1, E), lambda i: (0, 0))
    return pl.pallas_call(
        _sgu_kernel,
        grid=(m // c,),
        in_specs=[pl.BlockSpec((c, E), lambda i: (i, 0)),
                  pl.BlockSpec((c, E), lambda i: (i, 1)),
                  vec, vec,
                  pl.BlockSpec((SGU_GROUPS, c, c), lambda i: (0, 0, 0)),
                  pl.BlockSpec((c, SGU_GROUPS), lambda i: (0, 0))],
        out_specs=pl.BlockSpec((c, E), lambda i: (i, 0)),
        out_shape=jax.ShapeDtypeStruct((m, E), bf16),
        compiler_params=_cparams(1),
        name="sgu",
    )(uv, uv, ln_g.reshape(1, E), ln_b.reshape(1, E), w_s, b_s.T)


def kernel(x, c, ctx, c_ctx, mod_w, mod_b, ln1_g, ln1_b, ln2_g, ln2_b, a_w_in, a_conv_w, a_conv_b, a_w_q, a_w_k, a_w_v, a_w_gate, a_b_gate, a_norm_w, a_skip, a_w_out, b_w_in, b_ln_g, b_ln_b, b_w_s, b_b_s, b_w_out, r_w, r_b, e_w1, e_b1, e_w2, e_b2):
    cvec = jnp.concatenate([c, c_ctx[None, :], jnp.zeros((8 - BATCH - 1, D), f32)], axis=0)
    mods = _mod_rows(cvec, mod_w, mod_b).reshape(DEPTH, 8, 6, D)
    rows = [r for b in range(BATCH) for r in (BATCH, b)]
    mt = [jnp.stack([mods[l, r] for r in rows], axis=0) for l in range(DEPTH)]

    hs = jnp.concatenate([ctx, x], axis=1).reshape(BATCH * U, D)
    hx = _modulate(hs, mt[0], _group_joint)
    xz = _matmul(hx, a_w_in[0], tm=1024, tn=1024, name="mlstm_in_proj")
    xc, q, k, v, gates = _conv_qkv(xz, a_conv_w[0], a_conv_b[0], a_w_q[0], a_w_k[0], a_w_v[0],
                                   a_w_gate[0], a_b_gate[0])
    gates_t = gates.transpose(0, 2, 1)
    h_fwd = _mlstm(q, k, v, gates, gates_t, 0)
    pre = _mlstm(q, k, v, gates, gates_t, 1, (h_fwd, xc, xz, a_norm_w[0], a_skip[0]))
    o = _matmul(pre.reshape(BATCH * U, E), a_w_out[0], tm=1024, tn=512, name="mlstm_out_proj")
    hs2, hx1 = _moe_layer(hs, o, mt[0], _group_joint, ln1_g[0], ln1_b[0], ln2_g[0], ln2_b[0],
                          r_w[0], r_b[0], e_w1[0], e_b1[0], e_w2[0], e_b2[0], mt_next=mt[1])

    hs = hs2.reshape(BATCH, U, D)[:, LC:].reshape(BATCH * SEQ, D)
    hx = hx1.reshape(BATCH, U, D)[:, LC:].reshape(BATCH * SEQ, D)
    uv = _matmul(hx, b_w_in[0], tm=1024, tn=1024, act="gelu", name="sgu_in_proj")
    gated = _sgu(uv, b_ln_g[0], b_ln_b[0], b_w_s[0], b_b_s[0])
    o = _matmul(gated, b_w_out[0], tm=1024, tn=512, name="sgu_out_proj")
    (out,) = _moe_layer(hs, o, mt[1], _group_latent, ln1_g[1], ln1_b[1], ln2_g[1], ln2_b[1],
                        r_w[1], r_b[1], e_w1[1], e_b1[1], e_w2[1], e_b2[1])
    return out.reshape(BATCH, SEQ, D)
```

```python
import functools

import jax
import jax.numpy as jnp
from jax import lax
from jax.experimental import pallas as pl
from jax.experimental.pallas import tpu as pltpu

D = 2048
BATCH = 4
SEQ = 2048
DEPTH = 2
GRID_W = 64
LC = 256
U = LC + SEQ
E = 2 * D
NH = 4
DH = E // NH
QKV_BLOCK = 4
SGU_GROUPS = 8
SGU_CHUNK = 128
NE = 32
TOP_K = 4
F = D
SWIGLU_LIMIT = 7.0
SWIGLU_ALPHA = 1.702
ALPHA = (2 * DEPTH) ** 0.25
LN_EPS = 1e-5

ROW_TILE = 256
MLSTM_CHUNK = 256
MOE_TM = 512
BD_TILE = 256
VMEM_LIMIT = 56 * 1024 * 1024

f32 = jnp.float32
bf16 = jnp.bfloat16


def _cparams(n_axes):
    return pltpu.CompilerParams(dimension_semantics=("arbitrary",) * n_axes,
                                vmem_limit_bytes=VMEM_LIMIT)


def _dot(a, b):
    return jnp.dot(a, b, preferred_element_type=f32)


def _sigmoid(x):
    return 1.0 / (1.0 + jnp.exp(-x))


def _pack_halves(x):
    w = x.shape[1] // 2
    bits = lax.bitcast_convert_type(x.astype(bf16).astype(f32), jnp.uint32)
    return (bits[:, w:] & jnp.uint32(0xFFFF0000)) | (bits[:, :w] >> 16)


def _unpack_halves(p):
    lo = lax.bitcast_convert_type(p << 16, f32)
    hi = lax.bitcast_convert_type(p & jnp.uint32(0xFFFF0000), f32)
    return jnp.concatenate([lo, hi], axis=1)


def _mod_kernel(c_ref, w_ref, b_ref, o_ref):
    c = c_ref[...]
    a = (c * _sigmoid(c)).astype(bf16)
    o_ref[0] = _dot(a, w_ref[0].astype(bf16)) + b_ref[0]


def _mod_rows(cvec, mod_w, mod_b):
    tn = 1024
    return pl.pallas_call(
        _mod_kernel,
        grid=(DEPTH, 6 * D // tn),
        in_specs=[pl.BlockSpec((8, D), lambda l, j: (0, 0)),
                  pl.BlockSpec((1, D, tn), lambda l, j: (l, 0, j)),
                  pl.BlockSpec((1, 1, tn), lambda l, j: (l, 0, j))],
        out_specs=pl.BlockSpec((1, 8, tn), lambda l, j: (l, 0, j)),
        out_shape=jax.ShapeDtypeStruct((DEPTH, 8, 6 * D), f32),
        compiler_params=_cparams(2),
        name="adaln_rows",
    )(cvec, mod_w, mod_b.reshape(DEPTH, 1, 6 * D))


def _group_joint(i):
    tiles = U // ROW_TILE
    return 2 * (i // tiles) + jnp.minimum(i % tiles, 1)


def _group_latent(i):
    return 2 * (i // (SEQ // ROW_TILE)) + 1


def _modulate_kernel(x_ref, m_ref, o_ref):
    m = m_ref[0]
    o_ref[...] = (x_ref[...] * (1.0 + m[1:2]) + m[0:1]).astype(o_ref.dtype)


def _modulate(x2d, mt, group):
    m = x2d.shape[0]
    return pl.pallas_call(
        _modulate_kernel,
        grid=(m // ROW_TILE,),
        in_specs=[pl.BlockSpec((ROW_TILE, D), lambda i: (i, 0)),
                  pl.BlockSpec((1, 6, D), lambda i: (group(i), 0, 0))],
        out_specs=pl.BlockSpec((ROW_TILE, D), lambda i: (i, 0)),
        out_shape=jax.ShapeDtypeStruct((m, D), bf16),
        compiler_params=_cparams(1),
        name="modulate",
    )(x2d, mt)


def _erf(x):
    return lax.erf(x)


def _mm_kernel(x_ref, w_ref, o_ref, wb_ref, *, act):
    @pl.when(pl.program_id(1) == 0)
    def _():
        wb_ref[...] = w_ref[...].astype(bf16)

    acc = _dot(x_ref[...], wb_ref[...])
    if act == "gelu":
        acc = 0.5 * acc * (1.0 + _erf(acc * (2.0 ** -0.5)))
    o_ref[...] = acc.astype(o_ref.dtype)


def _matmul(x, w, *, tm, tn, act=None, name):
    m, k = x.shape
    n = w.shape[1]
    return pl.pallas_call(
        functools.partial(_mm_kernel, act=act),
        grid=(n // tn, m // tm),
        in_specs=[pl.BlockSpec((tm, k), lambda j, i: (i, 0)),
                  pl.BlockSpec((k, tn), lambda j, i: (0, j))],
        out_specs=pl.BlockSpec((tm, tn), lambda j, i: (i, j)),
        out_shape=jax.ShapeDtypeStruct((m, n), bf16),
        scratch_shapes=[pltpu.VMEM((k, tn), bf16)],
        compiler_params=_cparams(2),
        name=name,
    )(x, w)


def _conv_qkv_kernel(xm_ref, cw_ref, cb_ref, wq_ref, wk_ref, wv_ref, wg_ref, bg_ref,
                     xc_ref, q_ref, k_ref, v_ref, g_ref):
    ct = pl.program_id(1)
    a = xm_ref[0].astype(f32)
    c = a.shape[1]
    r = lax.broadcasted_iota(jnp.int32, (U, c), 0)
    latent = r >= LC
    p = r - LC
    col = jnp.where(latent, p & (GRID_W - 1), r)
    last = jnp.where(latent, GRID_W - 1, LC - 1)
    a_l = jnp.where(col > 0, pltpu.roll(a, 1, 0), 0.0)
    a_r = jnp.where(col < last, pltpu.roll(a, U - 1, 0), 0.0)
    w = cw_ref[...]
    rows = [w[3 * i:3 * i + 1] * a_l + w[3 * i + 1:3 * i + 2] * a + w[3 * i + 2:3 * i + 3] * a_r
            for i in range(3)]
    up = jnp.where(p >= GRID_W, pltpu.roll(rows[0], GRID_W, 0), 0.0)
    down = jnp.where(latent & (p < SEQ - GRID_W), pltpu.roll(rows[2], U - GRID_W, 0), 0.0)
    pre = rows[1] + up + down + cb_ref[...]
    xc = (pre * _sigmoid(pre)).astype(bf16)
    xc_ref[0] = xc
    xm = xm_ref[0]
    q = _dot(xc, wq_ref[0]).astype(bf16)
    k = _dot(xc, wk_ref[0]).astype(bf16)
    v = _dot(xm, wv_ref[0]).astype(bf16)
    q_ref[0] = q
    k_ref[0] = k
    v_ref[0] = v
    g = (_dot(q, wg_ref[0].astype(bf16)) + _dot(k, wg_ref[1].astype(bf16))
         + _dot(v, wg_ref[2].astype(bf16)))

    @pl.when(ct == 0)
    def _():
        g_ref[0] = g + bg_ref[...]

    @pl.when(ct > 0)
    def _():
        g_ref[0] += g


def _block_diag(w):
    per = BD_TILE // QKV_BLOCK
    wt = w.reshape(E // BD_TILE, per, QKV_BLOCK, QKV_BLOCK)
    eye = jnp.eye(per, dtype=w.dtype)
    full = jnp.einsum("tgio,gh->tgiho", wt, eye)
    return full.reshape(E // BD_TILE, BD_TILE, BD_TILE).astype(bf16)


def _conv_qkv(xz, conv_w, conv_b, w_q, w_k, w_v, w_gate, b_gate):
    c = BD_TILE
    n_gate = 4 * NH
    act = jax.ShapeDtypeStruct((BATCH, U, E), bf16)
    tile = pl.BlockSpec((1, U, c), lambda b, t: (b, 0, t))
    bd = pl.BlockSpec((1, c, c), lambda b, t: (t, 0, 0))
    return pl.pallas_call(
        _conv_qkv_kernel,
        grid=(BATCH, E // c),
        in_specs=[tile,
                  pl.BlockSpec((9, c), lambda b, t: (0, t)),
                  pl.BlockSpec((1, c), lambda b, t: (0, t)),
                  bd, bd, bd,
                  pl.BlockSpec((3, c, n_gate), lambda b, t: (0, t, 0)),
                  pl.BlockSpec((1, n_gate), lambda b, t: (0, 0))],
        out_specs=[tile, tile, tile, tile,
                   pl.BlockSpec((1, U, n_gate), lambda b, t: (b, 0, 0))],
        out_shape=[act, act, act, act, jax.ShapeDtypeStruct((BATCH, U, n_gate), f32)],
        compiler_params=_cparams(2),
        name="conv_qkv_gates",
    )(xz.reshape(BATCH, U, 2 * E), conv_w.reshape(9, E), conv_b.reshape(1, E),
      _block_diag(w_q), _block_diag(w_k), _block_diag(w_v),
      w_gate.reshape(3, E, n_gate), b_gate.reshape(1, n_gate))


def _log_sigmoid(x):
    return jnp.minimum(x, 0.0) - jnp.log1p(jnp.exp(-jnp.abs(x)))


def _mlstm_kernel(*refs, direction, finish):
    if finish:
        (q_ref, k_ref, v_ref, gc_ref, gr_ref, hf_ref, xc_ref, z_ref, nw_ref, sk_ref,
         o_ref, ct_ref, ctb_ref, n_ref, m_ref) = refs
    else:
        q_ref, k_ref, v_ref, gc_ref, gr_ref, o_ref, ct_ref, ctb_ref, n_ref, m_ref = refs
    L = MLSTM_CHUNK
    h = pl.program_id(1)
    step = pl.program_id(2)

    @pl.when(step == 0)
    def _():
        ct_ref[...] = jnp.zeros_like(ct_ref)
        ctb_ref[...] = jnp.zeros_like(ctb_ref)
        n_ref[...] = jnp.zeros_like(n_ref)
        m_ref[...] = jnp.full_like(m_ref, -jnp.inf)

    i_idx = 2 * direction * NH + h
    f_idx = (2 * direction + 1) * NH + h
    gc = gc_ref[0]
    gr = gr_ref[0]
    lane = lax.broadcasted_iota(jnp.int32, gc.shape, 1)
    sub = lax.broadcasted_iota(jnp.int32, gr.shape, 0)
    ig_col = jnp.sum(jnp.where(lane == i_idx, gc, 0.0), axis=1, keepdims=True)
    f_col = jnp.sum(jnp.where(lane == f_idx, gc, 0.0), axis=1, keepdims=True)
    ig_row = jnp.sum(jnp.where(sub == i_idx, gr, 0.0), axis=0, keepdims=True)
    f_row = jnp.sum(jnp.where(sub == f_idx, gr, 0.0), axis=0, keepdims=True)
    lf_col = _log_sigmoid(f_col)
    lf_row = _log_sigmoid(f_row)

    t_i = lax.broadcasted_iota(jnp.int32, (L, L), 0)
    s_i = lax.broadcasted_iota(jnp.int32, (L, L), 1)
    seen = (s_i <= t_i) if direction == 0 else (s_i >= t_i)
    b_col = jnp.sum(jnp.where(seen, lf_row, 0.0), axis=1, keepdims=True)
    seen_t = (t_i <= s_i) if direction == 0 else (t_i >= s_i)
    b_row = jnp.sum(jnp.where(seen_t, lf_col, 0.0), axis=0, keepdims=True)
    b_end = jnp.sum(lf_row, axis=1, keepdims=True)

    m_prev = m_ref[:, 0:1]
    log_d = jnp.where(seen, b_col - b_row + ig_row, -jnp.inf)
    g_col = b_col + m_prev
    m_t = jnp.maximum(g_col, jnp.max(log_d, axis=1, keepdims=True))
    dw = jnp.exp(log_d - m_t)
    inter = jnp.exp(g_col - m_t)

    q = q_ref[0]
    k = k_ref[0] * (DH ** -0.5)
    v = v_ref[0]
    s = lax.dot_general(q, k, (((1,), (1,)), ((), ())), preferred_element_type=f32) * dw
    num = _dot(s.astype(bf16), v) + inter * _dot(q, ctb_ref[...])
    qn = jnp.sum(q.astype(f32) * n_ref[...], axis=1, keepdims=True)
    den = jnp.sum(s, axis=1, keepdims=True) + inter * qn
    hout = num / jnp.maximum(jnp.abs(den), jnp.exp(-m_t))

    w_end = b_end - b_col + ig_col
    m_new = jnp.maximum(b_end + m_prev, jnp.max(w_end, axis=0, keepdims=True))
    decay = jnp.exp(b_end + m_prev - m_new)
    kw = k.astype(f32) * jnp.exp(w_end - m_new)
    upd = lax.dot_general(kw.astype(bf16), v, (((0,), (0,)), ((), ())), preferred_element_type=f32)
    c_new = decay * ct_ref[...] + upd
    ct_ref[...] = c_new
    ctb_ref[...] = c_new.astype(bf16)
    n_ref[...] = decay * n_ref[...] + jnp.sum(kw, axis=0, keepdims=True)
    m_ref[...] = jnp.broadcast_to(m_new, m_ref.shape)

    if not finish:
        o_ref[0] = hout.astype(o_ref.dtype)
    else:
        hs = hout + hf_ref[0].astype(f32)
        mu = jnp.mean(hs, axis=1, keepdims=True)
        xc_ = hs - mu
        var = jnp.mean(xc_ * xc_, axis=1, keepdims=True)
        hn = xc_ * lax.rsqrt(var + LN_EPS)
        z = z_ref[0].astype(f32)
        o_ref[0] = ((hn * nw_ref[...] + sk_ref[...] * xc_ref[0].astype(f32))
                    * (z * _sigmoid(z))).astype(o_ref.dtype)


def _mlstm(q, k, v, gates, gates_t, direction, finish_args=None):
    L = MLSTM_CHUNK
    n_chunks = U // L

    if direction == 0:
        def chunk(s):
            return s
    else:
        def chunk(s):
            return jnp.where(s == 0, 0, n_chunks - s)

    tile = pl.BlockSpec((1, L, DH), lambda b, h, s: (b, chunk(s), h))
    in_specs = [tile, tile, tile,
                pl.BlockSpec((1, L, 4 * NH), lambda b, h, s: (b, chunk(s), 0)),
                pl.BlockSpec((1, 4 * NH, L), lambda b, h, s: (b, 0, chunk(s)))]
    args = [q, k, v, gates, gates_t]
    finish = finish_args is not None
    if finish:
        h_fwd, xc, xz, norm_w, skip = finish_args
        vec = pl.BlockSpec((1, DH), lambda b, h, s: (0, h))
        in_specs += [tile, tile,
                     pl.BlockSpec((1, L, DH), lambda b, h, s: (b, chunk(s), NH + h)),
                     vec, vec]
        args += [h_fwd, xc, xz.reshape(BATCH, U, 2 * E), norm_w.reshape(1, E), skip.reshape(1, E)]
    return pl.pallas_call(
        functools.partial(_mlstm_kernel, direction=direction, finish=finish),
        grid=(BATCH, NH, n_chunks),
        in_specs=in_specs,
        out_specs=tile,
        out_shape=jax.ShapeDtypeStruct((BATCH, U, E), bf16),
        scratch_shapes=[pltpu.VMEM((DH, DH), f32), pltpu.VMEM((DH, DH), bf16),
                        pltpu.VMEM((1, DH), f32), pltpu.VMEM((1, 128), f32)],
        compiler_params=_cparams(3),
        name="mlstm_bwd_finish" if finish else "mlstm_fwd",
    )(*args)


def _ln_rows(r, g, b):
    mu = jnp.mean(r, axis=1, keepdims=True)
    rc = r - mu
    var = jnp.mean(rc * rc, axis=1, keepdims=True)
    return rc * lax.rsqrt(var + LN_EPS) * g + b


def _ln_router_kernel(hs_ref, o_ref, m_ref, g_ref, b_ref, rw_ref, rb_ref,
                      hs1_ref, tok_ref, lg_ref):
    m = m_ref[0]
    r = ALPHA * hs_ref[...] + m[2:3] * o_ref[...].astype(f32)
    hs1 = _ln_rows(r, g_ref[...], b_ref[...])
    hs1_ref[...] = hs1
    tok = hs1 * (1.0 + m[4:5]) + m[3:4]
    t_hi = tok.astype(bf16)
    tok_ref[...] = _pack_halves(tok)
    t_lo = (tok - t_hi.astype(f32)).astype(bf16)
    w = rw_ref[...]
    w_hi = w.astype(bf16)
    w_lo = (w - w_hi.astype(f32)).astype(bf16)
    lg_ref[...] = _dot(t_hi, w_hi) + _dot(t_hi, w_lo) + _dot(t_lo, w_hi) + rb_ref[...]


def _ln_router(hs, o, mt, group, ln_g, ln_b, r_w, r_b):
    m = hs.shape[0]
    row = pl.BlockSpec((ROW_TILE, D), lambda i: (i, 0))
    vec = pl.BlockSpec((1, D), lambda i: (0, 0))
    return pl.pallas_call(
        _ln_router_kernel,
        grid=(m // ROW_TILE,),
        in_specs=[row, row,
                  pl.BlockSpec((1, 6, D), lambda i: (group(i), 0, 0)),
                  vec, vec,
                  pl.BlockSpec((D, NE), lambda i: (0, 0)),
                  pl.BlockSpec((1, NE), lambda i: (0, 0))],
        out_specs=[row, pl.BlockSpec((ROW_TILE, D // 2), lambda i: (i, 0)),
                   pl.BlockSpec((ROW_TILE, NE), lambda i: (i, 0))],
        out_shape=[jax.ShapeDtypeStruct((m, D), f32), jax.ShapeDtypeStruct((m, D // 2), jnp.uint32),
                   jax.ShapeDtypeStruct((m, NE), f32)],
        compiler_params=_cparams(1),
        name="ln_router",
    )(hs, o, mt, ln_g.reshape(1, D), ln_b.reshape(1, D), r_w, r_b.reshape(1, NE))


def _first_of_expert(be_ref, i):
    return jnp.logical_or(i == 0, be_ref[i] != be_ref[jnp.maximum(i - 1, 0)])


def _moe_up_kernel(be_ref, nu_ref, x_ref, wg_ref, wl_ref, bg_ref, bl_ref, o_ref, wgb_ref, wlb_ref):
    i = pl.program_id(1)
    valid = i < nu_ref[0]

    @pl.when(jnp.logical_and(valid, _first_of_expert(be_ref, i)))
    def _():
        wgb_ref[...] = wg_ref[0, 0].astype(bf16)
        wlb_ref[...] = wl_ref[0, 0].astype(bf16)

    @pl.when(valid)
    def _():
        x = _unpack_halves(x_ref[...]).astype(bf16)
        glu = jnp.minimum(_dot(x, wgb_ref[...]) + bg_ref[0, 0], SWIGLU_LIMIT)
        lin = jnp.clip(_dot(x, wlb_ref[...]) + bl_ref[0, 0], -SWIGLU_LIMIT, SWIGLU_LIMIT)
        o_ref[...] = (glu * _sigmoid(SWIGLU_ALPHA * glu) * (lin + 1.0)).astype(o_ref.dtype)

    @pl.when(jnp.logical_not(valid))
    def _():
        o_ref[...] = jnp.zeros_like(o_ref)


def _moe_down_kernel(be_ref, nu_ref, a_ref, wlo_ref, whi_ref, blo_ref, bhi_ref, o_ref, wb_ref):
    i = pl.program_id(1)
    valid = i < nu_ref[0]
    tn = wlo_ref.shape[-1]

    @pl.when(jnp.logical_and(valid, _first_of_expert(be_ref, i)))
    def _():
        wb_ref[:, :tn] = wlo_ref[0, 0].astype(bf16)
        wb_ref[:, tn:] = whi_ref[0, 0].astype(bf16)

    @pl.when(valid)
    def _():
        y = _dot(a_ref[...], wb_ref[...]) + jnp.concatenate([blo_ref[0, 0], bhi_ref[0, 0]], axis=1)
        o_ref[...] = _pack_halves(y)

    @pl.when(jnp.logical_not(valid))
    def _():
        o_ref[...] = jnp.zeros_like(o_ref)


def _moe_experts(layer, xs, block_e, n_used, w1, b1, w2, b2):
    ns = xs.shape[0]
    nb = ns // MOE_TM
    tf = 512
    tn = 512
    lin0 = F // tf
    hi0 = D // 2 // tn
    b1 = b1.reshape(DEPTH, NE, 1, 2 * F)
    b2 = b2.reshape(DEPTH, NE, 1, D)

    def row_blk(j, i, be, nu):
        return (jnp.minimum(i, nu[0] - 1), 0)

    def wspec(rows, width, col0):
        return pl.BlockSpec((1, 1, rows, width), lambda j, i, be, nu: (layer, be[i], 0, col0 + j))

    act = pl.pallas_call(
        _moe_up_kernel,
        grid_spec=pltpu.PrefetchScalarGridSpec(
            num_scalar_prefetch=2,
            grid=(F // tf, nb),
            in_specs=[pl.BlockSpec((MOE_TM, D // 2), row_blk),
                      wspec(D, tf, 0), wspec(D, tf, lin0), wspec(1, tf, 0), wspec(1, tf, lin0)],
            out_specs=pl.BlockSpec((MOE_TM, tf), lambda j, i, be, nu: (i, j)),
            scratch_shapes=[pltpu.VMEM((D, tf), bf16), pltpu.VMEM((D, tf), bf16)]),
        out_shape=jax.ShapeDtypeStruct((ns, F), bf16),
        compiler_params=_cparams(2),
        name="moe_up",
    )(block_e, n_used, xs, w1, w1, b1, b1)

    return pl.pallas_call(
        _moe_down_kernel,
        grid_spec=pltpu.PrefetchScalarGridSpec(
            num_scalar_prefetch=2,
            grid=(D // 2 // tn, nb),
            in_specs=[pl.BlockSpec((MOE_TM, F), row_blk),
                      wspec(F, tn, 0), wspec(F, tn, hi0), wspec(1, tn, 0), wspec(1, tn, hi0)],
            out_specs=pl.BlockSpec((MOE_TM, tn), lambda j, i, be, nu: (i, j)),
            scratch_shapes=[pltpu.VMEM((F, 2 * tn), bf16)]),
        out_shape=jax.ShapeDtypeStruct((ns, D // 2), jnp.uint32),
        compiler_params=_cparams(2),
        name="moe_down",
    )(block_e, n_used, act, w2, w2, b2, b2)


GATHER_ROWS = 512


def _gather_kernel(idx_ref, src_ref, out_ref, sem):
    i = pl.program_id(0)
    n = GATHER_ROWS

    def block_copy(blk):
        return pltpu.make_async_copy(src_ref.at[pl.ds(0, n)], out_ref.at[pl.ds(blk * n, n)],
                                     sem.at[blk % 2])

    def issue(r, carry):
        pltpu.make_async_copy(src_ref.at[pl.ds(idx_ref[0, 0, r], 1)],
                              out_ref.at[pl.ds(i * n + r, 1)], sem.at[i % 2]).start()
        return carry

    lax.fori_loop(0, n, issue, 0, unroll=8)

    @pl.when(i > 0)
    def _():
        block_copy(i - 1).wait()

    @pl.when(i == pl.num_programs(0) - 1)
    def _():
        block_copy(i).wait()


def _gather_rows(src, idx):
    n = idx.shape[0]
    nblk = n // GATHER_ROWS
    return pl.pallas_call(
        _gather_kernel,
        grid=(nblk,),
        in_specs=[pl.BlockSpec((1, 1, GATHER_ROWS), lambda i: (i, 0, 0), memory_space=pltpu.SMEM),
                  pl.BlockSpec(memory_space=pl.ANY)],
        out_specs=pl.BlockSpec(memory_space=pl.ANY),
        out_shape=jax.ShapeDtypeStruct((n, src.shape[1]), src.dtype),
        scratch_shapes=[pltpu.SemaphoreType.DMA((2,))],
        compiler_params=pltpu.CompilerParams(dimension_semantics=("arbitrary",),
                                             disable_bounds_checks=True),
        name="gather_rows",
    )(idx.reshape(nblk, 1, GATHER_ROWS), src)


def _route(logits):
    t = logits.shape[0]
    n_assign = t * TOP_K
    nb = -(-n_assign // MOE_TM) + NE
    top_v, top_i = lax.top_k(logits, TOP_K)
    gate = jax.nn.softmax(top_v, axis=-1)
    flat_e = top_i.reshape(-1)
    onehot = (flat_e[:, None] == jnp.arange(NE, dtype=flat_e.dtype)[None, :]).astype(jnp.int32)
    csum = jnp.cumsum(onehot, axis=0)
    counts = csum[-1]
    rank = jnp.take_along_axis(csum, flat_e[:, None], axis=1)[:, 0] - 1
    pcounts = (counts + MOE_TM - 1) // MOE_TM * MOE_TM
    pends = jnp.cumsum(pcounts)
    pstarts = pends - pcounts
    dest = (pstarts[flat_e] + rank).astype(jnp.int32)
    slot_tok = jnp.zeros((nb * MOE_TM,), jnp.int32).at[dest].set(
        jnp.arange(n_assign, dtype=jnp.int32) // TOP_K)
    n_used = (pends[-1] // MOE_TM).astype(jnp.int32)
    blk = jnp.arange(nb, dtype=jnp.int32)
    block_e = jnp.clip(jnp.searchsorted(pends, blk * MOE_TM, side="right"), 0, NE - 1).astype(jnp.int32)
    block_e = jnp.where(blk < n_used, block_e, block_e[n_used - 1])
    return gate, dest, slot_tok, block_e, n_used.reshape(1)


def _combine_kernel(hs_ref, y_ref, gate_ref, m_ref, g_ref, b_ref, *rest, with_next):
    if with_next:
        mn_ref, hs2_ref, nxt_ref = rest
    else:
        (hs2_ref,) = rest
    m = m_ref[0]
    gate = gate_ref[...]
    y = gate[:, 0:1] * _unpack_halves(y_ref[0])
    for k in range(1, TOP_K):
        y = y + gate[:, k:k + 1] * _unpack_halves(y_ref[k])
    hs2 = _ln_rows(ALPHA * hs_ref[...] + m[5:6] * y, g_ref[...], b_ref[...])
    hs2_ref[...] = hs2
    if with_next:
        mn = mn_ref[0]
        nxt_ref[...] = (hs2 * (1.0 + mn[1:2]) + mn[0:1]).astype(nxt_ref.dtype)


def _combine(hs1, yg, gate, mt, group, ln_g, ln_b, mt_next=None):
    m = hs1.shape[0]
    row = pl.BlockSpec((ROW_TILE, D), lambda i: (i, 0))
    vec = pl.BlockSpec((1, D), lambda i: (0, 0))
    mod = pl.BlockSpec((1, 6, D), lambda i: (group(i), 0, 0))
    in_specs = [row, pl.BlockSpec((TOP_K, ROW_TILE, D // 2), lambda i: (0, i, 0)),
                pl.BlockSpec((ROW_TILE, TOP_K), lambda i: (i, 0)), mod, vec, vec]
    args = [hs1, yg, gate, mt, ln_g.reshape(1, D), ln_b.reshape(1, D)]
    out_specs = [row]
    out_shape = [jax.ShapeDtypeStruct((m, D), f32)]
    with_next = mt_next is not None
    if with_next:
        in_specs.append(mod)
        args.append(mt_next)
        out_specs.append(row)
        out_shape.append(jax.ShapeDtypeStruct((m, D), bf16))
    return pl.pallas_call(
        functools.partial(_combine_kernel, with_next=with_next),
        grid=(m // ROW_TILE,),
        in_specs=in_specs, out_specs=out_specs, out_shape=out_shape,
        compiler_params=_cparams(1),
        name="moe_combine_ln",
    )(*args)


def _moe_layer(layer, hs, o, mt, group, ln1_g, ln1_b, ln2_g, ln2_b, r_w, r_b, w1, b1, w2, b2, mt_next=None):
    t = hs.shape[0]
    hs1, tok, logits = _ln_router(hs, o, mt, group, ln1_g[layer], ln1_b[layer], r_w[layer], r_b[layer])
    gate, dest, slot_tok, block_e, n_used = _route(logits)
    xs = _gather_rows(tok, slot_tok)
    y = _moe_experts(layer, xs, block_e, n_used, w1, b1, w2, b2)
    yg = _gather_rows(y, dest.reshape(t, TOP_K).T.reshape(-1)).reshape(TOP_K, t, D // 2)
    return _combine(hs1, yg, gate, mt, group, ln2_g[layer], ln2_b[layer], mt_next)


def _sgu_kernel(g_ref, v_ref, lg_ref, lb_ref, ws_ref, bs_ref, o_ref):
    v = v_ref[...].astype(f32)
    vn = _ln_rows(v, lg_ref[...], lb_ref[...]).astype(bf16)
    gw = E // SGU_GROUPS
    for g in range(SGU_GROUPS):
        mixed = _dot(ws_ref[g].astype(bf16), vn[:, g * gw:(g + 1) * gw]) + bs_ref[:, g:g + 1]
        o_ref[:, g * gw:(g + 1) * gw] = (g_ref[:, g * gw:(g + 1) * gw].astype(f32) * mixed).astype(o_ref.dtype)


def _sgu(uv, ln_g, ln_b, w_s, b_s):
    m = uv.shape[0]
    c = SGU_CHUNK
    vec = pl.BlockSpec((1, E), lambda i: (0, 0))
    return pl.pallas_call(
        _sgu_kernel,
        grid=(m // c,),
        in_specs=[pl.BlockSpec((c, E), lambda i: (i, 0)),
                  pl.BlockSpec((c, E), lambda i: (i, 1)),
                  vec, vec,
                  pl.BlockSpec((SGU_GROUPS, c, c), lambda i: (0, 0, 0)),
                  pl.BlockSpec((c, SGU_GROUPS), lambda i: (0, 0))],
        out_specs=pl.BlockSpec((c, E), lambda i: (i, 0)),
        out_shape=jax.ShapeDtypeStruct((m, E), bf16),
        compiler_params=_cparams(1),
        name="sgu",
    )(uv, uv, ln_g.reshape(1, E), ln_b.reshape(1, E), w_s, b_s.T)


def kernel(x, c, ctx, c_ctx, mod_w, mod_b, ln1_g, ln1_b, ln2_g, ln2_b, a_w_in, a_conv_w, a_conv_b, a_w_q, a_w_k, a_w_v, a_w_gate, a_b_gate, a_norm_w, a_skip, a_w_out, b_w_in, b_ln_g, b_ln_b, b_w_s, b_b_s, b_w_out, r_w, r_b, e_w1, e_b1, e_w2, e_b2):
    cvec = jnp.concatenate([c, c_ctx[None, :], jnp.zeros((8 - BATCH - 1, D), f32)], axis=0)
    mods = _mod_rows(cvec, mod_w, mod_b).reshape(DEPTH, 8, 6, D)
    rows = [r for b in range(BATCH) for r in (BATCH, b)]
    mt = [jnp.stack([mods[l, r] for r in rows], axis=0) for l in range(DEPTH)]

    hs = jnp.concatenate([ctx, x], axis=1).reshape(BATCH * U, D)
    hx = _modulate(hs, mt[0], _group_joint)
    xz = _matmul(hx, a_w_in[0], tm=1024, tn=1024, name="mlstm_in_proj")
    xc, q, k, v, gates = _conv_qkv(xz, a_conv_w[0], a_conv_b[0], a_w_q[0], a_w_k[0], a_w_v[0],
                                   a_w_gate[0], a_b_gate[0])
    gates_t = gates.transpose(0, 2, 1)
    h_fwd = _mlstm(q, k, v, gates, gates_t, 0)
    pre = _mlstm(q, k, v, gates, gates_t, 1, (h_fwd, xc, xz, a_norm_w[0], a_skip[0]))
    o = _matmul(pre.reshape(BATCH * U, E), a_w_out[0], tm=1024, tn=512, name="mlstm_out_proj")
    hs2, hx1 = _moe_layer(0, hs, o, mt[0], _group_joint, ln1_g, ln1_b, ln2_g, ln2_b,
                          r_w, r_b, e_w1, e_b1, e_w2, e_b2, mt_next=mt[1])

    hs = hs2.reshape(BATCH, U, D)[:, LC:].reshape(BATCH * SEQ, D)
    hx = hx1.reshape(BATCH, U, D)[:, LC:].reshape(BATCH * SEQ, D)
    uv = _matmul(hx, b_w_in[0], tm=1024, tn=1024, act="gelu", name="sgu_in_proj")
    gated = _sgu(uv, b_ln_g[0], b_ln_b[0], b_w_s[0], b_b_s[0])
    o = _matmul(gated, b_w_out[0], tm=1024, tn=512, name="sgu_out_proj")
    (out,) = _moe_layer(1, hs, o, mt[1], _group_latent, ln1_g, ln1_b, ln2_g, ln2_b,
                        r_w, r_b, e_w1, e_b1, e_w2, e_b2)
    return out.reshape(BATCH, SEQ, D)
```

```python
import functools

import jax
import jax.numpy as jnp
from jax import lax
from jax.experimental import pallas as pl
from jax.experimental.pallas import tpu as pltpu

D = 2048
BATCH = 4
SEQ = 2048
DEPTH = 2
GRID_W = 64
LC = 256
U = LC + SEQ
E = 2 * D
NH = 4
DH = E // NH
QKV_BLOCK = 4
SGU_GROUPS = 8
SGU_CHUNK = 128
NE = 32
TOP_K = 4
F = D
SWIGLU_LIMIT = 7.0
SWIGLU_ALPHA = 1.702
ALPHA = (2 * DEPTH) ** 0.25
LN_EPS = 1e-5

ROW_TILE = 256
MLSTM_CHUNK = 256
MOE_TM = 512
BD_TILE = 256
VMEM_LIMIT = 56 * 1024 * 1024

f32 = jnp.float32
bf16 = jnp.bfloat16


def _cparams(n_axes):
    return pltpu.CompilerParams(dimension_semantics=("arbitrary",) * n_axes,
                                vmem_limit_bytes=VMEM_LIMIT)


def _dot(a, b):
    return jnp.dot(a, b, preferred_element_type=f32)


def _sigmoid(x):
    return 1.0 / (1.0 + jnp.exp(-x))


def _pack_halves(x):
    w = x.shape[1] // 2
    bits = lax.bitcast_convert_type(x.astype(bf16).astype(f32), jnp.uint32)
    return (bits[:, w:] & jnp.uint32(0xFFFF0000)) | (bits[:, :w] >> 16)


def _unpack_halves(p):
    lo = lax.bitcast_convert_type(p << 16, f32)
    hi = lax.bitcast_convert_type(p & jnp.uint32(0xFFFF0000), f32)
    return jnp.concatenate([lo, hi], axis=1)


PACKED = D // 2
LANES = 128
TILE_ROWS = PACKED // LANES


def _store_tile_rows(ref, p):
    r = p.shape[0]
    for s in range(TILE_ROWS):
        ref[pl.ds(s, r, stride=TILE_ROWS), :] = p[:, s * LANES:(s + 1) * LANES]


def _load_tile_rows(ref, r):
    return jnp.concatenate([ref[pl.ds(s, r, stride=TILE_ROWS), :] for s in range(TILE_ROWS)], axis=1)


def _mod_kernel(c_ref, w_ref, b_ref, o_ref):
    c = c_ref[...]
    a = (c * _sigmoid(c)).astype(bf16)
    o_ref[0] = _dot(a, w_ref[0].astype(bf16)) + b_ref[0]


def _mod_rows(cvec, mod_w, mod_b):
    tn = 1024
    return pl.pallas_call(
        _mod_kernel,
        grid=(DEPTH, 6 * D // tn),
        in_specs=[pl.BlockSpec((8, D), lambda l, j: (0, 0)),
                  pl.BlockSpec((1, D, tn), lambda l, j: (l, 0, j)),
                  pl.BlockSpec((1, 1, tn), lambda l, j: (l, 0, j))],
        out_specs=pl.BlockSpec((1, 8, tn), lambda l, j: (l, 0, j)),
        out_shape=jax.ShapeDtypeStruct((DEPTH, 8, 6 * D), f32),
        compiler_params=_cparams(2),
        name="adaln_rows",
    )(cvec, mod_w, mod_b.reshape(DEPTH, 1, 6 * D))


def _group_joint(i):
    tiles = U // ROW_TILE
    return 2 * (i // tiles) + jnp.minimum(i % tiles, 1)


def _group_latent(i):
    return 2 * (i // (SEQ // ROW_TILE)) + 1


def _modulate_kernel(x_ref, m_ref, o_ref):
    m = m_ref[0]
    o_ref[...] = (x_ref[...] * (1.0 + m[1:2]) + m[0:1]).astype(o_ref.dtype)


def _modulate(x2d, mt, group):
    m = x2d.shape[0]
    return pl.pallas_call(
        _modulate_kernel,
        grid=(m // ROW_TILE,),
        in_specs=[pl.BlockSpec((ROW_TILE, D), lambda i: (i, 0)),
                  pl.BlockSpec((1, 6, D), lambda i: (group(i), 0, 0))],
        out_specs=pl.BlockSpec((ROW_TILE, D), lambda i: (i, 0)),
        out_shape=jax.ShapeDtypeStruct((m, D), bf16),
        compiler_params=_cparams(1),
        name="modulate",
    )(x2d, mt)


def _erf(x):
    return lax.erf(x)


def _mm_kernel(x_ref, w_ref, o_ref, wb_ref, *, act):
    @pl.when(pl.program_id(1) == 0)
    def _():
        wb_ref[...] = w_ref[...].astype(bf16)

    acc = _dot(x_ref[...], wb_ref[...])
    if act == "gelu":
        acc = 0.5 * acc * (1.0 + _erf(acc * (2.0 ** -0.5)))
    o_ref[...] = acc.astype(o_ref.dtype)


def _matmul(x, w, *, tm, tn, act=None, name):
    m, k = x.shape
    n = w.shape[1]
    return pl.pallas_call(
        functools.partial(_mm_kernel, act=act),
        grid=(n // tn, m // tm),
        in_specs=[pl.BlockSpec((tm, k), lambda j, i: (i, 0)),
                  pl.BlockSpec((k, tn), lambda j, i: (0, j))],
        out_specs=pl.BlockSpec((tm, tn), lambda j, i: (i, j)),
        out_shape=jax.ShapeDtypeStruct((m, n), bf16),
        scratch_shapes=[pltpu.VMEM((k, tn), bf16)],
        compiler_params=_cparams(2),
        name=name,
    )(x, w)


def _conv_qkv_kernel(xm_ref, cw_ref, cb_ref, wq_ref, wk_ref, wv_ref, wg_ref, bg_ref,
                     xc_ref, q_ref, k_ref, v_ref, g_ref):
    ct = pl.program_id(1)
    a = xm_ref[0].astype(f32)
    c = a.shape[1]
    r = lax.broadcasted_iota(jnp.int32, (U, c), 0)
    latent = r >= LC
    p = r - LC
    col = jnp.where(latent, p & (GRID_W - 1), r)
    last = jnp.where(latent, GRID_W - 1, LC - 1)
    a_l = jnp.where(col > 0, pltpu.roll(a, 1, 0), 0.0)
    a_r = jnp.where(col < last, pltpu.roll(a, U - 1, 0), 0.0)
    w = cw_ref[...]
    rows = [w[3 * i:3 * i + 1] * a_l + w[3 * i + 1:3 * i + 2] * a + w[3 * i + 2:3 * i + 3] * a_r
            for i in range(3)]
    up = jnp.where(p >= GRID_W, pltpu.roll(rows[0], GRID_W, 0), 0.0)
    down = jnp.where(latent & (p < SEQ - GRID_W), pltpu.roll(rows[2], U - GRID_W, 0), 0.0)
    pre = rows[1] + up + down + cb_ref[...]
    xc = (pre * _sigmoid(pre)).astype(bf16)
    xc_ref[0] = xc
    xm = xm_ref[0]
    q = _dot(xc, wq_ref[0]).astype(bf16)
    k = _dot(xc, wk_ref[0]).astype(bf16)
    v = _dot(xm, wv_ref[0]).astype(bf16)
    q_ref[0] = q
    k_ref[0] = k
    v_ref[0] = v
    g = (_dot(q, wg_ref[0].astype(bf16)) + _dot(k, wg_ref[1].astype(bf16))
         + _dot(v, wg_ref[2].astype(bf16)))

    @pl.when(ct == 0)
    def _():
        g_ref[0] = g + bg_ref[...]

    @pl.when(ct > 0)
    def _():
        g_ref[0] += g


def _block_diag(w):
    per = BD_TILE // QKV_BLOCK
    wt = w.reshape(E // BD_TILE, per, QKV_BLOCK, QKV_BLOCK)
    eye = jnp.eye(per, dtype=w.dtype)
    full = jnp.einsum("tgio,gh->tgiho", wt, eye)
    return full.reshape(E // BD_TILE, BD_TILE, BD_TILE).astype(bf16)


def _conv_qkv(xz, conv_w, conv_b, w_q, w_k, w_v, w_gate, b_gate):
    c = BD_TILE
    n_gate = 4 * NH
    act = jax.ShapeDtypeStruct((BATCH, U, E), bf16)
    tile = pl.BlockSpec((1, U, c), lambda b, t: (b, 0, t))
    bd = pl.BlockSpec((1, c, c), lambda b, t: (t, 0, 0))
    return pl.pallas_call(
        _conv_qkv_kernel,
        grid=(BATCH, E // c),
        in_specs=[tile,
                  pl.BlockSpec((9, c), lambda b, t: (0, t)),
                  pl.BlockSpec((1, c), lambda b, t: (0, t)),
                  bd, bd, bd,
                  pl.BlockSpec((3, c, n_gate), lambda b, t: (0, t, 0)),
                  pl.BlockSpec((1, n_gate), lambda b, t: (0, 0))],
        out_specs=[tile, tile, tile, tile,
                   pl.BlockSpec((1, U, n_gate), lambda b, t: (b, 0, 0))],
        out_shape=[act, act, act, act, jax.ShapeDtypeStruct((BATCH, U, n_gate), f32)],
        compiler_params=_cparams(2),
        name="conv_qkv_gates",
    )(xz.reshape(BATCH, U, 2 * E), conv_w.reshape(9, E), conv_b.reshape(1, E),
      _block_diag(w_q), _block_diag(w_k), _block_diag(w_v),
      w_gate.reshape(3, E, n_gate), b_gate.reshape(1, n_gate))


def _log_sigmoid(x):
    return jnp.minimum(x, 0.0) - jnp.log1p(jnp.exp(-jnp.abs(x)))


def _mlstm_kernel(*refs, direction, finish):
    if finish:
        (q_ref, k_ref, v_ref, gc_ref, gr_ref, hf_ref, xc_ref, z_ref, nw_ref, sk_ref,
         o_ref, ct_ref, ctb_ref, n_ref, m_ref) = refs
    else:
        q_ref, k_ref, v_ref, gc_ref, gr_ref, o_ref, ct_ref, ctb_ref, n_ref, m_ref = refs
    L = MLSTM_CHUNK
    h = pl.program_id(1)
    step = pl.program_id(2)

    @pl.when(step == 0)
    def _():
        ct_ref[...] = jnp.zeros_like(ct_ref)
        ctb_ref[...] = jnp.zeros_like(ctb_ref)
        n_ref[...] = jnp.zeros_like(n_ref)
        m_ref[...] = jnp.full_like(m_ref, -jnp.inf)

    i_idx = 2 * direction * NH + h
    f_idx = (2 * direction + 1) * NH + h
    gc = gc_ref[0]
    gr = gr_ref[0]
    lane = lax.broadcasted_iota(jnp.int32, gc.shape, 1)
    sub = lax.broadcasted_iota(jnp.int32, gr.shape, 0)
    ig_col = jnp.sum(jnp.where(lane == i_idx, gc, 0.0), axis=1, keepdims=True)
    f_col = jnp.sum(jnp.where(lane == f_idx, gc, 0.0), axis=1, keepdims=True)
    ig_row = jnp.sum(jnp.where(sub == i_idx, gr, 0.0), axis=0, keepdims=True)
    f_row = jnp.sum(jnp.where(sub == f_idx, gr, 0.0), axis=0, keepdims=True)
    lf_col = _log_sigmoid(f_col)
    lf_row = _log_sigmoid(f_row)

    t_i = lax.broadcasted_iota(jnp.int32, (L, L), 0)
    s_i = lax.broadcasted_iota(jnp.int32, (L, L), 1)
    seen = (s_i <= t_i) if direction == 0 else (s_i >= t_i)
    b_col = jnp.sum(jnp.where(seen, lf_row, 0.0), axis=1, keepdims=True)
    seen_t = (t_i <= s_i) if direction == 0 else (t_i >= s_i)
    b_row = jnp.sum(jnp.where(seen_t, lf_col, 0.0), axis=0, keepdims=True)
    b_end = jnp.sum(lf_row, axis=1, keepdims=True)

    m_prev = m_ref[:, 0:1]
    log_d = jnp.where(seen, b_col - b_row + ig_row, -jnp.inf)
    g_col = b_col + m_prev
    m_t = jnp.maximum(g_col, jnp.max(log_d, axis=1, keepdims=True))
    dw = jnp.exp(log_d - m_t)
    inter = jnp.exp(g_col - m_t)

    q = q_ref[0]
    k = k_ref[0] * (DH ** -0.5)
    v = v_ref[0]
    s = lax.dot_general(q, k, (((1,), (1,)), ((), ())), preferred_element_type=f32) * dw
    num = _dot(s.astype(bf16), v) + inter * _dot(q, ctb_ref[...])
    qn = jnp.sum(q.astype(f32) * n_ref[...], axis=1, keepdims=True)
    den = jnp.sum(s, axis=1, keepdims=True) + inter * qn
    hout = num / jnp.maximum(jnp.abs(den), jnp.exp(-m_t))

    w_end = b_end - b_col + ig_col
    m_new = jnp.maximum(b_end + m_prev, jnp.max(w_end, axis=0, keepdims=True))
    decay = jnp.exp(b_end + m_prev - m_new)
    kw = k.astype(f32) * jnp.exp(w_end - m_new)
    upd = lax.dot_general(kw.astype(bf16), v, (((0,), (0,)), ((), ())), preferred_element_type=f32)
    c_new = decay * ct_ref[...] + upd
    ct_ref[...] = c_new
    ctb_ref[...] = c_new.astype(bf16)
    n_ref[...] = decay * n_ref[...] + jnp.sum(kw, axis=0, keepdims=True)
    m_ref[...] = jnp.broadcast_to(m_new, m_ref.shape)

    if not finish:
        o_ref[0] = hout.astype(o_ref.dtype)
    else:
        hs = hout + hf_ref[0].astype(f32)
        mu = jnp.mean(hs, axis=1, keepdims=True)
        xc_ = hs - mu
        var = jnp.mean(xc_ * xc_, axis=1, keepdims=True)
        hn = xc_ * lax.rsqrt(var + LN_EPS)
        z = z_ref[0].astype(f32)
        o_ref[0] = ((hn * nw_ref[...] + sk_ref[...] * xc_ref[0].astype(f32))
                    * (z * _sigmoid(z))).astype(o_ref.dtype)


def _mlstm(q, k, v, gates, gates_t, direction, finish_args=None):
    L = MLSTM_CHUNK
    n_chunks = U // L

    if direction == 0:
        def chunk(s):
            return s
    else:
        def chunk(s):
            return jnp.where(s == 0, 0, n_chunks - s)

    tile = pl.BlockSpec((1, L, DH), lambda b, h, s: (b, chunk(s), h))
    in_specs = [tile, tile, tile,
                pl.BlockSpec((1, L, 4 * NH), lambda b, h, s: (b, chunk(s), 0)),
                pl.BlockSpec((1, 4 * NH, L), lambda b, h, s: (b, 0, chunk(s)))]
    args = [q, k, v, gates, gates_t]
    finish = finish_args is not None
    if finish:
        h_fwd, xc, xz, norm_w, skip = finish_args
        vec = pl.BlockSpec((1, DH), lambda b, h, s: (0, h))
        in_specs += [tile, tile,
                     pl.BlockSpec((1, L, DH), lambda b, h, s: (b, chunk(s), NH + h)),
                     vec, vec]
        args += [h_fwd, xc, xz.reshape(BATCH, U, 2 * E), norm_w.reshape(1, E), skip.reshape(1, E)]
    return pl.pallas_call(
        functools.partial(_mlstm_kernel, direction=direction, finish=finish),
        grid=(BATCH, NH, n_chunks),
        in_specs=in_specs,
        out_specs=tile,
        out_shape=jax.ShapeDtypeStruct((BATCH, U, E), bf16),
        scratch_shapes=[pltpu.VMEM((DH, DH), f32), pltpu.VMEM((DH, DH), bf16),
                        pltpu.VMEM((1, DH), f32), pltpu.VMEM((1, 128), f32)],
        compiler_params=_cparams(3),
        name="mlstm_bwd_finish" if finish else "mlstm_fwd",
    )(*args)


def _ln_rows(r, g, b):
    mu = jnp.mean(r, axis=1, keepdims=True)
    rc = r - mu
    var = jnp.mean(rc * rc, axis=1, keepdims=True)
    return rc * lax.rsqrt(var + LN_EPS) * g + b


def _ln_router_kernel(hs_ref, o_ref, m_ref, g_ref, b_ref, rw_ref, rb_ref,
                      hs1_ref, tok_ref, lg_ref):
    m = m_ref[0]
    r = ALPHA * hs_ref[...] + m[2:3] * o_ref[...].astype(f32)
    hs1 = _ln_rows(r, g_ref[...], b_ref[...])
    hs1_ref[...] = hs1
    tok = hs1 * (1.0 + m[4:5]) + m[3:4]
    t_hi = tok.astype(bf16)
    _store_tile_rows(tok_ref, _pack_halves(tok))
    t_lo = (tok - t_hi.astype(f32)).astype(bf16)
    w = rw_ref[...]
    w_hi = w.astype(bf16)
    w_lo = (w - w_hi.astype(f32)).astype(bf16)
    lg_ref[...] = _dot(t_hi, w_hi) + _dot(t_hi, w_lo) + _dot(t_lo, w_hi) + rb_ref[...]


def _ln_router(hs, o, mt, group, ln_g, ln_b, r_w, r_b):
    m = hs.shape[0]
    row = pl.BlockSpec((ROW_TILE, D), lambda i: (i, 0))
    vec = pl.BlockSpec((1, D), lambda i: (0, 0))
    return pl.pallas_call(
        _ln_router_kernel,
        grid=(m // ROW_TILE,),
        in_specs=[row, row,
                  pl.BlockSpec((1, 6, D), lambda i: (group(i), 0, 0)),
                  vec, vec,
                  pl.BlockSpec((D, NE), lambda i: (0, 0)),
                  pl.BlockSpec((1, NE), lambda i: (0, 0))],
        out_specs=[row, pl.BlockSpec((ROW_TILE * TILE_ROWS, LANES), lambda i: (i, 0)),
                   pl.BlockSpec((ROW_TILE, NE), lambda i: (i, 0))],
        out_shape=[jax.ShapeDtypeStruct((m, D), f32),
                   jax.ShapeDtypeStruct((m * TILE_ROWS, LANES), jnp.uint32),
                   jax.ShapeDtypeStruct((m, NE), f32)],
        compiler_params=_cparams(1),
        name="ln_router",
    )(hs, o, mt, ln_g.reshape(1, D), ln_b.reshape(1, D), r_w, r_b.reshape(1, NE))


def _first_of_expert(be_ref, i):
    return jnp.logical_or(i == 0, be_ref[i] != be_ref[jnp.maximum(i - 1, 0)])


def _moe_up_kernel(be_ref, nu_ref, x_ref, wg_ref, wl_ref, bg_ref, bl_ref, o_ref, wgb_ref, wlb_ref):
    i = pl.program_id(1)
    valid = i < nu_ref[0]

    @pl.when(jnp.logical_and(valid, _first_of_expert(be_ref, i)))
    def _():
        wgb_ref[...] = wg_ref[0, 0].astype(bf16)
        wlb_ref[...] = wl_ref[0, 0].astype(bf16)

    @pl.when(valid)
    def _():
        x = _unpack_halves(_load_tile_rows(x_ref, MOE_TM)).astype(bf16)
        glu = jnp.minimum(_dot(x, wgb_ref[...]) + bg_ref[0, 0], SWIGLU_LIMIT)
        lin = jnp.clip(_dot(x, wlb_ref[...]) + bl_ref[0, 0], -SWIGLU_LIMIT, SWIGLU_LIMIT)
        o_ref[...] = (glu * _sigmoid(SWIGLU_ALPHA * glu) * (lin + 1.0)).astype(o_ref.dtype)

    @pl.when(jnp.logical_not(valid))
    def _():
        o_ref[...] = jnp.zeros_like(o_ref)


def _moe_down_kernel(be_ref, nu_ref, a_ref, w_ref, b_ref, o_ref, wb_ref):
    i = pl.program_id(0)
    valid = i < nu_ref[0]

    @pl.when(jnp.logical_and(valid, _first_of_expert(be_ref, i)))
    def _():
        wb_ref[...] = w_ref[0, 0].astype(bf16)

    @pl.when(valid)
    def _():
        y = _dot(a_ref[...], wb_ref[...]) + b_ref[0, 0]
        _store_tile_rows(o_ref, _pack_halves(y))

    @pl.when(jnp.logical_not(valid))
    def _():
        o_ref[...] = jnp.zeros_like(o_ref)


def _moe_experts(layer, xs, block_e, n_used, w1, b1, w2, b2):
    ns = xs.shape[0] // TILE_ROWS
    nb = ns // MOE_TM
    tf = 512
    lin0 = F // tf
    b1 = b1.reshape(DEPTH, NE, 1, 2 * F)
    b2 = b2.reshape(DEPTH, NE, 1, D)

    def wspec(rows, width, col0):
        return pl.BlockSpec((1, 1, rows, width), lambda j, i, be, nu: (layer, be[i], 0, col0 + j))

    act = pl.pallas_call(
        _moe_up_kernel,
        grid_spec=pltpu.PrefetchScalarGridSpec(
            num_scalar_prefetch=2,
            grid=(F // tf, nb),
            in_specs=[pl.BlockSpec((MOE_TM * TILE_ROWS, LANES),
                                   lambda j, i, be, nu: (jnp.minimum(i, nu[0] - 1), 0)),
                      wspec(D, tf, 0), wspec(D, tf, lin0), wspec(1, tf, 0), wspec(1, tf, lin0)],
            out_specs=pl.BlockSpec((MOE_TM, tf), lambda j, i, be, nu: (i, j)),
            scratch_shapes=[pltpu.VMEM((D, tf), bf16), pltpu.VMEM((D, tf), bf16)]),
        out_shape=jax.ShapeDtypeStruct((ns, F), bf16),
        compiler_params=_cparams(2),
        name="moe_up",
    )(block_e, n_used, xs, w1, w1, b1, b1)

    return pl.pallas_call(
        _moe_down_kernel,
        grid_spec=pltpu.PrefetchScalarGridSpec(
            num_scalar_prefetch=2,
            grid=(nb,),
            in_specs=[pl.BlockSpec((MOE_TM, F), lambda i, be, nu: (jnp.minimum(i, nu[0] - 1), 0)),
                      pl.BlockSpec((1, 1, F, D), lambda i, be, nu: (layer, be[i], 0, 0)),
                      pl.BlockSpec((1, 1, 1, D), lambda i, be, nu: (layer, be[i], 0, 0))],
            out_specs=pl.BlockSpec((MOE_TM * TILE_ROWS, LANES), lambda i, be, nu: (i, 0)),
            scratch_shapes=[pltpu.VMEM((F, D), bf16)]),
        out_shape=jax.ShapeDtypeStruct((ns * TILE_ROWS, LANES), jnp.uint32),
        compiler_params=_cparams(1),
        name="moe_down",
    )(block_e, n_used, act, w2, b2)


GATHER_ROWS = 512


def _gather_kernel(idx_ref, src_ref, out_ref, sem):
    i = pl.program_id(0)
    n = GATHER_ROWS
    t = TILE_ROWS

    def block_copy(blk):
        return pltpu.make_async_copy(src_ref.at[pl.ds(0, n * t)], out_ref.at[pl.ds(blk * (n * t), n * t)],
                                     sem.at[blk % 2])

    def issue(r, carry):
        src_row = pl.multiple_of(idx_ref[0, 0, r] * t, t)
        dst_row = pl.multiple_of((i * n + r) * t, t)
        pltpu.make_async_copy(src_ref.at[pl.ds(src_row, t)], out_ref.at[pl.ds(dst_row, t)],
                              sem.at[i % 2]).start()
        return carry

    lax.fori_loop(0, n, issue, 0, unroll=8)

    @pl.when(i > 0)
    def _():
        block_copy(i - 1).wait()

    @pl.when(i == pl.num_programs(0) - 1)
    def _():
        block_copy(i).wait()


def _gather_rows(src, idx):
    n = idx.shape[0]
    nblk = n // GATHER_ROWS
    return pl.pallas_call(
        _gather_kernel,
        grid=(nblk,),
        in_specs=[pl.BlockSpec((1, 1, GATHER_ROWS), lambda i: (i, 0, 0), memory_space=pltpu.SMEM),
                  pl.BlockSpec(memory_space=pl.ANY)],
        out_specs=pl.BlockSpec(memory_space=pl.ANY),
        out_shape=jax.ShapeDtypeStruct((n * TILE_ROWS, LANES), src.dtype),
        scratch_shapes=[pltpu.SemaphoreType.DMA((2,))],
        compiler_params=pltpu.CompilerParams(dimension_semantics=("arbitrary",),
                                             disable_bounds_checks=True),
        name="gather_rows",
    )(idx.reshape(nblk, 1, GATHER_ROWS), src)


def _route(logits):
    t = logits.shape[0]
    n_assign = t * TOP_K
    nb = -(-n_assign // MOE_TM) + NE
    top_v, top_i = lax.top_k(logits, TOP_K)
    gate = jax.nn.softmax(top_v, axis=-1)
    flat_e = top_i.reshape(-1)
    onehot = (flat_e[:, None] == jnp.arange(NE, dtype=flat_e.dtype)[None, :]).astype(jnp.int32)
    csum = jnp.cumsum(onehot, axis=0)
    counts = csum[-1]
    rank = jnp.take_along_axis(csum, flat_e[:, None], axis=1)[:, 0] - 1
    pcounts = (counts + MOE_TM - 1) // MOE_TM * MOE_TM
    pends = jnp.cumsum(pcounts)
    pstarts = pends - pcounts
    dest = (pstarts[flat_e] + rank).astype(jnp.int32)
    slot_tok = jnp.zeros((nb * MOE_TM,), jnp.int32).at[dest].set(
        jnp.arange(n_assign, dtype=jnp.int32) // TOP_K)
    n_used = (pends[-1] // MOE_TM).astype(jnp.int32)
    blk = jnp.arange(nb, dtype=jnp.int32)
    block_e = jnp.clip(jnp.searchsorted(pends, blk * MOE_TM, side="right"), 0, NE - 1).astype(jnp.int32)
    block_e = jnp.where(blk < n_used, block_e, block_e[n_used - 1])
    return gate, dest, slot_tok, block_e, n_used.reshape(1)


def _combine_kernel(hs_ref, y_ref, gate_ref, m_ref, g_ref, b_ref, *rest, with_next):
    if with_next:
        mn_ref, hs2_ref, nxt_ref = rest
    else:
        (hs2_ref,) = rest
    m = m_ref[0]
    gate = gate_ref[...]
    y = gate[:, 0:1] * _unpack_halves(_load_tile_rows(y_ref.at[0], ROW_TILE))
    for k in range(1, TOP_K):
        y = y + gate[:, k:k + 1] * _unpack_halves(_load_tile_rows(y_ref.at[k], ROW_TILE))
    hs2 = _ln_rows(ALPHA * hs_ref[...] + m[5:6] * y, g_ref[...], b_ref[...])
    hs2_ref[...] = hs2
    if with_next:
        mn = mn_ref[0]
        nxt_ref[...] = (hs2 * (1.0 + mn[1:2]) + mn[0:1]).astype(nxt_ref.dtype)


def _combine(hs1, yg, gate, mt, group, ln_g, ln_b, mt_next=None):
    m = hs1.shape[0]
    row = pl.BlockSpec((ROW_TILE, D), lambda i: (i, 0))
    vec = pl.BlockSpec((1, D), lambda i: (0, 0))
    mod = pl.BlockSpec((1, 6, D), lambda i: (group(i), 0, 0))
    in_specs = [row, pl.BlockSpec((TOP_K, ROW_TILE * TILE_ROWS, LANES), lambda i: (0, i, 0)),
                pl.BlockSpec((ROW_TILE, TOP_K), lambda i: (i, 0)), mod, vec, vec]
    args = [hs1, yg, gate, mt, ln_g.reshape(1, D), ln_b.reshape(1, D)]
    out_specs = [row]
    out_shape = [jax.ShapeDtypeStruct((m, D), f32)]
    with_next = mt_next is not None
    if with_next:
        in_specs.append(mod)
        args.append(mt_next)
        out_specs.append(row)
        out_shape.append(jax.ShapeDtypeStruct((m, D), bf16))
    return pl.pallas_call(
        functools.partial(_combine_kernel, with_next=with_next),
        grid=(m // ROW_TILE,),
        in_specs=in_specs, out_specs=out_specs, out_shape=out_shape,
        compiler_params=_cparams(1),
        name="moe_combine_ln",
    )(*args)


def _moe_layer(layer, hs, o, mt, group, ln1_g, ln1_b, ln2_g, ln2_b, r_w, r_b, w1, b1, w2, b2, mt_next=None):
    t = hs.shape[0]
    hs1, tok, logits = _ln_router(hs, o, mt, group, ln1_g[layer], ln1_b[layer], r_w[layer], r_b[layer])
    gate, dest, slot_tok, block_e, n_used = _route(logits)
    xs = _gather_rows(tok, slot_tok)
    y = _moe_experts(layer, xs, block_e, n_used, w1, b1, w2, b2)
    yg = _gather_rows(y, dest.reshape(t, TOP_K).T.reshape(-1)).reshape(TOP_K, t * TILE_ROWS, LANES)
    return _combine(hs1, yg, gate, mt, group, ln2_g[layer], ln2_b[layer], mt_next)


def _sgu_kernel(g_ref, v_ref, lg_ref, lb_ref, ws_ref, bs_ref, o_ref):
    v = v_ref[...].astype(f32)
    vn = _ln_rows(v, lg_ref[...], lb_ref[...]).astype(bf16)
    gw = E // SGU_GROUPS
    for g in range(SGU_GROUPS):
        mixed = _dot(ws_ref[g].astype(bf16), vn[:, g * gw:(g + 1) * gw]) + bs_ref[:, g:g + 1]
        o_ref[:, g * gw:(g + 1) * gw] = (g_ref[:, g * gw:(g + 1) * gw].astype(f32) * mixed).astype(o_ref.dtype)


def _sgu(uv, ln_g, ln_b, w_s, b_s):
    m = uv.shape[0]
    c = SGU_CHUNK
    vec = pl.BlockSpec((1, E), lambda i: (0, 0))
    return pl.pallas_call(
        _sgu_kernel,
        grid=(m // c,),
        in_specs=[pl.BlockSpec((c, E), lambda i: (i, 0)),
                  pl.BlockSpec((c, E), lambda i: (i, 1)),
                  vec, vec,
                  pl.BlockSpec((SGU_GROUPS, c, c), lambda i: (0, 0, 0)),
                  pl.BlockSpec((c, SGU_GROUPS), lambda i: (0, 0))],
        out_specs=pl.BlockSpec((c, E), lambda i: (i, 0)),
        out_shape=jax.ShapeDtypeStruct((m, E), bf16),
        compiler_params=_cparams(1),
        name="sgu",
    )(uv, uv, ln_g.reshape(1, E), ln_b.reshape(1, E), w_s, b_s.T)


def kernel(x, c, ctx, c_ctx, mod_w, mod_b, ln1_g, ln1_b, ln2_g, ln2_b, a_w_in, a_conv_w, a_conv_b, a_w_q, a_w_k, a_w_v, a_w_gate, a_b_gate, a_norm_w, a_skip, a_w_out, b_w_in, b_ln_g, b_ln_b, b_w_s, b_b_s, b_w_out, r_w, r_b, e_w1, e_b1, e_w2, e_b2):
    cvec = jnp.concatenate([c, c_ctx[None, :], jnp.zeros((8 - BATCH - 1, D), f32)], axis=0)
    mods = _mod_rows(cvec, mod_w, mod_b).reshape(DEPTH, 8, 6, D)
    rows = [r for b in range(BATCH) for r in (BATCH, b)]
    mt = [jnp.stack([mods[l, r] for r in rows], axis=0) for l in range(DEPTH)]

    hs = jnp.concatenate([ctx, x], axis=1).reshape(BATCH * U, D)
    hx = _modulate(hs, mt[0], _group_joint)
    xz = _matmul(hx, a_w_in[0], tm=1024, tn=1024, name="mlstm_in_proj")
    xc, q, k, v, gates = _conv_qkv(xz, a_conv_w[0], a_conv_b[0], a_w_q[0], a_w_k[0], a_w_v[0],
                                   a_w_gate[0], a_b_gate[0])
    gates_t = gates.transpose(0, 2, 1)
    h_fwd = _mlstm(q, k, v, gates, gates_t, 0)
    pre = _mlstm(q, k, v, gates, gates_t, 1, (h_fwd, xc, xz, a_norm_w[0], a_skip[0]))
    o = _matmul(pre.reshape(BATCH * U, E), a_w_out[0], tm=1024, tn=512, name="mlstm_out_proj")
    hs2, hx1 = _moe_layer(0, hs, o, mt[0], _group_joint, ln1_g, ln1_b, ln2_g, ln2_b,
                          r_w, r_b, e_w1, e_b1, e_w2, e_b2, mt_next=mt[1])

    hs = hs2.reshape(BATCH, U, D)[:, LC:].reshape(BATCH * SEQ, D)
    hx = hx1.reshape(BATCH, U, D)[:, LC:].reshape(BATCH * SEQ, D)
    uv = _matmul(hx, b_w_in[0], tm=1024, tn=1024, act="gelu", name="sgu_in_proj")
    gated = _sgu(uv, b_ln_g[0], b_ln_b[0], b_w_s[0], b_b_s[0])
    o = _matmul(gated, b_w_out[0], tm=1024, tn=512, name="sgu_out_proj")
    (out,) = _moe_layer(1, hs, o, mt[1], _group_latent, ln1_g, ln1_b, ln2_g, ln2_b,
                        r_w, r_b, e_w1, e_b1, e_w2, e_b2)
    return out.reshape(BATCH, SEQ, D)
```

```python
import functools

import jax
import jax.numpy as jnp
from jax import lax
from jax.experimental import pallas as pl
from jax.experimental.pallas import tpu as pltpu

D = 2048
BATCH = 4
SEQ = 2048
DEPTH = 2
GRID_W = 64
LC = 256
U = LC + SEQ
E = 2 * D
NH = 4
DH = E // NH
QKV_BLOCK = 4
SGU_GROUPS = 8
SGU_CHUNK = 128
NE = 32
TOP_K = 4
F = D
SWIGLU_LIMIT = 7.0
SWIGLU_ALPHA = 1.702
ALPHA = (2 * DEPTH) ** 0.25
LN_EPS = 1e-5

ROW_TILE = 256
MLSTM_CHUNK = 256
MOE_TM = 512
BD_TILE = 256
VMEM_LIMIT = 56 * 1024 * 1024

f32 = jnp.float32
bf16 = jnp.bfloat16


def _cparams(n_axes):
    return pltpu.CompilerParams(dimension_semantics=("arbitrary",) * n_axes,
                                vmem_limit_bytes=VMEM_LIMIT)


def _dot(a, b):
    return jnp.dot(a, b, preferred_element_type=f32)


def _sigmoid(x):
    return 1.0 / (1.0 + jnp.exp(-x))


def _pack_halves(x):
    w = x.shape[1] // 2
    bits = lax.bitcast_convert_type(x.astype(bf16).astype(f32), jnp.uint32)
    return (bits[:, w:] & jnp.uint32(0xFFFF0000)) | (bits[:, :w] >> 16)


def _unpack_halves(p):
    lo = lax.bitcast_convert_type(p << 16, f32)
    hi = lax.bitcast_convert_type(p & jnp.uint32(0xFFFF0000), f32)
    return jnp.concatenate([lo, hi], axis=1)


PACKED = D // 2
LANES = 128
TILE_ROWS = PACKED // LANES


def _store_tile_rows(ref, p):
    r = p.shape[0]
    for s in range(TILE_ROWS):
        ref[pl.ds(s, r, stride=TILE_ROWS), :] = p[:, s * LANES:(s + 1) * LANES]


def _load_tile_rows(ref, r):
    return jnp.concatenate([ref[pl.ds(s, r, stride=TILE_ROWS), :] for s in range(TILE_ROWS)], axis=1)


def _mod_kernel(c_ref, w_ref, b_ref, o_ref):
    c = c_ref[...]
    a = (c * _sigmoid(c)).astype(bf16)
    o_ref[0] = _dot(a, w_ref[0].astype(bf16)) + b_ref[0]


def _mod_rows(cvec, mod_w, mod_b):
    tn = 1024
    return pl.pallas_call(
        _mod_kernel,
        grid=(DEPTH, 6 * D // tn),
        in_specs=[pl.BlockSpec((8, D), lambda l, j: (0, 0)),
                  pl.BlockSpec((1, D, tn), lambda l, j: (l, 0, j)),
                  pl.BlockSpec((1, 1, tn), lambda l, j: (l, 0, j))],
        out_specs=pl.BlockSpec((1, 8, tn), lambda l, j: (l, 0, j)),
        out_shape=jax.ShapeDtypeStruct((DEPTH, 8, 6 * D), f32),
        compiler_params=_cparams(2),
        name="adaln_rows",
    )(cvec, mod_w, mod_b.reshape(DEPTH, 1, 6 * D))


def _group_joint(i):
    tiles = U // ROW_TILE
    return 2 * (i // tiles) + jnp.minimum(i % tiles, 1)


def _group_latent(i):
    return 2 * (i // (SEQ // ROW_TILE)) + 1


def _modulate_kernel(x_ref, m_ref, o_ref):
    m = m_ref[0]
    o_ref[...] = (x_ref[...] * (1.0 + m[1:2]) + m[0:1]).astype(o_ref.dtype)


def _modulate(x2d, mt, group):
    m = x2d.shape[0]
    return pl.pallas_call(
        _modulate_kernel,
        grid=(m // ROW_TILE,),
        in_specs=[pl.BlockSpec((ROW_TILE, D), lambda i: (i, 0)),
                  pl.BlockSpec((1, 6, D), lambda i: (group(i), 0, 0))],
        out_specs=pl.BlockSpec((ROW_TILE, D), lambda i: (i, 0)),
        out_shape=jax.ShapeDtypeStruct((m, D), bf16),
        compiler_params=_cparams(1),
        name="modulate",
    )(x2d, mt)


def _erf(x):
    return lax.erf(x)


def _mm_kernel(x_ref, w_ref, o_ref, wb_ref, *, act):
    @pl.when(pl.program_id(1) == 0)
    def _():
        wb_ref[...] = w_ref[...].astype(bf16)

    acc = _dot(x_ref[...], wb_ref[...])
    if act == "gelu":
        acc = 0.5 * acc * (1.0 + _erf(acc * (2.0 ** -0.5)))
    o_ref[...] = acc.astype(o_ref.dtype)


def _matmul(x, w, *, tm, tn, act=None, name):
    m, k = x.shape
    n = w.shape[1]
    return pl.pallas_call(
        functools.partial(_mm_kernel, act=act),
        grid=(n // tn, m // tm),
        in_specs=[pl.BlockSpec((tm, k), lambda j, i: (i, 0)),
                  pl.BlockSpec((k, tn), lambda j, i: (0, j))],
        out_specs=pl.BlockSpec((tm, tn), lambda j, i: (i, j)),
        out_shape=jax.ShapeDtypeStruct((m, n), bf16),
        scratch_shapes=[pltpu.VMEM((k, tn), bf16)],
        compiler_params=_cparams(2),
        name=name,
    )(x, w)


def _conv_qkv_kernel(xm_ref, cw_ref, cb_ref, wq_ref, wk_ref, wv_ref, wg_ref, bg_ref,
                     xc_ref, q_ref, k_ref, v_ref, g_ref):
    ct = pl.program_id(1)
    a = xm_ref[0].astype(f32)
    c = a.shape[1]
    r = lax.broadcasted_iota(jnp.int32, (U, c), 0)
    latent = r >= LC
    p = r - LC
    col = jnp.where(latent, p & (GRID_W - 1), r)
    last = jnp.where(latent, GRID_W - 1, LC - 1)
    a_l = jnp.where(col > 0, pltpu.roll(a, 1, 0), 0.0)
    a_r = jnp.where(col < last, pltpu.roll(a, U - 1, 0), 0.0)
    w = cw_ref[...]
    rows = [w[3 * i:3 * i + 1] * a_l + w[3 * i + 1:3 * i + 2] * a + w[3 * i + 2:3 * i + 3] * a_r
            for i in range(3)]
    up = jnp.where(p >= GRID_W, pltpu.roll(rows[0], GRID_W, 0), 0.0)
    down = jnp.where(latent & (p < SEQ - GRID_W), pltpu.roll(rows[2], U - GRID_W, 0), 0.0)
    pre = rows[1] + up + down + cb_ref[...]
    xc = (pre * _sigmoid(pre)).astype(bf16)
    xc_ref[0] = xc
    xm = xm_ref[0]
    q = _dot(xc, wq_ref[0]).astype(bf16)
    k = _dot(xc, wk_ref[0]).astype(bf16)
    v = _dot(xm, wv_ref[0]).astype(bf16)
    q_ref[0] = q
    k_ref[0] = k
    v_ref[0] = v
    g = (_dot(q, wg_ref[0].astype(bf16)) + _dot(k, wg_ref[1].astype(bf16))
         + _dot(v, wg_ref[2].astype(bf16)))

    @pl.when(ct == 0)
    def _():
        g_ref[0] = g + bg_ref[...]

    @pl.when(ct > 0)
    def _():
        g_ref[0] += g


def _block_diag(w):
    per = BD_TILE // QKV_BLOCK
    wt = w.reshape(E // BD_TILE, per, QKV_BLOCK, QKV_BLOCK)
    eye = jnp.eye(per, dtype=w.dtype)
    full = jnp.einsum("tgio,gh->tgiho", wt, eye)
    return full.reshape(E // BD_TILE, BD_TILE, BD_TILE).astype(bf16)


def _conv_qkv(xz, conv_w, conv_b, w_q, w_k, w_v, w_gate, b_gate):
    c = BD_TILE
    n_gate = 4 * NH
    act = jax.ShapeDtypeStruct((BATCH, U, E), bf16)
    tile = pl.BlockSpec((1, U, c), lambda b, t: (b, 0, t))
    bd = pl.BlockSpec((1, c, c), lambda b, t: (t, 0, 0))
    return pl.pallas_call(
        _conv_qkv_kernel,
        grid=(BATCH, E // c),
        in_specs=[tile,
                  pl.BlockSpec((9, c), lambda b, t: (0, t)),
                  pl.BlockSpec((1, c), lambda b, t: (0, t)),
                  bd, bd, bd,
                  pl.BlockSpec((3, c, n_gate), lambda b, t: (0, t, 0)),
                  pl.BlockSpec((1, n_gate), lambda b, t: (0, 0))],
        out_specs=[tile, tile, tile, tile,
                   pl.BlockSpec((1, U, n_gate), lambda b, t: (b, 0, 0))],
        out_shape=[act, act, act, act, jax.ShapeDtypeStruct((BATCH, U, n_gate), f32)],
        compiler_params=_cparams(2),
        name="conv_qkv_gates",
    )(xz.reshape(BATCH, U, 2 * E), conv_w.reshape(9, E), conv_b.reshape(1, E),
      _block_diag(w_q), _block_diag(w_k), _block_diag(w_v),
      w_gate.reshape(3, E, n_gate), b_gate.reshape(1, n_gate))


def _log_sigmoid(x):
    return jnp.minimum(x, 0.0) - jnp.log1p(jnp.exp(-jnp.abs(x)))


def _mlstm_kernel(*refs, direction, finish):
    if finish:
        (q_ref, k_ref, v_ref, gc_ref, gr_ref, hf_ref, xc_ref, z_ref, nw_ref, sk_ref,
         o_ref, ct_ref, ctb_ref, n_ref, m_ref) = refs
    else:
        q_ref, k_ref, v_ref, gc_ref, gr_ref, o_ref, ct_ref, ctb_ref, n_ref, m_ref = refs
    L = MLSTM_CHUNK
    h = pl.program_id(1)
    step = pl.program_id(2)

    @pl.when(step == 0)
    def _():
        ct_ref[...] = jnp.zeros_like(ct_ref)
        ctb_ref[...] = jnp.zeros_like(ctb_ref)
        n_ref[...] = jnp.zeros_like(n_ref)
        m_ref[...] = jnp.full_like(m_ref, -jnp.inf)

    i_idx = 2 * direction * NH + h
    f_idx = (2 * direction + 1) * NH + h
    gc = gc_ref[0]
    gr = gr_ref[0]
    lane = lax.broadcasted_iota(jnp.int32, gc.shape, 1)
    sub = lax.broadcasted_iota(jnp.int32, gr.shape, 0)
    ig_col = jnp.sum(jnp.where(lane == i_idx, gc, 0.0), axis=1, keepdims=True)
    f_col = jnp.sum(jnp.where(lane == f_idx, gc, 0.0), axis=1, keepdims=True)
    ig_row = jnp.sum(jnp.where(sub == i_idx, gr, 0.0), axis=0, keepdims=True)
    f_row = jnp.sum(jnp.where(sub == f_idx, gr, 0.0), axis=0, keepdims=True)
    lf_col = _log_sigmoid(f_col)
    lf_row = _log_sigmoid(f_row)

    t_i = lax.broadcasted_iota(jnp.int32, (L, L), 0)
    s_i = lax.broadcasted_iota(jnp.int32, (L, L), 1)
    seen = (s_i <= t_i) if direction == 0 else (s_i >= t_i)
    b_col = jnp.sum(jnp.where(seen, lf_row, 0.0), axis=1, keepdims=True)
    seen_t = (t_i <= s_i) if direction == 0 else (t_i >= s_i)
    b_row = jnp.sum(jnp.where(seen_t, lf_col, 0.0), axis=0, keepdims=True)
    b_end = jnp.sum(lf_row, axis=1, keepdims=True)

    m_prev = m_ref[:, 0:1]
    log_d = jnp.where(seen, b_col - b_row + ig_row, -jnp.inf)
    g_col = b_col + m_prev
    m_t = jnp.maximum(g_col, jnp.max(log_d, axis=1, keepdims=True))
    dw = jnp.exp(log_d - m_t)
    inter = jnp.exp(g_col - m_t)

    q = q_ref[0]
    k = k_ref[0] * (DH ** -0.5)
    v = v_ref[0]
    s = lax.dot_general(q, k, (((1,), (1,)), ((), ())), preferred_element_type=f32) * dw
    num = _dot(s.astype(bf16), v) + inter * _dot(q, ctb_ref[...])
    qn = jnp.sum(q.astype(f32) * n_ref[...], axis=1, keepdims=True)
    den = jnp.sum(s, axis=1, keepdims=True) + inter * qn
    hout = num / jnp.maximum(jnp.abs(den), jnp.exp(-m_t))

    w_end = b_end - b_col + ig_col
    m_new = jnp.maximum(b_end + m_prev, jnp.max(w_end, axis=0, keepdims=True))
    decay = jnp.exp(b_end + m_prev - m_new)
    kw = k.astype(f32) * jnp.exp(w_end - m_new)
    upd = lax.dot_general(kw.astype(bf16), v, (((0,), (0,)), ((), ())), preferred_element_type=f32)
    c_new = decay * ct_ref[...] + upd
    ct_ref[...] = c_new
    ctb_ref[...] = c_new.astype(bf16)
    n_ref[...] = decay * n_ref[...] + jnp.sum(kw, axis=0, keepdims=True)
    m_ref[...] = jnp.broadcast_to(m_new, m_ref.shape)

    if not finish:
        o_ref[0] = hout.astype(o_ref.dtype)
    else:
        hs = hout + hf_ref[0].astype(f32)
        mu = jnp.mean(hs, axis=1, keepdims=True)
        xc_ = hs - mu
        var = jnp.mean(xc_ * xc_, axis=1, keepdims=True)
        hn = xc_ * lax.rsqrt(var + LN_EPS)
        z = z_ref[0].astype(f32)
        o_ref[0] = ((hn * nw_ref[...] + sk_ref[...] * xc_ref[0].astype(f32))
                    * (z * _sigmoid(z))).astype(o_ref.dtype)


def _mlstm(q, k, v, gates, gates_t, direction, finish_args=None):
    L = MLSTM_CHUNK
    n_chunks = U // L

    if direction == 0:
        def chunk(s):
            return s
    else:
        def chunk(s):
            return jnp.where(s == 0, 0, n_chunks - s)

    tile = pl.BlockSpec((1, L, DH), lambda b, h, s: (b, chunk(s), h))
    in_specs = [tile, tile, tile,
                pl.BlockSpec((1, L, 4 * NH), lambda b, h, s: (b, chunk(s), 0)),
                pl.BlockSpec((1, 4 * NH, L), lambda b, h, s: (b, 0, chunk(s)))]
    args = [q, k, v, gates, gates_t]
    finish = finish_args is not None
    if finish:
        h_fwd, xc, xz, norm_w, skip = finish_args
        vec = pl.BlockSpec((1, DH), lambda b, h, s: (0, h))
        in_specs += [tile, tile,
                     pl.BlockSpec((1, L, DH), lambda b, h, s: (b, chunk(s), NH + h)),
                     vec, vec]
        args += [h_fwd, xc, xz.reshape(BATCH, U, 2 * E), norm_w.reshape(1, E), skip.reshape(1, E)]
    return pl.pallas_call(
        functools.partial(_mlstm_kernel, direction=direction, finish=finish),
        grid=(BATCH, NH, n_chunks),
        in_specs=in_specs,
        out_specs=tile,
        out_shape=jax.ShapeDtypeStruct((BATCH, U, E), bf16),
        scratch_shapes=[pltpu.VMEM((DH, DH), f32), pltpu.VMEM((DH, DH), bf16),
                        pltpu.VMEM((1, DH), f32), pltpu.VMEM((1, 128), f32)],
        compiler_params=_cparams(3),
        name="mlstm_bwd_finish" if finish else "mlstm_fwd",
    )(*args)


def _ln_rows(r, g, b):
    mu = jnp.mean(r, axis=1, keepdims=True)
    rc = r - mu
    var = jnp.mean(rc * rc, axis=1, keepdims=True)
    return rc * lax.rsqrt(var + LN_EPS) * g + b


def _ln_router_kernel(hs_ref, o_ref, m_ref, g_ref, b_ref, rw_ref, rb_ref,
                      hs1_ref, tok_ref, lg_ref):
    m = m_ref[0]
    r = ALPHA * hs_ref[...] + m[2:3] * o_ref[...].astype(f32)
    hs1 = _ln_rows(r, g_ref[...], b_ref[...])
    hs1_ref[...] = hs1
    tok = hs1 * (1.0 + m[4:5]) + m[3:4]
    t_hi = tok.astype(bf16)
    _store_tile_rows(tok_ref, _pack_halves(tok))
    t_lo = (tok - t_hi.astype(f32)).astype(bf16)
    w = rw_ref[...]
    w_hi = w.astype(bf16)
    w_lo = (w - w_hi.astype(f32)).astype(bf16)
    lg_ref[...] = _dot(t_hi, w_hi) + _dot(t_hi, w_lo) + _dot(t_lo, w_hi) + rb_ref[...]


def _ln_router(hs, o, mt, group, ln_g, ln_b, r_w, r_b):
    m = hs.shape[0]
    row = pl.BlockSpec((ROW_TILE, D), lambda i: (i, 0))
    vec = pl.BlockSpec((1, D), lambda i: (0, 0))
    return pl.pallas_call(
        _ln_router_kernel,
        grid=(m // ROW_TILE,),
        in_specs=[row, row,
                  pl.BlockSpec((1, 6, D), lambda i: (group(i), 0, 0)),
                  vec, vec,
                  pl.BlockSpec((D, NE), lambda i: (0, 0)),
                  pl.BlockSpec((1, NE), lambda i: (0, 0))],
        out_specs=[row, pl.BlockSpec((ROW_TILE * TILE_ROWS, LANES), lambda i: (i, 0)),
                   pl.BlockSpec((ROW_TILE, NE), lambda i: (i, 0))],
        out_shape=[jax.ShapeDtypeStruct((m, D), f32),
                   jax.ShapeDtypeStruct((m * TILE_ROWS, LANES), jnp.uint32),
                   jax.ShapeDtypeStruct((m, NE), f32)],
        compiler_params=_cparams(1),
        name="ln_router",
    )(hs, o, mt, ln_g.reshape(1, D), ln_b.reshape(1, D), r_w, r_b.reshape(1, NE))


def _first_of_expert(be_ref, i):
    return jnp.logical_or(i == 0, be_ref[i] != be_ref[jnp.maximum(i - 1, 0)])


def _moe_up_kernel(be_ref, nu_ref, x_ref, wg_ref, wl_ref, bg_ref, bl_ref, o_ref, wgb_ref, wlb_ref):
    i = pl.program_id(1)
    valid = i < nu_ref[0]

    @pl.when(jnp.logical_and(valid, _first_of_expert(be_ref, i)))
    def _():
        wgb_ref[...] = wg_ref[0, 0].astype(bf16)
        wlb_ref[...] = wl_ref[0, 0].astype(bf16)

    @pl.when(valid)
    def _():
        x = _unpack_halves(_load_tile_rows(x_ref, MOE_TM)).astype(bf16)
        glu = jnp.minimum(_dot(x, wgb_ref[...]) + bg_ref[0, 0], SWIGLU_LIMIT)
        lin = jnp.clip(_dot(x, wlb_ref[...]) + bl_ref[0, 0], -SWIGLU_LIMIT, SWIGLU_LIMIT)
        o_ref[...] = (glu * _sigmoid(SWIGLU_ALPHA * glu) * (lin + 1.0)).astype(o_ref.dtype)

    @pl.when(jnp.logical_not(valid))
    def _():
        o_ref[...] = jnp.zeros_like(o_ref)


def _moe_down_kernel(be_ref, nu_ref, a_ref, w_ref, b_ref, o_ref, wb_ref):
    i = pl.program_id(0)
    valid = i < nu_ref[0]

    @pl.when(jnp.logical_and(valid, _first_of_expert(be_ref, i)))
    def _():
        wb_ref[...] = w_ref[0, 0].astype(bf16)

    @pl.when(valid)
    def _():
        y = _dot(a_ref[...], wb_ref[...]) + b_ref[0, 0]
        _store_tile_rows(o_ref, _pack_halves(y))

    @pl.when(jnp.logical_not(valid))
    def _():
        o_ref[...] = jnp.zeros_like(o_ref)


def _moe_experts(layer, xs, block_e, n_used, w1, b1, w2, b2):
    ns = xs.shape[0] // TILE_ROWS
    nb = ns // MOE_TM
    tf = 512
    lin0 = F // tf
    b1 = b1.reshape(DEPTH, NE, 1, 2 * F)
    b2 = b2.reshape(DEPTH, NE, 1, D)

    def wspec(rows, width, col0):
        return pl.BlockSpec((1, 1, rows, width), lambda j, i, be, nu: (layer, be[i], 0, col0 + j))

    act = pl.pallas_call(
        _moe_up_kernel,
        grid_spec=pltpu.PrefetchScalarGridSpec(
            num_scalar_prefetch=2,
            grid=(F // tf, nb),
            in_specs=[pl.BlockSpec((MOE_TM * TILE_ROWS, LANES),
                                   lambda j, i, be, nu: (jnp.minimum(i, nu[0] - 1), 0)),
                      wspec(D, tf, 0), wspec(D, tf, lin0), wspec(1, tf, 0), wspec(1, tf, lin0)],
            out_specs=pl.BlockSpec((MOE_TM, tf), lambda j, i, be, nu: (i, j)),
            scratch_shapes=[pltpu.VMEM((D, tf), bf16), pltpu.VMEM((D, tf), bf16)]),
        out_shape=jax.ShapeDtypeStruct((ns, F), bf16),
        compiler_params=_cparams(2),
        name="moe_up",
    )(block_e, n_used, xs, w1, w1, b1, b1)

    return pl.pallas_call(
        _moe_down_kernel,
        grid_spec=pltpu.PrefetchScalarGridSpec(
            num_scalar_prefetch=2,
            grid=(nb,),
            in_specs=[pl.BlockSpec((MOE_TM, F), lambda i, be, nu: (jnp.minimum(i, nu[0] - 1), 0)),
                      pl.BlockSpec((1, 1, F, D), lambda i, be, nu: (layer, be[i], 0, 0)),
                      pl.BlockSpec((1, 1, 1, D), lambda i, be, nu: (layer, be[i], 0, 0))],
            out_specs=pl.BlockSpec((MOE_TM * TILE_ROWS, LANES), lambda i, be, nu: (i, 0)),
            scratch_shapes=[pltpu.VMEM((F, D), bf16)]),
        out_shape=jax.ShapeDtypeStruct((ns * TILE_ROWS, LANES), jnp.uint32),
        compiler_params=_cparams(1),
        name="moe_down",
    )(block_e, n_used, act, w2, b2)


GATHER_ROWS = 2048


def _gather_kernel(idx_ref, src_ref, out_ref, sem):
    n = GATHER_ROWS
    t = TILE_ROWS

    def issue(r, carry):
        src_row = pl.multiple_of(idx_ref[0, 0, r] * t, t)
        dst_row = pl.multiple_of(r * t, t)
        pltpu.make_async_copy(src_ref.at[pl.ds(src_row, t)], out_ref.at[pl.ds(dst_row, t)], sem).start()
        return carry

    lax.fori_loop(0, n, issue, 0, unroll=8)
    pltpu.make_async_copy(src_ref.at[pl.ds(0, n * t)], out_ref, sem).wait()


def _gather_rows(src, idx):
    n = idx.shape[0]
    nblk = n // GATHER_ROWS
    return pl.pallas_call(
        _gather_kernel,
        grid=(nblk,),
        in_specs=[pl.BlockSpec((1, 1, GATHER_ROWS), lambda i: (i, 0, 0), memory_space=pltpu.SMEM),
                  pl.BlockSpec(memory_space=pl.ANY)],
        out_specs=pl.BlockSpec((GATHER_ROWS * TILE_ROWS, LANES), lambda i: (i, 0)),
        out_shape=jax.ShapeDtypeStruct((n * TILE_ROWS, LANES), src.dtype),
        scratch_shapes=[pltpu.SemaphoreType.DMA],
        compiler_params=pltpu.CompilerParams(dimension_semantics=("arbitrary",),
                                             vmem_limit_bytes=VMEM_LIMIT,
                                             disable_bounds_checks=True),
        name="gather_rows",
    )(idx.reshape(nblk, 1, GATHER_ROWS), src)


def _route(logits):
    t = logits.shape[0]
    n_assign = t * TOP_K
    nb = -(-n_assign // MOE_TM) + NE
    top_v, top_i = lax.top_k(logits, TOP_K)
    gate = jax.nn.softmax(top_v, axis=-1)
    flat_e = top_i.reshape(-1)
    onehot = (flat_e[:, None] == jnp.arange(NE, dtype=flat_e.dtype)[None, :]).astype(jnp.int32)
    csum = jnp.cumsum(onehot, axis=0)
    counts = csum[-1]
    rank = jnp.take_along_axis(csum, flat_e[:, None], axis=1)[:, 0] - 1
    pcounts = (counts + MOE_TM - 1) // MOE_TM * MOE_TM
    pends = jnp.cumsum(pcounts)
    pstarts = pends - pcounts
    dest = (pstarts[flat_e] + rank).astype(jnp.int32)
    slot_tok = jnp.zeros((nb * MOE_TM,), jnp.int32).at[dest].set(
        jnp.arange(n_assign, dtype=jnp.int32) // TOP_K)
    n_used = (pends[-1] // MOE_TM).astype(jnp.int32)
    blk = jnp.arange(nb, dtype=jnp.int32)
    block_e = jnp.clip(jnp.searchsorted(pends, blk * MOE_TM, side="right"), 0, NE - 1).astype(jnp.int32)
    block_e = jnp.where(blk < n_used, block_e, block_e[n_used - 1])
    return gate, dest, slot_tok, block_e, n_used.reshape(1)


def _combine_kernel(hs_ref, y_ref, gate_ref, m_ref, g_ref, b_ref, *rest, with_next):
    if with_next:
        mn_ref, hs2_ref, nxt_ref = rest
    else:
        (hs2_ref,) = rest
    m = m_ref[0]
    gate = gate_ref[...]
    y = gate[:, 0:1] * _unpack_halves(_load_tile_rows(y_ref.at[0], ROW_TILE))
    for k in range(1, TOP_K):
        y = y + gate[:, k:k + 1] * _unpack_halves(_load_tile_rows(y_ref.at[k], ROW_TILE))
    hs2 = _ln_rows(ALPHA * hs_ref[...] + m[5:6] * y, g_ref[...], b_ref[...])
    hs2_ref[...] = hs2
    if with_next:
        mn = mn_ref[0]
        nxt_ref[...] = (hs2 * (1.0 + mn[1:2]) + mn[0:1]).astype(nxt_ref.dtype)


def _combine(hs1, yg, gate, mt, group, ln_g, ln_b, mt_next=None):
    m = hs1.shape[0]
    row = pl.BlockSpec((ROW_TILE, D), lambda i: (i, 0))
    vec = pl.BlockSpec((1, D), lambda i: (0, 0))
    mod = pl.BlockSpec((1, 6, D), lambda i: (group(i), 0, 0))
    in_specs = [row, pl.BlockSpec((TOP_K, ROW_TILE * TILE_ROWS, LANES), lambda i: (0, i, 0)),
                pl.BlockSpec((ROW_TILE, TOP_K), lambda i: (i, 0)), mod, vec, vec]
    args = [hs1, yg, gate, mt, ln_g.reshape(1, D), ln_b.reshape(1, D)]
    out_specs = [row]
    out_shape = [jax.ShapeDtypeStruct((m, D), f32)]
    with_next = mt_next is not None
    if with_next:
        in_specs.append(mod)
        args.append(mt_next)
        out_specs.append(row)
        out_shape.append(jax.ShapeDtypeStruct((m, D), bf16))
    return pl.pallas_call(
        functools.partial(_combine_kernel, with_next=with_next),
        grid=(m // ROW_TILE,),
        in_specs=in_specs, out_specs=out_specs, out_shape=out_shape,
        compiler_params=_cparams(1),
        name="moe_combine_ln",
    )(*args)


def _moe_layer(layer, hs, o, mt, group, ln1_g, ln1_b, ln2_g, ln2_b, r_w, r_b, w1, b1, w2, b2, mt_next=None):
    t = hs.shape[0]
    hs1, tok, logits = _ln_router(hs, o, mt, group, ln1_g[layer], ln1_b[layer], r_w[layer], r_b[layer])
    gate, dest, slot_tok, block_e, n_used = _route(logits)
    xs = _gather_rows(tok, slot_tok)
    y = _moe_experts(layer, xs, block_e, n_used, w1, b1, w2, b2)
    yg = _gather_rows(y, dest.reshape(t, TOP_K).T.reshape(-1)).reshape(TOP_K, t * TILE_ROWS, LANES)
    return _combine(hs1, yg, gate, mt, group, ln2_g[layer], ln2_b[layer], mt_next)


def _sgu_kernel(g_ref, v_ref, lg_ref, lb_ref, ws_ref, bs_ref, o_ref):
    v = v_ref[...].astype(f32)
    vn = _ln_rows(v, lg_ref[...], lb_ref[...]).astype(bf16)
    gw = E // SGU_GROUPS
    for g in range(SGU_GROUPS):
        mixed = _dot(ws_ref[g].astype(bf16), vn[:, g * gw:(g + 1) * gw]) + bs_ref[:, g:g + 1]
        o_ref[:, g * gw:(g + 1) * gw] = (g_ref[:, g * gw:(g + 1) * gw].astype(f32) * mixed).astype(o_ref.dtype)


def _sgu(uv, ln_g, ln_b, w_s, b_s):
    m = uv.shape[0]
    c = SGU_CHUNK
    vec = pl.BlockSpec((1, E), lambda i: (0, 0))
    return pl.pallas_call(
        _sgu_kernel,
        grid=(m // c,),
        in_specs=[pl.BlockSpec((c, E), lambda i: (i, 0)),
                  pl.BlockSpec((c, E), lambda i: (i, 1)),
                  vec, vec,
                  pl.BlockSpec((SGU_GROUPS, c, c), lambda i: (0, 0, 0)),
                  pl.BlockSpec((c, SGU_GROUPS), lambda i: (0, 0))],
        out_specs=pl.BlockSpec((c, E), lambda i: (i, 0)),
        out_shape=jax.ShapeDtypeStruct((m, E), bf16),
        compiler_params=_cparams(1),
        name="sgu",
    )(uv, uv, ln_g.reshape(1, E), ln_b.reshape(1, E), w_s, b_s.T)


def kernel(x, c, ctx, c_ctx, mod_w, mod_b, ln1_g, ln1_b, ln2_g, ln2_b, a_w_in, a_conv_w, a_conv_b, a_w_q, a_w_k, a_w_v, a_w_gate, a_b_gate, a_norm_w, a_skip, a_w_out, b_w_in, b_ln_g, b_ln_b, b_w_s, b_b_s, b_w_out, r_w, r_b, e_w1, e_b1, e_w2, e_b2):
    cvec = jnp.concatenate([c, c_ctx[None, :], jnp.zeros((8 - BATCH - 1, D), f32)], axis=0)
    mods = _mod_rows(cvec, mod_w, mod_b).reshape(DEPTH, 8, 6, D)
    rows = [r for b in range(BATCH) for r in (BATCH, b)]
    mt = [jnp.stack([mods[l, r] for r in rows], axis=0) for l in range(DEPTH)]

    hs = jnp.concatenate([ctx, x], axis=1).reshape(BATCH * U, D)
    hx = _modulate(hs, mt[0], _group_joint)
    xz = _matmul(hx, a_w_in[0], tm=1024, tn=1024, name="mlstm_in_proj")
    xc, q, k, v, gates = _conv_qkv(xz, a_conv_w[0], a_conv_b[0], a_w_q[0], a_w_k[0], a_w_v[0],
                                   a_w_gate[0], a_b_gate[0])
    gates_t = gates.transpose(0, 2, 1)
    h_fwd = _mlstm(q, k, v, gates, gates_t, 0)
    pre = _mlstm(q, k, v, gates, gates_t, 1, (h_fwd, xc, xz, a_norm_w[0], a_skip[0]))
    o = _matmul(pre.reshape(BATCH * U, E), a_w_out[0], tm=1024, tn=512, name="mlstm_out_proj")
    hs2, hx1 = _moe_layer(0, hs, o, mt[0], _group_joint, ln1_g, ln1_b, ln2_g, ln2_b,
                          r_w, r_b, e_w1, e_b1, e_w2, e_b2, mt_next=mt[1])

    hs = hs2.reshape(BATCH, U, D)[:, LC:].reshape(BATCH * SEQ, D)
    hx = hx1.reshape(BATCH, U, D)[:, LC:].reshape(BATCH * SEQ, D)
    uv = _matmul(hx, b_w_in[0], tm=1024, tn=1024, act="gelu", name="sgu_in_proj")
    gated = _sgu(uv, b_ln_g[0], b_ln_b[0], b_w_s[0], b_b_s[0])
    o = _matmul(gated, b_w_out[0], tm=1024, tn=512, name="sgu_out_proj")
    (out,) = _moe_layer(1, hs, o, mt[1], _group_latent, ln1_g, ln1_b, ln2_g, ln2_b,
                        r_w, r_b, e_w1, e_b1, e_w2, e_b2)
    return out.reshape(BATCH, SEQ, D)
```

```python
import functools

import jax
import jax.numpy as jnp
from jax import lax
from jax.experimental import pallas as pl
from jax.experimental.pallas import tpu as pltpu

D = 2048
BATCH = 4
SEQ = 2048
DEPTH = 2
GRID_W = 64
LC = 256
U = LC + SEQ
E = 2 * D
NH = 4
DH = E // NH
QKV_BLOCK = 4
SGU_GROUPS = 8
SGU_CHUNK = 128
NE = 32
TOP_K = 4
F = D
SWIGLU_LIMIT = 7.0
SWIGLU_ALPHA = 1.702
ALPHA = (2 * DEPTH) ** 0.25
LN_EPS = 1e-5

ROW_TILE = 256
MLSTM_CHUNK = 256
MOE_TM = 512
BD_TILE = 256
VMEM_LIMIT = 56 * 1024 * 1024

f32 = jnp.float32
bf16 = jnp.bfloat16


def _cparams(n_axes):
    return pltpu.CompilerParams(dimension_semantics=("arbitrary",) * n_axes,
                                vmem_limit_bytes=VMEM_LIMIT)


def _dot(a, b):
    return jnp.dot(a, b, preferred_element_type=f32)


def _sigmoid(x):
    return 1.0 / (1.0 + jnp.exp(-x))


def _pack_halves(x):
    w = x.shape[1] // 2
    bits = lax.bitcast_convert_type(x.astype(bf16).astype(f32), jnp.uint32)
    return (bits[:, w:] & jnp.uint32(0xFFFF0000)) | (bits[:, :w] >> 16)


def _unpack_halves(p):
    lo = lax.bitcast_convert_type(p << 16, f32)
    hi = lax.bitcast_convert_type(p & jnp.uint32(0xFFFF0000), f32)
    return jnp.concatenate([lo, hi], axis=1)


PACKED = D // 2
LANES = 128
TILE_ROWS = PACKED // LANES


def _store_tile_rows(ref, p):
    r = p.shape[0]
    for s in range(TILE_ROWS):
        ref[pl.ds(s, r, stride=TILE_ROWS), :] = p[:, s * LANES:(s + 1) * LANES]


def _load_tile_rows(ref, r):
    return jnp.concatenate([ref[pl.ds(s, r, stride=TILE_ROWS), :] for s in range(TILE_ROWS)], axis=1)


def _mod_kernel(c_ref, w_ref, b_ref, o_ref):
    c = c_ref[...]
    a = (c * _sigmoid(c)).astype(bf16)
    o_ref[0] = _dot(a, w_ref[0].astype(bf16)) + b_ref[0]


def _mod_rows(cvec, mod_w, mod_b):
    tn = 1024
    return pl.pallas_call(
        _mod_kernel,
        grid=(DEPTH, 6 * D // tn),
        in_specs=[pl.BlockSpec((8, D), lambda l, j: (0, 0)),
                  pl.BlockSpec((1, D, tn), lambda l, j: (l, 0, j)),
                  pl.BlockSpec((1, 1, tn), lambda l, j: (l, 0, j))],
        out_specs=pl.BlockSpec((1, 8, tn), lambda l, j: (l, 0, j)),
        out_shape=jax.ShapeDtypeStruct((DEPTH, 8, 6 * D), f32),
        compiler_params=_cparams(2),
        name="adaln_rows",
    )(cvec, mod_w, mod_b.reshape(DEPTH, 1, 6 * D))


def _group_joint(i):
    tiles = U // ROW_TILE
    return 2 * (i // tiles) + jnp.minimum(i % tiles, 1)


def _group_latent(i):
    return 2 * (i // (SEQ // ROW_TILE)) + 1


def _modulate_kernel(x_ref, m_ref, o_ref):
    m = m_ref[0]
    o_ref[...] = (x_ref[...] * (1.0 + m[1:2]) + m[0:1]).astype(o_ref.dtype)


def _modulate(x2d, mt, group):
    m = x2d.shape[0]
    return pl.pallas_call(
        _modulate_kernel,
        grid=(m // ROW_TILE,),
        in_specs=[pl.BlockSpec((ROW_TILE, D), lambda i: (i, 0)),
                  pl.BlockSpec((1, 6, D), lambda i: (group(i), 0, 0))],
        out_specs=pl.BlockSpec((ROW_TILE, D), lambda i: (i, 0)),
        out_shape=jax.ShapeDtypeStruct((m, D), bf16),
        compiler_params=_cparams(1),
        name="modulate",
    )(x2d, mt)


def _erf(x):
    return lax.erf(x)


def _mm_kernel(x_ref, w_ref, o_ref, wb_ref, *, act):
    @pl.when(pl.program_id(1) == 0)
    def _():
        wb_ref[...] = w_ref[...].astype(bf16)

    acc = _dot(x_ref[...], wb_ref[...])
    if act == "gelu":
        acc = 0.5 * acc * (1.0 + _erf(acc * (2.0 ** -0.5)))
    o_ref[...] = acc.astype(o_ref.dtype)


def _matmul(x, w, *, tm, tn, act=None, name):
    m, k = x.shape
    n = w.shape[1]
    return pl.pallas_call(
        functools.partial(_mm_kernel, act=act),
        grid=(n // tn, m // tm),
        in_specs=[pl.BlockSpec((tm, k), lambda j, i: (i, 0)),
                  pl.BlockSpec((k, tn), lambda j, i: (0, j))],
        out_specs=pl.BlockSpec((tm, tn), lambda j, i: (i, j)),
        out_shape=jax.ShapeDtypeStruct((m, n), bf16),
        scratch_shapes=[pltpu.VMEM((k, tn), bf16)],
        compiler_params=_cparams(2),
        name=name,
    )(x, w)


def _conv_qkv_kernel(xm_ref, cw_ref, cb_ref, wq_ref, wk_ref, wv_ref, wg_ref, bg_ref,
                     xc_ref, q_ref, k_ref, v_ref, g_ref):
    ct = pl.program_id(1)
    a = xm_ref[0].astype(f32)
    c = a.shape[1]
    r = lax.broadcasted_iota(jnp.int32, (U, c), 0)
    latent = r >= LC
    p = r - LC
    col = jnp.where(latent, p & (GRID_W - 1), r)
    last = jnp.where(latent, GRID_W - 1, LC - 1)
    a_l = jnp.where(col > 0, pltpu.roll(a, 1, 0), 0.0)
    a_r = jnp.where(col < last, pltpu.roll(a, U - 1, 0), 0.0)
    w = cw_ref[...]
    rows = [w[3 * i:3 * i + 1] * a_l + w[3 * i + 1:3 * i + 2] * a + w[3 * i + 2:3 * i + 3] * a_r
            for i in range(3)]
    up = jnp.where(p >= GRID_W, pltpu.roll(rows[0], GRID_W, 0), 0.0)
    down = jnp.where(latent & (p < SEQ - GRID_W), pltpu.roll(rows[2], U - GRID_W, 0), 0.0)
    pre = rows[1] + up + down + cb_ref[...]
    xc = (pre * _sigmoid(pre)).astype(bf16)
    xc_ref[0] = xc
    xm = xm_ref[0]
    q = _dot(xc, wq_ref[0]).astype(bf16)
    k = _dot(xc, wk_ref[0]).astype(bf16)
    v = _dot(xm, wv_ref[0]).astype(bf16)
    q_ref[0] = q
    k_ref[0] = k
    v_ref[0] = v
    g = (_dot(q, wg_ref[0].astype(bf16)) + _dot(k, wg_ref[1].astype(bf16))
         + _dot(v, wg_ref[2].astype(bf16)))

    @pl.when(ct == 0)
    def _():
        g_ref[0] = g + bg_ref[...]

    @pl.when(ct > 0)
    def _():
        g_ref[0] += g


def _block_diag(w):
    per = BD_TILE // QKV_BLOCK
    wt = w.reshape(E // BD_TILE, per, QKV_BLOCK, QKV_BLOCK)
    eye = jnp.eye(per, dtype=w.dtype)
    full = jnp.einsum("tgio,gh->tgiho", wt, eye)
    return full.reshape(E // BD_TILE, BD_TILE, BD_TILE).astype(bf16)


def _conv_qkv(xz, conv_w, conv_b, w_q, w_k, w_v, w_gate, b_gate):
    c = BD_TILE
    n_gate = 4 * NH
    act = jax.ShapeDtypeStruct((BATCH, U, E), bf16)
    tile = pl.BlockSpec((1, U, c), lambda b, t: (b, 0, t))
    bd = pl.BlockSpec((1, c, c), lambda b, t: (t, 0, 0))
    return pl.pallas_call(
        _conv_qkv_kernel,
        grid=(BATCH, E // c),
        in_specs=[tile,
                  pl.BlockSpec((9, c), lambda b, t: (0, t)),
                  pl.BlockSpec((1, c), lambda b, t: (0, t)),
                  bd, bd, bd,
                  pl.BlockSpec((3, c, n_gate), lambda b, t: (0, t, 0)),
                  pl.BlockSpec((1, n_gate), lambda b, t: (0, 0))],
        out_specs=[tile, tile, tile, tile,
                   pl.BlockSpec((1, U, n_gate), lambda b, t: (b, 0, 0))],
        out_shape=[act, act, act, act, jax.ShapeDtypeStruct((BATCH, U, n_gate), f32)],
        compiler_params=_cparams(2),
        name="conv_qkv_gates",
    )(xz.reshape(BATCH, U, 2 * E), conv_w.reshape(9, E), conv_b.reshape(1, E),
      _block_diag(w_q), _block_diag(w_k), _block_diag(w_v),
      w_gate.reshape(3, E, n_gate), b_gate.reshape(1, n_gate))


def _log_sigmoid(x):
    return jnp.minimum(x, 0.0) - jnp.log1p(jnp.exp(-jnp.abs(x)))


def _mlstm_kernel(*refs, direction, finish):
    if finish:
        (q_ref, k_ref, v_ref, gc_ref, gr_ref, hf_ref, xc_ref, z_ref, nw_ref, sk_ref,
         o_ref, ct_ref, ctb_ref, n_ref, m_ref) = refs
    else:
        q_ref, k_ref, v_ref, gc_ref, gr_ref, o_ref, ct_ref, ctb_ref, n_ref, m_ref = refs
    L = MLSTM_CHUNK
    h = pl.program_id(1)
    step = pl.program_id(2)

    @pl.when(step == 0)
    def _():
        ct_ref[...] = jnp.zeros_like(ct_ref)
        ctb_ref[...] = jnp.zeros_like(ctb_ref)
        n_ref[...] = jnp.zeros_like(n_ref)
        m_ref[...] = jnp.full_like(m_ref, -jnp.inf)

    i_idx = 2 * direction * NH + h
    f_idx = (2 * direction + 1) * NH + h
    gc = gc_ref[0]
    gr = gr_ref[0]
    lane = lax.broadcasted_iota(jnp.int32, gc.shape, 1)
    sub = lax.broadcasted_iota(jnp.int32, gr.shape, 0)
    ig_col = jnp.sum(jnp.where(lane == i_idx, gc, 0.0), axis=1, keepdims=True)
    f_col = jnp.sum(jnp.where(lane == f_idx, gc, 0.0), axis=1, keepdims=True)
    ig_row = jnp.sum(jnp.where(sub == i_idx, gr, 0.0), axis=0, keepdims=True)
    f_row = jnp.sum(jnp.where(sub == f_idx, gr, 0.0), axis=0, keepdims=True)
    lf_col = _log_sigmoid(f_col)
    lf_row = _log_sigmoid(f_row)

    t_i = lax.broadcasted_iota(jnp.int32, (L, L), 0)
    s_i = lax.broadcasted_iota(jnp.int32, (L, L), 1)
    seen = (s_i <= t_i) if direction == 0 else (s_i >= t_i)
    b_col = jnp.sum(jnp.where(seen, lf_row, 0.0), axis=1, keepdims=True)
    seen_t = (t_i <= s_i) if direction == 0 else (t_i >= s_i)
    b_row = jnp.sum(jnp.where(seen_t, lf_col, 0.0), axis=0, keepdims=True)
    b_end = jnp.sum(lf_row, axis=1, keepdims=True)

    m_prev = m_ref[:, 0:1]
    log_d = jnp.where(seen, b_col - b_row + ig_row, -jnp.inf)
    g_col = b_col + m_prev
    m_t = jnp.maximum(g_col, jnp.max(log_d, axis=1, keepdims=True))
    dw = jnp.exp(log_d - m_t)
    inter = jnp.exp(g_col - m_t)

    q = q_ref[0]
    k = k_ref[0] * (DH ** -0.5)
    v = v_ref[0]
    s = lax.dot_general(q, k, (((1,), (1,)), ((), ())), preferred_element_type=f32) * dw
    num = _dot(s.astype(bf16), v) + inter * _dot(q, ctb_ref[...])
    qn = jnp.sum(q.astype(f32) * n_ref[...], axis=1, keepdims=True)
    den = jnp.sum(s, axis=1, keepdims=True) + inter * qn
    hout = num / jnp.maximum(jnp.abs(den), jnp.exp(-m_t))

    w_end = b_end - b_col + ig_col
    m_new = jnp.maximum(b_end + m_prev, jnp.max(w_end, axis=0, keepdims=True))
    decay = jnp.exp(b_end + m_prev - m_new)
    kw = k.astype(f32) * jnp.exp(w_end - m_new)
    upd = lax.dot_general(kw.astype(bf16), v, (((0,), (0,)), ((), ())), preferred_element_type=f32)
    c_new = decay * ct_ref[...] + upd
    ct_ref[...] = c_new
    ctb_ref[...] = c_new.astype(bf16)
    n_ref[...] = decay * n_ref[...] + jnp.sum(kw, axis=0, keepdims=True)
    m_ref[...] = jnp.broadcast_to(m_new, m_ref.shape)

    if not finish:
        o_ref[0] = hout.astype(o_ref.dtype)
    else:
        hs = hout + hf_ref[0].astype(f32)
        mu = jnp.mean(hs, axis=1, keepdims=True)
        xc_ = hs - mu
        var = jnp.mean(xc_ * xc_, axis=1, keepdims=True)
        hn = xc_ * lax.rsqrt(var + LN_EPS)
        z = z_ref[0].astype(f32)
        o_ref[0] = ((hn * nw_ref[...] + sk_ref[...] * xc_ref[0].astype(f32))
                    * (z * _sigmoid(z))).astype(o_ref.dtype)


def _mlstm(q, k, v, gates, gates_t, direction, finish_args=None):
    L = MLSTM_CHUNK
    n_chunks = U // L

    if direction == 0:
        def chunk(s):
            return s
    else:
        def chunk(s):
            return jnp.where(s == 0, 0, n_chunks - s)

    tile = pl.BlockSpec((1, L, DH), lambda b, h, s: (b, chunk(s), h))
    in_specs = [tile, tile, tile,
                pl.BlockSpec((1, L, 4 * NH), lambda b, h, s: (b, chunk(s), 0)),
                pl.BlockSpec((1, 4 * NH, L), lambda b, h, s: (b, 0, chunk(s)))]
    args = [q, k, v, gates, gates_t]
    finish = finish_args is not None
    if finish:
        h_fwd, xc, xz, norm_w, skip = finish_args
        vec = pl.BlockSpec((1, DH), lambda b, h, s: (0, h))
        in_specs += [tile, tile,
                     pl.BlockSpec((1, L, DH), lambda b, h, s: (b, chunk(s), NH + h)),
                     vec, vec]
        args += [h_fwd, xc, xz.reshape(BATCH, U, 2 * E), norm_w.reshape(1, E), skip.reshape(1, E)]
    return pl.pallas_call(
        functools.partial(_mlstm_kernel, direction=direction, finish=finish),
        grid=(BATCH, NH, n_chunks),
        in_specs=in_specs,
        out_specs=tile,
        out_shape=jax.ShapeDtypeStruct((BATCH, U, E), bf16),
        scratch_shapes=[pltpu.VMEM((DH, DH), f32), pltpu.VMEM((DH, DH), bf16),
                        pltpu.VMEM((1, DH), f32), pltpu.VMEM((1, 128), f32)],
        compiler_params=_cparams(3),
        name="mlstm_bwd_finish" if finish else "mlstm_fwd",
    )(*args)


def _ln_rows(r, g, b):
    mu = jnp.mean(r, axis=1, keepdims=True)
    rc = r - mu
    var = jnp.mean(rc * rc, axis=1, keepdims=True)
    return rc * lax.rsqrt(var + LN_EPS) * g + b


def _ln_router_kernel(hs_ref, o_ref, m_ref, g_ref, b_ref, rw_ref, rb_ref,
                      hs1_ref, tok_ref, lg_ref):
    m = m_ref[0]
    r = ALPHA * hs_ref[...] + m[2:3] * o_ref[...].astype(f32)
    hs1 = _ln_rows(r, g_ref[...], b_ref[...])
    hs1_ref[...] = hs1
    tok = hs1 * (1.0 + m[4:5]) + m[3:4]
    t_hi = tok.astype(bf16)
    _store_tile_rows(tok_ref, _pack_halves(tok))
    t_lo = (tok - t_hi.astype(f32)).astype(bf16)
    w = rw_ref[...]
    w_hi = w.astype(bf16)
    w_lo = (w - w_hi.astype(f32)).astype(bf16)
    lg_ref[...] = _dot(t_hi, w_hi) + _dot(t_hi, w_lo) + _dot(t_lo, w_hi) + rb_ref[...]


def _ln_router(hs, o, mt, group, ln_g, ln_b, r_w, r_b):
    m = hs.shape[0]
    row = pl.BlockSpec((ROW_TILE, D), lambda i: (i, 0))
    vec = pl.BlockSpec((1, D), lambda i: (0, 0))
    return pl.pallas_call(
        _ln_router_kernel,
        grid=(m // ROW_TILE,),
        in_specs=[row, row,
                  pl.BlockSpec((1, 6, D), lambda i: (group(i), 0, 0)),
                  vec, vec,
                  pl.BlockSpec((D, NE), lambda i: (0, 0)),
                  pl.BlockSpec((1, NE), lambda i: (0, 0))],
        out_specs=[row, pl.BlockSpec((ROW_TILE * TILE_ROWS, LANES), lambda i: (i, 0)),
                   pl.BlockSpec((ROW_TILE, NE), lambda i: (i, 0))],
        out_shape=[jax.ShapeDtypeStruct((m, D), f32),
                   jax.ShapeDtypeStruct((m * TILE_ROWS, LANES), jnp.uint32),
                   jax.ShapeDtypeStruct((m, NE), f32)],
        compiler_params=_cparams(1),
        name="ln_router",
    )(hs, o, mt, ln_g.reshape(1, D), ln_b.reshape(1, D), r_w, r_b.reshape(1, NE))


def _first_of_expert(be_ref, i):
    return jnp.logical_or(i == 0, be_ref[i] != be_ref[jnp.maximum(i - 1, 0)])


def _moe_up_kernel(be_ref, nu_ref, x_ref, wg_ref, wl_ref, bg_ref, bl_ref, o_ref, wgb_ref, wlb_ref):
    i = pl.program_id(1)
    valid = i < nu_ref[0]

    @pl.when(jnp.logical_and(valid, _first_of_expert(be_ref, i)))
    def _():
        wgb_ref[...] = wg_ref[0, 0].astype(bf16)
        wlb_ref[...] = wl_ref[0, 0].astype(bf16)

    @pl.when(valid)
    def _():
        x = _unpack_halves(_load_tile_rows(x_ref, MOE_TM)).astype(bf16)
        glu = jnp.minimum(_dot(x, wgb_ref[...]) + bg_ref[0, 0], SWIGLU_LIMIT)
        lin = jnp.clip(_dot(x, wlb_ref[...]) + bl_ref[0, 0], -SWIGLU_LIMIT, SWIGLU_LIMIT)
        o_ref[...] = (glu * _sigmoid(SWIGLU_ALPHA * glu) * (lin + 1.0)).astype(o_ref.dtype)

    @pl.when(jnp.logical_not(valid))
    def _():
        o_ref[...] = jnp.zeros_like(o_ref)


def _moe_down_kernel(be_ref, nu_ref, a_ref, w_ref, b_ref, o_ref, wb_ref):
    i = pl.program_id(0)
    valid = i < nu_ref[0]

    @pl.when(jnp.logical_and(valid, _first_of_expert(be_ref, i)))
    def _():
        wb_ref[...] = w_ref[0, 0].astype(bf16)

    @pl.when(valid)
    def _():
        y = _dot(a_ref[...], wb_ref[...]) + b_ref[0, 0]
        _store_tile_rows(o_ref, _pack_halves(y))

    @pl.when(jnp.logical_not(valid))
    def _():
        o_ref[...] = jnp.zeros_like(o_ref)


def _moe_experts(layer, xs, block_e, n_used, w1, b1, w2, b2):
    ns = xs.shape[0] // TILE_ROWS
    nb = ns // MOE_TM
    tf = 512
    lin0 = F // tf
    b1 = b1.reshape(DEPTH, NE, 1, 2 * F)
    b2 = b2.reshape(DEPTH, NE, 1, D)

    def wspec(rows, width, col0):
        return pl.BlockSpec((1, 1, rows, width), lambda j, i, be, nu: (layer, be[i], 0, col0 + j))

    act = pl.pallas_call(
        _moe_up_kernel,
        grid_spec=pltpu.PrefetchScalarGridSpec(
            num_scalar_prefetch=2,
            grid=(F // tf, nb),
            in_specs=[pl.BlockSpec((MOE_TM * TILE_ROWS, LANES),
                                   lambda j, i, be, nu: (jnp.minimum(i, nu[0] - 1), 0)),
                      wspec(D, tf, 0), wspec(D, tf, lin0), wspec(1, tf, 0), wspec(1, tf, lin0)],
            out_specs=pl.BlockSpec((MOE_TM, tf), lambda j, i, be, nu: (i, j)),
            scratch_shapes=[pltpu.VMEM((D, tf), bf16), pltpu.VMEM((D, tf), bf16)]),
        out_shape=jax.ShapeDtypeStruct((ns, F), bf16),
        compiler_params=_cparams(2),
        name="moe_up",
    )(block_e, n_used, xs, w1, w1, b1, b1)

    return pl.pallas_call(
        _moe_down_kernel,
        grid_spec=pltpu.PrefetchScalarGridSpec(
            num_scalar_prefetch=2,
            grid=(nb,),
            in_specs=[pl.BlockSpec((MOE_TM, F), lambda i, be, nu: (jnp.minimum(i, nu[0] - 1), 0)),
                      pl.BlockSpec((1, 1, F, D), lambda i, be, nu: (layer, be[i], 0, 0)),
                      pl.BlockSpec((1, 1, 1, D), lambda i, be, nu: (layer, be[i], 0, 0))],
            out_specs=pl.BlockSpec((MOE_TM * TILE_ROWS, LANES), lambda i, be, nu: (i, 0)),
            scratch_shapes=[pltpu.VMEM((F, D), bf16)]),
        out_shape=jax.ShapeDtypeStruct((ns * TILE_ROWS, LANES), jnp.uint32),
        compiler_params=_cparams(1),
        name="moe_down",
    )(block_e, n_used, act, w2, b2)


GATHER_ROWS = 2048


def _gather_kernel(idx_ref, src_ref, out_ref, sem):
    n = GATHER_ROWS
    t = TILE_ROWS

    def issue(r, carry):
        src_row = pl.multiple_of(idx_ref[0, 0, r] * t, t)
        dst_row = pl.multiple_of(r * t, t)
        pltpu.make_async_copy(src_ref.at[pl.ds(src_row, t)], out_ref.at[pl.ds(dst_row, t)], sem).start()
        return carry

    lax.fori_loop(0, n, issue, 0, unroll=8)
    pltpu.make_async_copy(src_ref.at[pl.ds(0, n * t)], out_ref, sem).wait()


def _gather_rows(src, idx):
    n = idx.shape[0]
    nblk = n // GATHER_ROWS
    return pl.pallas_call(
        _gather_kernel,
        grid=(nblk,),
        in_specs=[pl.BlockSpec((1, 1, GATHER_ROWS), lambda i: (i, 0, 0), memory_space=pltpu.SMEM),
                  pl.BlockSpec(memory_space=pl.ANY)],
        out_specs=pl.BlockSpec((GATHER_ROWS * TILE_ROWS, LANES), lambda i: (i, 0)),
        out_shape=jax.ShapeDtypeStruct((n * TILE_ROWS, LANES), src.dtype),
        scratch_shapes=[pltpu.SemaphoreType.DMA],
        compiler_params=pltpu.CompilerParams(dimension_semantics=("arbitrary",),
                                             vmem_limit_bytes=VMEM_LIMIT,
                                             disable_bounds_checks=True),
        name="gather_rows",
    )(idx.reshape(nblk, 1, GATHER_ROWS), src)


def _route(logits):
    t = logits.shape[0]
    n_assign = t * TOP_K
    nb = -(-n_assign // MOE_TM) + NE
    top_v, top_i = lax.top_k(logits, TOP_K)
    gate = jax.nn.softmax(top_v, axis=-1)
    flat_e = top_i.reshape(-1)
    onehot = (flat_e[:, None] == jnp.arange(NE, dtype=flat_e.dtype)[None, :]).astype(jnp.int32)
    csum = jnp.cumsum(onehot, axis=0)
    counts = csum[-1]
    rank = jnp.take_along_axis(csum, flat_e[:, None], axis=1)[:, 0] - 1
    pcounts = (counts + MOE_TM - 1) // MOE_TM * MOE_TM
    pends = jnp.cumsum(pcounts)
    pstarts = pends - pcounts
    dest = (pstarts[flat_e] + rank).astype(jnp.int32)
    slot_tok = (jnp.arange(nb * MOE_TM, dtype=jnp.int32) % t).at[dest].set(
        jnp.arange(n_assign, dtype=jnp.int32) // TOP_K)
    n_used = (pends[-1] // MOE_TM).astype(jnp.int32)
    blk = jnp.arange(nb, dtype=jnp.int32)
    block_e = jnp.clip(jnp.searchsorted(pends, blk * MOE_TM, side="right"), 0, NE - 1).astype(jnp.int32)
    block_e = jnp.where(blk < n_used, block_e, block_e[n_used - 1])
    return gate, dest, slot_tok, block_e, n_used.reshape(1)


def _combine_kernel(hs_ref, y_ref, gate_ref, m_ref, g_ref, b_ref, *rest, with_next):
    if with_next:
        mn_ref, hs2_ref, nxt_ref = rest
    else:
        (hs2_ref,) = rest
    m = m_ref[0]
    gate = gate_ref[...]
    y = gate[:, 0:1] * _unpack_halves(_load_tile_rows(y_ref.at[0], ROW_TILE))
    for k in range(1, TOP_K):
        y = y + gate[:, k:k + 1] * _unpack_halves(_load_tile_rows(y_ref.at[k], ROW_TILE))
    hs2 = _ln_rows(ALPHA * hs_ref[...] + m[5:6] * y, g_ref[...], b_ref[...])
    hs2_ref[...] = hs2
    if with_next:
        mn = mn_ref[0]
        nxt_ref[...] = (hs2 * (1.0 + mn[1:2]) + mn[0:1]).astype(nxt_ref.dtype)


def _combine(hs1, yg, gate, mt, group, ln_g, ln_b, mt_next=None):
    m = hs1.shape[0]
    row = pl.BlockSpec((ROW_TILE, D), lambda i: (i, 0))
    vec = pl.BlockSpec((1, D), lambda i: (0, 0))
    mod = pl.BlockSpec((1, 6, D), lambda i: (group(i), 0, 0))
    in_specs = [row, pl.BlockSpec((TOP_K, ROW_TILE * TILE_ROWS, LANES), lambda i: (0, i, 0)),
                pl.BlockSpec((ROW_TILE, TOP_K), lambda i: (i, 0)), mod, vec, vec]
    args = [hs1, yg, gate, mt, ln_g.reshape(1, D), ln_b.reshape(1, D)]
    out_specs = [row]
    out_shape = [jax.ShapeDtypeStruct((m, D), f32)]
    with_next = mt_next is not None
    if with_next:
        in_specs.append(mod)
        args.append(mt_next)
        out_specs.append(row)
        out_shape.append(jax.ShapeDtypeStruct((m, D), bf16))
    return pl.pallas_call(
        functools.partial(_combine_kernel, with_next=with_next),
        grid=(m // ROW_TILE,),
        in_specs=in_specs, out_specs=out_specs, out_shape=out_shape,
        compiler_params=_cparams(1),
        name="moe_combine_ln",
    )(*args)


def _moe_layer(layer, hs, o, mt, group, ln1_g, ln1_b, ln2_g, ln2_b, r_w, r_b, w1, b1, w2, b2, mt_next=None):
    t = hs.shape[0]
    hs1, tok, logits = _ln_router(hs, o, mt, group, ln1_g[layer], ln1_b[layer], r_w[layer], r_b[layer])
    gate, dest, slot_tok, block_e, n_used = _route(logits)
    xs = _gather_rows(tok, slot_tok)
    y = _moe_experts(layer, xs, block_e, n_used, w1, b1, w2, b2)
    yg = _gather_rows(y, dest.reshape(t, TOP_K).T.reshape(-1)).reshape(TOP_K, t * TILE_ROWS, LANES)
    return _combine(hs1, yg, gate, mt, group, ln2_g[layer], ln2_b[layer], mt_next)


def _sgu_kernel(g_ref, v_ref, lg_ref, lb_ref, ws_ref, bs_ref, o_ref):
    v = v_ref[...].astype(f32)
    vn = _ln_rows(v, lg_ref[...], lb_ref[...]).astype(bf16)
    gw = E // SGU_GROUPS
    for g in range(SGU_GROUPS):
        mixed = _dot(ws_ref[g].astype(bf16), vn[:, g * gw:(g + 1) * gw]) + bs_ref[:, g:g + 1]
        o_ref[:, g * gw:(g + 1) * gw] = (g_ref[:, g * gw:(g + 1) * gw].astype(f32) * mixed).astype(o_ref.dtype)


def _sgu(uv, ln_g, ln_b, w_s, b_s):
    m = uv.shape[0]
    c = SGU_CHUNK
    vec = pl.BlockSpec((1, E), lambda i: (0, 0))
    return pl.pallas_call(
        _sgu_kernel,
        grid=(m // c,),
        in_specs=[pl.BlockSpec((c, E), lambda i: (i, 0)),
                  pl.BlockSpec((c, E), lambda i: (i, 1)),
                  vec, vec,
                  pl.BlockSpec((SGU_GROUPS, c, c), lambda i: (0, 0, 0)),
                  pl.BlockSpec((c, SGU_GROUPS), lambda i: (0, 0))],
        out_specs=pl.BlockSpec((c, E), lambda i: (i, 0)),
        out_shape=jax.ShapeDtypeStruct((m, E), bf16),
        compiler_params=_cparams(1),
        name="sgu",
    )(uv, uv, ln_g.reshape(1, E), ln_b.reshape(1, E), w_s, b_s.T)


def kernel(x, c, ctx, c_ctx, mod_w, mod_b, ln1_g, ln1_b, ln2_g, ln2_b, a_w_in, a_conv_w, a_conv_b, a_w_q, a_w_k, a_w_v, a_w_gate, a_b_gate, a_norm_w, a_skip, a_w_out, b_w_in, b_ln_g, b_ln_b, b_w_s, b_b_s, b_w_out, r_w, r_b, e_w1, e_b1, e_w2, e_b2):
    cvec = jnp.concatenate([c, c_ctx[None, :], jnp.zeros((8 - BATCH - 1, D), f32)], axis=0)
    mods = _mod_rows(cvec, mod_w, mod_b).reshape(DEPTH, 8, 6, D)
    rows = [r for b in range(BATCH) for r in (BATCH, b)]
    mt = [jnp.stack([mods[l, r] for r in rows], axis=0) for l in range(DEPTH)]

    hs = jnp.concatenate([ctx, x], axis=1).reshape(BATCH * U, D)
    hx = _modulate(hs, mt[0], _group_joint)
    xz = _matmul(hx, a_w_in[0], tm=1024, tn=1024, name="mlstm_in_proj")
    xc, q, k, v, gates = _conv_qkv(xz, a_conv_w[0], a_conv_b[0], a_w_q[0], a_w_k[0], a_w_v[0],
                                   a_w_gate[0], a_b_gate[0])
    gates_t = gates.transpose(0, 2, 1)
    h_fwd = _mlstm(q, k, v, gates, gates_t, 0)
    pre = _mlstm(q, k, v, gates, gates_t, 1, (h_fwd, xc, xz, a_norm_w[0], a_skip[0]))
    o = _matmul(pre.reshape(BATCH * U, E), a_w_out[0], tm=1024, tn=512, name="mlstm_out_proj")
    hs2, hx1 = _moe_layer(0, hs, o, mt[0], _group_joint, ln1_g, ln1_b, ln2_g, ln2_b,
                          r_w, r_b, e_w1, e_b1, e_w2, e_b2, mt_next=mt[1])

    hs = hs2.reshape(BATCH, U, D)[:, LC:].reshape(BATCH * SEQ, D)
    hx = hx1.reshape(BATCH, U, D)[:, LC:].reshape(BATCH * SEQ, D)
    uv = _matmul(hx, b_w_in[0], tm=1024, tn=1024, act="gelu", name="sgu_in_proj")
    gated = _sgu(uv, b_ln_g[0], b_ln_b[0], b_w_s[0], b_b_s[0])
    o = _matmul(gated, b_w_out[0], tm=1024, tn=512, name="sgu_out_proj")
    (out,) = _moe_layer(1, hs, o, mt[1], _group_latent, ln1_g, ln1_b, ln2_g, ln2_b,
                        r_w, r_b, e_w1, e_b1, e_w2, e_b2)
    return out.reshape(BATCH, SEQ, D)
```

```python
import functools

import jax
import jax.numpy as jnp
from jax import lax
from jax.experimental import pallas as pl
from jax.experimental.pallas import tpu as pltpu

D = 2048
BATCH = 4
SEQ = 2048
DEPTH = 2
GRID_W = 64
LC = 256
U = LC + SEQ
E = 2 * D
NH = 4
DH = E // NH
QKV_BLOCK = 4
SGU_GROUPS = 8
SGU_CHUNK = 128
NE = 32
TOP_K = 4
F = D
SWIGLU_LIMIT = 7.0
SWIGLU_ALPHA = 1.702
ALPHA = (2 * DEPTH) ** 0.25
LN_EPS = 1e-5

ROW_TILE = 256
MLSTM_CHUNK = 256
MOE_TM = 512
BD_TILE = 256
VMEM_LIMIT = 56 * 1024 * 1024

f32 = jnp.float32
bf16 = jnp.bfloat16


def _cparams(n_axes):
    return pltpu.CompilerParams(dimension_semantics=("arbitrary",) * n_axes,
                                vmem_limit_bytes=VMEM_LIMIT)


def _dot(a, b):
    return jnp.dot(a, b, preferred_element_type=f32)


def _sigmoid(x):
    return 1.0 / (1.0 + jnp.exp(-x))


def _pack_halves(x):
    w = x.shape[1] // 2
    bits = lax.bitcast_convert_type(x.astype(bf16).astype(f32), jnp.uint32)
    return (bits[:, w:] & jnp.uint32(0xFFFF0000)) | (bits[:, :w] >> 16)


def _unpack_halves(p):
    lo = lax.bitcast_convert_type(p << 16, f32)
    hi = lax.bitcast_convert_type(p & jnp.uint32(0xFFFF0000), f32)
    return jnp.concatenate([lo, hi], axis=1)


PACKED = D // 2
LANES = 128
TILE_ROWS = PACKED // LANES


def _store_tile_rows(ref, p):
    r = p.shape[0]
    for s in range(TILE_ROWS):
        ref[pl.ds(s, r, stride=TILE_ROWS), :] = p[:, s * LANES:(s + 1) * LANES]


def _load_tile_rows(ref, r):
    return jnp.concatenate([ref[pl.ds(s, r, stride=TILE_ROWS), :] for s in range(TILE_ROWS)], axis=1)


def _mod_kernel(c_ref, w_ref, b_ref, o_ref):
    c = c_ref[...]
    a = (c * _sigmoid(c)).astype(bf16)
    o_ref[0] = _dot(a, w_ref[0].astype(bf16)) + b_ref[0]


def _mod_rows(cvec, mod_w, mod_b):
    tn = 1024
    return pl.pallas_call(
        _mod_kernel,
        grid=(DEPTH, 6 * D // tn),
        in_specs=[pl.BlockSpec((8, D), lambda l, j: (0, 0)),
                  pl.BlockSpec((1, D, tn), lambda l, j: (l, 0, j)),
                  pl.BlockSpec((1, 1, tn), lambda l, j: (l, 0, j))],
        out_specs=pl.BlockSpec((1, 8, tn), lambda l, j: (l, 0, j)),
        out_shape=jax.ShapeDtypeStruct((DEPTH, 8, 6 * D), f32),
        compiler_params=_cparams(2),
        name="adaln_rows",
    )(cvec, mod_w, mod_b.reshape(DEPTH, 1, 6 * D))


def _group_joint(i):
    tiles = U // ROW_TILE
    return 2 * (i // tiles) + jnp.minimum(i % tiles, 1)


def _group_latent(i):
    return 2 * (i // (SEQ // ROW_TILE)) + 1


def _modulate_kernel(x_ref, m_ref, o_ref):
    m = m_ref[0]
    o_ref[...] = (x_ref[...] * (1.0 + m[1:2]) + m[0:1]).astype(o_ref.dtype)


def _modulate(x2d, mt, group):
    m = x2d.shape[0]
    return pl.pallas_call(
        _modulate_kernel,
        grid=(m // ROW_TILE,),
        in_specs=[pl.BlockSpec((ROW_TILE, D), lambda i: (i, 0)),
                  pl.BlockSpec((1, 6, D), lambda i: (group(i), 0, 0))],
        out_specs=pl.BlockSpec((ROW_TILE, D), lambda i: (i, 0)),
        out_shape=jax.ShapeDtypeStruct((m, D), bf16),
        compiler_params=_cparams(1),
        name="modulate",
    )(x2d, mt)


def _erf(x):
    return lax.erf(x)


def _mm_kernel(x_ref, w_ref, o_ref, wb_ref, *, act):
    @pl.when(pl.program_id(1) == 0)
    def _():
        wb_ref[...] = w_ref[...].astype(bf16)

    acc = _dot(x_ref[...], wb_ref[...])
    if act == "gelu":
        acc = 0.5 * acc * (1.0 + _erf(acc * (2.0 ** -0.5)))
    o_ref[...] = acc.astype(o_ref.dtype)


def _matmul(x, w, *, tm, tn, act=None, name):
    m, k = x.shape
    n = w.shape[1]
    return pl.pallas_call(
        functools.partial(_mm_kernel, act=act),
        grid=(n // tn, m // tm),
        in_specs=[pl.BlockSpec((tm, k), lambda j, i: (i, 0)),
                  pl.BlockSpec((k, tn), lambda j, i: (0, j))],
        out_specs=pl.BlockSpec((tm, tn), lambda j, i: (i, j)),
        out_shape=jax.ShapeDtypeStruct((m, n), bf16),
        scratch_shapes=[pltpu.VMEM((k, tn), bf16)],
        compiler_params=_cparams(2),
        name=name,
    )(x, w)


def _conv_qkv_kernel(xm_ref, cw_ref, cb_ref, wq_ref, wk_ref, wv_ref, wg_ref, bg_ref,
                     xc_ref, q_ref, k_ref, v_ref, g_ref):
    ct = pl.program_id(1)
    a = xm_ref[0].astype(f32)
    c = a.shape[1]
    r = lax.broadcasted_iota(jnp.int32, (U, c), 0)
    latent = r >= LC
    p = r - LC
    col = jnp.where(latent, p & (GRID_W - 1), r)
    last = jnp.where(latent, GRID_W - 1, LC - 1)
    a_l = jnp.where(col > 0, pltpu.roll(a, 1, 0), 0.0)
    a_r = jnp.where(col < last, pltpu.roll(a, U - 1, 0), 0.0)
    w = cw_ref[...]
    rows = [w[3 * i:3 * i + 1] * a_l + w[3 * i + 1:3 * i + 2] * a + w[3 * i + 2:3 * i + 3] * a_r
            for i in range(3)]
    up = jnp.where(p >= GRID_W, pltpu.roll(rows[0], GRID_W, 0), 0.0)
    down = jnp.where(latent & (p < SEQ - GRID_W), pltpu.roll(rows[2], U - GRID_W, 0), 0.0)
    pre = rows[1] + up + down + cb_ref[...]
    xc = (pre * _sigmoid(pre)).astype(bf16)
    xc_ref[0] = xc
    xm = xm_ref[0]
    q = _dot(xc, wq_ref[0]).astype(bf16)
    k = _dot(xc, wk_ref[0]).astype(bf16)
    v = _dot(xm, wv_ref[0]).astype(bf16)
    q_ref[0] = q
    k_ref[0] = k
    v_ref[0] = v
    g = (_dot(q, wg_ref[0].astype(bf16)) + _dot(k, wg_ref[1].astype(bf16))
         + _dot(v, wg_ref[2].astype(bf16)))

    @pl.when(ct == 0)
    def _():
        g_ref[0] = g + bg_ref[...]

    @pl.when(ct > 0)
    def _():
        g_ref[0] += g


def _block_diag(w):
    per = BD_TILE // QKV_BLOCK
    wt = w.reshape(E // BD_TILE, per, QKV_BLOCK, QKV_BLOCK)
    eye = jnp.eye(per, dtype=w.dtype)
    full = jnp.einsum("tgio,gh->tgiho", wt, eye)
    return full.reshape(E // BD_TILE, BD_TILE, BD_TILE).astype(bf16)


def _conv_qkv(xz, conv_w, conv_b, w_q, w_k, w_v, w_gate, b_gate):
    c = BD_TILE
    n_gate = 4 * NH
    act = jax.ShapeDtypeStruct((BATCH, U, E), bf16)
    tile = pl.BlockSpec((1, U, c), lambda b, t: (b, 0, t))
    bd = pl.BlockSpec((1, c, c), lambda b, t: (t, 0, 0))
    return pl.pallas_call(
        _conv_qkv_kernel,
        grid=(BATCH, E // c),
        in_specs=[tile,
                  pl.BlockSpec((9, c), lambda b, t: (0, t)),
                  pl.BlockSpec((1, c), lambda b, t: (0, t)),
                  bd, bd, bd,
                  pl.BlockSpec((3, c, n_gate), lambda b, t: (0, t, 0)),
                  pl.BlockSpec((1, n_gate), lambda b, t: (0, 0))],
        out_specs=[tile, tile, tile, tile,
                   pl.BlockSpec((1, U, n_gate), lambda b, t: (b, 0, 0))],
        out_shape=[act, act, act, act, jax.ShapeDtypeStruct((BATCH, U, n_gate), f32)],
        compiler_params=_cparams(2),
        name="conv_qkv_gates",
    )(xz.reshape(BATCH, U, 2 * E), conv_w.reshape(9, E), conv_b.reshape(1, E),
      _block_diag(w_q), _block_diag(w_k), _block_diag(w_v),
      w_gate.reshape(3, E, n_gate), b_gate.reshape(1, n_gate))


def _log_sigmoid(x):
    return jnp.minimum(x, 0.0) - jnp.log1p(jnp.exp(-jnp.abs(x)))


def _mlstm_kernel(*refs, direction, finish):
    if finish:
        (q_ref, k_ref, v_ref, gc_ref, gr_ref, hf_ref, xc_ref, z_ref, nw_ref, sk_ref,
         o_ref, ct_ref, ctb_ref, n_ref, m_ref) = refs
    else:
        q_ref, k_ref, v_ref, gc_ref, gr_ref, o_ref, ct_ref, ctb_ref, n_ref, m_ref = refs
    L = MLSTM_CHUNK
    h = pl.program_id(1)
    step = pl.program_id(2)

    @pl.when(step == 0)
    def _():
        ct_ref[...] = jnp.zeros_like(ct_ref)
        ctb_ref[...] = jnp.zeros_like(ctb_ref)
        n_ref[...] = jnp.zeros_like(n_ref)
        m_ref[...] = jnp.full_like(m_ref, -jnp.inf)

    i_idx = 2 * direction * NH + h
    f_idx = (2 * direction + 1) * NH + h
    gc = gc_ref[0]
    gr = gr_ref[0]
    lane = lax.broadcasted_iota(jnp.int32, gc.shape, 1)
    sub = lax.broadcasted_iota(jnp.int32, gr.shape, 0)
    ig_col = jnp.sum(jnp.where(lane == i_idx, gc, 0.0), axis=1, keepdims=True)
    f_col = jnp.sum(jnp.where(lane == f_idx, gc, 0.0), axis=1, keepdims=True)
    ig_row = jnp.sum(jnp.where(sub == i_idx, gr, 0.0), axis=0, keepdims=True)
    f_row = jnp.sum(jnp.where(sub == f_idx, gr, 0.0), axis=0, keepdims=True)
    lf_col = _log_sigmoid(f_col)
    lf_row = _log_sigmoid(f_row)

    t_i = lax.broadcasted_iota(jnp.int32, (L, L), 0)
    s_i = lax.broadcasted_iota(jnp.int32, (L, L), 1)
    seen = (s_i <= t_i) if direction == 0 else (s_i >= t_i)
    b_col = jnp.sum(jnp.where(seen, lf_row, 0.0), axis=1, keepdims=True)
    seen_t = (t_i <= s_i) if direction == 0 else (t_i >= s_i)
    b_row = jnp.sum(jnp.where(seen_t, lf_col, 0.0), axis=0, keepdims=True)
    b_end = jnp.sum(lf_row, axis=1, keepdims=True)

    m_prev = m_ref[:, 0:1]
    log_d = jnp.where(seen, b_col - b_row + ig_row, -jnp.inf)
    g_col = b_col + m_prev
    m_t = jnp.maximum(g_col, jnp.max(log_d, axis=1, keepdims=True))
    dw = jnp.exp(log_d - m_t)
    inter = jnp.exp(g_col - m_t)

    q = q_ref[0]
    k = k_ref[0] * (DH ** -0.5)
    v = v_ref[0]
    s = lax.dot_general(q, k, (((1,), (1,)), ((), ())), preferred_element_type=f32) * dw
    num = _dot(s.astype(bf16), v) + inter * _dot(q, ctb_ref[...])
    qn = jnp.sum(q.astype(f32) * n_ref[...], axis=1, keepdims=True)
    den = jnp.sum(s, axis=1, keepdims=True) + inter * qn
    hout = num / jnp.maximum(jnp.abs(den), jnp.exp(-m_t))

    w_end = b_end - b_col + ig_col
    m_new = jnp.maximum(b_end + m_prev, jnp.max(w_end, axis=0, keepdims=True))
    decay = jnp.exp(b_end + m_prev - m_new)
    kw = k.astype(f32) * jnp.exp(w_end - m_new)
    upd = lax.dot_general(kw.astype(bf16), v, (((0,), (0,)), ((), ())), preferred_element_type=f32)
    c_new = decay * ct_ref[...] + upd
    ct_ref[...] = c_new
    ctb_ref[...] = c_new.astype(bf16)
    n_ref[...] = decay * n_ref[...] + jnp.sum(kw, axis=0, keepdims=True)
    m_ref[...] = jnp.broadcast_to(m_new, m_ref.shape)

    if not finish:
        o_ref[0] = hout.astype(o_ref.dtype)
    else:
        hs = hout + hf_ref[0].astype(f32)
        mu = jnp.mean(hs, axis=1, keepdims=True)
        xc_ = hs - mu
        var = jnp.mean(xc_ * xc_, axis=1, keepdims=True)
        hn = xc_ * lax.rsqrt(var + LN_EPS)
        z = z_ref[0].astype(f32)
        o_ref[0] = ((hn * nw_ref[...] + sk_ref[...] * xc_ref[0].astype(f32))
                    * (z * _sigmoid(z))).astype(o_ref.dtype)


def _mlstm(q, k, v, gates, gates_t, direction, finish_args=None):
    L = MLSTM_CHUNK
    n_chunks = U // L

    if direction == 0:
        def chunk(s):
            return s
    else:
        def chunk(s):
            return jnp.where(s == 0, 0, n_chunks - s)

    tile = pl.BlockSpec((1, L, DH), lambda b, h, s: (b, chunk(s), h))
    in_specs = [tile, tile, tile,
                pl.BlockSpec((1, L, 4 * NH), lambda b, h, s: (b, chunk(s), 0)),
                pl.BlockSpec((1, 4 * NH, L), lambda b, h, s: (b, 0, chunk(s)))]
    args = [q, k, v, gates, gates_t]
    finish = finish_args is not None
    if finish:
        h_fwd, xc, xz, norm_w, skip = finish_args
        vec = pl.BlockSpec((1, DH), lambda b, h, s: (0, h))
        in_specs += [tile, tile,
                     pl.BlockSpec((1, L, DH), lambda b, h, s: (b, chunk(s), NH + h)),
                     vec, vec]
        args += [h_fwd, xc, xz.reshape(BATCH, U, 2 * E), norm_w.reshape(1, E), skip.reshape(1, E)]
    return pl.pallas_call(
        functools.partial(_mlstm_kernel, direction=direction, finish=finish),
        grid=(BATCH, NH, n_chunks),
        in_specs=in_specs,
        out_specs=tile,
        out_shape=jax.ShapeDtypeStruct((BATCH, U, E), bf16),
        scratch_shapes=[pltpu.VMEM((DH, DH), f32), pltpu.VMEM((DH, DH), bf16),
                        pltpu.VMEM((1, DH), f32), pltpu.VMEM((1, 128), f32)],
        compiler_params=_cparams(3),
        name="mlstm_bwd_finish" if finish else "mlstm_fwd",
    )(*args)


def _ln_rows(r, g, b):
    mu = jnp.mean(r, axis=1, keepdims=True)
    rc = r - mu
    var = jnp.mean(rc * rc, axis=1, keepdims=True)
    return rc * lax.rsqrt(var + LN_EPS) * g + b


def _ln_router_kernel(hs_ref, o_ref, m_ref, g_ref, b_ref, rw_ref, rb_ref,
                      hs1_ref, tok_ref, lg_ref):
    m = m_ref[0]
    r = ALPHA * hs_ref[...] + m[2:3] * o_ref[...].astype(f32)
    hs1 = _ln_rows(r, g_ref[...], b_ref[...])
    hs1_ref[...] = hs1
    tok = hs1 * (1.0 + m[4:5]) + m[3:4]
    t_hi = tok.astype(bf16)
    _store_tile_rows(tok_ref, _pack_halves(tok))
    t_lo = (tok - t_hi.astype(f32)).astype(bf16)
    w = rw_ref[...]
    w_hi = w.astype(bf16)
    w_lo = (w - w_hi.astype(f32)).astype(bf16)
    lg_ref[...] = _dot(t_hi, w_hi) + _dot(t_hi, w_lo) + _dot(t_lo, w_hi) + rb_ref[...]


def _ln_router(hs, o, mt, group, ln_g, ln_b, r_w, r_b):
    m = hs.shape[0]
    row = pl.BlockSpec((ROW_TILE, D), lambda i: (i, 0))
    vec = pl.BlockSpec((1, D), lambda i: (0, 0))
    return pl.pallas_call(
        _ln_router_kernel,
        grid=(m // ROW_TILE,),
        in_specs=[row, row,
                  pl.BlockSpec((1, 6, D), lambda i: (group(i), 0, 0)),
                  vec, vec,
                  pl.BlockSpec((D, NE), lambda i: (0, 0)),
                  pl.BlockSpec((1, NE), lambda i: (0, 0))],
        out_specs=[row, pl.BlockSpec((ROW_TILE * TILE_ROWS, LANES), lambda i: (i, 0)),
                   pl.BlockSpec((ROW_TILE, NE), lambda i: (i, 0))],
        out_shape=[jax.ShapeDtypeStruct((m, D), f32),
                   jax.ShapeDtypeStruct((m * TILE_ROWS, LANES), jnp.uint32),
                   jax.ShapeDtypeStruct((m, NE), f32)],
        compiler_params=_cparams(1),
        name="ln_router",
    )(hs, o, mt, ln_g.reshape(1, D), ln_b.reshape(1, D), r_w, r_b.reshape(1, NE))


def _first_of_expert(be_ref, i):
    return jnp.logical_or(i == 0, be_ref[i] != be_ref[jnp.maximum(i - 1, 0)])


def _moe_up_kernel(be_ref, nu_ref, nxt_ref, x_ref, w_hbm, bg_ref, bl_ref, o_ref, stage_ref, cache_ref, sem,
                   *, layer, tf):
    j = pl.program_id(0)
    i = pl.program_id(1)
    valid = i < nu_ref[0]

    def tile_copy(e, jj, half):
        col = pl.multiple_of(half * F + jj * tf, tf)
        return pltpu.make_async_copy(w_hbm.at[layer, e, :, pl.ds(col, tf)], stage_ref.at[half], sem.at[half])

    def start(e, jj):
        tile_copy(e, jj, 0).start()
        tile_copy(e, jj, 1).start()

    @pl.when(jnp.logical_and(j == 0, i == 0))
    def _():
        start(be_ref[0], 0)

    @pl.when(jnp.logical_and(valid, _first_of_expert(be_ref, i)))
    def _():
        tile_copy(0, 0, 0).wait()
        tile_copy(0, 0, 1).wait()
        cache_ref[...] = stage_ref[...].astype(bf16)
        nxt = nxt_ref[i]

        @pl.when(nxt >= 0)
        def _():
            start(nxt, j)

        @pl.when(jnp.logical_and(nxt < 0, j + 1 < pl.num_programs(0)))
        def _():
            start(be_ref[0], j + 1)

    @pl.when(valid)
    def _():
        x = _unpack_halves(_load_tile_rows(x_ref, MOE_TM)).astype(bf16)
        glu = jnp.minimum(_dot(x, cache_ref[0]) + bg_ref[0, 0], SWIGLU_LIMIT)
        lin = jnp.clip(_dot(x, cache_ref[1]) + bl_ref[0, 0], -SWIGLU_LIMIT, SWIGLU_LIMIT)
        o_ref[...] = (glu * _sigmoid(SWIGLU_ALPHA * glu) * (lin + 1.0)).astype(o_ref.dtype)

    @pl.when(jnp.logical_not(valid))
    def _():
        o_ref[...] = jnp.zeros_like(o_ref)


def _moe_down_kernel(be_ref, nu_ref, nxt_ref, a_ref, w_hbm, b_ref, o_ref, stage_ref, cache_ref, sem, *, layer):
    i = pl.program_id(0)
    valid = i < nu_ref[0]

    def expert_copy(e):
        return pltpu.make_async_copy(w_hbm.at[layer, e], stage_ref, sem)

    @pl.when(i == 0)
    def _():
        expert_copy(be_ref[0]).start()

    @pl.when(jnp.logical_and(valid, _first_of_expert(be_ref, i)))
    def _():
        expert_copy(0).wait()
        cache_ref[...] = stage_ref[...].astype(bf16)
        nxt = nxt_ref[i]

        @pl.when(nxt >= 0)
        def _():
            expert_copy(nxt).start()

    @pl.when(valid)
    def _():
        y = _dot(a_ref[...], cache_ref[...]) + b_ref[0, 0]
        _store_tile_rows(o_ref, _pack_halves(y))

    @pl.when(jnp.logical_not(valid))
    def _():
        o_ref[...] = jnp.zeros_like(o_ref)


def _moe_experts(layer, xs, block_e, n_used, next_e, w1, b1, w2, b2):
    ns = xs.shape[0] // TILE_ROWS
    nb = ns // MOE_TM
    tf = 1024
    lin0 = F // tf
    b1 = b1.reshape(DEPTH, NE, 1, 2 * F)
    b2 = b2.reshape(DEPTH, NE, 1, D)

    def bspec(col0):
        return pl.BlockSpec((1, 1, 1, tf), lambda j, i, be, nu, nx: (layer, be[i], 0, col0 + j))

    act = pl.pallas_call(
        functools.partial(_moe_up_kernel, layer=layer, tf=tf),
        grid_spec=pltpu.PrefetchScalarGridSpec(
            num_scalar_prefetch=3,
            grid=(F // tf, nb),
            in_specs=[pl.BlockSpec((MOE_TM * TILE_ROWS, LANES),
                                   lambda j, i, be, nu, nx: (jnp.minimum(i, nu[0] - 1), 0)),
                      pl.BlockSpec(memory_space=pl.ANY), bspec(0), bspec(lin0)],
            out_specs=pl.BlockSpec((MOE_TM, tf), lambda j, i, be, nu, nx: (i, j)),
            scratch_shapes=[pltpu.VMEM((2, D, tf), f32), pltpu.VMEM((2, D, tf), bf16),
                            pltpu.SemaphoreType.DMA((2,))]),
        out_shape=jax.ShapeDtypeStruct((ns, F), bf16),
        compiler_params=_cparams(2),
        name="moe_up",
    )(block_e, n_used, next_e, xs, w1, b1, b1)

    return pl.pallas_call(
        functools.partial(_moe_down_kernel, layer=layer),
        grid_spec=pltpu.PrefetchScalarGridSpec(
            num_scalar_prefetch=3,
            grid=(nb,),
            in_specs=[pl.BlockSpec((MOE_TM, F), lambda i, be, nu, nx: (jnp.minimum(i, nu[0] - 1), 0)),
                      pl.BlockSpec(memory_space=pl.ANY),
                      pl.BlockSpec((1, 1, 1, D), lambda i, be, nu, nx: (layer, be[i], 0, 0))],
            out_specs=pl.BlockSpec((MOE_TM * TILE_ROWS, LANES), lambda i, be, nu, nx: (i, 0)),
            scratch_shapes=[pltpu.VMEM((F, D), f32), pltpu.VMEM((F, D), bf16), pltpu.SemaphoreType.DMA]),
        out_shape=jax.ShapeDtypeStruct((ns * TILE_ROWS, LANES), jnp.uint32),
        compiler_params=_cparams(1),
        name="moe_down",
    )(block_e, n_used, next_e, act, w2, b2)


GATHER_ROWS = 2048


def _gather_kernel(idx_ref, src_ref, out_ref, sem):
    n = GATHER_ROWS
    t = TILE_ROWS

    def issue(r, carry):
        src_row = pl.multiple_of(idx_ref[0, 0, r] * t, t)
        dst_row = pl.multiple_of(r * t, t)
        pltpu.make_async_copy(src_ref.at[pl.ds(src_row, t)], out_ref.at[pl.ds(dst_row, t)], sem).start()
        return carry

    lax.fori_loop(0, n, issue, 0, unroll=8)
    pltpu.make_async_copy(src_ref.at[pl.ds(0, n * t)], out_ref, sem).wait()


def _gather_rows(src, idx):
    n = idx.shape[0]
    nblk = n // GATHER_ROWS
    return pl.pallas_call(
        _gather_kernel,
        grid=(nblk,),
        in_specs=[pl.BlockSpec((1, 1, GATHER_ROWS), lambda i: (i, 0, 0), memory_space=pltpu.SMEM),
                  pl.BlockSpec(memory_space=pl.ANY)],
        out_specs=pl.BlockSpec((GATHER_ROWS * TILE_ROWS, LANES), lambda i: (i, 0)),
        out_shape=jax.ShapeDtypeStruct((n * TILE_ROWS, LANES), src.dtype),
        scratch_shapes=[pltpu.SemaphoreType.DMA],
        compiler_params=pltpu.CompilerParams(dimension_semantics=("arbitrary",),
                                             vmem_limit_bytes=VMEM_LIMIT,
                                             disable_bounds_checks=True),
        name="gather_rows",
    )(idx.reshape(nblk, 1, GATHER_ROWS), src)


def _route(logits):
    t = logits.shape[0]
    n_assign = t * TOP_K
    nb = -(-n_assign // MOE_TM) + NE
    top_v, top_i = lax.top_k(logits, TOP_K)
    gate = jax.nn.softmax(top_v, axis=-1)
    flat_e = top_i.reshape(-1)
    onehot = (flat_e[:, None] == jnp.arange(NE, dtype=flat_e.dtype)[None, :]).astype(jnp.int32)
    csum = jnp.cumsum(onehot, axis=0)
    counts = csum[-1]
    rank = jnp.take_along_axis(csum, flat_e[:, None], axis=1)[:, 0] - 1
    pcounts = (counts + MOE_TM - 1) // MOE_TM * MOE_TM
    pends = jnp.cumsum(pcounts)
    pstarts = pends - pcounts
    dest = (pstarts[flat_e] + rank).astype(jnp.int32)
    slot_tok = (jnp.arange(nb * MOE_TM, dtype=jnp.int32) % t).at[dest].set(
        jnp.arange(n_assign, dtype=jnp.int32) // TOP_K)
    n_used = (pends[-1] // MOE_TM).astype(jnp.int32)
    blk = jnp.arange(nb, dtype=jnp.int32)
    block_e = jnp.clip(jnp.searchsorted(pends, blk * MOE_TM, side="right"), 0, NE - 1).astype(jnp.int32)
    group_end = (pends // MOE_TM).astype(jnp.int32)[block_e]
    next_e = jnp.where(group_end < n_used, block_e[jnp.minimum(group_end, nb - 1)], -1).astype(jnp.int32)
    block_e = jnp.where(blk < n_used, block_e, block_e[n_used - 1])
    return gate, dest, slot_tok, block_e, n_used.reshape(1), next_e


def _combine_kernel(hs_ref, y_ref, gate_ref, m_ref, g_ref, b_ref, *rest, with_next):
    if with_next:
        mn_ref, hs2_ref, nxt_ref = rest
    else:
        (hs2_ref,) = rest
    m = m_ref[0]
    gate = gate_ref[...]
    y = gate[:, 0:1] * _unpack_halves(_load_tile_rows(y_ref.at[0], ROW_TILE))
    for k in range(1, TOP_K):
        y = y + gate[:, k:k + 1] * _unpack_halves(_load_tile_rows(y_ref.at[k], ROW_TILE))
    hs2 = _ln_rows(ALPHA * hs_ref[...] + m[5:6] * y, g_ref[...], b_ref[...])
    hs2_ref[...] = hs2
    if with_next:
        mn = mn_ref[0]
        nxt_ref[...] = (hs2 * (1.0 + mn[1:2]) + mn[0:1]).astype(nxt_ref.dtype)


def _combine(hs1, yg, gate, mt, group, ln_g, ln_b, mt_next=None):
    m = hs1.shape[0]
    row = pl.BlockSpec((ROW_TILE, D), lambda i: (i, 0))
    vec = pl.BlockSpec((1, D), lambda i: (0, 0))
    mod = pl.BlockSpec((1, 6, D), lambda i: (group(i), 0, 0))
    in_specs = [row, pl.BlockSpec((TOP_K, ROW_TILE * TILE_ROWS, LANES), lambda i: (0, i, 0)),
                pl.BlockSpec((ROW_TILE, TOP_K), lambda i: (i, 0)), mod, vec, vec]
    args = [hs1, yg, gate, mt, ln_g.reshape(1, D), ln_b.reshape(1, D)]
    out_specs = [row]
    out_shape = [jax.ShapeDtypeStruct((m, D), f32)]
    with_next = mt_next is not None
    if with_next:
        in_specs.append(mod)
        args.append(mt_next)
        out_specs.append(row)
        out_shape.append(jax.ShapeDtypeStruct((m, D), bf16))
    return pl.pallas_call(
        functools.partial(_combine_kernel, with_next=with_next),
        grid=(m // ROW_TILE,),
        in_specs=in_specs, out_specs=out_specs, out_shape=out_shape,
        compiler_params=_cparams(1),
        name="moe_combine_ln",
    )(*args)


def _moe_layer(layer, hs, o, mt, group, ln1_g, ln1_b, ln2_g, ln2_b, r_w, r_b, w1, b1, w2, b2, mt_next=None):
    t = hs.shape[0]
    hs1, tok, logits = _ln_router(hs, o, mt, group, ln1_g[layer], ln1_b[layer], r_w[layer], r_b[layer])
    gate, dest, slot_tok, block_e, n_used, next_e = _route(logits)
    xs = _gather_rows(tok, slot_tok)
    y = _moe_experts(layer, xs, block_e, n_used, next_e, w1, b1, w2, b2)
    yg = _gather_rows(y, dest.reshape(t, TOP_K).T.reshape(-1)).reshape(TOP_K, t * TILE_ROWS, LANES)
    return _combine(hs1, yg, gate, mt, group, ln2_g[layer], ln2_b[layer], mt_next)


def _sgu_kernel(g_ref, v_ref, lg_ref, lb_ref, ws_ref, bs_ref, o_ref):
    v = v_ref[...].astype(f32)
    vn = _ln_rows(v, lg_ref[...], lb_ref[...]).astype(bf16)
    gw = E // SGU_GROUPS
    for g in range(SGU_GROUPS):
        mixed = _dot(ws_ref[g].astype(bf16), vn[:, g * gw:(g + 1) * gw]) + bs_ref[:, g:g + 1]
        o_ref[:, g * gw:(g + 1) * gw] = (g_ref[:, g * gw:(g + 1) * gw].astype(f32) * mixed).astype(o_ref.dtype)


def _sgu(uv, ln_g, ln_b, w_s, b_s):
    m = uv.shape[0]
    c = SGU_CHUNK
    vec = pl.BlockSpec((1, E), lambda i: (0, 0))
    return pl.pallas_call(
        _sgu_kernel,
        grid=(m // c,),
        in_specs=[pl.BlockSpec((c, E), lambda i: (i, 0)),
                  pl.BlockSpec((c, E), lambda i: (i, 1)),
                  vec, vec,
                  pl.BlockSpec((SGU_GROUPS, c, c), lambda i: (0, 0, 0)),
                  pl.BlockSpec((c, SGU_GROUPS), lambda i: (0, 0))],
        out_specs=pl.BlockSpec((c, E), lambda i: (i, 0)),
        out_shape=jax.ShapeDtypeStruct((m, E), bf16),
        compiler_params=_cparams(1),
        name="sgu",
    )(uv, uv, ln_g.reshape(1, E), ln_b.reshape(1, E), w_s, b_s.T)


def kernel(x, c, ctx, c_ctx, mod_w, mod_b, ln1_g, ln1_b, ln2_g, ln2_b, a_w_in, a_conv_w, a_conv_b, a_w_q, a_w_k, a_w_v, a_w_gate, a_b_gate, a_norm_w, a_skip, a_w_out, b_w_in, b_ln_g, b_ln_b, b_w_s, b_b_s, b_w_out, r_w, r_b, e_w1, e_b1, e_w2, e_b2):
    cvec = jnp.concatenate([c, c_ctx[None, :], jnp.zeros((8 - BATCH - 1, D), f32)], axis=0)
    mods = _mod_rows(cvec, mod_w, mod_b).reshape(DEPTH, 8, 6, D)
    rows = [r for b in range(BATCH) for r in (BATCH, b)]
    mt = [jnp.stack([mods[l, r] for r in rows], axis=0) for l in range(DEPTH)]

    hs = jnp.concatenate([ctx, x], axis=1).reshape(BATCH * U, D)
    hx = _modulate(hs, mt[0], _group_joint)
    xz = _matmul(hx, a_w_in[0], tm=1024, tn=1024, name="mlstm_in_proj")
    xc, q, k, v, gates = _conv_qkv(xz, a_conv_w[0], a_conv_b[0], a_w_q[0], a_w_k[0], a_w_v[0],
                                   a_w_gate[0], a_b_gate[0])
    gates_t = gates.transpose(0, 2, 1)
    h_fwd = _mlstm(q, k, v, gates, gates_t, 0)
    pre = _mlstm(q, k, v, gates, gates_t, 1, (h_fwd, xc, xz, a_norm_w[0], a_skip[0]))
    o = _matmul(pre.reshape(BATCH * U, E), a_w_out[0], tm=1024, tn=512, name="mlstm_out_proj")
    hs2, hx1 = _moe_layer(0, hs, o, mt[0], _group_joint, ln1_g, ln1_b, ln2_g, ln2_b,
                          r_w, r_b, e_w1, e_b1, e_w2, e_b2, mt_next=mt[1])

    hs = hs2.reshape(BATCH, U, D)[:, LC:].reshape(BATCH * SEQ, D)
    hx = hx1.reshape(BATCH, U, D)[:, LC:].reshape(BATCH * SEQ, D)
    uv = _matmul(hx, b_w_in[0], tm=1024, tn=1024, act="gelu", name="sgu_in_proj")
    gated = _sgu(uv, b_ln_g[0], b_ln_b[0], b_w_s[0], b_b_s[0])
    o = _matmul(gated, b_w_out[0], tm=1024, tn=512, name="sgu_out_proj")
    (out,) = _moe_layer(1, hs, o, mt[1], _group_latent, ln1_g, ln1_b, ln2_g, ln2_b,
                        r_w, r_b, e_w1, e_b1, e_w2, e_b2)
    return out.reshape(BATCH, SEQ, D)
```

```python
import functools

import jax
import jax.numpy as jnp
from jax import lax
from jax.experimental import pallas as pl
from jax.experimental.pallas import tpu as pltpu

D = 2048
BATCH = 4
SEQ = 2048
DEPTH = 2
GRID_W = 64
LC = 256
U = LC + SEQ
E = 2 * D
NH = 4
DH = E // NH
QKV_BLOCK = 4
SGU_GROUPS = 8
SGU_CHUNK = 128
NE = 32
TOP_K = 4
F = D
SWIGLU_LIMIT = 7.0
SWIGLU_ALPHA = 1.702
ALPHA = (2 * DEPTH) ** 0.25
LN_EPS = 1e-5

ROW_TILE = 256
MLSTM_CHUNK = 256
MOE_TM = 512
BD_TILE = 256
VMEM_LIMIT = 56 * 1024 * 1024

f32 = jnp.float32
bf16 = jnp.bfloat16


def _cparams(n_axes):
    return pltpu.CompilerParams(dimension_semantics=("arbitrary",) * n_axes,
                                vmem_limit_bytes=VMEM_LIMIT)


def _dot(a, b):
    return jnp.dot(a, b, preferred_element_type=f32)


def _sigmoid(x):
    return 1.0 / (1.0 + jnp.exp(-x))


def _pack_halves(x):
    w = x.shape[1] // 2
    bits = lax.bitcast_convert_type(x.astype(bf16).astype(f32), jnp.uint32)
    return (bits[:, w:] & jnp.uint32(0xFFFF0000)) | (bits[:, :w] >> 16)


def _unpack_halves(p):
    lo = lax.bitcast_convert_type(p << 16, f32)
    hi = lax.bitcast_convert_type(p & jnp.uint32(0xFFFF0000), f32)
    return jnp.concatenate([lo, hi], axis=1)


PACKED = D // 2
LANES = 128
TILE_ROWS = PACKED // LANES


def _store_tile_rows(ref, p):
    r = p.shape[0]
    for s in range(TILE_ROWS):
        ref[pl.ds(s, r, stride=TILE_ROWS), :] = p[:, s * LANES:(s + 1) * LANES]


def _load_tile_rows(ref, r):
    return jnp.concatenate([ref[pl.ds(s, r, stride=TILE_ROWS), :] for s in range(TILE_ROWS)], axis=1)


def _mod_kernel(c_ref, w_ref, b_ref, o_ref):
    c = c_ref[...]
    a = (c * _sigmoid(c)).astype(bf16)
    o_ref[0] = _dot(a, w_ref[0].astype(bf16)) + b_ref[0]


def _mod_rows(cvec, mod_w, mod_b):
    tn = 1024
    return pl.pallas_call(
        _mod_kernel,
        grid=(DEPTH, 6 * D // tn),
        in_specs=[pl.BlockSpec((8, D), lambda l, j: (0, 0)),
                  pl.BlockSpec((1, D, tn), lambda l, j: (l, 0, j)),
                  pl.BlockSpec((1, 1, tn), lambda l, j: (l, 0, j))],
        out_specs=pl.BlockSpec((1, 8, tn), lambda l, j: (l, 0, j)),
        out_shape=jax.ShapeDtypeStruct((DEPTH, 8, 6 * D), f32),
        compiler_params=_cparams(2),
        name="adaln_rows",
    )(cvec, mod_w, mod_b.reshape(DEPTH, 1, 6 * D))


def _group_joint(i):
    tiles = U // ROW_TILE
    return 2 * (i // tiles) + jnp.minimum(i % tiles, 1)


def _group_latent(i):
    return 2 * (i // (SEQ // ROW_TILE)) + 1


def _modulate_kernel(x_ref, m_ref, o_ref):
    m = m_ref[0]
    o_ref[...] = (x_ref[...] * (1.0 + m[1:2]) + m[0:1]).astype(o_ref.dtype)


def _modulate(x2d, mt, group):
    m = x2d.shape[0]
    return pl.pallas_call(
        _modulate_kernel,
        grid=(m // ROW_TILE,),
        in_specs=[pl.BlockSpec((ROW_TILE, D), lambda i: (i, 0)),
                  pl.BlockSpec((1, 6, D), lambda i: (group(i), 0, 0))],
        out_specs=pl.BlockSpec((ROW_TILE, D), lambda i: (i, 0)),
        out_shape=jax.ShapeDtypeStruct((m, D), bf16),
        compiler_params=_cparams(1),
        name="modulate",
    )(x2d, mt)


def _erf(x):
    return lax.erf(x)


def _mm_kernel(x_ref, w_ref, o_ref, wb_ref, *, act):
    @pl.when(pl.program_id(1) == 0)
    def _():
        wb_ref[...] = w_ref[...].astype(bf16)

    acc = _dot(x_ref[...], wb_ref[...])
    if act == "gelu":
        acc = 0.5 * acc * (1.0 + _erf(acc * (2.0 ** -0.5)))
    o_ref[...] = acc.astype(o_ref.dtype)


def _matmul(x, w, *, tm, tn, act=None, name):
    m, k = x.shape
    n = w.shape[1]
    return pl.pallas_call(
        functools.partial(_mm_kernel, act=act),
        grid=(n // tn, m // tm),
        in_specs=[pl.BlockSpec((tm, k), lambda j, i: (i, 0)),
                  pl.BlockSpec((k, tn), lambda j, i: (0, j))],
        out_specs=pl.BlockSpec((tm, tn), lambda j, i: (i, j)),
        out_shape=jax.ShapeDtypeStruct((m, n), bf16),
        scratch_shapes=[pltpu.VMEM((k, tn), bf16)],
        compiler_params=_cparams(2),
        name=name,
    )(x, w)


def _conv_qkv_kernel(xm_ref, cw_ref, cb_ref, wq_ref, wk_ref, wv_ref, wg_ref, bg_ref,
                     xc_ref, q_ref, k_ref, v_ref, g_ref):
    ct = pl.program_id(1)
    a = xm_ref[0].astype(f32)
    c = a.shape[1]
    r = lax.broadcasted_iota(jnp.int32, (U, c), 0)
    latent = r >= LC
    p = r - LC
    col = jnp.where(latent, p & (GRID_W - 1), r)
    last = jnp.where(latent, GRID_W - 1, LC - 1)
    a_l = jnp.where(col > 0, pltpu.roll(a, 1, 0), 0.0)
    a_r = jnp.where(col < last, pltpu.roll(a, U - 1, 0), 0.0)
    w = cw_ref[...]
    rows = [w[3 * i:3 * i + 1] * a_l + w[3 * i + 1:3 * i + 2] * a + w[3 * i + 2:3 * i + 3] * a_r
            for i in range(3)]
    up = jnp.where(p >= GRID_W, pltpu.roll(rows[0], GRID_W, 0), 0.0)
    down = jnp.where(latent & (p < SEQ - GRID_W), pltpu.roll(rows[2], U - GRID_W, 0), 0.0)
    pre = rows[1] + up + down + cb_ref[...]
    xc = (pre * _sigmoid(pre)).astype(bf16)
    xc_ref[0] = xc
    xm = xm_ref[0]
    q = _dot(xc, wq_ref[0]).astype(bf16)
    k = _dot(xc, wk_ref[0]).astype(bf16)
    v = _dot(xm, wv_ref[0]).astype(bf16)
    q_ref[0] = q
    k_ref[0] = k
    v_ref[0] = v
    g = (_dot(q, wg_ref[0].astype(bf16)) + _dot(k, wg_ref[1].astype(bf16))
         + _dot(v, wg_ref[2].astype(bf16)))

    @pl.when(ct == 0)
    def _():
        g_ref[0] = g + bg_ref[...]

    @pl.when(ct > 0)
    def _():
        g_ref[0] += g


def _block_diag(w):
    per = BD_TILE // QKV_BLOCK
    wt = w.reshape(E // BD_TILE, per, QKV_BLOCK, QKV_BLOCK)
    eye = jnp.eye(per, dtype=w.dtype)
    full = jnp.einsum("tgio,gh->tgiho", wt, eye)
    return full.reshape(E // BD_TILE, BD_TILE, BD_TILE).astype(bf16)


def _conv_qkv(xz, conv_w, conv_b, w_q, w_k, w_v, w_gate, b_gate):
    c = BD_TILE
    n_gate = 4 * NH
    act = jax.ShapeDtypeStruct((BATCH, U, E), bf16)
    tile = pl.BlockSpec((1, U, c), lambda b, t: (b, 0, t))
    bd = pl.BlockSpec((1, c, c), lambda b, t: (t, 0, 0))
    return pl.pallas_call(
        _conv_qkv_kernel,
        grid=(BATCH, E // c),
        in_specs=[tile,
                  pl.BlockSpec((9, c), lambda b, t: (0, t)),
                  pl.BlockSpec((1, c), lambda b, t: (0, t)),
                  bd, bd, bd,
                  pl.BlockSpec((3, c, n_gate), lambda b, t: (0, t, 0)),
                  pl.BlockSpec((1, n_gate), lambda b, t: (0, 0))],
        out_specs=[tile, tile, tile, tile,
                   pl.BlockSpec((1, U, n_gate), lambda b, t: (b, 0, 0))],
        out_shape=[act, act, act, act, jax.ShapeDtypeStruct((BATCH, U, n_gate), f32)],
        compiler_params=_cparams(2),
        name="conv_qkv_gates",
    )(xz.reshape(BATCH, U, 2 * E), conv_w.reshape(9, E), conv_b.reshape(1, E),
      _block_diag(w_q), _block_diag(w_k), _block_diag(w_v),
      w_gate.reshape(3, E, n_gate), b_gate.reshape(1, n_gate))


def _log_sigmoid(x):
    return jnp.minimum(x, 0.0) - jnp.log1p(jnp.exp(-jnp.abs(x)))


def _mlstm_kernel(*refs, direction, finish):
    if finish:
        (q_ref, k_ref, v_ref, gc_ref, gr_ref, hf_ref, xc_ref, z_ref, nw_ref, sk_ref,
         o_ref, ct_ref, ctb_ref, n_ref, m_ref) = refs
    else:
        q_ref, k_ref, v_ref, gc_ref, gr_ref, o_ref, ct_ref, ctb_ref, n_ref, m_ref = refs
    L = MLSTM_CHUNK
    h = pl.program_id(1)
    step = pl.program_id(2)

    @pl.when(step == 0)
    def _():
        ct_ref[...] = jnp.zeros_like(ct_ref)
        ctb_ref[...] = jnp.zeros_like(ctb_ref)
        n_ref[...] = jnp.zeros_like(n_ref)
        m_ref[...] = jnp.full_like(m_ref, -jnp.inf)

    i_idx = 2 * direction * NH + h
    f_idx = (2 * direction + 1) * NH + h
    gc = gc_ref[0]
    gr = gr_ref[0]
    lane = lax.broadcasted_iota(jnp.int32, gc.shape, 1)
    sub = lax.broadcasted_iota(jnp.int32, gr.shape, 0)
    ig_col = jnp.sum(jnp.where(lane == i_idx, gc, 0.0), axis=1, keepdims=True)
    f_col = jnp.sum(jnp.where(lane == f_idx, gc, 0.0), axis=1, keepdims=True)
    ig_row = jnp.sum(jnp.where(sub == i_idx, gr, 0.0), axis=0, keepdims=True)
    f_row = jnp.sum(jnp.where(sub == f_idx, gr, 0.0), axis=0, keepdims=True)
    lf_col = _log_sigmoid(f_col)
    lf_row = _log_sigmoid(f_row)

    t_i = lax.broadcasted_iota(jnp.int32, (L, L), 0)
    s_i = lax.broadcasted_iota(jnp.int32, (L, L), 1)
    seen = (s_i <= t_i) if direction == 0 else (s_i >= t_i)
    b_col = jnp.sum(jnp.where(seen, lf_row, 0.0), axis=1, keepdims=True)
    seen_t = (t_i <= s_i) if direction == 0 else (t_i >= s_i)
    b_row = jnp.sum(jnp.where(seen_t, lf_col, 0.0), axis=0, keepdims=True)
    b_end = jnp.sum(lf_row, axis=1, keepdims=True)

    m_prev = m_ref[:, 0:1]
    log_d = jnp.where(seen, b_col - b_row + ig_row, -jnp.inf)
    g_col = b_col + m_prev
    m_t = jnp.maximum(g_col, jnp.max(log_d, axis=1, keepdims=True))
    dw = jnp.exp(log_d - m_t)
    inter = jnp.exp(g_col - m_t)

    q = q_ref[0]
    k = k_ref[0] * (DH ** -0.5)
    v = v_ref[0]
    s = lax.dot_general(q, k, (((1,), (1,)), ((), ())), preferred_element_type=f32) * dw
    num = _dot(s.astype(bf16), v) + inter * _dot(q, ctb_ref[...])
    qn = jnp.sum(q.astype(f32) * n_ref[...], axis=1, keepdims=True)
    den = jnp.sum(s, axis=1, keepdims=True) + inter * qn
    hout = num / jnp.maximum(jnp.abs(den), jnp.exp(-m_t))

    w_end = b_end - b_col + ig_col
    m_new = jnp.maximum(b_end + m_prev, jnp.max(w_end, axis=0, keepdims=True))
    decay = jnp.exp(b_end + m_prev - m_new)
    kw = k.astype(f32) * jnp.exp(w_end - m_new)
    upd = lax.dot_general(kw.astype(bf16), v, (((0,), (0,)), ((), ())), preferred_element_type=f32)
    c_new = decay * ct_ref[...] + upd
    ct_ref[...] = c_new
    ctb_ref[...] = c_new.astype(bf16)
    n_ref[...] = decay * n_ref[...] + jnp.sum(kw, axis=0, keepdims=True)
    m_ref[...] = jnp.broadcast_to(m_new, m_ref.shape)

    if not finish:
        o_ref[0] = hout.astype(o_ref.dtype)
    else:
        hs = hout + hf_ref[0].astype(f32)
        mu = jnp.mean(hs, axis=1, keepdims=True)
        xc_ = hs - mu
        var = jnp.mean(xc_ * xc_, axis=1, keepdims=True)
        hn = xc_ * lax.rsqrt(var + LN_EPS)
        z = z_ref[0].astype(f32)
        o_ref[0] = ((hn * nw_ref[...] + sk_ref[...] * xc_ref[0].astype(f32))
                    * (z * _sigmoid(z))).astype(o_ref.dtype)


def _mlstm(q, k, v, gates, gates_t, direction, finish_args=None):
    L = MLSTM_CHUNK
    n_chunks = U // L

    if direction == 0:
        def chunk(s):
            return s
    else:
        def chunk(s):
            return jnp.where(s == 0, 0, n_chunks - s)

    tile = pl.BlockSpec((1, L, DH), lambda b, h, s: (b, chunk(s), h))
    in_specs = [tile, tile, tile,
                pl.BlockSpec((1, L, 4 * NH), lambda b, h, s: (b, chunk(s), 0)),
                pl.BlockSpec((1, 4 * NH, L), lambda b, h, s: (b, 0, chunk(s)))]
    args = [q, k, v, gates, gates_t]
    finish = finish_args is not None
    if finish:
        h_fwd, xc, xz, norm_w, skip = finish_args
        vec = pl.BlockSpec((1, DH), lambda b, h, s: (0, h))
        in_specs += [tile, tile,
                     pl.BlockSpec((1, L, DH), lambda b, h, s: (b, chunk(s), NH + h)),
                     vec, vec]
        args += [h_fwd, xc, xz.reshape(BATCH, U, 2 * E), norm_w.reshape(1, E), skip.reshape(1, E)]
    return pl.pallas_call(
        functools.partial(_mlstm_kernel, direction=direction, finish=finish),
        grid=(BATCH, NH, n_chunks),
        in_specs=in_specs,
        out_specs=tile,
        out_shape=jax.ShapeDtypeStruct((BATCH, U, E), bf16),
        scratch_shapes=[pltpu.VMEM((DH, DH), f32), pltpu.VMEM((DH, DH), bf16),
                        pltpu.VMEM((1, DH), f32), pltpu.VMEM((1, 128), f32)],
        compiler_params=_cparams(3),
        name="mlstm_bwd_finish" if finish else "mlstm_fwd",
    )(*args)


def _ln_rows(r, g, b):
    mu = jnp.mean(r, axis=1, keepdims=True)
    rc = r - mu
    var = jnp.mean(rc * rc, axis=1, keepdims=True)
    return rc * lax.rsqrt(var + LN_EPS) * g + b


def _ln_router_kernel(hs_ref, o_ref, m_ref, g_ref, b_ref, rw_ref, rb_ref,
                      hs1_ref, tok_ref, lg_ref):
    m = m_ref[0]
    r = ALPHA * hs_ref[...] + m[2:3] * o_ref[...].astype(f32)
    hs1 = _ln_rows(r, g_ref[...], b_ref[...])
    hs1_ref[...] = hs1
    tok = hs1 * (1.0 + m[4:5]) + m[3:4]
    t_hi = tok.astype(bf16)
    _store_tile_rows(tok_ref, _pack_halves(tok))
    t_lo = (tok - t_hi.astype(f32)).astype(bf16)
    w = rw_ref[...]
    w_hi = w.astype(bf16)
    w_lo = (w - w_hi.astype(f32)).astype(bf16)
    lg_ref[...] = _dot(t_hi, w_hi) + _dot(t_hi, w_lo) + _dot(t_lo, w_hi) + rb_ref[...]


def _ln_router(hs, o, mt, group, ln_g, ln_b, r_w, r_b):
    m = hs.shape[0]
    row = pl.BlockSpec((ROW_TILE, D), lambda i: (i, 0))
    vec = pl.BlockSpec((1, D), lambda i: (0, 0))
    return pl.pallas_call(
        _ln_router_kernel,
        grid=(m // ROW_TILE,),
        in_specs=[row, row,
                  pl.BlockSpec((1, 6, D), lambda i: (group(i), 0, 0)),
                  vec, vec,
                  pl.BlockSpec((D, NE), lambda i: (0, 0)),
                  pl.BlockSpec((1, NE), lambda i: (0, 0))],
        out_specs=[row, pl.BlockSpec((ROW_TILE * TILE_ROWS, LANES), lambda i: (i, 0)),
                   pl.BlockSpec((ROW_TILE, NE), lambda i: (i, 0))],
        out_shape=[jax.ShapeDtypeStruct((m, D), f32),
                   jax.ShapeDtypeStruct((m * TILE_ROWS, LANES), jnp.uint32),
                   jax.ShapeDtypeStruct((m, NE), f32)],
        compiler_params=_cparams(1),
        name="ln_router",
    )(hs, o, mt, ln_g.reshape(1, D), ln_b.reshape(1, D), r_w, r_b.reshape(1, NE))


def _first_of_expert(be_ref, i):
    return jnp.logical_or(i == 0, be_ref[i] != be_ref[jnp.maximum(i - 1, 0)])


def _moe_up_kernel(be_ref, nu_ref, nxt_ref, rows_ref, x_ref, w_hbm, bg_ref, bl_ref, o_ref, stage_ref,
                   cache_ref, sem, *, layer, tf):
    j = pl.program_id(0)
    i = pl.program_id(1)
    valid = i < nu_ref[0]

    def tile_copy(e, jj, half):
        col = pl.multiple_of(half * F + jj * tf, tf)
        return pltpu.make_async_copy(w_hbm.at[layer, e, :, pl.ds(col, tf)], stage_ref.at[half], sem.at[half])

    def start(e, jj):
        tile_copy(e, jj, 0).start()
        tile_copy(e, jj, 1).start()

    @pl.when(jnp.logical_and(j == 0, i == 0))
    def _():
        start(be_ref[0], 0)

    @pl.when(jnp.logical_and(valid, _first_of_expert(be_ref, i)))
    def _():
        tile_copy(0, 0, 0).wait()
        tile_copy(0, 0, 1).wait()
        cache_ref[...] = stage_ref[...].astype(bf16)
        nxt = nxt_ref[i]

        @pl.when(nxt >= 0)
        def _():
            start(nxt, j)

        @pl.when(jnp.logical_and(nxt < 0, j + 1 < pl.num_programs(0)))
        def _():
            start(be_ref[0], j + 1)

    def compute(rows):
        x = _unpack_halves(_load_tile_rows(x_ref, rows)).astype(bf16)
        glu = jnp.minimum(_dot(x, cache_ref[0]) + bg_ref[0, 0], SWIGLU_LIMIT)
        lin = jnp.clip(_dot(x, cache_ref[1]) + bl_ref[0, 0], -SWIGLU_LIMIT, SWIGLU_LIMIT)
        o_ref[pl.ds(0, rows), :] = (glu * _sigmoid(SWIGLU_ALPHA * glu) * (lin + 1.0)).astype(o_ref.dtype)
        if rows < MOE_TM:
            o_ref[pl.ds(rows, MOE_TM - rows), :] = jnp.zeros((MOE_TM - rows, tf), o_ref.dtype)

    half = rows_ref[i] <= MOE_TM // 2

    @pl.when(jnp.logical_and(valid, jnp.logical_not(half)))
    def _():
        compute(MOE_TM)

    @pl.when(jnp.logical_and(valid, half))
    def _():
        compute(MOE_TM // 2)

    @pl.when(jnp.logical_not(valid))
    def _():
        o_ref[...] = jnp.zeros_like(o_ref)


def _moe_down_kernel(be_ref, nu_ref, nxt_ref, rows_ref, a_ref, w_hbm, b_ref, o_ref, stage_ref, cache_ref, sem,
                     *, layer):
    i = pl.program_id(0)
    valid = i < nu_ref[0]

    def expert_copy(e):
        return pltpu.make_async_copy(w_hbm.at[layer, e], stage_ref, sem)

    @pl.when(i == 0)
    def _():
        expert_copy(be_ref[0]).start()

    @pl.when(jnp.logical_and(valid, _first_of_expert(be_ref, i)))
    def _():
        expert_copy(0).wait()
        cache_ref[...] = stage_ref[...].astype(bf16)
        nxt = nxt_ref[i]

        @pl.when(nxt >= 0)
        def _():
            expert_copy(nxt).start()

    def compute(rows):
        y = _dot(a_ref[pl.ds(0, rows), :], cache_ref[...]) + b_ref[0, 0]
        _store_tile_rows(o_ref, _pack_halves(y))
        if rows < MOE_TM:
            o_ref[pl.ds(rows * TILE_ROWS, (MOE_TM - rows) * TILE_ROWS), :] = jnp.zeros(
                ((MOE_TM - rows) * TILE_ROWS, LANES), o_ref.dtype)

    half = rows_ref[i] <= MOE_TM // 2

    @pl.when(jnp.logical_and(valid, jnp.logical_not(half)))
    def _():
        compute(MOE_TM)

    @pl.when(jnp.logical_and(valid, half))
    def _():
        compute(MOE_TM // 2)

    @pl.when(jnp.logical_not(valid))
    def _():
        o_ref[...] = jnp.zeros_like(o_ref)


def _moe_experts(layer, xs, block_e, n_used, next_e, block_rows, w1, b1, w2, b2):
    ns = xs.shape[0] // TILE_ROWS
    nb = ns // MOE_TM
    tf = 1024
    lin0 = F // tf
    b1 = b1.reshape(DEPTH, NE, 1, 2 * F)
    b2 = b2.reshape(DEPTH, NE, 1, D)

    def bspec(col0):
        return pl.BlockSpec((1, 1, 1, tf), lambda j, i, be, nu, nx, br: (layer, be[i], 0, col0 + j))

    act = pl.pallas_call(
        functools.partial(_moe_up_kernel, layer=layer, tf=tf),
        grid_spec=pltpu.PrefetchScalarGridSpec(
            num_scalar_prefetch=4,
            grid=(F // tf, nb),
            in_specs=[pl.BlockSpec((MOE_TM * TILE_ROWS, LANES),
                                   lambda j, i, be, nu, nx, br: (jnp.minimum(i, nu[0] - 1), 0)),
                      pl.BlockSpec(memory_space=pl.ANY), bspec(0), bspec(lin0)],
            out_specs=pl.BlockSpec((MOE_TM, tf), lambda j, i, be, nu, nx, br: (i, j)),
            scratch_shapes=[pltpu.VMEM((2, D, tf), f32), pltpu.VMEM((2, D, tf), bf16),
                            pltpu.SemaphoreType.DMA((2,))]),
        out_shape=jax.ShapeDtypeStruct((ns, F), bf16),
        compiler_params=_cparams(2),
        name="moe_up",
    )(block_e, n_used, next_e, block_rows, xs, w1, b1, b1)

    return pl.pallas_call(
        functools.partial(_moe_down_kernel, layer=layer),
        grid_spec=pltpu.PrefetchScalarGridSpec(
            num_scalar_prefetch=4,
            grid=(nb,),
            in_specs=[pl.BlockSpec((MOE_TM, F), lambda i, be, nu, nx, br: (jnp.minimum(i, nu[0] - 1), 0)),
                      pl.BlockSpec(memory_space=pl.ANY),
                      pl.BlockSpec((1, 1, 1, D), lambda i, be, nu, nx, br: (layer, be[i], 0, 0))],
            out_specs=pl.BlockSpec((MOE_TM * TILE_ROWS, LANES), lambda i, be, nu, nx, br: (i, 0)),
            scratch_shapes=[pltpu.VMEM((F, D), f32), pltpu.VMEM((F, D), bf16), pltpu.SemaphoreType.DMA]),
        out_shape=jax.ShapeDtypeStruct((ns * TILE_ROWS, LANES), jnp.uint32),
        compiler_params=_cparams(1),
        name="moe_down",
    )(block_e, n_used, next_e, block_rows, act, w2, b2)


GATHER_ROWS = 2048


def _gather_kernel(idx_ref, src_ref, out_ref, sem):
    n = GATHER_ROWS
    t = TILE_ROWS

    def issue(r, carry):
        src_row = pl.multiple_of(idx_ref[0, 0, r] * t, t)
        dst_row = pl.multiple_of(r * t, t)
        pltpu.make_async_copy(src_ref.at[pl.ds(src_row, t)], out_ref.at[pl.ds(dst_row, t)], sem).start()
        return carry

    lax.fori_loop(0, n, issue, 0, unroll=8)
    pltpu.make_async_copy(src_ref.at[pl.ds(0, n * t)], out_ref, sem).wait()


def _gather_rows(src, idx):
    n = idx.shape[0]
    nblk = n // GATHER_ROWS
    return pl.pallas_call(
        _gather_kernel,
        grid=(nblk,),
        in_specs=[pl.BlockSpec((1, 1, GATHER_ROWS), lambda i: (i, 0, 0), memory_space=pltpu.SMEM),
                  pl.BlockSpec(memory_space=pl.ANY)],
        out_specs=pl.BlockSpec((GATHER_ROWS * TILE_ROWS, LANES), lambda i: (i, 0)),
        out_shape=jax.ShapeDtypeStruct((n * TILE_ROWS, LANES), src.dtype),
        scratch_shapes=[pltpu.SemaphoreType.DMA],
        compiler_params=pltpu.CompilerParams(dimension_semantics=("arbitrary",),
                                             vmem_limit_bytes=VMEM_LIMIT,
                                             disable_bounds_checks=True),
        name="gather_rows",
    )(idx.reshape(nblk, 1, GATHER_ROWS), src)


def _route(logits):
    t = logits.shape[0]
    n_assign = t * TOP_K
    nb = -(-n_assign // MOE_TM) + NE
    top_v, top_i = lax.top_k(logits, TOP_K)
    gate = jax.nn.softmax(top_v, axis=-1)
    flat_e = top_i.reshape(-1)
    onehot = (flat_e[:, None] == jnp.arange(NE, dtype=flat_e.dtype)[None, :]).astype(jnp.int32)
    csum = jnp.cumsum(onehot, axis=0)
    counts = csum[-1]
    rank = jnp.take_along_axis(csum, flat_e[:, None], axis=1)[:, 0] - 1
    pcounts = (counts + MOE_TM - 1) // MOE_TM * MOE_TM
    pends = jnp.cumsum(pcounts)
    pstarts = pends - pcounts
    dest = (pstarts[flat_e] + rank).astype(jnp.int32)
    n_used = (pends[-1] // MOE_TM).astype(jnp.int32)
    blk = jnp.arange(nb, dtype=jnp.int32)
    block_e = jnp.clip(jnp.searchsorted(pends, blk * MOE_TM, side="right"), 0, NE - 1).astype(jnp.int32)
    block_rows = jnp.clip(counts[block_e] - (blk * MOE_TM - pstarts[block_e]), 0, MOE_TM).astype(jnp.int32)
    block_rows = jnp.where(blk < n_used, block_rows, 0)
    order = jnp.argsort(flat_e, stable=True).astype(jnp.int32)
    slot = jnp.arange(nb * MOE_TM, dtype=jnp.int32)
    slot_e = jnp.repeat(block_e, MOE_TM)
    slot_rank = slot - pstarts[slot_e].astype(jnp.int32)
    real = slot_rank < counts[slot_e]
    starts = jnp.cumsum(counts) - counts
    src = order[jnp.clip(starts[slot_e].astype(jnp.int32) + slot_rank, 0, n_assign - 1)] // TOP_K
    slot_tok = jnp.where(real, src, slot % t).astype(jnp.int32)
    group_end = (pends // MOE_TM).astype(jnp.int32)[block_e]
    next_e = jnp.where(group_end < n_used, block_e[jnp.minimum(group_end, nb - 1)], -1).astype(jnp.int32)
    block_e = jnp.where(blk < n_used, block_e, block_e[n_used - 1])
    return gate, dest, slot_tok, block_e, n_used.reshape(1), next_e, block_rows


def _combine_kernel(hs_ref, y_ref, gate_ref, m_ref, g_ref, b_ref, *rest, with_next):
    if with_next:
        mn_ref, hs2_ref, nxt_ref = rest
    else:
        (hs2_ref,) = rest
    m = m_ref[0]
    gate = gate_ref[...]
    y = gate[:, 0:1] * _unpack_halves(_load_tile_rows(y_ref.at[0], ROW_TILE))
    for k in range(1, TOP_K):
        y = y + gate[:, k:k + 1] * _unpack_halves(_load_tile_rows(y_ref.at[k], ROW_TILE))
    hs2 = _ln_rows(ALPHA * hs_ref[...] + m[5:6] * y, g_ref[...], b_ref[...])
    hs2_ref[...] = hs2
    if with_next:
        mn = mn_ref[0]
        nxt_ref[...] = (hs2 * (1.0 + mn[1:2]) + mn[0:1]).astype(nxt_ref.dtype)


def _combine(hs1, yg, gate, mt, group, ln_g, ln_b, mt_next=None):
    m = hs1.shape[0]
    row = pl.BlockSpec((ROW_TILE, D), lambda i: (i, 0))
    vec = pl.BlockSpec((1, D), lambda i: (0, 0))
    mod = pl.BlockSpec((1, 6, D), lambda i: (group(i), 0, 0))
    in_specs = [row, pl.BlockSpec((TOP_K, ROW_TILE * TILE_ROWS, LANES), lambda i: (0, i, 0)),
                pl.BlockSpec((ROW_TILE, TOP_K), lambda i: (i, 0)), mod, vec, vec]
    args = [hs1, yg, gate, mt, ln_g.reshape(1, D), ln_b.reshape(1, D)]
    out_specs = [row]
    out_shape = [jax.ShapeDtypeStruct((m, D), f32)]
    with_next = mt_next is not None
    if with_next:
        in_specs.append(mod)
        args.append(mt_next)
        out_specs.append(row)
        out_shape.append(jax.ShapeDtypeStruct((m, D), bf16))
    return pl.pallas_call(
        functools.partial(_combine_kernel, with_next=with_next),
        grid=(m // ROW_TILE,),
        in_specs=in_specs, out_specs=out_specs, out_shape=out_shape,
        compiler_params=_cparams(1),
        name="moe_combine_ln",
    )(*args)


def _moe_layer(layer, hs, o, mt, group, ln1_g, ln1_b, ln2_g, ln2_b, r_w, r_b, w1, b1, w2, b2, mt_next=None):
    t = hs.shape[0]
    hs1, tok, logits = _ln_router(hs, o, mt, group, ln1_g[layer], ln1_b[layer], r_w[layer], r_b[layer])
    gate, dest, slot_tok, block_e, n_used, next_e, block_rows = _route(logits)
    xs = _gather_rows(tok, slot_tok)
    y = _moe_experts(layer, xs, block_e, n_used, next_e, block_rows, w1, b1, w2, b2)
    yg = _gather_rows(y, dest.reshape(t, TOP_K).T.reshape(-1)).reshape(TOP_K, t * TILE_ROWS, LANES)
    return _combine(hs1, yg, gate, mt, group, ln2_g[layer], ln2_b[layer], mt_next)


def _sgu_kernel(g_ref, v_ref, lg_ref, lb_ref, ws_ref, bs_ref, o_ref):
    v = v_ref[...].astype(f32)
    vn = _ln_rows(v, lg_ref[...], lb_ref[...]).astype(bf16)
    gw = E // SGU_GROUPS
    for g in range(SGU_GROUPS):
        mixed = _dot(ws_ref[g].astype(bf16), vn[:, g * gw:(g + 1) * gw]) + bs_ref[:, g:g + 1]
        o_ref[:, g * gw:(g + 1) * gw] = (g_ref[:, g * gw:(g + 1) * gw].astype(f32) * mixed).astype(o_ref.dtype)


def _sgu(uv, ln_g, ln_b, w_s, b_s):
    m = uv.shape[0]
    c = SGU_CHUNK
    vec = pl.BlockSpec((1, E), lambda i: (0, 0))
    return pl.pallas_call(
        _sgu_kernel,
        grid=(m // c,),
        in_specs=[pl.BlockSpec((c, E), lambda i: (i, 0)),
                  pl.BlockSpec((c, E), lambda i: (i, 1)),
                  vec, vec,
                  pl.BlockSpec((SGU_GROUPS, c, c), lambda i: (0, 0, 0)),
                  pl.BlockSpec((c, SGU_GROUPS), lambda i: (0, 0))],
        out_specs=pl.BlockSpec((c, E), lambda i: (i, 0)),
        out_shape=jax.ShapeDtypeStruct((m, E), bf16),
        compiler_params=_cparams(1),
        name="sgu",
    )(uv, uv, ln_g.reshape(1, E), ln_b.reshape(1, E), w_s, b_s.T)


def kernel(x, c, ctx, c_ctx, mod_w, mod_b, ln1_g, ln1_b, ln2_g, ln2_b, a_w_in, a_conv_w, a_conv_b, a_w_q, a_w_k, a_w_v, a_w_gate, a_b_gate, a_norm_w, a_skip, a_w_out, b_w_in, b_ln_g, b_ln_b, b_w_s, b_b_s, b_w_out, r_w, r_b, e_w1, e_b1, e_w2, e_b2):
    cvec = jnp.concatenate([c, c_ctx[None, :], jnp.zeros((8 - BATCH - 1, D), f32)], axis=0)
    mods = _mod_rows(cvec, mod_w, mod_b).reshape(DEPTH, 8, 6, D)
    rows = [r for b in range(BATCH) for r in (BATCH, b)]
    mt = [jnp.stack([mods[l, r] for r in rows], axis=0) for l in range(DEPTH)]

    hs = jnp.concatenate([ctx, x], axis=1).reshape(BATCH * U, D)
    hx = _modulate(hs, mt[0], _group_joint)
    xz = _matmul(hx, a_w_in[0], tm=1024, tn=1024, name="mlstm_in_proj")
    xc, q, k, v, gates = _conv_qkv(xz, a_conv_w[0], a_conv_b[0], a_w_q[0], a_w_k[0], a_w_v[0],
                                   a_w_gate[0], a_b_gate[0])
    gates_t = gates.transpose(0, 2, 1)
    h_fwd = _mlstm(q, k, v, gates, gates_t, 0)
    pre = _mlstm(q, k, v, gates, gates_t, 1, (h_fwd, xc, xz, a_norm_w[0], a_skip[0]))
    o = _matmul(pre.reshape(BATCH * U, E), a_w_out[0], tm=1024, tn=512, name="mlstm_out_proj")
    hs2, hx1 = _moe_layer(0, hs, o, mt[0], _group_joint, ln1_g, ln1_b, ln2_g, ln2_b,
                          r_w, r_b, e_w1, e_b1, e_w2, e_b2, mt_next=mt[1])

    hs = hs2.reshape(BATCH, U, D)[:, LC:].reshape(BATCH * SEQ, D)
    hx = hx1.reshape(BATCH, U, D)[:, LC:].reshape(BATCH * SEQ, D)
    uv = _matmul(hx, b_w_in[0], tm=1024, tn=1024, act="gelu", name="sgu_in_proj")
    gated = _sgu(uv, b_ln_g[0], b_ln_b[0], b_w_s[0], b_b_s[0])
    o = _matmul(gated, b_w_out[0], tm=1024, tn=512, name="sgu_out_proj")
    (out,) = _moe_layer(1, hs, o, mt[1], _group_latent, ln1_g, ln1_b, ln2_g, ln2_b,
                        r_w, r_b, e_w1, e_b1, e_w2, e_b2)
    return out.reshape(BATCH, SEQ, D)
```

```python
import functools

import jax
import jax.numpy as jnp
from jax import lax
from jax.experimental import pallas as pl
from jax.experimental.pallas import tpu as pltpu

D = 2048
BATCH = 4
SEQ = 2048
DEPTH = 2
GRID_W = 64
LC = 256
U = LC + SEQ
E = 2 * D
NH = 4
DH = E // NH
QKV_BLOCK = 4
SGU_GROUPS = 8
SGU_CHUNK = 128
NE = 32
TOP_K = 4
F = D
SWIGLU_LIMIT = 7.0
SWIGLU_ALPHA = 1.702
ALPHA = (2 * DEPTH) ** 0.25
LN_EPS = 1e-5

ROW_TILE = 256
MLSTM_CHUNK = 256
MOE_TM = 512
BD_TILE = 256
VMEM_LIMIT = 56 * 1024 * 1024

f32 = jnp.float32
bf16 = jnp.bfloat16


def _cparams(n_axes):
    return pltpu.CompilerParams(dimension_semantics=("arbitrary",) * n_axes,
                                vmem_limit_bytes=VMEM_LIMIT)


def _dot(a, b):
    return jnp.dot(a, b, preferred_element_type=f32)


def _sigmoid(x):
    return 1.0 / (1.0 + jnp.exp(-x))


def _pack_halves(x):
    w = x.shape[1] // 2
    bits = lax.bitcast_convert_type(x.astype(bf16).astype(f32), jnp.uint32)
    return (bits[:, w:] & jnp.uint32(0xFFFF0000)) | (bits[:, :w] >> 16)


def _unpack_halves(p):
    lo = lax.bitcast_convert_type(p << 16, f32)
    hi = lax.bitcast_convert_type(p & jnp.uint32(0xFFFF0000), f32)
    return jnp.concatenate([lo, hi], axis=1)


PACKED = D // 2
LANES = 128
TILE_ROWS = PACKED // LANES


def _store_tile_rows(ref, p):
    r = p.shape[0]
    for s in range(TILE_ROWS):
        ref[pl.ds(s, r, stride=TILE_ROWS), :] = p[:, s * LANES:(s + 1) * LANES]


def _load_tile_rows(ref, r):
    return jnp.concatenate([ref[pl.ds(s, r, stride=TILE_ROWS), :] for s in range(TILE_ROWS)], axis=1)


def _mod_kernel(c_ref, w_ref, b_ref, o_ref):
    c = c_ref[...]
    a = (c * _sigmoid(c)).astype(bf16)
    o_ref[0] = _dot(a, w_ref[0].astype(bf16)) + b_ref[0]


def _mod_rows(cvec, mod_w, mod_b):
    tn = 1024
    return pl.pallas_call(
        _mod_kernel,
        grid=(DEPTH, 6 * D // tn),
        in_specs=[pl.BlockSpec((8, D), lambda l, j: (0, 0)),
                  pl.BlockSpec((1, D, tn), lambda l, j: (l, 0, j)),
                  pl.BlockSpec((1, 1, tn), lambda l, j: (l, 0, j))],
        out_specs=pl.BlockSpec((1, 8, tn), lambda l, j: (l, 0, j)),
        out_shape=jax.ShapeDtypeStruct((DEPTH, 8, 6 * D), f32),
        compiler_params=_cparams(2),
        name="adaln_rows",
    )(cvec, mod_w, mod_b.reshape(DEPTH, 1, 6 * D))


def _group_joint(i):
    tiles = U // ROW_TILE
    return 2 * (i // tiles) + jnp.minimum(i % tiles, 1)


def _group_latent(i):
    return 2 * (i // (SEQ // ROW_TILE)) + 1


def _modulate_kernel(x_ref, m_ref, o_ref):
    m = m_ref[0]
    o_ref[...] = (x_ref[...] * (1.0 + m[1:2]) + m[0:1]).astype(o_ref.dtype)


def _modulate(x2d, mt, group):
    m = x2d.shape[0]
    return pl.pallas_call(
        _modulate_kernel,
        grid=(m // ROW_TILE,),
        in_specs=[pl.BlockSpec((ROW_TILE, D), lambda i: (i, 0)),
                  pl.BlockSpec((1, 6, D), lambda i: (group(i), 0, 0))],
        out_specs=pl.BlockSpec((ROW_TILE, D), lambda i: (i, 0)),
        out_shape=jax.ShapeDtypeStruct((m, D), bf16),
        compiler_params=_cparams(1),
        name="modulate",
    )(x2d, mt)


def _erf(x):
    return lax.erf(x)


def _mm_kernel(x_ref, w_ref, o_ref, wb_ref, *, act):
    @pl.when(pl.program_id(1) == 0)
    def _():
        wb_ref[...] = w_ref[...].astype(bf16)

    acc = _dot(x_ref[...], wb_ref[...])
    if act == "gelu":
        acc = 0.5 * acc * (1.0 + _erf(acc * (2.0 ** -0.5)))
    o_ref[...] = acc.astype(o_ref.dtype)


def _matmul(x, w, *, tm, tn, act=None, name):
    m, k = x.shape
    n = w.shape[1]
    return pl.pallas_call(
        functools.partial(_mm_kernel, act=act),
        grid=(n // tn, m // tm),
        in_specs=[pl.BlockSpec((tm, k), lambda j, i: (i, 0)),
                  pl.BlockSpec((k, tn), lambda j, i: (0, j))],
        out_specs=pl.BlockSpec((tm, tn), lambda j, i: (i, j)),
        out_shape=jax.ShapeDtypeStruct((m, n), bf16),
        scratch_shapes=[pltpu.VMEM((k, tn), bf16)],
        compiler_params=_cparams(2),
        name=name,
    )(x, w)


def _conv_qkv_kernel(xm_ref, cw_ref, cb_ref, wq_ref, wk_ref, wv_ref, wg_ref, bg_ref,
                     xc_ref, q_ref, k_ref, v_ref, g_ref):
    ct = pl.program_id(1)
    a = xm_ref[0].astype(f32)
    c = a.shape[1]
    r = lax.broadcasted_iota(jnp.int32, (U, c), 0)
    latent = r >= LC
    p = r - LC
    col = jnp.where(latent, p & (GRID_W - 1), r)
    last = jnp.where(latent, GRID_W - 1, LC - 1)
    a_l = jnp.where(col > 0, pltpu.roll(a, 1, 0), 0.0)
    a_r = jnp.where(col < last, pltpu.roll(a, U - 1, 0), 0.0)
    w = cw_ref[...]
    rows = [w[3 * i:3 * i + 1] * a_l + w[3 * i + 1:3 * i + 2] * a + w[3 * i + 2:3 * i + 3] * a_r
            for i in range(3)]
    up = jnp.where(p >= GRID_W, pltpu.roll(rows[0], GRID_W, 0), 0.0)
    down = jnp.where(latent & (p < SEQ - GRID_W), pltpu.roll(rows[2], U - GRID_W, 0), 0.0)
    pre = rows[1] + up + down + cb_ref[...]
    xc = (pre * _sigmoid(pre)).astype(bf16)
    xc_ref[0] = xc
    xm = xm_ref[0]
    q = _dot(xc, wq_ref[0]).astype(bf16)
    k = _dot(xc, wk_ref[0]).astype(bf16)
    v = _dot(xm, wv_ref[0]).astype(bf16)
    q_ref[0] = q
    k_ref[0] = k
    v_ref[0] = v
    g = (_dot(q, wg_ref[0].astype(bf16)) + _dot(k, wg_ref[1].astype(bf16))
         + _dot(v, wg_ref[2].astype(bf16)))

    @pl.when(ct == 0)
    def _():
        g_ref[0] = g + bg_ref[...]

    @pl.when(ct > 0)
    def _():
        g_ref[0] += g


def _block_diag(w):
    per = BD_TILE // QKV_BLOCK
    wt = w.reshape(E // BD_TILE, per, QKV_BLOCK, QKV_BLOCK)
    eye = jnp.eye(per, dtype=w.dtype)
    full = jnp.einsum("tgio,gh->tgiho", wt, eye)
    return full.reshape(E // BD_TILE, BD_TILE, BD_TILE).astype(bf16)


def _conv_qkv(xz, conv_w, conv_b, w_q, w_k, w_v, w_gate, b_gate):
    c = BD_TILE
    n_gate = 4 * NH
    act = jax.ShapeDtypeStruct((BATCH, U, E), bf16)
    tile = pl.BlockSpec((1, U, c), lambda b, t: (b, 0, t))
    bd = pl.BlockSpec((1, c, c), lambda b, t: (t, 0, 0))
    return pl.pallas_call(
        _conv_qkv_kernel,
        grid=(BATCH, E // c),
        in_specs=[tile,
                  pl.BlockSpec((9, c), lambda b, t: (0, t)),
                  pl.BlockSpec((1, c), lambda b, t: (0, t)),
                  bd, bd, bd,
                  pl.BlockSpec((3, c, n_gate), lambda b, t: (0, t, 0)),
                  pl.BlockSpec((1, n_gate), lambda b, t: (0, 0))],
        out_specs=[tile, tile, tile, tile,
                   pl.BlockSpec((1, U, n_gate), lambda b, t: (b, 0, 0))],
        out_shape=[act, act, act, act, jax.ShapeDtypeStruct((BATCH, U, n_gate), f32)],
        compiler_params=_cparams(2),
        name="conv_qkv_gates",
    )(xz.reshape(BATCH, U, 2 * E), conv_w.reshape(9, E), conv_b.reshape(1, E),
      _block_diag(w_q), _block_diag(w_k), _block_diag(w_v),
      w_gate.reshape(3, E, n_gate), b_gate.reshape(1, n_gate))


def _log_sigmoid(x):
    return jnp.minimum(x, 0.0) - jnp.log1p(jnp.exp(-jnp.abs(x)))


def _mlstm_kernel(*refs, direction, finish):
    if finish:
        (q_ref, k_ref, v_ref, gc_ref, gr_ref, hf_ref, xc_ref, z_ref, nw_ref, sk_ref,
         o_ref, ct_ref, ctb_ref, n_ref, m_ref) = refs
    else:
        q_ref, k_ref, v_ref, gc_ref, gr_ref, o_ref, ct_ref, ctb_ref, n_ref, m_ref = refs
    L = MLSTM_CHUNK
    h = pl.program_id(1)
    step = pl.program_id(2)

    @pl.when(step == 0)
    def _():
        ct_ref[...] = jnp.zeros_like(ct_ref)
        ctb_ref[...] = jnp.zeros_like(ctb_ref)
        n_ref[...] = jnp.zeros_like(n_ref)
        m_ref[...] = jnp.full_like(m_ref, -jnp.inf)

    i_idx = 2 * direction * NH + h
    f_idx = (2 * direction + 1) * NH + h
    gc = gc_ref[0]
    gr = gr_ref[0]
    lane = lax.broadcasted_iota(jnp.int32, gc.shape, 1)
    sub = lax.broadcasted_iota(jnp.int32, gr.shape, 0)
    ig_col = jnp.sum(jnp.where(lane == i_idx, gc, 0.0), axis=1, keepdims=True)
    f_col = jnp.sum(jnp.where(lane == f_idx, gc, 0.0), axis=1, keepdims=True)
    ig_row = jnp.sum(jnp.where(sub == i_idx, gr, 0.0), axis=0, keepdims=True)
    f_row = jnp.sum(jnp.where(sub == f_idx, gr, 0.0), axis=0, keepdims=True)
    lf_col = _log_sigmoid(f_col)
    lf_row = _log_sigmoid(f_row)

    t_i = lax.broadcasted_iota(jnp.int32, (L, L), 0)
    s_i = lax.broadcasted_iota(jnp.int32, (L, L), 1)
    seen = (s_i <= t_i) if direction == 0 else (s_i >= t_i)
    b_col = jnp.sum(jnp.where(seen, lf_row, 0.0), axis=1, keepdims=True)
    seen_t = (t_i <= s_i) if direction == 0 else (t_i >= s_i)
    b_row = jnp.sum(jnp.where(seen_t, lf_col, 0.0), axis=0, keepdims=True)
    b_end = jnp.sum(lf_row, axis=1, keepdims=True)

    m_prev = m_ref[:, 0:1]
    log_d = jnp.where(seen, b_col - b_row + ig_row, -jnp.inf)
    g_col = b_col + m_prev
    m_t = jnp.maximum(g_col, jnp.max(log_d, axis=1, keepdims=True))
    dw = jnp.exp(log_d - m_t)
    inter = jnp.exp(g_col - m_t)

    q = q_ref[0]
    k = k_ref[0] * (DH ** -0.5)
    v = v_ref[0]
    s = lax.dot_general(q, k, (((1,), (1,)), ((), ())), preferred_element_type=f32) * dw
    num = _dot(s.astype(bf16), v) + inter * _dot(q, ctb_ref[...])
    qn = jnp.sum(q.astype(f32) * n_ref[...], axis=1, keepdims=True)
    den = jnp.sum(s, axis=1, keepdims=True) + inter * qn
    hout = num / jnp.maximum(jnp.abs(den), jnp.exp(-m_t))

    w_end = b_end - b_col + ig_col
    m_new = jnp.maximum(b_end + m_prev, jnp.max(w_end, axis=0, keepdims=True))
    decay = jnp.exp(b_end + m_prev - m_new)
    kw = k.astype(f32) * jnp.exp(w_end - m_new)
    upd = lax.dot_general(kw.astype(bf16), v, (((0,), (0,)), ((), ())), preferred_element_type=f32)
    c_new = decay * ct_ref[...] + upd
    ct_ref[...] = c_new
    ctb_ref[...] = c_new.astype(bf16)
    n_ref[...] = decay * n_ref[...] + jnp.sum(kw, axis=0, keepdims=True)
    m_ref[...] = jnp.broadcast_to(m_new, m_ref.shape)

    if not finish:
        o_ref[0] = hout.astype(o_ref.dtype)
    else:
        hs = hout + hf_ref[0].astype(f32)
        mu = jnp.mean(hs, axis=1, keepdims=True)
        xc_ = hs - mu
        var = jnp.mean(xc_ * xc_, axis=1, keepdims=True)
        hn = xc_ * lax.rsqrt(var + LN_EPS)
        z = z_ref[0].astype(f32)
        o_ref[0] = ((hn * nw_ref[...] + sk_ref[...] * xc_ref[0].astype(f32))
                    * (z * _sigmoid(z))).astype(o_ref.dtype)


def _mlstm(q, k, v, gates, gates_t, direction, finish_args=None):
    L = MLSTM_CHUNK
    n_chunks = U // L

    if direction == 0:
        def chunk(s):
            return s
    else:
        def chunk(s):
            return jnp.where(s == 0, 0, n_chunks - s)

    tile = pl.BlockSpec((1, L, DH), lambda b, h, s: (b, chunk(s), h))
    in_specs = [tile, tile, tile,
                pl.BlockSpec((1, L, 4 * NH), lambda b, h, s: (b, chunk(s), 0)),
                pl.BlockSpec((1, 4 * NH, L), lambda b, h, s: (b, 0, chunk(s)))]
    args = [q, k, v, gates, gates_t]
    finish = finish_args is not None
    if finish:
        h_fwd, xc, xz, norm_w, skip = finish_args
        vec = pl.BlockSpec((1, DH), lambda b, h, s: (0, h))
        in_specs += [tile, tile,
                     pl.BlockSpec((1, L, DH), lambda b, h, s: (b, chunk(s), NH + h)),
                     vec, vec]
        args += [h_fwd, xc, xz.reshape(BATCH, U, 2 * E), norm_w.reshape(1, E), skip.reshape(1, E)]
    return pl.pallas_call(
        functools.partial(_mlstm_kernel, direction=direction, finish=finish),
        grid=(BATCH, NH, n_chunks),
        in_specs=in_specs,
        out_specs=tile,
        out_shape=jax.ShapeDtypeStruct((BATCH, U, E), bf16),
        scratch_shapes=[pltpu.VMEM((DH, DH), f32), pltpu.VMEM((DH, DH), bf16),
                        pltpu.VMEM((1, DH), f32), pltpu.VMEM((1, 128), f32)],
        compiler_params=_cparams(3),
        name="mlstm_bwd_finish" if finish else "mlstm_fwd",
    )(*args)


def _ln_rows(r, g, b):
    mu = jnp.mean(r, axis=1, keepdims=True)
    rc = r - mu
    var = jnp.mean(rc * rc, axis=1, keepdims=True)
    return rc * lax.rsqrt(var + LN_EPS) * g + b


def _ln_router_kernel(hs_ref, o_ref, m_ref, g_ref, b_ref, rw_ref, rb_ref,
                      hs1_ref, tok_ref, gate_ref, eid_ref, rank_ref, cnt_ref, carry_ref):
    @pl.when(pl.program_id(0) == 0)
    def _():
        carry_ref[...] = jnp.zeros_like(carry_ref)

    m = m_ref[0]
    r = ALPHA * hs_ref[...] + m[2:3] * o_ref[...].astype(f32)
    hs1 = _ln_rows(r, g_ref[...], b_ref[...])
    hs1_ref[...] = hs1
    tok = hs1 * (1.0 + m[4:5]) + m[3:4]
    t_hi = tok.astype(bf16)
    _store_tile_rows(tok_ref, _pack_halves(tok))
    t_lo = (tok - t_hi.astype(f32)).astype(bf16)
    w = rw_ref[...]
    w_hi = w.astype(bf16)
    w_lo = (w - w_hi.astype(f32)).astype(bf16)
    logits = _dot(t_hi, w_hi) + _dot(t_hi, w_lo) + _dot(t_lo, w_hi) + rb_ref[...]

    lane = lax.broadcasted_iota(jnp.int32, logits.shape, 1).astype(f32)
    left = logits
    vals, ids = [], []
    for _ in range(TOP_K):
        v = jnp.max(left, axis=1, keepdims=True)
        e = jnp.min(jnp.where(left == v, lane, float(NE)), axis=1, keepdims=True)
        vals.append(v)
        ids.append(e)
        left = jnp.where(lane == e, -jnp.inf, left)
    exps = [jnp.exp(v - vals[0]) for v in vals]
    den = exps[0]
    for x in exps[1:]:
        den = den + x
    gate_ref[...] = jnp.concatenate([x / den for x in exps], axis=1)
    eid_ref[...] = jnp.concatenate(ids, axis=1).astype(jnp.int32)

    member = (lane == ids[0]).astype(f32)
    for e in ids[1:]:
        member = member + (lane == e).astype(f32)
    rows = logits.shape[0]
    earlier = (lax.broadcasted_iota(jnp.int32, (rows, rows), 1)
               < lax.broadcasted_iota(jnp.int32, (rows, rows), 0)).astype(bf16)
    before = _dot(earlier, member.astype(bf16)) + carry_ref[...]
    rank_ref[...] = jnp.concatenate(
        [jnp.sum(jnp.where(lane == e, before, 0.0), axis=1, keepdims=True) for e in ids],
        axis=1).astype(jnp.int32)
    total = carry_ref[...] + jnp.sum(member, axis=0, keepdims=True)
    carry_ref[...] = total
    cnt_ref[...] = total


def _ln_router(hs, o, mt, group, ln_g, ln_b, r_w, r_b):
    m = hs.shape[0]
    row = pl.BlockSpec((ROW_TILE, D), lambda i: (i, 0))
    vec = pl.BlockSpec((1, D), lambda i: (0, 0))
    topk = pl.BlockSpec((ROW_TILE, TOP_K), lambda i: (i, 0))
    return pl.pallas_call(
        _ln_router_kernel,
        grid=(m // ROW_TILE,),
        in_specs=[row, row,
                  pl.BlockSpec((1, 6, D), lambda i: (group(i), 0, 0)),
                  vec, vec,
                  pl.BlockSpec((D, NE), lambda i: (0, 0)),
                  pl.BlockSpec((1, NE), lambda i: (0, 0))],
        out_specs=[row, pl.BlockSpec((ROW_TILE * TILE_ROWS, LANES), lambda i: (i, 0)),
                   topk, topk, topk, pl.BlockSpec((1, NE), lambda i: (0, 0))],
        out_shape=[jax.ShapeDtypeStruct((m, D), f32),
                   jax.ShapeDtypeStruct((m * TILE_ROWS, LANES), jnp.uint32),
                   jax.ShapeDtypeStruct((m, TOP_K), f32),
                   jax.ShapeDtypeStruct((m, TOP_K), jnp.int32),
                   jax.ShapeDtypeStruct((m, TOP_K), jnp.int32),
                   jax.ShapeDtypeStruct((1, NE), f32)],
        scratch_shapes=[pltpu.VMEM((1, NE), f32)],
        compiler_params=_cparams(1),
        name="ln_router",
    )(hs, o, mt, ln_g.reshape(1, D), ln_b.reshape(1, D), r_w, r_b.reshape(1, NE))


def _first_of_expert(be_ref, i):
    return jnp.logical_or(i == 0, be_ref[i] != be_ref[jnp.maximum(i - 1, 0)])


def _moe_up_kernel(be_ref, nu_ref, nxt_ref, rows_ref, x_ref, w_hbm, bg_ref, bl_ref, o_ref, stage_ref,
                   cache_ref, sem, *, layer, tf):
    j = pl.program_id(0)
    i = pl.program_id(1)
    valid = i < nu_ref[0]

    def tile_copy(e, jj, half):
        col = pl.multiple_of(half * F + jj * tf, tf)
        return pltpu.make_async_copy(w_hbm.at[layer, e, :, pl.ds(col, tf)], stage_ref.at[half], sem.at[half])

    def start(e, jj):
        tile_copy(e, jj, 0).start()
        tile_copy(e, jj, 1).start()

    @pl.when(jnp.logical_and(j == 0, i == 0))
    def _():
        start(be_ref[0], 0)

    @pl.when(jnp.logical_and(valid, _first_of_expert(be_ref, i)))
    def _():
        tile_copy(0, 0, 0).wait()
        tile_copy(0, 0, 1).wait()
        cache_ref[...] = stage_ref[...].astype(bf16)
        nxt = nxt_ref[i]

        @pl.when(nxt >= 0)
        def _():
            start(nxt, j)

        @pl.when(jnp.logical_and(nxt < 0, j + 1 < pl.num_programs(0)))
        def _():
            start(be_ref[0], j + 1)

    def compute(rows):
        x = _unpack_halves(_load_tile_rows(x_ref, rows)).astype(bf16)
        glu = jnp.minimum(_dot(x, cache_ref[0]) + bg_ref[0, 0], SWIGLU_LIMIT)
        lin = jnp.clip(_dot(x, cache_ref[1]) + bl_ref[0, 0], -SWIGLU_LIMIT, SWIGLU_LIMIT)
        o_ref[pl.ds(0, rows), :] = (glu * _sigmoid(SWIGLU_ALPHA * glu) * (lin + 1.0)).astype(o_ref.dtype)
        if rows < MOE_TM:
            o_ref[pl.ds(rows, MOE_TM - rows), :] = jnp.zeros((MOE_TM - rows, tf), o_ref.dtype)

    half = rows_ref[i] <= MOE_TM // 2

    @pl.when(jnp.logical_and(valid, jnp.logical_not(half)))
    def _():
        compute(MOE_TM)

    @pl.when(jnp.logical_and(valid, half))
    def _():
        compute(MOE_TM // 2)

    @pl.when(jnp.logical_not(valid))
    def _():
        o_ref[...] = jnp.zeros_like(o_ref)


def _moe_down_kernel(be_ref, nu_ref, nxt_ref, rows_ref, a_ref, w_hbm, b_ref, o_ref, stage_ref, cache_ref, sem,
                     *, layer):
    i = pl.program_id(0)
    valid = i < nu_ref[0]

    def expert_copy(e):
        return pltpu.make_async_copy(w_hbm.at[layer, e], stage_ref, sem)

    @pl.when(i == 0)
    def _():
        expert_copy(be_ref[0]).start()

    @pl.when(jnp.logical_and(valid, _first_of_expert(be_ref, i)))
    def _():
        expert_copy(0).wait()
        cache_ref[...] = stage_ref[...].astype(bf16)
        nxt = nxt_ref[i]

        @pl.when(nxt >= 0)
        def _():
            expert_copy(nxt).start()

    def compute(rows):
        y = _dot(a_ref[pl.ds(0, rows), :], cache_ref[...]) + b_ref[0, 0]
        _store_tile_rows(o_ref, _pack_halves(y))
        if rows < MOE_TM:
            o_ref[pl.ds(rows * TILE_ROWS, (MOE_TM - rows) * TILE_ROWS), :] = jnp.zeros(
                ((MOE_TM - rows) * TILE_ROWS, LANES), o_ref.dtype)

    half = rows_ref[i] <= MOE_TM // 2

    @pl.when(jnp.logical_and(valid, jnp.logical_not(half)))
    def _():
        compute(MOE_TM)

    @pl.when(jnp.logical_and(valid, half))
    def _():
        compute(MOE_TM // 2)

    @pl.when(jnp.logical_not(valid))
    def _():
        o_ref[...] = jnp.zeros_like(o_ref)


def _moe_experts(layer, xs, block_e, n_used, next_e, block_rows, w1, b1, w2, b2):
    ns = xs.shape[0] // TILE_ROWS
    nb = ns // MOE_TM
    tf = 1024
    lin0 = F // tf
    b1 = b1.reshape(DEPTH, NE, 1, 2 * F)
    b2 = b2.reshape(DEPTH, NE, 1, D)

    def bspec(col0):
        return pl.BlockSpec((1, 1, 1, tf), lambda j, i, be, nu, nx, br: (layer, be[i], 0, col0 + j))

    act = pl.pallas_call(
        functools.partial(_moe_up_kernel, layer=layer, tf=tf),
        grid_spec=pltpu.PrefetchScalarGridSpec(
            num_scalar_prefetch=4,
            grid=(F // tf, nb),
            in_specs=[pl.BlockSpec((MOE_TM * TILE_ROWS, LANES),
                                   lambda j, i, be, nu, nx, br: (jnp.minimum(i, nu[0] - 1), 0)),
                      pl.BlockSpec(memory_space=pl.ANY), bspec(0), bspec(lin0)],
            out_specs=pl.BlockSpec((MOE_TM, tf), lambda j, i, be, nu, nx, br: (i, j)),
            scratch_shapes=[pltpu.VMEM((2, D, tf), f32), pltpu.VMEM((2, D, tf), bf16),
                            pltpu.SemaphoreType.DMA((2,))]),
        out_shape=jax.ShapeDtypeStruct((ns, F), bf16),
        compiler_params=_cparams(2),
        name="moe_up",
    )(block_e, n_used, next_e, block_rows, xs, w1, b1, b1)

    return pl.pallas_call(
        functools.partial(_moe_down_kernel, layer=layer),
        grid_spec=pltpu.PrefetchScalarGridSpec(
            num_scalar_prefetch=4,
            grid=(nb,),
            in_specs=[pl.BlockSpec((MOE_TM, F), lambda i, be, nu, nx, br: (jnp.minimum(i, nu[0] - 1), 0)),
                      pl.BlockSpec(memory_space=pl.ANY),
                      pl.BlockSpec((1, 1, 1, D), lambda i, be, nu, nx, br: (layer, be[i], 0, 0))],
            out_specs=pl.BlockSpec((MOE_TM * TILE_ROWS, LANES), lambda i, be, nu, nx, br: (i, 0)),
            scratch_shapes=[pltpu.VMEM((F, D), f32), pltpu.VMEM((F, D), bf16), pltpu.SemaphoreType.DMA]),
        out_shape=jax.ShapeDtypeStruct((ns * TILE_ROWS, LANES), jnp.uint32),
        compiler_params=_cparams(1),
        name="moe_down",
    )(block_e, n_used, next_e, block_rows, act, w2, b2)


GATHER_ROWS = 2048


def _gather_kernel(idx_ref, src_ref, out_ref, sem):
    n = GATHER_ROWS
    t = TILE_ROWS

    def issue(r, carry):
        src_row = pl.multiple_of(idx_ref[0, 0, r] * t, t)
        dst_row = pl.multiple_of(r * t, t)
        pltpu.make_async_copy(src_ref.at[pl.ds(src_row, t)], out_ref.at[pl.ds(dst_row, t)], sem).start()
        return carry

    lax.fori_loop(0, n, issue, 0, unroll=8)
    pltpu.make_async_copy(src_ref.at[pl.ds(0, n * t)], out_ref, sem).wait()


def _gather_rows(src, idx):
    n = idx.shape[0]
    nblk = n // GATHER_ROWS
    return pl.pallas_call(
        _gather_kernel,
        grid=(nblk,),
        in_specs=[pl.BlockSpec((1, 1, GATHER_ROWS), lambda i: (i, 0, 0), memory_space=pltpu.SMEM),
                  pl.BlockSpec(memory_space=pl.ANY)],
        out_specs=pl.BlockSpec((GATHER_ROWS * TILE_ROWS, LANES), lambda i: (i, 0)),
        out_shape=jax.ShapeDtypeStruct((n * TILE_ROWS, LANES), src.dtype),
        scratch_shapes=[pltpu.SemaphoreType.DMA],
        compiler_params=pltpu.CompilerParams(dimension_semantics=("arbitrary",),
                                             vmem_limit_bytes=VMEM_LIMIT,
                                             disable_bounds_checks=True),
        name="gather_rows",
    )(idx.reshape(nblk, 1, GATHER_ROWS), src)


SCATTER_TOKENS = 512


def _scatter_kernel(dest_ref, src_ref, init_ref, out_ref, sem):
    del init_ref
    n = SCATTER_TOKENS
    t = TILE_ROWS

    def issue(a, carry):
        src_row = pl.multiple_of((a // TOP_K) * t, t)
        dst_row = pl.multiple_of(dest_ref[0, 0, a] * t, t)
        pltpu.make_async_copy(src_ref.at[pl.ds(src_row, t)], out_ref.at[pl.ds(dst_row, t)], sem).start()
        return carry

    lax.fori_loop(0, n * TOP_K, issue, 0, unroll=8)
    for _ in range(TOP_K):
        pltpu.make_async_copy(src_ref, out_ref.at[pl.ds(0, n * t)], sem).wait()


def _scatter_rows(src, dest, n_slots):
    n = src.shape[0] // TILE_ROWS
    nblk = n // SCATTER_TOKENS
    init = jnp.zeros((n_slots * TILE_ROWS, LANES), src.dtype)
    return pl.pallas_call(
        _scatter_kernel,
        grid=(nblk,),
        in_specs=[pl.BlockSpec((1, 1, SCATTER_TOKENS * TOP_K), lambda i: (i, 0, 0), memory_space=pltpu.SMEM),
                  pl.BlockSpec((SCATTER_TOKENS * TILE_ROWS, LANES), lambda i: (i, 0)),
                  pl.BlockSpec(memory_space=pl.ANY)],
        out_specs=pl.BlockSpec(memory_space=pl.ANY),
        out_shape=jax.ShapeDtypeStruct(init.shape, src.dtype),
        input_output_aliases={2: 0},
        scratch_shapes=[pltpu.SemaphoreType.DMA],
        compiler_params=pltpu.CompilerParams(dimension_semantics=("arbitrary",),
                                             vmem_limit_bytes=VMEM_LIMIT,
                                             disable_bounds_checks=True),
        name="scatter_rows",
    )(dest.reshape(nblk, 1, SCATTER_TOKENS * TOP_K), src, init)


def _slot_layout(eid, rank, counts):
    t = eid.shape[0]
    nb = -(-(t * TOP_K) // MOE_TM) + NE
    counts = counts.reshape(NE).astype(jnp.int32)
    experts = jnp.arange(NE, dtype=jnp.int32)
    pcounts = (counts + MOE_TM - 1) // MOE_TM * MOE_TM
    pends = jnp.cumsum(pcounts)
    pstarts = pends - pcounts
    dest = rank + jnp.sum(jnp.where(eid[:, :, None] == experts, pstarts, 0), axis=-1)
    n_used = pends[-1] // MOE_TM
    blk = jnp.arange(nb, dtype=jnp.int32)
    block_e = jnp.minimum(jnp.sum(pends[None, :] <= (blk * MOE_TM)[:, None], axis=1), NE - 1)
    of_block = block_e[:, None] == experts

    def per_block(table):
        return jnp.sum(jnp.where(of_block, table, 0), axis=1)

    block_rows = jnp.clip(per_block(counts) - (blk * MOE_TM - per_block(pstarts)), 0, MOE_TM)
    block_rows = jnp.where(blk < n_used, block_rows, 0)
    group_end = per_block(pends) // MOE_TM
    next_e = jnp.sum(jnp.where(group_end[:, None] == blk, block_e, 0), axis=1)
    next_e = jnp.where(group_end < n_used, next_e, -1)
    last_e = jnp.sum(jnp.where(blk == n_used - 1, block_e, 0))
    block_e = jnp.where(blk < n_used, block_e, last_e)
    i32 = jnp.int32
    return (dest.astype(i32), block_e.astype(i32), n_used.astype(i32).reshape(1), next_e.astype(i32),
            block_rows.astype(i32))


def _combine_kernel(hs_ref, y_ref, gate_ref, m_ref, g_ref, b_ref, *rest, with_next):
    if with_next:
        mn_ref, hs2_ref, nxt_ref = rest
    else:
        (hs2_ref,) = rest
    m = m_ref[0]
    gate = gate_ref[...]
    y = gate[:, 0:1] * _unpack_halves(_load_tile_rows(y_ref.at[0], ROW_TILE))
    for k in range(1, TOP_K):
        y = y + gate[:, k:k + 1] * _unpack_halves(_load_tile_rows(y_ref.at[k], ROW_TILE))
    hs2 = _ln_rows(ALPHA * hs_ref[...] + m[5:6] * y, g_ref[...], b_ref[...])
    hs2_ref[...] = hs2
    if with_next:
        mn = mn_ref[0]
        nxt_ref[...] = (hs2 * (1.0 + mn[1:2]) + mn[0:1]).astype(nxt_ref.dtype)


def _combine(hs1, yg, gate, mt, group, ln_g, ln_b, mt_next=None):
    m = hs1.shape[0]
    row = pl.BlockSpec((ROW_TILE, D), lambda i: (i, 0))
    vec = pl.BlockSpec((1, D), lambda i: (0, 0))
    mod = pl.BlockSpec((1, 6, D), lambda i: (group(i), 0, 0))
    in_specs = [row, pl.BlockSpec((TOP_K, ROW_TILE * TILE_ROWS, LANES), lambda i: (0, i, 0)),
                pl.BlockSpec((ROW_TILE, TOP_K), lambda i: (i, 0)), mod, vec, vec]
    args = [hs1, yg, gate, mt, ln_g.reshape(1, D), ln_b.reshape(1, D)]
    out_specs = [row]
    out_shape = [jax.ShapeDtypeStruct((m, D), f32)]
    with_next = mt_next is not None
    if with_next:
        in_specs.append(mod)
        args.append(mt_next)
        out_specs.append(row)
        out_shape.append(jax.ShapeDtypeStruct((m, D), bf16))
    return pl.pallas_call(
        functools.partial(_combine_kernel, with_next=with_next),
        grid=(m // ROW_TILE,),
        in_specs=in_specs, out_specs=out_specs, out_shape=out_shape,
        compiler_params=_cparams(1),
        name="moe_combine_ln",
    )(*args)


def _moe_layer(layer, hs, o, mt, group, ln1_g, ln1_b, ln2_g, ln2_b, r_w, r_b, w1, b1, w2, b2, mt_next=None):
    t = hs.shape[0]
    hs1, tok, gate, eid, rank, counts = _ln_router(hs, o, mt, group, ln1_g[layer], ln1_b[layer],
                                                   r_w[layer], r_b[layer])
    dest, block_e, n_used, next_e, block_rows = _slot_layout(eid, rank, counts)
    xs = _scatter_rows(tok, dest.reshape(-1), block_e.shape[0] * MOE_TM)
    y = _moe_experts(layer, xs, block_e, n_used, next_e, block_rows, w1, b1, w2, b2)
    yg = _gather_rows(y, dest.T.reshape(-1)).reshape(TOP_K, t * TILE_ROWS, LANES)
    return _combine(hs1, yg, gate, mt, group, ln2_g[layer], ln2_b[layer], mt_next)


def _sgu_kernel(g_ref, v_ref, lg_ref, lb_ref, ws_ref, bs_ref, o_ref):
    v = v_ref[...].astype(f32)
    vn = _ln_rows(v, lg_ref[...], lb_ref[...]).astype(bf16)
    gw = E // SGU_GROUPS
    for g in range(SGU_GROUPS):
        mixed = _dot(ws_ref[g].astype(bf16), vn[:, g * gw:(g + 1) * gw]) + bs_ref[:, g:g + 1]
        o_ref[:, g * gw:(g + 1) * gw] = (g_ref[:, g * gw:(g + 1) * gw].astype(f32) * mixed).astype(o_ref.dtype)


def _sgu(uv, ln_g, ln_b, w_s, b_s):
    m = uv.shape[0]
    c = SGU_CHUNK
    vec = pl.BlockSpec((1, E), lambda i: (0, 0))
    return pl.pallas_call(
        _sgu_kernel,
        grid=(m // c,),
        in_specs=[pl.BlockSpec((c, E), lambda i: (i, 0)),
                  pl.BlockSpec((c, E), lambda i: (i, 1)),
                  vec, vec,
                  pl.BlockSpec((SGU_GROUPS, c, c), lambda i: (0, 0, 0)),
                  pl.BlockSpec((c, SGU_GROUPS), lambda i: (0, 0))],
        out_specs=pl.BlockSpec((c, E), lambda i: (i, 0)),
        out_shape=jax.ShapeDtypeStruct((m, E), bf16),
        compiler_params=_cparams(1),
        name="sgu",
    )(uv, uv, ln_g.reshape(1, E), ln_b.reshape(1, E), w_s, b_s.T)


def kernel(x, c, ctx, c_ctx, mod_w, mod_b, ln1_g, ln1_b, ln2_g, ln2_b, a_w_in, a_conv_w, a_conv_b, a_w_q, a_w_k, a_w_v, a_w_gate, a_b_gate, a_norm_w, a_skip, a_w_out, b_w_in, b_ln_g, b_ln_b, b_w_s, b_b_s, b_w_out, r_w, r_b, e_w1, e_b1, e_w2, e_b2):
    cvec = jnp.concatenate([c, c_ctx[None, :], jnp.zeros((8 - BATCH - 1, D), f32)], axis=0)
    mods = _mod_rows(cvec, mod_w, mod_b).reshape(DEPTH, 8, 6, D)
    rows = [r for b in range(BATCH) for r in (BATCH, b)]
    mt = [jnp.stack([mods[l, r] for r in rows], axis=0) for l in range(DEPTH)]

    hs = jnp.concatenate([ctx, x], axis=1).reshape(BATCH * U, D)
    hx = _modulate(hs, mt[0], _group_joint)
    xz = _matmul(hx, a_w_in[0], tm=1024, tn=1024, name="mlstm_in_proj")
    xc, q, k, v, gates = _conv_qkv(xz, a_conv_w[0], a_conv_b[0], a_w_q[0], a_w_k[0], a_w_v[0],
                                   a_w_gate[0], a_b_gate[0])
    gates_t = gates.transpose(0, 2, 1)
    h_fwd = _mlstm(q, k, v, gates, gates_t, 0)
    pre = _mlstm(q, k, v, gates, gates_t, 1, (h_fwd, xc, xz, a_norm_w[0], a_skip[0]))
    o = _matmul(pre.reshape(BATCH * U, E), a_w_out[0], tm=1024, tn=512, name="mlstm_out_proj")
    hs2, hx1 = _moe_layer(0, hs, o, mt[0], _group_joint, ln1_g, ln1_b, ln2_g, ln2_b,
                          r_w, r_b, e_w1, e_b1, e_w2, e_b2, mt_next=mt[1])

    hs = hs2.reshape(BATCH, U, D)[:, LC:].reshape(BATCH * SEQ, D)
    hx = hx1.reshape(BATCH, U, D)[:, LC:].reshape(BATCH * SEQ, D)
    uv = _matmul(hx, b_w_in[0], tm=1024, tn=1024, act="gelu", name="sgu_in_proj")
    gated = _sgu(uv, b_ln_g[0], b_ln_b[0], b_w_s[0], b_b_s[0])
    o = _matmul(gated, b_w_out[0], tm=1024, tn=512, name="sgu_out_proj")
    (out,) = _moe_layer(1, hs, o, mt[1], _group_latent, ln1_g, ln1_b, ln2_g, ln2_b,
                        r_w, r_b, e_w1, e_b1, e_w2, e_b2)
    return out.reshape(BATCH, SEQ, D)
```

```python
import functools

import jax
import jax.numpy as jnp
from jax import lax
from jax.experimental import pallas as pl
from jax.experimental.pallas import tpu as pltpu

D = 2048
BATCH = 4
SEQ = 2048
DEPTH = 2
GRID_W = 64
LC = 256
U = LC + SEQ
E = 2 * D
NH = 4
DH = E // NH
QKV_BLOCK = 4
SGU_GROUPS = 8
SGU_CHUNK = 128
NE = 32
TOP_K = 4
F = D
SWIGLU_LIMIT = 7.0
SWIGLU_ALPHA = 1.702
ALPHA = (2 * DEPTH) ** 0.25
LN_EPS = 1e-5

ROW_TILE = 256
MLSTM_CHUNK = 256
MOE_TM = 512
BD_TILE = 256
VMEM_LIMIT = 56 * 1024 * 1024

f32 = jnp.float32
bf16 = jnp.bfloat16


def _cparams(n_axes):
    return pltpu.CompilerParams(dimension_semantics=("arbitrary",) * n_axes,
                                vmem_limit_bytes=VMEM_LIMIT)


def _dot(a, b):
    return jnp.dot(a, b, preferred_element_type=f32)


def _sigmoid(x):
    return 1.0 / (1.0 + jnp.exp(-x))


def _pack_halves(x):
    w = x.shape[1] // 2
    bits = lax.bitcast_convert_type(x.astype(bf16).astype(f32), jnp.uint32)
    return (bits[:, w:] & jnp.uint32(0xFFFF0000)) | (bits[:, :w] >> 16)


def _unpack_halves(p):
    lo = lax.bitcast_convert_type(p << 16, f32)
    hi = lax.bitcast_convert_type(p & jnp.uint32(0xFFFF0000), f32)
    return jnp.concatenate([lo, hi], axis=1)


PACKED = D // 2
LANES = 128
TILE_ROWS = PACKED // LANES


def _store_tile_rows(ref, p):
    r = p.shape[0]
    for s in range(TILE_ROWS):
        ref[pl.ds(s, r, stride=TILE_ROWS), :] = p[:, s * LANES:(s + 1) * LANES]


def _load_tile_rows(ref, r):
    return jnp.concatenate([ref[pl.ds(s, r, stride=TILE_ROWS), :] for s in range(TILE_ROWS)], axis=1)


def _mod_kernel(c_ref, w_ref, b_ref, o_ref):
    c = c_ref[...]
    a = (c * _sigmoid(c)).astype(bf16)
    o_ref[0] = _dot(a, w_ref[0].astype(bf16)) + b_ref[0]


def _mod_rows(cvec, mod_w, mod_b):
    tn = 1024
    return pl.pallas_call(
        _mod_kernel,
        grid=(DEPTH, 6 * D // tn),
        in_specs=[pl.BlockSpec((8, D), lambda l, j: (0, 0)),
                  pl.BlockSpec((1, D, tn), lambda l, j: (l, 0, j)),
                  pl.BlockSpec((1, 1, tn), lambda l, j: (l, 0, j))],
        out_specs=pl.BlockSpec((1, 8, tn), lambda l, j: (l, 0, j)),
        out_shape=jax.ShapeDtypeStruct((DEPTH, 8, 6 * D), f32),
        compiler_params=_cparams(2),
        name="adaln_rows",
    )(cvec, mod_w, mod_b.reshape(DEPTH, 1, 6 * D))


def _group_joint(i):
    tiles = U // ROW_TILE
    return 2 * (i // tiles) + jnp.minimum(i % tiles, 1)


def _group_latent(i):
    return 2 * (i // (SEQ // ROW_TILE)) + 1


JOINT_TILES = U // ROW_TILE


def _is_ctx_tile(i):
    return i % JOINT_TILES == 0


def _ctx_tile(i):
    return i // JOINT_TILES


def _latent_tile(i):
    return (i // JOINT_TILES) * (SEQ // ROW_TILE) + jnp.maximum(i % JOINT_TILES - 1, 0)


_CTX_ROWS = pl.BlockSpec((ROW_TILE, D), lambda i: (_ctx_tile(i), 0))
_LATENT_ROWS = pl.BlockSpec((ROW_TILE, D), lambda i: (_latent_tile(i), 0))


def _joint_rows(ctx_ref, lat_ref):
    is_ctx = _is_ctx_tile(pl.program_id(0))
    return jnp.where(is_ctx, ctx_ref[...], lat_ref[...])


def _modulate_kernel(ctx_ref, lat_ref, m_ref, o_ref):
    m = m_ref[0]
    o_ref[...] = (_joint_rows(ctx_ref, lat_ref) * (1.0 + m[1:2]) + m[0:1]).astype(o_ref.dtype)


def _modulate(ctx2d, x2d, mt):
    m = ctx2d.shape[0] + x2d.shape[0]
    return pl.pallas_call(
        _modulate_kernel,
        grid=(m // ROW_TILE,),
        in_specs=[_CTX_ROWS, _LATENT_ROWS,
                  pl.BlockSpec((1, 6, D), lambda i: (_group_joint(i), 0, 0))],
        out_specs=pl.BlockSpec((ROW_TILE, D), lambda i: (i, 0)),
        out_shape=jax.ShapeDtypeStruct((m, D), bf16),
        compiler_params=_cparams(1),
        name="modulate",
    )(ctx2d, x2d, mt)


def _erf(x):
    return lax.erf(x)


def _mm_kernel(x_ref, w_ref, o_ref, wb_ref, *, act):
    @pl.when(pl.program_id(1) == 0)
    def _():
        wb_ref[...] = w_ref[...].astype(bf16)

    acc = _dot(x_ref[...], wb_ref[...])
    if act == "gelu":
        acc = 0.5 * acc * (1.0 + _erf(acc * (2.0 ** -0.5)))
    o_ref[...] = acc.astype(o_ref.dtype)


def _matmul(x, w, *, tm, tn, act=None, name):
    m, k = x.shape
    n = w.shape[1]
    return pl.pallas_call(
        functools.partial(_mm_kernel, act=act),
        grid=(n // tn, m // tm),
        in_specs=[pl.BlockSpec((tm, k), lambda j, i: (i, 0)),
                  pl.BlockSpec((k, tn), lambda j, i: (0, j))],
        out_specs=pl.BlockSpec((tm, tn), lambda j, i: (i, j)),
        out_shape=jax.ShapeDtypeStruct((m, n), bf16),
        scratch_shapes=[pltpu.VMEM((k, tn), bf16)],
        compiler_params=_cparams(2),
        name=name,
    )(x, w)


def _conv_qkv_kernel(xm_ref, cw_ref, cb_ref, wq_ref, wk_ref, wv_ref, wg_ref, bg_ref,
                     xc_ref, q_ref, k_ref, v_ref, g_ref):
    ct = pl.program_id(1)
    w = cw_ref[...]
    cb = cb_ref[...]

    def taps(a, i):
        n = a.shape[0]
        pos = lax.broadcasted_iota(jnp.int32, a.shape, 0)
        a_l = jnp.where(pos > 0, pltpu.roll(a, 1, 0), 0.0)
        a_r = jnp.where(pos < n - 1, pltpu.roll(a, n - 1, 0), 0.0)
        return w[3 * i:3 * i + 1] * a_l + w[3 * i + 1:3 * i + 2] * a + w[3 * i + 2:3 * i + 3] * a_r

    def finish(start, n, pre):
        pre = pre + cb
        xc_ref[0, pl.ds(start, n), :] = (pre * _sigmoid(pre)).astype(bf16)

    def piece(start, n):
        return xm_ref[0, pl.ds(start, n), :].astype(f32)

    finish(0, LC, taps(piece(0, LC), 1))

    def image_row(y, carry):
        partial, above = carry
        a = piece(pl.multiple_of(LC + y * GRID_W, GRID_W), GRID_W)

        @pl.when(y > 0)
        def _():
            finish(pl.multiple_of(LC + (y - 1) * GRID_W, GRID_W), GRID_W, partial + taps(a, 2))

        return taps(a, 1) + above, taps(a, 0)

    zero = jnp.zeros((GRID_W, w.shape[1]), f32)
    partial, _ = lax.fori_loop(0, SEQ // GRID_W, image_row, (zero, zero))
    finish(U - GRID_W, GRID_W, partial)

    xc = xc_ref[0]
    xm = xm_ref[0]
    q = _dot(xc, _diag_tile(wq_ref)).astype(bf16)
    k = _dot(xc, _diag_tile(wk_ref)).astype(bf16)
    v = _dot(xm, _diag_tile(wv_ref)).astype(bf16)
    q_ref[0] = q
    k_ref[0] = k
    v_ref[0] = v
    g = (_dot(q, wg_ref[0].astype(bf16)) + _dot(k, wg_ref[1].astype(bf16))
         + _dot(v, wg_ref[2].astype(bf16)))

    @pl.when(ct == 0)
    def _():
        g_ref[0] = g + bg_ref[...]

    @pl.when(ct > 0)
    def _():
        g_ref[0] += g


def _diag_tile(ref):
    r = lax.broadcasted_iota(jnp.int32, ref.shape, 0) // QKV_BLOCK
    c = lax.broadcasted_iota(jnp.int32, ref.shape, 1) // QKV_BLOCK
    return jnp.where(r == c, ref[...], 0.0).astype(bf16)


def _block_rows_tiled(w):
    return jnp.tile(w.reshape(E, QKV_BLOCK), (1, BD_TILE // QKV_BLOCK))


def _conv_qkv(xz, conv_w, conv_b, w_q, w_k, w_v, w_gate, b_gate):
    c = BD_TILE
    n_gate = 4 * NH
    act = jax.ShapeDtypeStruct((BATCH, U, E), bf16)
    tile = pl.BlockSpec((1, U, c), lambda b, t: (b, 0, t))
    bd = pl.BlockSpec((c, c), lambda b, t: (t, 0))
    return pl.pallas_call(
        _conv_qkv_kernel,
        grid=(BATCH, E // c),
        in_specs=[tile,
                  pl.BlockSpec((9, c), lambda b, t: (0, t)),
                  pl.BlockSpec((1, c), lambda b, t: (0, t)),
                  bd, bd, bd,
                  pl.BlockSpec((3, c, n_gate), lambda b, t: (0, t, 0)),
                  pl.BlockSpec((1, n_gate), lambda b, t: (0, 0))],
        out_specs=[tile, tile, tile, tile,
                   pl.BlockSpec((1, U, n_gate), lambda b, t: (b, 0, 0))],
        out_shape=[act, act, act, act, jax.ShapeDtypeStruct((BATCH, U, n_gate), f32)],
        compiler_params=_cparams(2),
        name="conv_qkv_gates",
    )(xz.reshape(BATCH, U, 2 * E), conv_w.reshape(9, E), conv_b.reshape(1, E),
      _block_rows_tiled(w_q), _block_rows_tiled(w_k), _block_rows_tiled(w_v),
      w_gate.reshape(3, E, n_gate), b_gate.reshape(1, n_gate))


def _log_sigmoid(x):
    return jnp.minimum(x, 0.0) - jnp.log1p(jnp.exp(-jnp.abs(x)))


def _mlstm_kernel(*refs, direction, finish):
    if finish:
        (q_ref, k_ref, v_ref, gc_ref, gr_ref, hf_ref, xc_ref, z_ref, nw_ref, sk_ref,
         o_ref, ct_ref, ctb_ref, n_ref, m_ref) = refs
    else:
        q_ref, k_ref, v_ref, gc_ref, gr_ref, o_ref, ct_ref, ctb_ref, n_ref, m_ref = refs
    L = MLSTM_CHUNK
    h = pl.program_id(1)
    step = pl.program_id(2)

    @pl.when(step == 0)
    def _():
        ct_ref[...] = jnp.zeros_like(ct_ref)
        ctb_ref[...] = jnp.zeros_like(ctb_ref)
        n_ref[...] = jnp.zeros_like(n_ref)
        m_ref[...] = jnp.full_like(m_ref, -jnp.inf)

    i_idx = 2 * direction * NH + h
    f_idx = (2 * direction + 1) * NH + h
    gc = gc_ref[0]
    gr = gr_ref[0]
    lane = lax.broadcasted_iota(jnp.int32, gc.shape, 1)
    sub = lax.broadcasted_iota(jnp.int32, gr.shape, 0)
    ig_col = jnp.sum(jnp.where(lane == i_idx, gc, 0.0), axis=1, keepdims=True)
    f_col = jnp.sum(jnp.where(lane == f_idx, gc, 0.0), axis=1, keepdims=True)
    ig_row = jnp.sum(jnp.where(sub == i_idx, gr, 0.0), axis=0, keepdims=True)
    f_row = jnp.sum(jnp.where(sub == f_idx, gr, 0.0), axis=0, keepdims=True)
    lf_col = _log_sigmoid(f_col)
    lf_row = _log_sigmoid(f_row)

    t_i = lax.broadcasted_iota(jnp.int32, (L, L), 0)
    s_i = lax.broadcasted_iota(jnp.int32, (L, L), 1)
    seen = (s_i <= t_i) if direction == 0 else (s_i >= t_i)
    b_col = jnp.sum(jnp.where(seen, lf_row, 0.0), axis=1, keepdims=True)
    seen_t = (t_i <= s_i) if direction == 0 else (t_i >= s_i)
    b_row = jnp.sum(jnp.where(seen_t, lf_col, 0.0), axis=0, keepdims=True)
    b_end = jnp.sum(lf_row, axis=1, keepdims=True)

    m_prev = m_ref[:, 0:1]
    log_d = jnp.where(seen, b_col - b_row + ig_row, -jnp.inf)
    g_col = b_col + m_prev
    m_t = jnp.maximum(g_col, jnp.max(log_d, axis=1, keepdims=True))
    dw = jnp.exp(log_d - m_t)
    inter = jnp.exp(g_col - m_t)

    q = q_ref[0]
    k = k_ref[0] * (DH ** -0.5)
    v = v_ref[0]
    s = lax.dot_general(q, k, (((1,), (1,)), ((), ())), preferred_element_type=f32) * dw
    num = _dot(s.astype(bf16), v) + inter * _dot(q, ctb_ref[...])
    qn = jnp.sum(q.astype(f32) * n_ref[...], axis=1, keepdims=True)
    den = jnp.sum(s, axis=1, keepdims=True) + inter * qn
    hout = num / jnp.maximum(jnp.abs(den), jnp.exp(-m_t))

    w_end = b_end - b_col + ig_col
    m_new = jnp.maximum(b_end + m_prev, jnp.max(w_end, axis=0, keepdims=True))
    decay = jnp.exp(b_end + m_prev - m_new)
    kw = k.astype(f32) * jnp.exp(w_end - m_new)
    upd = lax.dot_general(kw.astype(bf16), v, (((0,), (0,)), ((), ())), preferred_element_type=f32)
    c_new = decay * ct_ref[...] + upd
    ct_ref[...] = c_new
    ctb_ref[...] = c_new.astype(bf16)
    n_ref[...] = decay * n_ref[...] + jnp.sum(kw, axis=0, keepdims=True)
    m_ref[...] = jnp.broadcast_to(m_new, m_ref.shape)

    if not finish:
        o_ref[0] = hout.astype(o_ref.dtype)
    else:
        hs = hout + hf_ref[0].astype(f32)
        mu = jnp.mean(hs, axis=1, keepdims=True)
        xc_ = hs - mu
        var = jnp.mean(xc_ * xc_, axis=1, keepdims=True)
        hn = xc_ * lax.rsqrt(var + LN_EPS)
        z = z_ref[0].astype(f32)
        o_ref[0] = ((hn * nw_ref[...] + sk_ref[...] * xc_ref[0].astype(f32))
                    * (z * _sigmoid(z))).astype(o_ref.dtype)


def _mlstm(q, k, v, gates, gates_t, direction, finish_args=None):
    L = MLSTM_CHUNK
    n_chunks = U // L

    if direction == 0:
        def chunk(s):
            return s
    else:
        def chunk(s):
            return jnp.where(s == 0, 0, n_chunks - s)

    tile = pl.BlockSpec((1, L, DH), lambda b, h, s: (b, chunk(s), h))
    in_specs = [tile, tile, tile,
                pl.BlockSpec((1, L, 4 * NH), lambda b, h, s: (b, chunk(s), 0)),
                pl.BlockSpec((1, 4 * NH, L), lambda b, h, s: (b, 0, chunk(s)))]
    args = [q, k, v, gates, gates_t]
    finish = finish_args is not None
    if finish:
        h_fwd, xc, xz, norm_w, skip = finish_args
        vec = pl.BlockSpec((1, DH), lambda b, h, s: (0, h))
        in_specs += [tile, tile,
                     pl.BlockSpec((1, L, DH), lambda b, h, s: (b, chunk(s), NH + h)),
                     vec, vec]
        args += [h_fwd, xc, xz.reshape(BATCH, U, 2 * E), norm_w.reshape(1, E), skip.reshape(1, E)]
    return pl.pallas_call(
        functools.partial(_mlstm_kernel, direction=direction, finish=finish),
        grid=(BATCH, NH, n_chunks),
        in_specs=in_specs,
        out_specs=tile,
        out_shape=jax.ShapeDtypeStruct((BATCH, U, E), bf16),
        scratch_shapes=[pltpu.VMEM((DH, DH), f32), pltpu.VMEM((DH, DH), bf16),
                        pltpu.VMEM((1, DH), f32), pltpu.VMEM((1, 128), f32)],
        compiler_params=_cparams(3),
        name="mlstm_bwd_finish" if finish else "mlstm_fwd",
    )(*args)


def _ln_rows(r, g, b):
    mu = jnp.mean(r, axis=1, keepdims=True)
    rc = r - mu
    var = jnp.mean(rc * rc, axis=1, keepdims=True)
    return rc * lax.rsqrt(var + LN_EPS) * g + b


def _ln_router_kernel(*refs, joint):
    hs_refs, refs = refs[:1 + joint], refs[1 + joint:]
    (o_ref, m_ref, g_ref, b_ref, rw_ref, rb_ref,
     hs1_ref, tok_ref, gate_ref, eid_ref, rank_ref, cnt_ref, carry_ref) = refs

    @pl.when(pl.program_id(0) == 0)
    def _():
        carry_ref[...] = jnp.zeros_like(carry_ref)

    m = m_ref[0]
    hs = _joint_rows(*hs_refs) if joint else hs_refs[0][...]
    r = ALPHA * hs + m[2:3] * o_ref[...].astype(f32)
    hs1 = _ln_rows(r, g_ref[...], b_ref[...])
    hs1_ref[...] = hs1
    tok = hs1 * (1.0 + m[4:5]) + m[3:4]
    t_hi = tok.astype(bf16)
    _store_tile_rows(tok_ref, _pack_halves(tok))
    t_lo = (tok - t_hi.astype(f32)).astype(bf16)
    w = rw_ref[...]
    w_hi = w.astype(bf16)
    w_lo = (w - w_hi.astype(f32)).astype(bf16)
    logits = _dot(t_hi, w_hi) + _dot(t_hi, w_lo) + _dot(t_lo, w_hi) + rb_ref[...]

    lane = lax.broadcasted_iota(jnp.int32, logits.shape, 1).astype(f32)
    left = logits
    vals, ids = [], []
    for _ in range(TOP_K):
        v = jnp.max(left, axis=1, keepdims=True)
        e = jnp.min(jnp.where(left == v, lane, float(NE)), axis=1, keepdims=True)
        vals.append(v)
        ids.append(e)
        left = jnp.where(lane == e, -jnp.inf, left)
    exps = [jnp.exp(v - vals[0]) for v in vals]
    den = exps[0]
    for x in exps[1:]:
        den = den + x
    gate_ref[...] = jnp.concatenate([x / den for x in exps], axis=1)
    eid_ref[...] = jnp.concatenate(ids, axis=1).astype(jnp.int32)

    member = (lane == ids[0]).astype(f32)
    for e in ids[1:]:
        member = member + (lane == e).astype(f32)
    rows = logits.shape[0]
    earlier = (lax.broadcasted_iota(jnp.int32, (rows, rows), 1)
               < lax.broadcasted_iota(jnp.int32, (rows, rows), 0)).astype(bf16)
    before = _dot(earlier, member.astype(bf16)) + carry_ref[...]
    rank_ref[...] = jnp.concatenate(
        [jnp.sum(jnp.where(lane == e, before, 0.0), axis=1, keepdims=True) for e in ids],
        axis=1).astype(jnp.int32)
    total = carry_ref[...] + jnp.sum(member, axis=0, keepdims=True)
    carry_ref[...] = total
    cnt_ref[...] = total


def _ln_router(hs_parts, o, mt, group, ln_g, ln_b, r_w, r_b):
    m = o.shape[0]
    joint = len(hs_parts) == 2
    row = pl.BlockSpec((ROW_TILE, D), lambda i: (i, 0))
    vec = pl.BlockSpec((1, D), lambda i: (0, 0))
    topk = pl.BlockSpec((ROW_TILE, TOP_K), lambda i: (i, 0))
    return pl.pallas_call(
        functools.partial(_ln_router_kernel, joint=joint),
        grid=(m // ROW_TILE,),
        in_specs=([_CTX_ROWS, _LATENT_ROWS] if joint else [row]) + [
                  row,
                  pl.BlockSpec((1, 6, D), lambda i: (group(i), 0, 0)),
                  vec, vec,
                  pl.BlockSpec((D, NE), lambda i: (0, 0)),
                  pl.BlockSpec((1, NE), lambda i: (0, 0))],
        out_specs=[row, pl.BlockSpec((ROW_TILE * TILE_ROWS, LANES), lambda i: (i, 0)),
                   topk, topk, topk, pl.BlockSpec((1, NE), lambda i: (0, 0))],
        out_shape=[jax.ShapeDtypeStruct((m, D), f32),
                   jax.ShapeDtypeStruct((m * TILE_ROWS, LANES), jnp.uint32),
                   jax.ShapeDtypeStruct((m, TOP_K), f32),
                   jax.ShapeDtypeStruct((m, TOP_K), jnp.int32),
                   jax.ShapeDtypeStruct((m, TOP_K), jnp.int32),
                   jax.ShapeDtypeStruct((1, NE), f32)],
        scratch_shapes=[pltpu.VMEM((1, NE), f32)],
        compiler_params=_cparams(1),
        name="ln_router",
    )(*hs_parts, o, mt, ln_g.reshape(1, D), ln_b.reshape(1, D), r_w, r_b.reshape(1, NE))


def _first_of_expert(be_ref, i):
    return jnp.logical_or(i == 0, be_ref[i] != be_ref[jnp.maximum(i - 1, 0)])


ROW_STEPS = 4


def _for_real_rows(valid, rows, compute):
    q = MOE_TM // ROW_STEPS
    for step in range(1, ROW_STEPS + 1):
        covers = rows <= step * q
        if step > 1:
            covers = jnp.logical_and(covers, rows > (step - 1) * q)
        pl.when(jnp.logical_and(valid, covers))(functools.partial(compute, step * q))


def _moe_up_kernel(be_ref, nu_ref, nxt_ref, rows_ref, x_ref, w_hbm, bg_ref, bl_ref, o_ref, stage_ref,
                   cache_ref, sem, *, layer, tf):
    j = pl.program_id(0)
    i = pl.program_id(1)
    valid = i < nu_ref[0]

    def tile_copy(e, jj, half):
        col = pl.multiple_of(half * F + jj * tf, tf)
        return pltpu.make_async_copy(w_hbm.at[layer, e, :, pl.ds(col, tf)], stage_ref.at[half], sem.at[half])

    def start(e, jj):
        tile_copy(e, jj, 0).start()
        tile_copy(e, jj, 1).start()

    @pl.when(jnp.logical_and(j == 0, i == 0))
    def _():
        start(be_ref[0], 0)

    @pl.when(jnp.logical_and(valid, _first_of_expert(be_ref, i)))
    def _():
        tile_copy(0, 0, 0).wait()
        tile_copy(0, 0, 1).wait()
        cache_ref[...] = stage_ref[...].astype(bf16)
        nxt = nxt_ref[i]

        @pl.when(nxt >= 0)
        def _():
            start(nxt, j)

        @pl.when(jnp.logical_and(nxt < 0, j + 1 < pl.num_programs(0)))
        def _():
            start(be_ref[0], j + 1)

    def compute(rows):
        x = _unpack_halves(_load_tile_rows(x_ref, rows)).astype(bf16)
        glu = jnp.minimum(_dot(x, cache_ref[0]) + bg_ref[0, 0], SWIGLU_LIMIT)
        lin = jnp.clip(_dot(x, cache_ref[1]) + bl_ref[0, 0], -SWIGLU_LIMIT, SWIGLU_LIMIT)
        o_ref[pl.ds(0, rows), :] = (glu * _sigmoid(SWIGLU_ALPHA * glu) * (lin + 1.0)).astype(o_ref.dtype)
        if rows < MOE_TM:
            o_ref[pl.ds(rows, MOE_TM - rows), :] = jnp.zeros((MOE_TM - rows, tf), o_ref.dtype)

    _for_real_rows(valid, rows_ref[i], compute)

    @pl.when(jnp.logical_not(valid))
    def _():
        o_ref[...] = jnp.zeros_like(o_ref)


def _moe_down_kernel(be_ref, nu_ref, nxt_ref, rows_ref, a_ref, w_hbm, b_ref, o_ref, stage_ref, cache_ref, sem,
                     *, layer):
    i = pl.program_id(0)
    valid = i < nu_ref[0]

    def expert_copy(e):
        return pltpu.make_async_copy(w_hbm.at[layer, e], stage_ref, sem)

    @pl.when(i == 0)
    def _():
        expert_copy(be_ref[0]).start()

    @pl.when(jnp.logical_and(valid, _first_of_expert(be_ref, i)))
    def _():
        expert_copy(0).wait()
        cache_ref[...] = stage_ref[...].astype(bf16)
        nxt = nxt_ref[i]

        @pl.when(nxt >= 0)
        def _():
            expert_copy(nxt).start()

    def compute(rows):
        y = _dot(a_ref[pl.ds(0, rows), :], cache_ref[...]) + b_ref[0, 0]
        _store_tile_rows(o_ref, _pack_halves(y))
        if rows < MOE_TM:
            o_ref[pl.ds(rows * TILE_ROWS, (MOE_TM - rows) * TILE_ROWS), :] = jnp.zeros(
                ((MOE_TM - rows) * TILE_ROWS, LANES), o_ref.dtype)

    _for_real_rows(valid, rows_ref[i], compute)

    @pl.when(jnp.logical_not(valid))
    def _():
        o_ref[...] = jnp.zeros_like(o_ref)


def _moe_experts(layer, xs, block_e, n_used, next_e, block_rows, w1, b1, w2, b2):
    ns = xs.shape[0] // TILE_ROWS
    nb = ns // MOE_TM
    tf = 1024
    lin0 = F // tf
    b1 = b1.reshape(DEPTH, NE, 1, 2 * F)
    b2 = b2.reshape(DEPTH, NE, 1, D)

    def bspec(col0):
        return pl.BlockSpec((1, 1, 1, tf), lambda j, i, be, nu, nx, br: (layer, be[i], 0, col0 + j))

    act = pl.pallas_call(
        functools.partial(_moe_up_kernel, layer=layer, tf=tf),
        grid_spec=pltpu.PrefetchScalarGridSpec(
            num_scalar_prefetch=4,
            grid=(F // tf, nb),
            in_specs=[pl.BlockSpec((MOE_TM * TILE_ROWS, LANES),
                                   lambda j, i, be, nu, nx, br: (jnp.minimum(i, nu[0] - 1), 0)),
                      pl.BlockSpec(memory_space=pl.ANY), bspec(0), bspec(lin0)],
            out_specs=pl.BlockSpec((MOE_TM, tf), lambda j, i, be, nu, nx, br: (i, j)),
            scratch_shapes=[pltpu.VMEM((2, D, tf), f32), pltpu.VMEM((2, D, tf), bf16),
                            pltpu.SemaphoreType.DMA((2,))]),
        out_shape=jax.ShapeDtypeStruct((ns, F), bf16),
        compiler_params=_cparams(2),
        name="moe_up",
    )(block_e, n_used, next_e, block_rows, xs, w1, b1, b1)

    return pl.pallas_call(
        functools.partial(_moe_down_kernel, layer=layer),
        grid_spec=pltpu.PrefetchScalarGridSpec(
            num_scalar_prefetch=4,
            grid=(nb,),
            in_specs=[pl.BlockSpec((MOE_TM, F), lambda i, be, nu, nx, br: (jnp.minimum(i, nu[0] - 1), 0)),
                      pl.BlockSpec(memory_space=pl.ANY),
                      pl.BlockSpec((1, 1, 1, D), lambda i, be, nu, nx, br: (layer, be[i], 0, 0))],
            out_specs=pl.BlockSpec((MOE_TM * TILE_ROWS, LANES), lambda i, be, nu, nx, br: (i, 0)),
            scratch_shapes=[pltpu.VMEM((F, D), f32), pltpu.VMEM((F, D), bf16), pltpu.SemaphoreType.DMA]),
        out_shape=jax.ShapeDtypeStruct((ns * TILE_ROWS, LANES), jnp.uint32),
        compiler_params=_cparams(1),
        name="moe_down",
    )(block_e, n_used, next_e, block_rows, act, w2, b2)


GATHER_ROWS = 2048


def _gather_kernel(idx_ref, src_ref, out_ref, sem):
    n = GATHER_ROWS
    t = TILE_ROWS

    def issue(r, carry):
        src_row = pl.multiple_of(idx_ref[0, 0, r] * t, t)
        dst_row = pl.multiple_of(r * t, t)
        pltpu.make_async_copy(src_ref.at[pl.ds(src_row, t)], out_ref.at[pl.ds(dst_row, t)], sem).start()
        return carry

    lax.fori_loop(0, n, issue, 0, unroll=8)
    pltpu.make_async_copy(src_ref.at[pl.ds(0, n * t)], out_ref, sem).wait()


def _gather_rows(src, idx):
    n = idx.shape[0]
    nblk = n // GATHER_ROWS
    return pl.pallas_call(
        _gather_kernel,
        grid=(nblk,),
        in_specs=[pl.BlockSpec((1, 1, GATHER_ROWS), lambda i: (i, 0, 0), memory_space=pltpu.SMEM),
                  pl.BlockSpec(memory_space=pl.ANY)],
        out_specs=pl.BlockSpec((GATHER_ROWS * TILE_ROWS, LANES), lambda i: (i, 0)),
        out_shape=jax.ShapeDtypeStruct((n * TILE_ROWS, LANES), src.dtype),
        scratch_shapes=[pltpu.SemaphoreType.DMA],
        compiler_params=pltpu.CompilerParams(dimension_semantics=("arbitrary",),
                                             vmem_limit_bytes=VMEM_LIMIT,
                                             disable_bounds_checks=True),
        name="gather_rows",
    )(idx.reshape(nblk, 1, GATHER_ROWS), src)


SCATTER_TOKENS = 512


def _scatter_kernel(dest_ref, src_ref, init_ref, out_ref, sem):
    del init_ref
    n = SCATTER_TOKENS
    t = TILE_ROWS

    def issue(a, carry):
        src_row = pl.multiple_of((a // TOP_K) * t, t)
        dst_row = pl.multiple_of(dest_ref[0, 0, a] * t, t)
        pltpu.make_async_copy(src_ref.at[pl.ds(src_row, t)], out_ref.at[pl.ds(dst_row, t)], sem).start()
        return carry

    lax.fori_loop(0, n * TOP_K, issue, 0, unroll=8)
    for _ in range(TOP_K):
        pltpu.make_async_copy(src_ref, out_ref.at[pl.ds(0, n * t)], sem).wait()


def _scatter_rows(src, dest, n_slots):
    n = src.shape[0] // TILE_ROWS
    nblk = n // SCATTER_TOKENS
    init = jnp.zeros((n_slots * TILE_ROWS, LANES), src.dtype)
    return pl.pallas_call(
        _scatter_kernel,
        grid=(nblk,),
        in_specs=[pl.BlockSpec((1, 1, SCATTER_TOKENS * TOP_K), lambda i: (i, 0, 0), memory_space=pltpu.SMEM),
                  pl.BlockSpec((SCATTER_TOKENS * TILE_ROWS, LANES), lambda i: (i, 0)),
                  pl.BlockSpec(memory_space=pl.ANY)],
        out_specs=pl.BlockSpec(memory_space=pl.ANY),
        out_shape=jax.ShapeDtypeStruct(init.shape, src.dtype),
        input_output_aliases={2: 0},
        scratch_shapes=[pltpu.SemaphoreType.DMA],
        compiler_params=pltpu.CompilerParams(dimension_semantics=("arbitrary",),
                                             vmem_limit_bytes=VMEM_LIMIT,
                                             disable_bounds_checks=True),
        name="scatter_rows",
    )(dest.reshape(nblk, 1, SCATTER_TOKENS * TOP_K), src, init)


def _slot_layout(eid, rank, counts):
    t = eid.shape[0]
    nb = -(-(t * TOP_K) // MOE_TM) + NE
    counts = counts.reshape(NE).astype(jnp.int32)
    experts = jnp.arange(NE, dtype=jnp.int32)
    pcounts = (counts + MOE_TM - 1) // MOE_TM * MOE_TM
    pends = jnp.cumsum(pcounts)
    pstarts = pends - pcounts
    dest = rank + jnp.sum(jnp.where(eid[:, :, None] == experts, pstarts, 0), axis=-1)
    n_used = pends[-1] // MOE_TM
    blk = jnp.arange(nb, dtype=jnp.int32)
    block_e = jnp.minimum(jnp.sum(pends[None, :] <= (blk * MOE_TM)[:, None], axis=1), NE - 1)
    of_block = block_e[:, None] == experts

    def per_block(table):
        return jnp.sum(jnp.where(of_block, table, 0), axis=1)

    block_rows = jnp.clip(per_block(counts) - (blk * MOE_TM - per_block(pstarts)), 0, MOE_TM)
    block_rows = jnp.where(blk < n_used, block_rows, 0)
    group_end = per_block(pends) // MOE_TM
    next_e = jnp.sum(jnp.where(group_end[:, None] == blk, block_e, 0), axis=1)
    next_e = jnp.where(group_end < n_used, next_e, -1)
    last_e = jnp.sum(jnp.where(blk == n_used - 1, block_e, 0))
    block_e = jnp.where(blk < n_used, block_e, last_e)
    i32 = jnp.int32
    return (dest.astype(i32), block_e.astype(i32), n_used.astype(i32).reshape(1), next_e.astype(i32),
            block_rows.astype(i32))


def _combine_kernel(hs_ref, y_ref, gate_ref, m_ref, g_ref, b_ref, *rest, joint):
    if joint:
        mn_ref, hs2_ref, nxt_ref, ctx_ref = rest
    else:
        (hs2_ref,) = rest
    m = m_ref[0]
    gate = gate_ref[...]
    y = gate[:, 0:1] * _unpack_halves(_load_tile_rows(y_ref.at[0], ROW_TILE))
    for k in range(1, TOP_K):
        y = y + gate[:, k:k + 1] * _unpack_halves(_load_tile_rows(y_ref.at[k], ROW_TILE))
    hs2 = _ln_rows(ALPHA * hs_ref[...] + m[5:6] * y, g_ref[...], b_ref[...])
    if not joint:
        hs2_ref[...] = hs2
        return
    is_ctx = _is_ctx_tile(pl.program_id(0))

    @pl.when(is_ctx)
    def _():
        ctx_ref[...] = hs2

    @pl.when(jnp.logical_not(is_ctx))
    def _():
        mn = mn_ref[0]
        hs2_ref[...] = hs2
        nxt_ref[...] = (hs2 * (1.0 + mn[1:2]) + mn[0:1]).astype(nxt_ref.dtype)


def _combine(hs1, yg, gate, mt, group, ln_g, ln_b, mt_next=None):
    m = hs1.shape[0]
    row = pl.BlockSpec((ROW_TILE, D), lambda i: (i, 0))
    vec = pl.BlockSpec((1, D), lambda i: (0, 0))
    mod = pl.BlockSpec((1, 6, D), lambda i: (group(i), 0, 0))
    in_specs = [row, pl.BlockSpec((TOP_K, ROW_TILE * TILE_ROWS, LANES), lambda i: (0, i, 0)),
                pl.BlockSpec((ROW_TILE, TOP_K), lambda i: (i, 0)), mod, vec, vec]
    args = [hs1, yg, gate, mt, ln_g.reshape(1, D), ln_b.reshape(1, D)]
    joint = mt_next is not None
    if joint:
        in_specs.append(mod)
        args.append(mt_next)
        out_specs = [_LATENT_ROWS, _LATENT_ROWS, _CTX_ROWS]
        out_shape = [jax.ShapeDtypeStruct((BATCH * SEQ, D), f32), jax.ShapeDtypeStruct((BATCH * SEQ, D), bf16),
                     jax.ShapeDtypeStruct((BATCH * LC, D), f32)]
    else:
        out_specs = [row]
        out_shape = [jax.ShapeDtypeStruct((m, D), f32)]
    return pl.pallas_call(
        functools.partial(_combine_kernel, joint=joint),
        grid=(m // ROW_TILE,),
        in_specs=in_specs, out_specs=out_specs, out_shape=out_shape,
        compiler_params=_cparams(1),
        name="moe_combine_ln",
    )(*args)


def _moe_layer(layer, hs_parts, o, mt, group, ln1_g, ln1_b, ln2_g, ln2_b, r_w, r_b, w1, b1, w2, b2,
               mt_next=None):
    t = o.shape[0]
    hs1, tok, gate, eid, rank, counts = _ln_router(hs_parts, o, mt, group, ln1_g[layer], ln1_b[layer],
                                                   r_w[layer], r_b[layer])
    dest, block_e, n_used, next_e, block_rows = _slot_layout(eid, rank, counts)
    xs = _scatter_rows(tok, dest.reshape(-1), block_e.shape[0] * MOE_TM)
    y = _moe_experts(layer, xs, block_e, n_used, next_e, block_rows, w1, b1, w2, b2)
    yg = _gather_rows(y, dest.T.reshape(-1)).reshape(TOP_K, t * TILE_ROWS, LANES)
    return _combine(hs1, yg, gate, mt, group, ln2_g[layer], ln2_b[layer], mt_next)


def _sgu_kernel(g_ref, v_ref, lg_ref, lb_ref, ws_ref, bs_ref, o_ref):
    v = v_ref[...].astype(f32)
    vn = _ln_rows(v, lg_ref[...], lb_ref[...]).astype(bf16)
    gw = E // SGU_GROUPS
    for g in range(SGU_GROUPS):
        mixed = _dot(ws_ref[g].astype(bf16), vn[:, g * gw:(g + 1) * gw]) + bs_ref[:, g:g + 1]
        o_ref[:, g * gw:(g + 1) * gw] = (g_ref[:, g * gw:(g + 1) * gw].astype(f32) * mixed).astype(o_ref.dtype)


def _sgu(uv, ln_g, ln_b, w_s, b_s):
    m = uv.shape[0]
    c = SGU_CHUNK
    vec = pl.BlockSpec((1, E), lambda i: (0, 0))
    return pl.pallas_call(
        _sgu_kernel,
        grid=(m // c,),
        in_specs=[pl.BlockSpec((c, E), lambda i: (i, 0)),
                  pl.BlockSpec((c, E), lambda i: (i, 1)),
                  vec, vec,
                  pl.BlockSpec((SGU_GROUPS, c, c), lambda i: (0, 0, 0)),
                  pl.BlockSpec((c, SGU_GROUPS), lambda i: (0, 0))],
        out_specs=pl.BlockSpec((c, E), lambda i: (i, 0)),
        out_shape=jax.ShapeDtypeStruct((m, E), bf16),
        compiler_params=_cparams(1),
        name="sgu",
    )(uv, uv, ln_g.reshape(1, E), ln_b.reshape(1, E), w_s, b_s.T)


def kernel(x, c, ctx, c_ctx, mod_w, mod_b, ln1_g, ln1_b, ln2_g, ln2_b, a_w_in, a_conv_w, a_conv_b, a_w_q, a_w_k, a_w_v, a_w_gate, a_b_gate, a_norm_w, a_skip, a_w_out, b_w_in, b_ln_g, b_ln_b, b_w_s, b_b_s, b_w_out, r_w, r_b, e_w1, e_b1, e_w2, e_b2):
    cvec = jnp.concatenate([c, c_ctx[None, :], jnp.zeros((8 - BATCH - 1, D), f32)], axis=0)
    mods = _mod_rows(cvec, mod_w, mod_b).reshape(DEPTH, 8, 6, D)
    rows = [r for b in range(BATCH) for r in (BATCH, b)]
    mt = [jnp.stack([mods[l, r] for r in rows], axis=0) for l in range(DEPTH)]

    hs = (ctx.reshape(BATCH * LC, D), x.reshape(BATCH * SEQ, D))
    hx = _modulate(*hs, mt[0])
    xz = _matmul(hx, a_w_in[0], tm=1024, tn=1024, name="mlstm_in_proj")
    xc, q, k, v, gates = _conv_qkv(xz, a_conv_w[0], a_conv_b[0], a_w_q[0], a_w_k[0], a_w_v[0],
                                   a_w_gate[0], a_b_gate[0])
    gates_t = gates.transpose(0, 2, 1)
    h_fwd = _mlstm(q, k, v, gates, gates_t, 0)
    pre = _mlstm(q, k, v, gates, gates_t, 1, (h_fwd, xc, xz, a_norm_w[0], a_skip[0]))
    o = _matmul(pre.reshape(BATCH * U, E), a_w_out[0], tm=1024, tn=512, name="mlstm_out_proj")
    hs, hx, _ = _moe_layer(0, hs, o, mt[0], _group_joint, ln1_g, ln1_b, ln2_g, ln2_b,
                           r_w, r_b, e_w1, e_b1, e_w2, e_b2, mt_next=mt[1])

    uv = _matmul(hx, b_w_in[0], tm=1024, tn=1024, act="gelu", name="sgu_in_proj")
    gated = _sgu(uv, b_ln_g[0], b_ln_b[0], b_w_s[0], b_b_s[0])
    o = _matmul(gated, b_w_out[0], tm=1024, tn=512, name="sgu_out_proj")
    (out,) = _moe_layer(1, (hs,), o, mt[1], _group_latent, ln1_g, ln1_b, ln2_g, ln2_b,
                        r_w, r_b, e_w1, e_b1, e_w2, e_b2)
    return out.reshape(BATCH, SEQ, D)
```

```python
import functools

import jax
import jax.numpy as jnp
from jax import lax
from jax.experimental import pallas as pl
from jax.experimental.pallas import tpu as pltpu

D = 2048
BATCH = 4
SEQ = 2048
DEPTH = 2
GRID_W = 64
LC = 256
U = LC + SEQ
E = 2 * D
NH = 4
DH = E // NH
QKV_BLOCK = 4
SGU_GROUPS = 8
SGU_CHUNK = 128
NE = 32
TOP_K = 4
F = D
SWIGLU_LIMIT = 7.0
SWIGLU_ALPHA = 1.702
ALPHA = (2 * DEPTH) ** 0.25
LN_EPS = 1e-5

ROW_TILE = 256
MLSTM_CHUNK = 256
MOE_TM = 512
BD_TILE = 256
VMEM_LIMIT = 56 * 1024 * 1024

f32 = jnp.float32
bf16 = jnp.bfloat16


def _cparams(n_axes):
    return pltpu.CompilerParams(dimension_semantics=("arbitrary",) * n_axes,
                                vmem_limit_bytes=VMEM_LIMIT)


def _dot(a, b):
    return jnp.dot(a, b, preferred_element_type=f32)


def _sigmoid(x):
    return 1.0 / (1.0 + jnp.exp(-x))


def _pack_halves(x):
    w = x.shape[1] // 2
    bits = lax.bitcast_convert_type(x.astype(bf16).astype(f32), jnp.uint32)
    return (bits[:, w:] & jnp.uint32(0xFFFF0000)) | (bits[:, :w] >> 16)


def _unpack_halves(p):
    lo = lax.bitcast_convert_type(p << 16, f32)
    hi = lax.bitcast_convert_type(p & jnp.uint32(0xFFFF0000), f32)
    return jnp.concatenate([lo, hi], axis=1)


PACKED = D // 2
LANES = 128
TILE_ROWS = PACKED // LANES


def _store_tile_rows(ref, p):
    r = p.shape[0]
    for s in range(TILE_ROWS):
        ref[pl.ds(s, r, stride=TILE_ROWS), :] = p[:, s * LANES:(s + 1) * LANES]


def _load_tile_rows(ref, r):
    return jnp.concatenate([ref[pl.ds(s, r, stride=TILE_ROWS), :] for s in range(TILE_ROWS)], axis=1)


def _mod_kernel(c_ref, w_ref, b_ref, o_ref):
    c = c_ref[...]
    a = (c * _sigmoid(c)).astype(bf16)
    o_ref[0] = _dot(a, w_ref[0].astype(bf16)) + b_ref[0]


def _mod_rows(cvec, mod_w, mod_b):
    tn = 1024
    return pl.pallas_call(
        _mod_kernel,
        grid=(DEPTH, 6 * D // tn),
        in_specs=[pl.BlockSpec((8, D), lambda l, j: (0, 0)),
                  pl.BlockSpec((1, D, tn), lambda l, j: (l, 0, j)),
                  pl.BlockSpec((1, 1, tn), lambda l, j: (l, 0, j))],
        out_specs=pl.BlockSpec((1, 8, tn), lambda l, j: (l, 0, j)),
        out_shape=jax.ShapeDtypeStruct((DEPTH, 8, 6 * D), f32),
        compiler_params=_cparams(2),
        name="adaln_rows",
    )(cvec, mod_w, mod_b.reshape(DEPTH, 1, 6 * D))


def _group_joint(i):
    tiles = U // ROW_TILE
    return 2 * (i // tiles) + jnp.minimum(i % tiles, 1)


def _group_latent(i):
    return 2 * (i // (SEQ // ROW_TILE)) + 1


JOINT_TILES = U // ROW_TILE


def _is_ctx_tile(i):
    return i % JOINT_TILES == 0


def _ctx_tile(i):
    return i // JOINT_TILES


def _latent_tile(i):
    return (i // JOINT_TILES) * (SEQ // ROW_TILE) + jnp.maximum(i % JOINT_TILES - 1, 0)


_CTX_ROWS = pl.BlockSpec((ROW_TILE, D), lambda i: (_ctx_tile(i), 0))
_LATENT_ROWS = pl.BlockSpec((ROW_TILE, D), lambda i: (_latent_tile(i), 0))


def _joint_rows(ctx_ref, lat_ref):
    is_ctx = _is_ctx_tile(pl.program_id(0))
    return jnp.where(is_ctx, ctx_ref[...], lat_ref[...])


def _modulate_kernel(ctx_ref, lat_ref, m_ref, o_ref):
    m = m_ref[0]
    o_ref[...] = (_joint_rows(ctx_ref, lat_ref) * (1.0 + m[1:2]) + m[0:1]).astype(o_ref.dtype)


def _modulate(ctx2d, x2d, mt):
    m = ctx2d.shape[0] + x2d.shape[0]
    return pl.pallas_call(
        _modulate_kernel,
        grid=(m // ROW_TILE,),
        in_specs=[_CTX_ROWS, _LATENT_ROWS,
                  pl.BlockSpec((1, 6, D), lambda i: (_group_joint(i), 0, 0))],
        out_specs=pl.BlockSpec((ROW_TILE, D), lambda i: (i, 0)),
        out_shape=jax.ShapeDtypeStruct((m, D), bf16),
        compiler_params=_cparams(1),
        name="modulate",
    )(ctx2d, x2d, mt)


def _erf(x):
    return lax.erf(x)


def _mm_kernel(x_ref, w_ref, o_ref, wb_ref, *, act):
    @pl.when(pl.program_id(1) == 0)
    def _():
        wb_ref[...] = w_ref[...].astype(bf16)

    acc = _dot(x_ref[...], wb_ref[...])
    if act == "gelu":
        acc = 0.5 * acc * (1.0 + _erf(acc * (2.0 ** -0.5)))
    o_ref[...] = acc.astype(o_ref.dtype)


def _matmul(x, w, *, tm, tn, act=None, name):
    m, k = x.shape
    n = w.shape[1]
    return pl.pallas_call(
        functools.partial(_mm_kernel, act=act),
        grid=(n // tn, m // tm),
        in_specs=[pl.BlockSpec((tm, k), lambda j, i: (i, 0)),
                  pl.BlockSpec((k, tn), lambda j, i: (0, j))],
        out_specs=pl.BlockSpec((tm, tn), lambda j, i: (i, j)),
        out_shape=jax.ShapeDtypeStruct((m, n), bf16),
        scratch_shapes=[pltpu.VMEM((k, tn), bf16)],
        compiler_params=_cparams(2),
        name=name,
    )(x, w)


def _conv_qkv_kernel(xm_ref, cw_ref, cb_ref, wq_ref, wk_ref, wv_ref, wg_ref, bg_ref,
                     xc_ref, q_ref, k_ref, v_ref, g_ref):
    ct = pl.program_id(1)
    a = xm_ref[0].astype(f32)
    c = a.shape[1]
    r = lax.broadcasted_iota(jnp.int32, (U, c), 0)
    latent = r >= LC
    p = r - LC
    col = jnp.where(latent, p & (GRID_W - 1), r)
    last = jnp.where(latent, GRID_W - 1, LC - 1)
    a_l = jnp.where(col > 0, pltpu.roll(a, 1, 0), 0.0)
    a_r = jnp.where(col < last, pltpu.roll(a, U - 1, 0), 0.0)
    w = cw_ref[...]
    rows = [w[3 * i:3 * i + 1] * a_l + w[3 * i + 1:3 * i + 2] * a + w[3 * i + 2:3 * i + 3] * a_r
            for i in range(3)]
    up = jnp.where(p >= GRID_W, pltpu.roll(rows[0], GRID_W, 0), 0.0)
    down = jnp.where(latent & (p < SEQ - GRID_W), pltpu.roll(rows[2], U - GRID_W, 0), 0.0)
    pre = rows[1] + up + down + cb_ref[...]
    xc = (pre * _sigmoid(pre)).astype(bf16)
    xc_ref[0] = xc
    xm = xm_ref[0]
    q = _dot(xc, _diag_tile(wq_ref)).astype(bf16)
    k = _dot(xc, _diag_tile(wk_ref)).astype(bf16)
    v = _dot(xm, _diag_tile(wv_ref)).astype(bf16)
    q_ref[0] = q
    k_ref[0] = k
    v_ref[0] = v
    g = (_dot(q, wg_ref[0].astype(bf16)) + _dot(k, wg_ref[1].astype(bf16))
         + _dot(v, wg_ref[2].astype(bf16)))

    @pl.when(ct == 0)
    def _():
        g_ref[0] = g + bg_ref[...]

    @pl.when(ct > 0)
    def _():
        g_ref[0] += g


def _diag_tile(ref):
    r = lax.broadcasted_iota(jnp.int32, ref.shape, 0) // QKV_BLOCK
    c = lax.broadcasted_iota(jnp.int32, ref.shape, 1) // QKV_BLOCK
    return jnp.where(r == c, ref[...], 0.0).astype(bf16)


def _block_rows_tiled(w):
    return jnp.tile(w.reshape(E, QKV_BLOCK), (1, BD_TILE // QKV_BLOCK))


def _conv_qkv(xz, conv_w, conv_b, w_q, w_k, w_v, w_gate, b_gate):
    c = BD_TILE
    n_gate = 4 * NH
    act = jax.ShapeDtypeStruct((BATCH, U, E), bf16)
    tile = pl.BlockSpec((1, U, c), lambda b, t: (b, 0, t))
    bd = pl.BlockSpec((c, c), lambda b, t: (t, 0))
    return pl.pallas_call(
        _conv_qkv_kernel,
        grid=(BATCH, E // c),
        in_specs=[tile,
                  pl.BlockSpec((9, c), lambda b, t: (0, t)),
                  pl.BlockSpec((1, c), lambda b, t: (0, t)),
                  bd, bd, bd,
                  pl.BlockSpec((3, c, n_gate), lambda b, t: (0, t, 0)),
                  pl.BlockSpec((1, n_gate), lambda b, t: (0, 0))],
        out_specs=[tile, tile, tile, tile,
                   pl.BlockSpec((1, U, n_gate), lambda b, t: (b, 0, 0))],
        out_shape=[act, act, act, act, jax.ShapeDtypeStruct((BATCH, U, n_gate), f32)],
        compiler_params=_cparams(2),
        name="conv_qkv_gates",
    )(xz.reshape(BATCH, U, 2 * E), conv_w.reshape(9, E), conv_b.reshape(1, E),
      _block_rows_tiled(w_q), _block_rows_tiled(w_k), _block_rows_tiled(w_v),
      w_gate.reshape(3, E, n_gate), b_gate.reshape(1, n_gate))


def _log_sigmoid(x):
    return jnp.minimum(x, 0.0) - jnp.log1p(jnp.exp(-jnp.abs(x)))


def _mlstm_kernel(*refs, direction, finish):
    if finish:
        (q_ref, k_ref, v_ref, gc_ref, gr_ref, hf_ref, xc_ref, z_ref, nw_ref, sk_ref,
         o_ref, ct_ref, ctb_ref, n_ref, m_ref) = refs
    else:
        q_ref, k_ref, v_ref, gc_ref, gr_ref, o_ref, ct_ref, ctb_ref, n_ref, m_ref = refs
    L = MLSTM_CHUNK
    h = pl.program_id(1)
    step = pl.program_id(2)

    @pl.when(step == 0)
    def _():
        ct_ref[...] = jnp.zeros_like(ct_ref)
        ctb_ref[...] = jnp.zeros_like(ctb_ref)
        n_ref[...] = jnp.zeros_like(n_ref)
        m_ref[...] = jnp.full_like(m_ref, -jnp.inf)

    i_idx = 2 * direction * NH + h
    f_idx = (2 * direction + 1) * NH + h
    gc = gc_ref[0]
    gr = gr_ref[0]
    lane = lax.broadcasted_iota(jnp.int32, gc.shape, 1)
    sub = lax.broadcasted_iota(jnp.int32, gr.shape, 0)
    ig_col = jnp.sum(jnp.where(lane == i_idx, gc, 0.0), axis=1, keepdims=True)
    f_col = jnp.sum(jnp.where(lane == f_idx, gc, 0.0), axis=1, keepdims=True)
    ig_row = jnp.sum(jnp.where(sub == i_idx, gr, 0.0), axis=0, keepdims=True)
    f_row = jnp.sum(jnp.where(sub == f_idx, gr, 0.0), axis=0, keepdims=True)
    lf_col = _log_sigmoid(f_col)
    lf_row = _log_sigmoid(f_row)

    t_i = lax.broadcasted_iota(jnp.int32, (L, L), 0)
    s_i = lax.broadcasted_iota(jnp.int32, (L, L), 1)
    seen = (s_i <= t_i) if direction == 0 else (s_i >= t_i)
    b_col = jnp.sum(jnp.where(seen, lf_row, 0.0), axis=1, keepdims=True)
    seen_t = (t_i <= s_i) if direction == 0 else (t_i >= s_i)
    b_row = jnp.sum(jnp.where(seen_t, lf_col, 0.0), axis=0, keepdims=True)
    b_end = jnp.sum(lf_row, axis=1, keepdims=True)

    m_prev = m_ref[:, 0:1]
    log_d = jnp.where(seen, b_col - b_row + ig_row, -jnp.inf)
    g_col = b_col + m_prev
    m_t = jnp.maximum(g_col, jnp.max(log_d, axis=1, keepdims=True))
    dw = jnp.exp(log_d - m_t)
    inter = jnp.exp(g_col - m_t)

    q = q_ref[0]
    k = k_ref[0] * (DH ** -0.5)
    v = v_ref[0]
    s = lax.dot_general(q, k, (((1,), (1,)), ((), ())), preferred_element_type=f32) * dw
    num = _dot(s.astype(bf16), v) + inter * _dot(q, ctb_ref[...])
    qn = jnp.sum(q.astype(f32) * n_ref[...], axis=1, keepdims=True)
    den = jnp.sum(s, axis=1, keepdims=True) + inter * qn
    hout = num / jnp.maximum(jnp.abs(den), jnp.exp(-m_t))

    w_end = b_end - b_col + ig_col
    m_new = jnp.maximum(b_end + m_prev, jnp.max(w_end, axis=0, keepdims=True))
    decay = jnp.exp(b_end + m_prev - m_new)
    kw = k.astype(f32) * jnp.exp(w_end - m_new)
    upd = lax.dot_general(kw.astype(bf16), v, (((0,), (0,)), ((), ())), preferred_element_type=f32)
    c_new = decay * ct_ref[...] + upd
    ct_ref[...] = c_new
    ctb_ref[...] = c_new.astype(bf16)
    n_ref[...] = decay * n_ref[...] + jnp.sum(kw, axis=0, keepdims=True)
    m_ref[...] = jnp.broadcast_to(m_new, m_ref.shape)

    if not finish:
        o_ref[0] = hout.astype(o_ref.dtype)
    else:
        hs = hout + hf_ref[0].astype(f32)
        mu = jnp.mean(hs, axis=1, keepdims=True)
        xc_ = hs - mu
        var = jnp.mean(xc_ * xc_, axis=1, keepdims=True)
        hn = xc_ * lax.rsqrt(var + LN_EPS)
        z = z_ref[0].astype(f32)
        o_ref[0] = ((hn * nw_ref[...] + sk_ref[...] * xc_ref[0].astype(f32))
                    * (z * _sigmoid(z))).astype(o_ref.dtype)


def _mlstm(q, k, v, gates, gates_t, direction, finish_args=None):
    L = MLSTM_CHUNK
    n_chunks = U // L

    if direction == 0:
        def chunk(s):
            return s
    else:
        def chunk(s):
            return jnp.where(s == 0, 0, n_chunks - s)

    tile = pl.BlockSpec((1, L, DH), lambda b, h, s: (b, chunk(s), h))
    in_specs = [tile, tile, tile,
                pl.BlockSpec((1, L, 4 * NH), lambda b, h, s: (b, chunk(s), 0)),
                pl.BlockSpec((1, 4 * NH, L), lambda b, h, s: (b, 0, chunk(s)))]
    args = [q, k, v, gates, gates_t]
    finish = finish_args is not None
    if finish:
        h_fwd, xc, xz, norm_w, skip = finish_args
        vec = pl.BlockSpec((1, DH), lambda b, h, s: (0, h))
        in_specs += [tile, tile,
                     pl.BlockSpec((1, L, DH), lambda b, h, s: (b, chunk(s), NH + h)),
                     vec, vec]
        args += [h_fwd, xc, xz.reshape(BATCH, U, 2 * E), norm_w.reshape(1, E), skip.reshape(1, E)]
    return pl.pallas_call(
        functools.partial(_mlstm_kernel, direction=direction, finish=finish),
        grid=(BATCH, NH, n_chunks),
        in_specs=in_specs,
        out_specs=tile,
        out_shape=jax.ShapeDtypeStruct((BATCH, U, E), bf16),
        scratch_shapes=[pltpu.VMEM((DH, DH), f32), pltpu.VMEM((DH, DH), bf16),
                        pltpu.VMEM((1, DH), f32), pltpu.VMEM((1, 128), f32)],
        compiler_params=_cparams(3),
        name="mlstm_bwd_finish" if finish else "mlstm_fwd",
    )(*args)


def _ln_rows(r, g, b):
    mu = jnp.mean(r, axis=1, keepdims=True)
    rc = r - mu
    var = jnp.mean(rc * rc, axis=1, keepdims=True)
    return rc * lax.rsqrt(var + LN_EPS) * g + b


def _ln_router_kernel(*refs, joint):
    hs_refs, refs = refs[:1 + joint], refs[1 + joint:]
    (o_ref, m_ref, g_ref, b_ref, rw_ref, rb_ref,
     hs1_ref, tok_ref, gate_ref, eid_ref, rank_ref, cnt_ref, carry_ref) = refs

    @pl.when(pl.program_id(0) == 0)
    def _():
        carry_ref[...] = jnp.zeros_like(carry_ref)

    m = m_ref[0]
    hs = _joint_rows(*hs_refs) if joint else hs_refs[0][...]
    r = ALPHA * hs + m[2:3] * o_ref[...].astype(f32)
    hs1 = _ln_rows(r, g_ref[...], b_ref[...])
    hs1_ref[...] = hs1
    tok = hs1 * (1.0 + m[4:5]) + m[3:4]
    t_hi = tok.astype(bf16)
    _store_tile_rows(tok_ref, _pack_halves(tok))
    t_lo = (tok - t_hi.astype(f32)).astype(bf16)
    w = rw_ref[...]
    w_hi = w.astype(bf16)
    w_lo = (w - w_hi.astype(f32)).astype(bf16)
    logits = _dot(t_hi, w_hi) + _dot(t_hi, w_lo) + _dot(t_lo, w_hi) + rb_ref[...]

    lane = lax.broadcasted_iota(jnp.int32, logits.shape, 1).astype(f32)
    left = logits
    vals, ids = [], []
    for _ in range(TOP_K):
        v = jnp.max(left, axis=1, keepdims=True)
        e = jnp.min(jnp.where(left == v, lane, float(NE)), axis=1, keepdims=True)
        vals.append(v)
        ids.append(e)
        left = jnp.where(lane == e, -jnp.inf, left)
    exps = [jnp.exp(v - vals[0]) for v in vals]
    den = exps[0]
    for x in exps[1:]:
        den = den + x
    gate_ref[...] = jnp.concatenate([x / den for x in exps], axis=1)
    eid_ref[...] = jnp.concatenate(ids, axis=1).astype(jnp.int32)

    member = (lane == ids[0]).astype(f32)
    for e in ids[1:]:
        member = member + (lane == e).astype(f32)
    rows = logits.shape[0]
    earlier = (lax.broadcasted_iota(jnp.int32, (rows, rows), 1)
               < lax.broadcasted_iota(jnp.int32, (rows, rows), 0)).astype(bf16)
    before = _dot(earlier, member.astype(bf16)) + carry_ref[...]
    rank_ref[...] = jnp.concatenate(
        [jnp.sum(jnp.where(lane == e, before, 0.0), axis=1, keepdims=True) for e in ids],
        axis=1).astype(jnp.int32)
    total = carry_ref[...] + jnp.sum(member, axis=0, keepdims=True)
    carry_ref[...] = total
    cnt_ref[...] = total


def _ln_router(hs_parts, o, mt, group, ln_g, ln_b, r_w, r_b):
    m = o.shape[0]
    joint = len(hs_parts) == 2
    row = pl.BlockSpec((ROW_TILE, D), lambda i: (i, 0))
    vec = pl.BlockSpec((1, D), lambda i: (0, 0))
    topk = pl.BlockSpec((ROW_TILE, TOP_K), lambda i: (i, 0))
    return pl.pallas_call(
        functools.partial(_ln_router_kernel, joint=joint),
        grid=(m // ROW_TILE,),
        in_specs=([_CTX_ROWS, _LATENT_ROWS] if joint else [row]) + [
                  row,
                  pl.BlockSpec((1, 6, D), lambda i: (group(i), 0, 0)),
                  vec, vec,
                  pl.BlockSpec((D, NE), lambda i: (0, 0)),
                  pl.BlockSpec((1, NE), lambda i: (0, 0))],
        out_specs=[row, pl.BlockSpec((ROW_TILE * TILE_ROWS, LANES), lambda i: (i, 0)),
                   topk, topk, topk, pl.BlockSpec((1, NE), lambda i: (0, 0))],
        out_shape=[jax.ShapeDtypeStruct((m, D), f32),
                   jax.ShapeDtypeStruct((m * TILE_ROWS, LANES), jnp.uint32),
                   jax.ShapeDtypeStruct((m, TOP_K), f32),
                   jax.ShapeDtypeStruct((m, TOP_K), jnp.int32),
                   jax.ShapeDtypeStruct((m, TOP_K), jnp.int32),
                   jax.ShapeDtypeStruct((1, NE), f32)],
        scratch_shapes=[pltpu.VMEM((1, NE), f32)],
        compiler_params=_cparams(1),
        name="ln_router",
    )(*hs_parts, o, mt, ln_g.reshape(1, D), ln_b.reshape(1, D), r_w, r_b.reshape(1, NE))


def _first_of_expert(be_ref, i):
    return jnp.logical_or(i == 0, be_ref[i] != be_ref[jnp.maximum(i - 1, 0)])


ROW_STEPS = 4


def _for_real_rows(valid, rows, compute):
    q = MOE_TM // ROW_STEPS
    for step in range(1, ROW_STEPS + 1):
        covers = rows <= step * q
        if step > 1:
            covers = jnp.logical_and(covers, rows > (step - 1) * q)
        pl.when(jnp.logical_and(valid, covers))(functools.partial(compute, step * q))


def _moe_up_kernel(be_ref, nu_ref, nxt_ref, rows_ref, x_ref, w_hbm, bg_ref, bl_ref, o_ref, stage_ref,
                   cache_ref, sem, *, layer, tf):
    j = pl.program_id(0)
    i = pl.program_id(1)
    valid = i < nu_ref[0]

    def tile_copy(e, jj, half):
        col = pl.multiple_of(half * F + jj * tf, tf)
        return pltpu.make_async_copy(w_hbm.at[layer, e, :, pl.ds(col, tf)], stage_ref.at[half], sem.at[half])

    def start(e, jj):
        tile_copy(e, jj, 0).start()
        tile_copy(e, jj, 1).start()

    @pl.when(jnp.logical_and(j == 0, i == 0))
    def _():
        start(be_ref[0], 0)

    @pl.when(jnp.logical_and(valid, _first_of_expert(be_ref, i)))
    def _():
        tile_copy(0, 0, 0).wait()
        tile_copy(0, 0, 1).wait()
        cache_ref[...] = stage_ref[...].astype(bf16)
        nxt = nxt_ref[i]

        @pl.when(nxt >= 0)
        def _():
            start(nxt, j)

        @pl.when(jnp.logical_and(nxt < 0, j + 1 < pl.num_programs(0)))
        def _():
            start(be_ref[0], j + 1)

    def compute(rows):
        x = _unpack_halves(_load_tile_rows(x_ref, rows)).astype(bf16)
        glu = jnp.minimum(_dot(x, cache_ref[0]) + bg_ref[0, 0], SWIGLU_LIMIT)
        lin = jnp.clip(_dot(x, cache_ref[1]) + bl_ref[0, 0], -SWIGLU_LIMIT, SWIGLU_LIMIT)
        o_ref[pl.ds(0, rows), :] = (glu * _sigmoid(SWIGLU_ALPHA * glu) * (lin + 1.0)).astype(o_ref.dtype)
        if rows < MOE_TM:
            o_ref[pl.ds(rows, MOE_TM - rows), :] = jnp.zeros((MOE_TM - rows, tf), o_ref.dtype)

    _for_real_rows(valid, rows_ref[i], compute)

    @pl.when(jnp.logical_not(valid))
    def _():
        o_ref[...] = jnp.zeros_like(o_ref)


def _moe_down_kernel(be_ref, nu_ref, nxt_ref, rows_ref, a_ref, w_hbm, b_ref, o_ref, stage_ref, cache_ref, sem,
                     *, layer):
    i = pl.program_id(0)
    valid = i < nu_ref[0]

    def expert_copy(e):
        return pltpu.make_async_copy(w_hbm.at[layer, e], stage_ref, sem)

    @pl.when(i == 0)
    def _():
        expert_copy(be_ref[0]).start()

    @pl.when(jnp.logical_and(valid, _first_of_expert(be_ref, i)))
    def _():
        expert_copy(0).wait()
        cache_ref[...] = stage_ref[...].astype(bf16)
        nxt = nxt_ref[i]

        @pl.when(nxt >= 0)
        def _():
            expert_copy(nxt).start()

    def compute(rows):
        y = _dot(a_ref[pl.ds(0, rows), :], cache_ref[...]) + b_ref[0, 0]
        _store_tile_rows(o_ref, _pack_halves(y))
        if rows < MOE_TM:
            o_ref[pl.ds(rows * TILE_ROWS, (MOE_TM - rows) * TILE_ROWS), :] = jnp.zeros(
                ((MOE_TM - rows) * TILE_ROWS, LANES), o_ref.dtype)

    _for_real_rows(valid, rows_ref[i], compute)

    @pl.when(jnp.logical_not(valid))
    def _():
        o_ref[...] = jnp.zeros_like(o_ref)


def _moe_experts(layer, xs, block_e, n_used, next_e, block_rows, w1, b1, w2, b2):
    ns = xs.shape[0] // TILE_ROWS
    nb = ns // MOE_TM
    tf = 1024
    lin0 = F // tf
    b1 = b1.reshape(DEPTH, NE, 1, 2 * F)
    b2 = b2.reshape(DEPTH, NE, 1, D)

    def bspec(col0):
        return pl.BlockSpec((1, 1, 1, tf), lambda j, i, be, nu, nx, br: (layer, be[i], 0, col0 + j))

    act = pl.pallas_call(
        functools.partial(_moe_up_kernel, layer=layer, tf=tf),
        grid_spec=pltpu.PrefetchScalarGridSpec(
            num_scalar_prefetch=4,
            grid=(F // tf, nb),
            in_specs=[pl.BlockSpec((MOE_TM * TILE_ROWS, LANES),
                                   lambda j, i, be, nu, nx, br: (jnp.minimum(i, nu[0] - 1), 0)),
                      pl.BlockSpec(memory_space=pl.ANY), bspec(0), bspec(lin0)],
            out_specs=pl.BlockSpec((MOE_TM, tf), lambda j, i, be, nu, nx, br: (i, j)),
            scratch_shapes=[pltpu.VMEM((2, D, tf), f32), pltpu.VMEM((2, D, tf), bf16),
                            pltpu.SemaphoreType.DMA((2,))]),
        out_shape=jax.ShapeDtypeStruct((ns, F), bf16),
        compiler_params=_cparams(2),
        name="moe_up",
    )(block_e, n_used, next_e, block_rows, xs, w1, b1, b1)

    return pl.pallas_call(
        functools.partial(_moe_down_kernel, layer=layer),
        grid_spec=pltpu.PrefetchScalarGridSpec(
            num_scalar_prefetch=4,
            grid=(nb,),
            in_specs=[pl.BlockSpec((MOE_TM, F), lambda i, be, nu, nx, br: (jnp.minimum(i, nu[0] - 1), 0)),
                      pl.BlockSpec(memory_space=pl.ANY),
                      pl.BlockSpec((1, 1, 1, D), lambda i, be, nu, nx, br: (layer, be[i], 0, 0))],
            out_specs=pl.BlockSpec((MOE_TM * TILE_ROWS, LANES), lambda i, be, nu, nx, br: (i, 0)),
            scratch_shapes=[pltpu.VMEM((F, D), f32), pltpu.VMEM((F, D), bf16), pltpu.SemaphoreType.DMA]),
        out_shape=jax.ShapeDtypeStruct((ns * TILE_ROWS, LANES), jnp.uint32),
        compiler_params=_cparams(1),
        name="moe_down",
    )(block_e, n_used, next_e, block_rows, act, w2, b2)


GATHER_ROWS = 2048


def _gather_kernel(idx_ref, src_ref, out_ref, sem):
    n = GATHER_ROWS
    t = TILE_ROWS

    def issue(r, carry):
        src_row = pl.multiple_of(idx_ref[0, 0, r] * t, t)
        dst_row = pl.multiple_of(r * t, t)
        pltpu.make_async_copy(src_ref.at[pl.ds(src_row, t)], out_ref.at[pl.ds(dst_row, t)], sem).start()
        return carry

    lax.fori_loop(0, n, issue, 0, unroll=8)
    pltpu.make_async_copy(src_ref.at[pl.ds(0, n * t)], out_ref, sem).wait()


def _gather_rows(src, idx):
    n = idx.shape[0]
    nblk = n // GATHER_ROWS
    return pl.pallas_call(
        _gather_kernel,
        grid=(nblk,),
        in_specs=[pl.BlockSpec((1, 1, GATHER_ROWS), lambda i: (i, 0, 0), memory_space=pltpu.SMEM),
                  pl.BlockSpec(memory_space=pl.ANY)],
        out_specs=pl.BlockSpec((GATHER_ROWS * TILE_ROWS, LANES), lambda i: (i, 0)),
        out_shape=jax.ShapeDtypeStruct((n * TILE_ROWS, LANES), src.dtype),
        scratch_shapes=[pltpu.SemaphoreType.DMA],
        compiler_params=pltpu.CompilerParams(dimension_semantics=("arbitrary",),
                                             vmem_limit_bytes=VMEM_LIMIT,
                                             disable_bounds_checks=True),
        name="gather_rows",
    )(idx.reshape(nblk, 1, GATHER_ROWS), src)


SCATTER_TOKENS = 512


def _scatter_kernel(dest_ref, src_ref, init_ref, out_ref, sem):
    del init_ref
    n = SCATTER_TOKENS
    t = TILE_ROWS

    def issue(a, carry):
        src_row = pl.multiple_of(lax.shift_right_logical(a, TOP_K.bit_length() - 1) * t, t)
        dst_row = pl.multiple_of(dest_ref[0, 0, a] * t, t)
        pltpu.make_async_copy(src_ref.at[pl.ds(src_row, t)], out_ref.at[pl.ds(dst_row, t)], sem).start()
        return carry

    lax.fori_loop(0, n * TOP_K, issue, 0, unroll=8)
    for _ in range(TOP_K):
        pltpu.make_async_copy(src_ref, out_ref.at[pl.ds(0, n * t)], sem).wait()


def _scatter_rows(src, dest, n_slots):
    n = src.shape[0] // TILE_ROWS
    nblk = n // SCATTER_TOKENS
    init = jnp.zeros((n_slots * TILE_ROWS, LANES), src.dtype)
    return pl.pallas_call(
        _scatter_kernel,
        grid=(nblk,),
        in_specs=[pl.BlockSpec((1, 1, SCATTER_TOKENS * TOP_K), lambda i: (i, 0, 0), memory_space=pltpu.SMEM),
                  pl.BlockSpec((SCATTER_TOKENS * TILE_ROWS, LANES), lambda i: (i, 0)),
                  pl.BlockSpec(memory_space=pl.ANY)],
        out_specs=pl.BlockSpec(memory_space=pl.ANY),
        out_shape=jax.ShapeDtypeStruct(init.shape, src.dtype),
        input_output_aliases={2: 0},
        scratch_shapes=[pltpu.SemaphoreType.DMA],
        compiler_params=pltpu.CompilerParams(dimension_semantics=("arbitrary",),
                                             vmem_limit_bytes=VMEM_LIMIT,
                                             disable_bounds_checks=True),
        name="scatter_rows",
    )(dest.reshape(nblk, 1, SCATTER_TOKENS * TOP_K), src, init)


def _slot_layout(eid, rank, counts):
    t = eid.shape[0]
    nb = -(-(t * TOP_K) // MOE_TM) + NE
    counts = counts.reshape(NE).astype(jnp.int32)
    experts = jnp.arange(NE, dtype=jnp.int32)
    pcounts = (counts + MOE_TM - 1) // MOE_TM * MOE_TM
    pends = jnp.cumsum(pcounts)
    pstarts = pends - pcounts
    dest = rank + jnp.sum(jnp.where(eid[:, :, None] == experts, pstarts, 0), axis=-1)
    n_used = pends[-1] // MOE_TM
    blk = jnp.arange(nb, dtype=jnp.int32)
    block_e = jnp.minimum(jnp.sum(pends[None, :] <= (blk * MOE_TM)[:, None], axis=1), NE - 1)
    of_block = block_e[:, None] == experts

    def per_block(table):
        return jnp.sum(jnp.where(of_block, table, 0), axis=1)

    block_rows = jnp.clip(per_block(counts) - (blk * MOE_TM - per_block(pstarts)), 0, MOE_TM)
    block_rows = jnp.where(blk < n_used, block_rows, 0)
    group_end = per_block(pends) // MOE_TM
    next_e = jnp.sum(jnp.where(group_end[:, None] == blk, block_e, 0), axis=1)
    next_e = jnp.where(group_end < n_used, next_e, -1)
    last_e = jnp.sum(jnp.where(blk == n_used - 1, block_e, 0))
    block_e = jnp.where(blk < n_used, block_e, last_e)
    i32 = jnp.int32
    return (dest.astype(i32), block_e.astype(i32), n_used.astype(i32).reshape(1), next_e.astype(i32),
            block_rows.astype(i32))


def _combine_kernel(hs_ref, y_ref, gate_ref, m_ref, g_ref, b_ref, *rest, joint):
    if joint:
        mn_ref, hs2_ref, nxt_ref, ctx_ref = rest
    else:
        (hs2_ref,) = rest
    m = m_ref[0]
    gate = gate_ref[...]
    y = gate[:, 0:1] * _unpack_halves(_load_tile_rows(y_ref.at[0], ROW_TILE))
    for k in range(1, TOP_K):
        y = y + gate[:, k:k + 1] * _unpack_halves(_load_tile_rows(y_ref.at[k], ROW_TILE))
    hs2 = _ln_rows(ALPHA * hs_ref[...] + m[5:6] * y, g_ref[...], b_ref[...])
    if not joint:
        hs2_ref[...] = hs2
        return
    is_ctx = _is_ctx_tile(pl.program_id(0))

    @pl.when(is_ctx)
    def _():
        ctx_ref[...] = hs2

    @pl.when(jnp.logical_not(is_ctx))
    def _():
        mn = mn_ref[0]
        hs2_ref[...] = hs2
        nxt_ref[...] = (hs2 * (1.0 + mn[1:2]) + mn[0:1]).astype(nxt_ref.dtype)


def _combine(hs1, yg, gate, mt, group, ln_g, ln_b, mt_next=None):
    m = hs1.shape[0]
    row = pl.BlockSpec((ROW_TILE, D), lambda i: (i, 0))
    vec = pl.BlockSpec((1, D), lambda i: (0, 0))
    mod = pl.BlockSpec((1, 6, D), lambda i: (group(i), 0, 0))
    in_specs = [row, pl.BlockSpec((TOP_K, ROW_TILE * TILE_ROWS, LANES), lambda i: (0, i, 0)),
                pl.BlockSpec((ROW_TILE, TOP_K), lambda i: (i, 0)), mod, vec, vec]
    args = [hs1, yg, gate, mt, ln_g.reshape(1, D), ln_b.reshape(1, D)]
    joint = mt_next is not None
    if joint:
        in_specs.append(mod)
        args.append(mt_next)
        out_specs = [_LATENT_ROWS, _LATENT_ROWS, _CTX_ROWS]
        out_shape = [jax.ShapeDtypeStruct((BATCH * SEQ, D), f32), jax.ShapeDtypeStruct((BATCH * SEQ, D), bf16),
                     jax.ShapeDtypeStruct((BATCH * LC, D), f32)]
    else:
        out_specs = [row]
        out_shape = [jax.ShapeDtypeStruct((m, D), f32)]
    return pl.pallas_call(
        functools.partial(_combine_kernel, joint=joint),
        grid=(m // ROW_TILE,),
        in_specs=in_specs, out_specs=out_specs, out_shape=out_shape,
        compiler_params=_cparams(1),
        name="moe_combine_ln",
    )(*args)


def _moe_layer(layer, hs_parts, o, mt, group, ln1_g, ln1_b, ln2_g, ln2_b, r_w, r_b, w1, b1, w2, b2,
               mt_next=None):
    t = o.shape[0]
    hs1, tok, gate, eid, rank, counts = _ln_router(hs_parts, o, mt, group, ln1_g[layer], ln1_b[layer],
                                                   r_w[layer], r_b[layer])
    dest, block_e, n_used, next_e, block_rows = _slot_layout(eid, rank, counts)
    xs = _scatter_rows(tok, dest.reshape(-1), block_e.shape[0] * MOE_TM)
    y = _moe_experts(layer, xs, block_e, n_used, next_e, block_rows, w1, b1, w2, b2)
    yg = _gather_rows(y, dest.T.reshape(-1)).reshape(TOP_K, t * TILE_ROWS, LANES)
    return _combine(hs1, yg, gate, mt, group, ln2_g[layer], ln2_b[layer], mt_next)


def _sgu_kernel(g_ref, v_ref, lg_ref, lb_ref, ws_ref, bs_ref, o_ref):
    v = v_ref[...].astype(f32)
    vn = _ln_rows(v, lg_ref[...], lb_ref[...]).astype(bf16)
    gw = E // SGU_GROUPS
    for g in range(SGU_GROUPS):
        mixed = _dot(ws_ref[g].astype(bf16), vn[:, g * gw:(g + 1) * gw]) + bs_ref[:, g:g + 1]
        o_ref[:, g * gw:(g + 1) * gw] = (g_ref[:, g * gw:(g + 1) * gw].astype(f32) * mixed).astype(o_ref.dtype)


def _sgu(uv, ln_g, ln_b, w_s, b_s):
    m = uv.shape[0]
    c = SGU_CHUNK
    vec = pl.BlockSpec((1, E), lambda i: (0, 0))
    return pl.pallas_call(
        _sgu_kernel,
        grid=(m // c,),
        in_specs=[pl.BlockSpec((c, E), lambda i: (i, 0)),
                  pl.BlockSpec((c, E), lambda i: (i, 1)),
                  vec, vec,
                  pl.BlockSpec((SGU_GROUPS, c, c), lambda i: (0, 0, 0)),
                  pl.BlockSpec((c, SGU_GROUPS), lambda i: (0, 0))],
        out_specs=pl.BlockSpec((c, E), lambda i: (i, 0)),
        out_shape=jax.ShapeDtypeStruct((m, E), bf16),
        compiler_params=_cparams(1),
        name="sgu",
    )(uv, uv, ln_g.reshape(1, E), ln_b.reshape(1, E), w_s, b_s.T)


def kernel(x, c, ctx, c_ctx, mod_w, mod_b, ln1_g, ln1_b, ln2_g, ln2_b, a_w_in, a_conv_w, a_conv_b, a_w_q, a_w_k, a_w_v, a_w_gate, a_b_gate, a_norm_w, a_skip, a_w_out, b_w_in, b_ln_g, b_ln_b, b_w_s, b_b_s, b_w_out, r_w, r_b, e_w1, e_b1, e_w2, e_b2):
    cvec = jnp.concatenate([c, c_ctx[None, :], jnp.zeros((8 - BATCH - 1, D), f32)], axis=0)
    mods = _mod_rows(cvec, mod_w, mod_b).reshape(DEPTH, 8, 6, D)
    rows = [r for b in range(BATCH) for r in (BATCH, b)]
    mt = [jnp.stack([mods[l, r] for r in rows], axis=0) for l in range(DEPTH)]

    hs = (ctx.reshape(BATCH * LC, D), x.reshape(BATCH * SEQ, D))
    hx = _modulate(*hs, mt[0])
    xz = _matmul(hx, a_w_in[0], tm=1024, tn=1024, name="mlstm_in_proj")
    xc, q, k, v, gates = _conv_qkv(xz, a_conv_w[0], a_conv_b[0], a_w_q[0], a_w_k[0], a_w_v[0],
                                   a_w_gate[0], a_b_gate[0])
    gates_t = gates.transpose(0, 2, 1)
    h_fwd = _mlstm(q, k, v, gates, gates_t, 0)
    pre = _mlstm(q, k, v, gates, gates_t, 1, (h_fwd, xc, xz, a_norm_w[0], a_skip[0]))
    o = _matmul(pre.reshape(BATCH * U, E), a_w_out[0], tm=1024, tn=512, name="mlstm_out_proj")
    hs, hx, _ = _moe_layer(0, hs, o, mt[0], _group_joint, ln1_g, ln1_b, ln2_g, ln2_b,
                           r_w, r_b, e_w1, e_b1, e_w2, e_b2, mt_next=mt[1])

    uv = _matmul(hx, b_w_in[0], tm=1024, tn=1024, act="gelu", name="sgu_in_proj")
    gated = _sgu(uv, b_ln_g[0], b_ln_b[0], b_w_s[0], b_b_s[0])
    o = _matmul(gated, b_w_out[0], tm=1024, tn=512, name="sgu_out_proj")
    (out,) = _moe_layer(1, (hs,), o, mt[1], _group_latent, ln1_g, ln1_b, ln2_g, ln2_b,
                        r_w, r_b, e_w1, e_b1, e_w2, e_b2)
    return out.reshape(BATCH, SEQ, D)
```

```python
import functools

import jax
import jax.numpy as jnp
from jax import lax
from jax.experimental import pallas as pl
from jax.experimental.pallas import tpu as pltpu

D = 2048
BATCH = 4
SEQ = 2048
DEPTH = 2
GRID_W = 64
LC = 256
U = LC + SEQ
E = 2 * D
NH = 4
DH = E // NH
QKV_BLOCK = 4
SGU_GROUPS = 8
SGU_CHUNK = 128
NE = 32
TOP_K = 4
F = D
SWIGLU_LIMIT = 7.0
SWIGLU_ALPHA = 1.702
ALPHA = (2 * DEPTH) ** 0.25
LN_EPS = 1e-5

ROW_TILE = 256
MLSTM_CHUNK = 256
MOE_TM = 512
BD_TILE = 256
VMEM_LIMIT = 56 * 1024 * 1024

f32 = jnp.float32
bf16 = jnp.bfloat16


def _cparams(n_axes):
    return pltpu.CompilerParams(dimension_semantics=("arbitrary",) * n_axes,
                                vmem_limit_bytes=VMEM_LIMIT)


def _dot(a, b):
    return jnp.dot(a, b, preferred_element_type=f32)


def _sigmoid(x):
    return 1.0 / (1.0 + jnp.exp(-x))


def _pack_halves(x):
    w = x.shape[1] // 2
    bits = lax.bitcast_convert_type(x.astype(bf16).astype(f32), jnp.uint32)
    return (bits[:, w:] & jnp.uint32(0xFFFF0000)) | (bits[:, :w] >> 16)


def _unpack_halves(p):
    lo = lax.bitcast_convert_type(p << 16, f32)
    hi = lax.bitcast_convert_type(p & jnp.uint32(0xFFFF0000), f32)
    return jnp.concatenate([lo, hi], axis=1)


PACKED = D // 2
LANES = 128
TILE_ROWS = PACKED // LANES


def _store_tile_rows(ref, p):
    r = p.shape[0]
    for s in range(TILE_ROWS):
        ref[pl.ds(s, r, stride=TILE_ROWS), :] = p[:, s * LANES:(s + 1) * LANES]


def _load_tile_rows(ref, r):
    return jnp.concatenate([ref[pl.ds(s, r, stride=TILE_ROWS), :] for s in range(TILE_ROWS)], axis=1)


def _mod_kernel(c_ref, w_ref, b_ref, o_ref):
    c = c_ref[...]
    a = (c * _sigmoid(c)).astype(bf16)
    o_ref[0] = _dot(a, w_ref[0].astype(bf16)) + b_ref[0]


def _mod_rows(cvec, mod_w, mod_b):
    tn = 1024
    return pl.pallas_call(
        _mod_kernel,
        grid=(DEPTH, 6 * D // tn),
        in_specs=[pl.BlockSpec((8, D), lambda l, j: (0, 0)),
                  pl.BlockSpec((1, D, tn), lambda l, j: (l, 0, j)),
                  pl.BlockSpec((1, 1, tn), lambda l, j: (l, 0, j))],
        out_specs=pl.BlockSpec((1, 8, tn), lambda l, j: (l, 0, j)),
        out_shape=jax.ShapeDtypeStruct((DEPTH, 8, 6 * D), f32),
        compiler_params=_cparams(2),
        name="adaln_rows",
    )(cvec, mod_w, mod_b.reshape(DEPTH, 1, 6 * D))


def _group_joint(i):
    tiles = U // ROW_TILE
    return 2 * (i // tiles) + jnp.minimum(i % tiles, 1)


def _group_latent(i):
    return 2 * (i // (SEQ // ROW_TILE)) + 1


JOINT_TILES = U // ROW_TILE


def _is_ctx_tile(i):
    return i % JOINT_TILES == 0


def _ctx_tile(i):
    return i // JOINT_TILES


def _latent_tile(i):
    return (i // JOINT_TILES) * (SEQ // ROW_TILE) + jnp.maximum(i % JOINT_TILES - 1, 0)


_CTX_ROWS = pl.BlockSpec((ROW_TILE, D), lambda i: (_ctx_tile(i), 0))
_LATENT_ROWS = pl.BlockSpec((ROW_TILE, D), lambda i: (_latent_tile(i), 0))


def _joint_rows(ctx_ref, lat_ref):
    is_ctx = _is_ctx_tile(pl.program_id(0))
    return jnp.where(is_ctx, ctx_ref[...], lat_ref[...])


def _modulate_kernel(ctx_ref, lat_ref, m_ref, o_ref):
    m = m_ref[0]
    o_ref[...] = (_joint_rows(ctx_ref, lat_ref) * (1.0 + m[1:2]) + m[0:1]).astype(o_ref.dtype)


def _modulate(ctx2d, x2d, mt):
    m = ctx2d.shape[0] + x2d.shape[0]
    return pl.pallas_call(
        _modulate_kernel,
        grid=(m // ROW_TILE,),
        in_specs=[_CTX_ROWS, _LATENT_ROWS,
                  pl.BlockSpec((1, 6, D), lambda i: (_group_joint(i), 0, 0))],
        out_specs=pl.BlockSpec((ROW_TILE, D), lambda i: (i, 0)),
        out_shape=jax.ShapeDtypeStruct((m, D), bf16),
        compiler_params=_cparams(1),
        name="modulate",
    )(ctx2d, x2d, mt)


def _erf(x):
    return lax.erf(x)


def _mm_kernel(x_ref, w_ref, o_ref, wb_ref, *, act):
    @pl.when(pl.program_id(1) == 0)
    def _():
        wb_ref[...] = w_ref[...].astype(bf16)

    acc = _dot(x_ref[...], wb_ref[...])
    if act == "gelu":
        acc = 0.5 * acc * (1.0 + _erf(acc * (2.0 ** -0.5)))
    o_ref[...] = acc.astype(o_ref.dtype)


def _matmul(x, w, *, tm, tn, act=None, name):
    m, k = x.shape
    n = w.shape[1]
    return pl.pallas_call(
        functools.partial(_mm_kernel, act=act),
        grid=(n // tn, m // tm),
        in_specs=[pl.BlockSpec((tm, k), lambda j, i: (i, 0)),
                  pl.BlockSpec((k, tn), lambda j, i: (0, j))],
        out_specs=pl.BlockSpec((tm, tn), lambda j, i: (i, j)),
        out_shape=jax.ShapeDtypeStruct((m, n), bf16),
        scratch_shapes=[pltpu.VMEM((k, tn), bf16)],
        compiler_params=_cparams(2),
        name=name,
    )(x, w)


def _conv_qkv_kernel(xm_ref, cw_ref, cb_ref, wq_ref, wk_ref, wv_ref, wg_ref, bg_ref,
                     xc_ref, q_ref, k_ref, v_ref, g_ref):
    ct = pl.program_id(1)
    a = xm_ref[0].astype(f32)
    c = a.shape[1]
    r = lax.broadcasted_iota(jnp.int32, (U, c), 0)
    latent = r >= LC
    p = r - LC
    col = jnp.where(latent, p & (GRID_W - 1), r)
    last = jnp.where(latent, GRID_W - 1, LC - 1)
    a_l = jnp.where(col > 0, pltpu.roll(a, 1, 0), 0.0)
    a_r = jnp.where(col < last, pltpu.roll(a, U - 1, 0), 0.0)
    w = cw_ref[...]
    rows = [w[3 * i:3 * i + 1] * a_l + w[3 * i + 1:3 * i + 2] * a + w[3 * i + 2:3 * i + 3] * a_r
            for i in range(3)]
    up = jnp.where(p >= GRID_W, pltpu.roll(rows[0], GRID_W, 0), 0.0)
    down = jnp.where(latent & (p < SEQ - GRID_W), pltpu.roll(rows[2], U - GRID_W, 0), 0.0)
    pre = rows[1] + up + down + cb_ref[...]
    xc = (pre * _sigmoid(pre)).astype(bf16)
    xc_ref[0] = xc
    xm = xm_ref[0]
    q = _dot(xc, _diag_tile(wq_ref)).astype(bf16)
    k = _dot(xc, _diag_tile(wk_ref)).astype(bf16)
    v = _dot(xm, _diag_tile(wv_ref)).astype(bf16)
    q_ref[0] = q
    k_ref[0] = k
    v_ref[0] = v
    g = (_dot(q, wg_ref[0].astype(bf16)) + _dot(k, wg_ref[1].astype(bf16))
         + _dot(v, wg_ref[2].astype(bf16)))

    @pl.when(ct == 0)
    def _():
        g_ref[0] = g + bg_ref[...]

    @pl.when(ct > 0)
    def _():
        g_ref[0] += g


def _diag_tile(ref):
    r = lax.broadcasted_iota(jnp.int32, ref.shape, 0) // QKV_BLOCK
    c = lax.broadcasted_iota(jnp.int32, ref.shape, 1) // QKV_BLOCK
    return jnp.where(r == c, ref[...], 0.0).astype(bf16)


def _block_rows_tiled(w):
    return jnp.tile(w.reshape(E, QKV_BLOCK), (1, BD_TILE // QKV_BLOCK))


def _conv_qkv(xz, conv_w, conv_b, w_q, w_k, w_v, w_gate, b_gate):
    c = BD_TILE
    n_gate = 4 * NH
    act = jax.ShapeDtypeStruct((BATCH, U, E), bf16)
    tile = pl.BlockSpec((1, U, c), lambda b, t: (b, 0, t))
    bd = pl.BlockSpec((c, c), lambda b, t: (t, 0))
    return pl.pallas_call(
        _conv_qkv_kernel,
        grid=(BATCH, E // c),
        in_specs=[tile,
                  pl.BlockSpec((9, c), lambda b, t: (0, t)),
                  pl.BlockSpec((1, c), lambda b, t: (0, t)),
                  bd, bd, bd,
                  pl.BlockSpec((3, c, n_gate), lambda b, t: (0, t, 0)),
                  pl.BlockSpec((1, n_gate), lambda b, t: (0, 0))],
        out_specs=[tile, tile, tile, tile,
                   pl.BlockSpec((1, U, n_gate), lambda b, t: (b, 0, 0))],
        out_shape=[act, act, act, act, jax.ShapeDtypeStruct((BATCH, U, n_gate), f32)],
        compiler_params=_cparams(2),
        name="conv_qkv_gates",
    )(xz.reshape(BATCH, U, 2 * E), conv_w.reshape(9, E), conv_b.reshape(1, E),
      _block_rows_tiled(w_q), _block_rows_tiled(w_k), _block_rows_tiled(w_v),
      w_gate.reshape(3, E, n_gate), b_gate.reshape(1, n_gate))


def _log_sigmoid(x):
    return jnp.minimum(x, 0.0) - jnp.log1p(jnp.exp(-jnp.abs(x)))


def _mlstm_kernel(*refs, direction, finish):
    if finish:
        (q_ref, k_ref, v_ref, gc_ref, gr_ref, hf_ref, xc_ref, z_ref, nw_ref, sk_ref,
         o_ref, ct_ref, ctb_ref, n_ref, m_ref) = refs
    else:
        q_ref, k_ref, v_ref, gc_ref, gr_ref, o_ref, ct_ref, ctb_ref, n_ref, m_ref = refs
    L = MLSTM_CHUNK
    h = pl.program_id(1)
    step = pl.program_id(2)

    @pl.when(step == 0)
    def _():
        ct_ref[...] = jnp.zeros_like(ct_ref)
        ctb_ref[...] = jnp.zeros_like(ctb_ref)
        n_ref[...] = jnp.zeros_like(n_ref)
        m_ref[...] = jnp.full_like(m_ref, -jnp.inf)

    i_idx = 2 * direction * NH + h
    f_idx = (2 * direction + 1) * NH + h
    gc = gc_ref[0]
    gr = gr_ref[0]
    lane = lax.broadcasted_iota(jnp.int32, gc.shape, 1)
    sub = lax.broadcasted_iota(jnp.int32, gr.shape, 0)
    ig_col = jnp.sum(jnp.where(lane == i_idx, gc, 0.0), axis=1, keepdims=True)
    f_col = jnp.sum(jnp.where(lane == f_idx, gc, 0.0), axis=1, keepdims=True)
    ig_row = jnp.sum(jnp.where(sub == i_idx, gr, 0.0), axis=0, keepdims=True)
    f_row = jnp.sum(jnp.where(sub == f_idx, gr, 0.0), axis=0, keepdims=True)
    lf_col = _log_sigmoid(f_col)
    lf_row = _log_sigmoid(f_row)

    t_i = lax.broadcasted_iota(jnp.int32, (L, L), 0)
    s_i = lax.broadcasted_iota(jnp.int32, (L, L), 1)
    seen = (s_i <= t_i) if direction == 0 else (s_i >= t_i)
    b_col = jnp.sum(jnp.where(seen, lf_row, 0.0), axis=1, keepdims=True)
    seen_t = (t_i <= s_i) if direction == 0 else (t_i >= s_i)
    b_row = jnp.sum(jnp.where(seen_t, lf_col, 0.0), axis=0, keepdims=True)
    b_end = jnp.sum(lf_row, axis=1, keepdims=True)

    m_prev = m_ref[:, 0:1]
    log_d = jnp.where(seen, b_col - b_row + ig_row, -jnp.inf)
    g_col = b_col + m_prev
    m_t = jnp.maximum(g_col, jnp.max(log_d, axis=1, keepdims=True))
    dw = jnp.exp(log_d - m_t)
    inter = jnp.exp(g_col - m_t)

    q = q_ref[0]
    k = k_ref[0] * (DH ** -0.5)
    v = v_ref[0]
    s = lax.dot_general(q, k, (((1,), (1,)), ((), ())), preferred_element_type=f32) * dw
    num = _dot(s.astype(bf16), v) + inter * _dot(q, ctb_ref[...])
    qn = jnp.sum(q.astype(f32) * n_ref[...], axis=1, keepdims=True)
    den = jnp.sum(s, axis=1, keepdims=True) + inter * qn
    hout = num / jnp.maximum(jnp.abs(den), jnp.exp(-m_t))

    w_end = b_end - b_col + ig_col
    m_new = jnp.maximum(b_end + m_prev, jnp.max(w_end, axis=0, keepdims=True))
    decay = jnp.exp(b_end + m_prev - m_new)
    kw = k.astype(f32) * jnp.exp(w_end - m_new)
    upd = lax.dot_general(kw.astype(bf16), v, (((0,), (0,)), ((), ())), preferred_element_type=f32)
    c_new = decay * ct_ref[...] + upd
    ct_ref[...] = c_new
    ctb_ref[...] = c_new.astype(bf16)
    n_ref[...] = decay * n_ref[...] + jnp.sum(kw, axis=0, keepdims=True)
    m_ref[...] = jnp.broadcast_to(m_new, m_ref.shape)

    if not finish:
        o_ref[0] = hout.astype(o_ref.dtype)
    else:
        hs = hout + hf_ref[0].astype(f32)
        mu = jnp.mean(hs, axis=1, keepdims=True)
        xc_ = hs - mu
        var = jnp.mean(xc_ * xc_, axis=1, keepdims=True)
        hn = xc_ * lax.rsqrt(var + LN_EPS)
        z = z_ref[0].astype(f32)
        o_ref[0] = ((hn * nw_ref[...] + sk_ref[...] * xc_ref[0].astype(f32))
                    * (z * _sigmoid(z))).astype(o_ref.dtype)


def _mlstm(q, k, v, gates, gates_t, direction, finish_args=None):
    L = MLSTM_CHUNK
    n_chunks = U // L

    if direction == 0:
        def chunk(s):
            return s
    else:
        def chunk(s):
            return jnp.where(s == 0, 0, n_chunks - s)

    tile = pl.BlockSpec((1, L, DH), lambda b, h, s: (b, chunk(s), h))
    in_specs = [tile, tile, tile,
                pl.BlockSpec((1, L, 4 * NH), lambda b, h, s: (b, chunk(s), 0)),
                pl.BlockSpec((1, 4 * NH, L), lambda b, h, s: (b, 0, chunk(s)))]
    args = [q, k, v, gates, gates_t]
    finish = finish_args is not None
    if finish:
        h_fwd, xc, xz, norm_w, skip = finish_args
        vec = pl.BlockSpec((1, DH), lambda b, h, s: (0, h))
        in_specs += [tile, tile,
                     pl.BlockSpec((1, L, DH), lambda b, h, s: (b, chunk(s), NH + h)),
                     vec, vec]
        args += [h_fwd, xc, xz.reshape(BATCH, U, 2 * E), norm_w.reshape(1, E), skip.reshape(1, E)]
    return pl.pallas_call(
        functools.partial(_mlstm_kernel, direction=direction, finish=finish),
        grid=(BATCH, NH, n_chunks),
        in_specs=in_specs,
        out_specs=tile,
        out_shape=jax.ShapeDtypeStruct((BATCH, U, E), bf16),
        scratch_shapes=[pltpu.VMEM((DH, DH), f32), pltpu.VMEM((DH, DH), bf16),
                        pltpu.VMEM((1, DH), f32), pltpu.VMEM((1, 128), f32)],
        compiler_params=_cparams(3),
        name="mlstm_bwd_finish" if finish else "mlstm_fwd",
    )(*args)


def _ln_rows(r, g, b):
    mu = jnp.mean(r, axis=1, keepdims=True)
    rc = r - mu
    var = jnp.mean(rc * rc, axis=1, keepdims=True)
    return rc * lax.rsqrt(var + LN_EPS) * g + b


def _ln_router_kernel(*refs, joint):
    hs_refs, refs = refs[:1 + joint], refs[1 + joint:]
    (o_ref, m_ref, g_ref, b_ref, rw_ref, rb_ref,
     hs1_ref, tok_ref, gate_ref, eid_ref, rank_ref, cnt_ref, carry_ref) = refs

    @pl.when(pl.program_id(0) == 0)
    def _():
        carry_ref[...] = jnp.zeros_like(carry_ref)

    m = m_ref[0]
    hs = _joint_rows(*hs_refs) if joint else hs_refs[0][...]
    r = ALPHA * hs + m[2:3] * o_ref[...].astype(f32)
    hs1 = _ln_rows(r, g_ref[...], b_ref[...])
    hs1_ref[...] = hs1
    tok = hs1 * (1.0 + m[4:5]) + m[3:4]
    t_hi = tok.astype(bf16)
    _store_tile_rows(tok_ref, _pack_halves(tok))
    t_lo = (tok - t_hi.astype(f32)).astype(bf16)
    w = rw_ref[...]
    w_hi = w.astype(bf16)
    w_lo = (w - w_hi.astype(f32)).astype(bf16)
    logits = _dot(t_hi, w_hi) + _dot(t_hi, w_lo) + _dot(t_lo, w_hi) + rb_ref[...]

    lane = lax.broadcasted_iota(jnp.int32, logits.shape, 1).astype(f32)
    left = logits
    vals, ids = [], []
    for _ in range(TOP_K):
        v = jnp.max(left, axis=1, keepdims=True)
        e = jnp.min(jnp.where(left == v, lane, float(NE)), axis=1, keepdims=True)
        vals.append(v)
        ids.append(e)
        left = jnp.where(lane == e, -jnp.inf, left)
    exps = [jnp.exp(v - vals[0]) for v in vals]
    den = exps[0]
    for x in exps[1:]:
        den = den + x
    gate_ref[...] = jnp.concatenate([x / den for x in exps], axis=1)
    eid_ref[...] = jnp.concatenate(ids, axis=1).astype(jnp.int32)

    member = (lane == ids[0]).astype(f32)
    for e in ids[1:]:
        member = member + (lane == e).astype(f32)
    rows = logits.shape[0]
    earlier = (lax.broadcasted_iota(jnp.int32, (rows, rows), 1)
               < lax.broadcasted_iota(jnp.int32, (rows, rows), 0)).astype(bf16)
    before = _dot(earlier, member.astype(bf16)) + carry_ref[...]
    rank_ref[...] = jnp.concatenate(
        [jnp.sum(jnp.where(lane == e, before, 0.0), axis=1, keepdims=True) for e in ids],
        axis=1).astype(jnp.int32)
    total = carry_ref[...] + jnp.sum(member, axis=0, keepdims=True)
    carry_ref[...] = total
    cnt_ref[...] = total


def _ln_router(hs_parts, o, mt, group, ln_g, ln_b, r_w, r_b):
    m = o.shape[0]
    joint = len(hs_parts) == 2
    row = pl.BlockSpec((ROW_TILE, D), lambda i: (i, 0))
    vec = pl.BlockSpec((1, D), lambda i: (0, 0))
    topk = pl.BlockSpec((ROW_TILE, TOP_K), lambda i: (i, 0))
    return pl.pallas_call(
        functools.partial(_ln_router_kernel, joint=joint),
        grid=(m // ROW_TILE,),
        in_specs=([_CTX_ROWS, _LATENT_ROWS] if joint else [row]) + [
                  row,
                  pl.BlockSpec((1, 6, D), lambda i: (group(i), 0, 0)),
                  vec, vec,
                  pl.BlockSpec((D, NE), lambda i: (0, 0)),
                  pl.BlockSpec((1, NE), lambda i: (0, 0))],
        out_specs=[row, pl.BlockSpec((ROW_TILE * TILE_ROWS, LANES), lambda i: (i, 0)),
                   topk, topk, topk, pl.BlockSpec((1, NE), lambda i: (0, 0))],
        out_shape=[jax.ShapeDtypeStruct((m, D), f32),
                   jax.ShapeDtypeStruct((m * TILE_ROWS, LANES), jnp.uint32),
                   jax.ShapeDtypeStruct((m, TOP_K), f32),
                   jax.ShapeDtypeStruct((m, TOP_K), jnp.int32),
                   jax.ShapeDtypeStruct((m, TOP_K), jnp.int32),
                   jax.ShapeDtypeStruct((1, NE), f32)],
        scratch_shapes=[pltpu.VMEM((1, NE), f32)],
        compiler_params=_cparams(1),
        name="ln_router",
    )(*hs_parts, o, mt, ln_g.reshape(1, D), ln_b.reshape(1, D), r_w, r_b.reshape(1, NE))


def _first_of_expert(be_ref, i):
    return jnp.logical_or(i == 0, be_ref[i] != be_ref[jnp.maximum(i - 1, 0)])


ROW_STEPS = 4


def _for_real_rows(valid, first, rows, load_weights, start_next, compute):
    q = MOE_TM // ROW_STEPS
    for step in range(1, ROW_STEPS + 1):
        covers = rows <= step * q
        if step > 1:
            covers = jnp.logical_and(covers, rows > (step - 1) * q)
        chosen = jnp.logical_and(valid, covers)

        @pl.when(jnp.logical_and(chosen, first))
        def _(n=step * q):
            load_weights()
            compute(n)
            start_next()

        @pl.when(jnp.logical_and(chosen, jnp.logical_not(first)))
        def _(n=step * q):
            compute(n)


def _moe_up_kernel(be_ref, nu_ref, nxt_ref, rows_ref, x_ref, w_hbm, bg_ref, bl_ref, o_ref, stage_ref,
                   cache_ref, sem, *, layer, tf):
    j = pl.program_id(0)
    i = pl.program_id(1)
    valid = i < nu_ref[0]

    def tile_copy(e, jj, half):
        col = pl.multiple_of(half * F + jj * tf, tf)
        return pltpu.make_async_copy(w_hbm.at[layer, e, :, pl.ds(col, tf)], stage_ref.at[half], sem.at[half])

    def start(e, jj):
        tile_copy(e, jj, 0).start()
        tile_copy(e, jj, 1).start()

    def wait():
        tile_copy(0, 0, 0).wait()
        tile_copy(0, 0, 1).wait()

    @pl.when(jnp.logical_and(j == 0, i == 0))
    def _():
        start(be_ref[0], 0)

    def load_weights():
        wait()
        cache_ref[...] = stage_ref[...].astype(bf16)

    def start_next():
        nxt = nxt_ref[i]
        more = nxt >= 0
        last_pass = j + 1 >= pl.num_programs(0)
        start(jnp.where(more, nxt, be_ref[0]),
              jnp.where(more, j, jnp.where(last_pass, 0, j + 1)))

    def compute(rows):
        x = _unpack_halves(_load_tile_rows(x_ref, rows)).astype(bf16)
        glu = jnp.minimum(_dot(x, cache_ref[0]) + bg_ref[0, 0], SWIGLU_LIMIT)
        lin = jnp.clip(_dot(x, cache_ref[1]) + bl_ref[0, 0], -SWIGLU_LIMIT, SWIGLU_LIMIT)
        o_ref[pl.ds(0, rows), :] = (glu * _sigmoid(SWIGLU_ALPHA * glu) * (lin + 1.0)).astype(o_ref.dtype)
        if rows < MOE_TM:
            o_ref[pl.ds(rows, MOE_TM - rows), :] = jnp.zeros((MOE_TM - rows, tf), o_ref.dtype)

    _for_real_rows(valid, _first_of_expert(be_ref, i), rows_ref[i], load_weights, start_next, compute)

    @pl.when(jnp.logical_not(valid))
    def _():
        o_ref[...] = jnp.zeros_like(o_ref)

    @pl.when(jnp.logical_and(j == pl.num_programs(0) - 1, i == pl.num_programs(1) - 1))
    def _():
        wait()


def _moe_down_kernel(be_ref, nu_ref, nxt_ref, rows_ref, a_ref, w_hbm, b_ref, o_ref, stage_ref, cache_ref, sem,
                     *, layer):
    i = pl.program_id(0)
    valid = i < nu_ref[0]

    def expert_copy(e):
        return pltpu.make_async_copy(w_hbm.at[layer, e], stage_ref, sem)

    @pl.when(i == 0)
    def _():
        expert_copy(be_ref[0]).start()

    def load_weights():
        expert_copy(0).wait()
        cache_ref[...] = stage_ref[...].astype(bf16)

    def start_next():
        nxt = nxt_ref[i]
        expert_copy(jnp.where(nxt >= 0, nxt, be_ref[0])).start()

    def compute(rows):
        y = _dot(a_ref[pl.ds(0, rows), :], cache_ref[...]) + b_ref[0, 0]
        _store_tile_rows(o_ref, _pack_halves(y))
        if rows < MOE_TM:
            o_ref[pl.ds(rows * TILE_ROWS, (MOE_TM - rows) * TILE_ROWS), :] = jnp.zeros(
                ((MOE_TM - rows) * TILE_ROWS, LANES), o_ref.dtype)

    _for_real_rows(valid, _first_of_expert(be_ref, i), rows_ref[i], load_weights, start_next, compute)

    @pl.when(jnp.logical_not(valid))
    def _():
        o_ref[...] = jnp.zeros_like(o_ref)

    @pl.when(i == pl.num_programs(0) - 1)
    def _():
        expert_copy(0).wait()


def _moe_experts(layer, xs, block_e, n_used, next_e, block_rows, w1, b1, w2, b2):
    ns = xs.shape[0] // TILE_ROWS
    nb = ns // MOE_TM
    tf = 1024
    lin0 = F // tf
    b1 = b1.reshape(DEPTH, NE, 1, 2 * F)
    b2 = b2.reshape(DEPTH, NE, 1, D)

    def bspec(col0):
        return pl.BlockSpec((1, 1, 1, tf), lambda j, i, be, nu, nx, br: (layer, be[i], 0, col0 + j))

    act = pl.pallas_call(
        functools.partial(_moe_up_kernel, layer=layer, tf=tf),
        grid_spec=pltpu.PrefetchScalarGridSpec(
            num_scalar_prefetch=4,
            grid=(F // tf, nb),
            in_specs=[pl.BlockSpec((MOE_TM * TILE_ROWS, LANES),
                                   lambda j, i, be, nu, nx, br: (jnp.minimum(i, nu[0] - 1), 0)),
                      pl.BlockSpec(memory_space=pl.ANY), bspec(0), bspec(lin0)],
            out_specs=pl.BlockSpec((MOE_TM, tf), lambda j, i, be, nu, nx, br: (i, j)),
            scratch_shapes=[pltpu.VMEM((2, D, tf), f32), pltpu.VMEM((2, D, tf), bf16),
                            pltpu.SemaphoreType.DMA((2,))]),
        out_shape=jax.ShapeDtypeStruct((ns, F), bf16),
        compiler_params=_cparams(2),
        name="moe_up",
    )(block_e, n_used, next_e, block_rows, xs, w1, b1, b1)

    return pl.pallas_call(
        functools.partial(_moe_down_kernel, layer=layer),
        grid_spec=pltpu.PrefetchScalarGridSpec(
            num_scalar_prefetch=4,
            grid=(nb,),
            in_specs=[pl.BlockSpec((MOE_TM, F), lambda i, be, nu, nx, br: (jnp.minimum(i, nu[0] - 1), 0)),
                      pl.BlockSpec(memory_space=pl.ANY),
                      pl.BlockSpec((1, 1, 1, D), lambda i, be, nu, nx, br: (layer, be[i], 0, 0))],
            out_specs=pl.BlockSpec((MOE_TM * TILE_ROWS, LANES), lambda i, be, nu, nx, br: (i, 0)),
            scratch_shapes=[pltpu.VMEM((F, D), f32), pltpu.VMEM((F, D), bf16), pltpu.SemaphoreType.DMA]),
        out_shape=jax.ShapeDtypeStruct((ns * TILE_ROWS, LANES), jnp.uint32),
        compiler_params=_cparams(1),
        name="moe_down",
    )(block_e, n_used, next_e, block_rows, act, w2, b2)


GATHER_ROWS = 2048


def _gather_kernel(idx_ref, src_ref, out_ref, sem):
    n = GATHER_ROWS
    t = TILE_ROWS

    def issue(r, carry):
        src_row = pl.multiple_of(idx_ref[0, 0, r] * t, t)
        dst_row = pl.multiple_of(r * t, t)
        pltpu.make_async_copy(src_ref.at[pl.ds(src_row, t)], out_ref.at[pl.ds(dst_row, t)], sem).start()
        return carry

    lax.fori_loop(0, n, issue, 0, unroll=8)
    pltpu.make_async_copy(src_ref.at[pl.ds(0, n * t)], out_ref, sem).wait()


def _gather_rows(src, idx):
    n = idx.shape[0]
    nblk = n // GATHER_ROWS
    return pl.pallas_call(
        _gather_kernel,
        grid=(nblk,),
        in_specs=[pl.BlockSpec((1, 1, GATHER_ROWS), lambda i: (i, 0, 0), memory_space=pltpu.SMEM),
                  pl.BlockSpec(memory_space=pl.ANY)],
        out_specs=pl.BlockSpec((GATHER_ROWS * TILE_ROWS, LANES), lambda i: (i, 0)),
        out_shape=jax.ShapeDtypeStruct((n * TILE_ROWS, LANES), src.dtype),
        scratch_shapes=[pltpu.SemaphoreType.DMA],
        compiler_params=pltpu.CompilerParams(dimension_semantics=("arbitrary",),
                                             vmem_limit_bytes=VMEM_LIMIT,
                                             disable_bounds_checks=True),
        name="gather_rows",
    )(idx.reshape(nblk, 1, GATHER_ROWS), src)


SCATTER_TOKENS = 512


def _scatter_kernel(dest_ref, src_ref, init_ref, out_ref, sem):
    del init_ref
    n = SCATTER_TOKENS
    t = TILE_ROWS

    def issue(a, carry):
        src_row = pl.multiple_of(lax.shift_right_logical(a, TOP_K.bit_length() - 1) * t, t)
        dst_row = pl.multiple_of(dest_ref[0, 0, a] * t, t)
        pltpu.make_async_copy(src_ref.at[pl.ds(src_row, t)], out_ref.at[pl.ds(dst_row, t)], sem).start()
        return carry

    lax.fori_loop(0, n * TOP_K, issue, 0, unroll=8)
    for _ in range(TOP_K):
        pltpu.make_async_copy(src_ref, out_ref.at[pl.ds(0, n * t)], sem).wait()


def _scatter_rows(src, dest, n_slots):
    n = src.shape[0] // TILE_ROWS
    nblk = n // SCATTER_TOKENS
    init = jnp.zeros((n_slots * TILE_ROWS, LANES), src.dtype)
    return pl.pallas_call(
        _scatter_kernel,
        grid=(nblk,),
        in_specs=[pl.BlockSpec((1, 1, SCATTER_TOKENS * TOP_K), lambda i: (i, 0, 0), memory_space=pltpu.SMEM),
                  pl.BlockSpec((SCATTER_TOKENS * TILE_ROWS, LANES), lambda i: (i, 0)),
                  pl.BlockSpec(memory_space=pl.ANY)],
        out_specs=pl.BlockSpec(memory_space=pl.ANY),
        out_shape=jax.ShapeDtypeStruct(init.shape, src.dtype),
        input_output_aliases={2: 0},
        scratch_shapes=[pltpu.SemaphoreType.DMA],
        compiler_params=pltpu.CompilerParams(dimension_semantics=("arbitrary",),
                                             vmem_limit_bytes=VMEM_LIMIT,
                                             disable_bounds_checks=True),
        name="scatter_rows",
    )(dest.reshape(nblk, 1, SCATTER_TOKENS * TOP_K), src, init)


def _slot_layout(eid, rank, counts):
    t = eid.shape[0]
    nb = -(-(t * TOP_K) // MOE_TM) + NE
    counts = counts.reshape(NE).astype(jnp.int32)
    experts = jnp.arange(NE, dtype=jnp.int32)
    pcounts = (counts + MOE_TM - 1) // MOE_TM * MOE_TM
    pends = jnp.cumsum(pcounts)
    pstarts = pends - pcounts
    dest = rank + jnp.sum(jnp.where(eid[:, :, None] == experts, pstarts, 0), axis=-1)
    n_used = pends[-1] // MOE_TM
    blk = jnp.arange(nb, dtype=jnp.int32)
    block_e = jnp.minimum(jnp.sum(pends[None, :] <= (blk * MOE_TM)[:, None], axis=1), NE - 1)
    of_block = block_e[:, None] == experts

    def per_block(table):
        return jnp.sum(jnp.where(of_block, table, 0), axis=1)

    block_rows = jnp.clip(per_block(counts) - (blk * MOE_TM - per_block(pstarts)), 0, MOE_TM)
    block_rows = jnp.where(blk < n_used, block_rows, 0)
    group_end = per_block(pends) // MOE_TM
    next_e = jnp.sum(jnp.where(group_end[:, None] == blk, block_e, 0), axis=1)
    next_e = jnp.where(group_end < n_used, next_e, -1)
    last_e = jnp.sum(jnp.where(blk == n_used - 1, block_e, 0))
    block_e = jnp.where(blk < n_used, block_e, last_e)
    i32 = jnp.int32
    return (dest.astype(i32), block_e.astype(i32), n_used.astype(i32).reshape(1), next_e.astype(i32),
            block_rows.astype(i32))


def _combine_kernel(hs_ref, y_ref, gate_ref, m_ref, g_ref, b_ref, *rest, joint):
    if joint:
        mn_ref, hs2_ref, nxt_ref, ctx_ref = rest
    else:
        (hs2_ref,) = rest
    m = m_ref[0]
    gate = gate_ref[...]
    y = gate[:, 0:1] * _unpack_halves(_load_tile_rows(y_ref.at[0], ROW_TILE))
    for k in range(1, TOP_K):
        y = y + gate[:, k:k + 1] * _unpack_halves(_load_tile_rows(y_ref.at[k], ROW_TILE))
    hs2 = _ln_rows(ALPHA * hs_ref[...] + m[5:6] * y, g_ref[...], b_ref[...])
    if not joint:
        hs2_ref[...] = hs2
        return
    is_ctx = _is_ctx_tile(pl.program_id(0))

    @pl.when(is_ctx)
    def _():
        ctx_ref[...] = hs2

    @pl.when(jnp.logical_not(is_ctx))
    def _():
        mn = mn_ref[0]
        hs2_ref[...] = hs2
        nxt_ref[...] = (hs2 * (1.0 + mn[1:2]) + mn[0:1]).astype(nxt_ref.dtype)


def _combine(hs1, yg, gate, mt, group, ln_g, ln_b, mt_next=None):
    m = hs1.shape[0]
    row = pl.BlockSpec((ROW_TILE, D), lambda i: (i, 0))
    vec = pl.BlockSpec((1, D), lambda i: (0, 0))
    mod = pl.BlockSpec((1, 6, D), lambda i: (group(i), 0, 0))
    in_specs = [row, pl.BlockSpec((TOP_K, ROW_TILE * TILE_ROWS, LANES), lambda i: (0, i, 0)),
                pl.BlockSpec((ROW_TILE, TOP_K), lambda i: (i, 0)), mod, vec, vec]
    args = [hs1, yg, gate, mt, ln_g.reshape(1, D), ln_b.reshape(1, D)]
    joint = mt_next is not None
    if joint:
        in_specs.append(mod)
        args.append(mt_next)
        out_specs = [_LATENT_ROWS, _LATENT_ROWS, _CTX_ROWS]
        out_shape = [jax.ShapeDtypeStruct((BATCH * SEQ, D), f32), jax.ShapeDtypeStruct((BATCH * SEQ, D), bf16),
                     jax.ShapeDtypeStruct((BATCH * LC, D), f32)]
    else:
        out_specs = [row]
        out_shape = [jax.ShapeDtypeStruct((m, D), f32)]
    return pl.pallas_call(
        functools.partial(_combine_kernel, joint=joint),
        grid=(m // ROW_TILE,),
        in_specs=in_specs, out_specs=out_specs, out_shape=out_shape,
        compiler_params=_cparams(1),
        name="moe_combine_ln",
    )(*args)


def _moe_layer(layer, hs_parts, o, mt, group, ln1_g, ln1_b, ln2_g, ln2_b, r_w, r_b, w1, b1, w2, b2,
               mt_next=None):
    t = o.shape[0]
    hs1, tok, gate, eid, rank, counts = _ln_router(hs_parts, o, mt, group, ln1_g[layer], ln1_b[layer],
                                                   r_w[layer], r_b[layer])
    dest, block_e, n_used, next_e, block_rows = _slot_layout(eid, rank, counts)
    xs = _scatter_rows(tok, dest.reshape(-1), block_e.shape[0] * MOE_TM)
    y = _moe_experts(layer, xs, block_e, n_used, next_e, block_rows, w1, b1, w2, b2)
    yg = _gather_rows(y, dest.T.reshape(-1)).reshape(TOP_K, t * TILE_ROWS, LANES)
    return _combine(hs1, yg, gate, mt, group, ln2_g[layer], ln2_b[layer], mt_next)


def _sgu_kernel(g_ref, v_ref, lg_ref, lb_ref, ws_ref, bs_ref, o_ref):
    v = v_ref[...].astype(f32)
    vn = _ln_rows(v, lg_ref[...], lb_ref[...]).astype(bf16)
    gw = E // SGU_GROUPS
    for g in range(SGU_GROUPS):
        mixed = _dot(ws_ref[g].astype(bf16), vn[:, g * gw:(g + 1) * gw]) + bs_ref[:, g:g + 1]
        o_ref[:, g * gw:(g + 1) * gw] = (g_ref[:, g * gw:(g + 1) * gw].astype(f32) * mixed).astype(o_ref.dtype)


def _sgu(uv, ln_g, ln_b, w_s, b_s):
    m = uv.shape[0]
    c = SGU_CHUNK
    vec = pl.BlockSpec((1, E), lambda i: (0, 0))
    return pl.pallas_call(
        _sgu_kernel,
        grid=(m // c,),
        in_specs=[pl.BlockSpec((c, E), lambda i: (i, 0)),
                  pl.BlockSpec((c, E), lambda i: (i, 1)),
                  vec, vec,
                  pl.BlockSpec((SGU_GROUPS, c, c), lambda i: (0, 0, 0)),
                  pl.BlockSpec((c, SGU_GROUPS), lambda i: (0, 0))],
        out_specs=pl.BlockSpec((c, E), lambda i: (i, 0)),
        out_shape=jax.ShapeDtypeStruct((m, E), bf16),
        compiler_params=_cparams(1),
        name="sgu",
    )(uv, uv, ln_g.reshape(1, E), ln_b.reshape(1, E), w_s, b_s.T)


def kernel(x, c, ctx, c_ctx, mod_w, mod_b, ln1_g, ln1_b, ln2_g, ln2_b, a_w_in, a_conv_w, a_conv_b, a_w_q, a_w_k, a_w_v, a_w_gate, a_b_gate, a_norm_w, a_skip, a_w_out, b_w_in, b_ln_g, b_ln_b, b_w_s, b_b_s, b_w_out, r_w, r_b, e_w1, e_b1, e_w2, e_b2):
    cvec = jnp.concatenate([c, c_ctx[None, :], jnp.zeros((8 - BATCH - 1, D), f32)], axis=0)
    mods = _mod_rows(cvec, mod_w, mod_b).reshape(DEPTH, 8, 6, D)
    rows = [r for b in range(BATCH) for r in (BATCH, b)]
    mt = [jnp.stack([mods[l, r] for r in rows], axis=0) for l in range(DEPTH)]

    hs = (ctx.reshape(BATCH * LC, D), x.reshape(BATCH * SEQ, D))
    hx = _modulate(*hs, mt[0])
    xz = _matmul(hx, a_w_in[0], tm=1024, tn=1024, name="mlstm_in_proj")
    xc, q, k, v, gates = _conv_qkv(xz, a_conv_w[0], a_conv_b[0], a_w_q[0], a_w_k[0], a_w_v[0],
                                   a_w_gate[0], a_b_gate[0])
    gates_t = gates.transpose(0, 2, 1)
    h_fwd = _mlstm(q, k, v, gates, gates_t, 0)
    pre = _mlstm(q, k, v, gates, gates_t, 1, (h_fwd, xc, xz, a_norm_w[0], a_skip[0]))
    o = _matmul(pre.reshape(BATCH * U, E), a_w_out[0], tm=1024, tn=512, name="mlstm_out_proj")
    hs, hx, _ = _moe_layer(0, hs, o, mt[0], _group_joint, ln1_g, ln1_b, ln2_g, ln2_b,
                           r_w, r_b, e_w1, e_b1, e_w2, e_b2, mt_next=mt[1])

    uv = _matmul(hx, b_w_in[0], tm=1024, tn=1024, act="gelu", name="sgu_in_proj")
    gated = _sgu(uv, b_ln_g[0], b_ln_b[0], b_w_s[0], b_b_s[0])
    o = _matmul(gated, b_w_out[0], tm=1024, tn=512, name="sgu_out_proj")
    (out,) = _moe_layer(1, (hs,), o, mt[1], _group_latent, ln1_g, ln1_b, ln2_g, ln2_b,
                        r_w, r_b, e_w1, e_b1, e_w2, e_b2)
    return out.reshape(BATCH, SEQ, D)
```

```python
import functools

import jax
import jax.numpy as jnp
from jax import lax
from jax.experimental import pallas as pl
from jax.experimental.pallas import tpu as pltpu

D = 2048
BATCH = 4
SEQ = 2048
DEPTH = 2
GRID_W = 64
LC = 256
U = LC + SEQ
E = 2 * D
NH = 4
DH = E // NH
QKV_BLOCK = 4
SGU_GROUPS = 8
SGU_CHUNK = 128
NE = 32
TOP_K = 4
F = D
SWIGLU_LIMIT = 7.0
SWIGLU_ALPHA = 1.702
ALPHA = (2 * DEPTH) ** 0.25
LN_EPS = 1e-5

ROW_TILE = 256
MLSTM_CHUNK = 256
MOE_TM = 1024
BD_TILE = 256
VMEM_LIMIT = 56 * 1024 * 1024

f32 = jnp.float32
bf16 = jnp.bfloat16


def _cparams(n_axes):
    return pltpu.CompilerParams(dimension_semantics=("arbitrary",) * n_axes,
                                vmem_limit_bytes=VMEM_LIMIT)


def _dot(a, b):
    return jnp.dot(a, b, preferred_element_type=f32)


def _sigmoid(x):
    return 1.0 / (1.0 + jnp.exp(-x))


def _pack_halves(x):
    w = x.shape[1] // 2
    bits = lax.bitcast_convert_type(x.astype(bf16).astype(f32), jnp.uint32)
    return (bits[:, w:] & jnp.uint32(0xFFFF0000)) | (bits[:, :w] >> 16)


def _unpack_halves(p):
    lo = lax.bitcast_convert_type(p << 16, f32)
    hi = lax.bitcast_convert_type(p & jnp.uint32(0xFFFF0000), f32)
    return jnp.concatenate([lo, hi], axis=1)


PACKED = D // 2
LANES = 128
TILE_ROWS = PACKED // LANES


def _store_tile_rows(ref, p):
    r = p.shape[0]
    for s in range(TILE_ROWS):
        ref[pl.ds(s, r, stride=TILE_ROWS), :] = p[:, s * LANES:(s + 1) * LANES]


def _load_tile_rows(ref, r):
    return jnp.concatenate([ref[pl.ds(s, r, stride=TILE_ROWS), :] for s in range(TILE_ROWS)], axis=1)


def _mod_kernel(c_ref, w_ref, b_ref, o_ref):
    c = c_ref[...]
    a = (c * _sigmoid(c)).astype(bf16)
    o_ref[0] = _dot(a, w_ref[0].astype(bf16)) + b_ref[0]


def _mod_rows(cvec, mod_w, mod_b):
    tn = 1024
    return pl.pallas_call(
        _mod_kernel,
        grid=(DEPTH, 6 * D // tn),
        in_specs=[pl.BlockSpec((8, D), lambda l, j: (0, 0)),
                  pl.BlockSpec((1, D, tn), lambda l, j: (l, 0, j)),
                  pl.BlockSpec((1, 1, tn), lambda l, j: (l, 0, j))],
        out_specs=pl.BlockSpec((1, 8, tn), lambda l, j: (l, 0, j)),
        out_shape=jax.ShapeDtypeStruct((DEPTH, 8, 6 * D), f32),
        compiler_params=_cparams(2),
        name="adaln_rows",
    )(cvec, mod_w, mod_b.reshape(DEPTH, 1, 6 * D))


def _group_joint(i):
    tiles = U // ROW_TILE
    return 2 * (i // tiles) + jnp.minimum(i % tiles, 1)


def _group_latent(i):
    return 2 * (i // (SEQ // ROW_TILE)) + 1


JOINT_TILES = U // ROW_TILE


def _is_ctx_tile(i):
    return i % JOINT_TILES == 0


def _ctx_tile(i):
    return i // JOINT_TILES


def _latent_tile(i):
    return (i // JOINT_TILES) * (SEQ // ROW_TILE) + jnp.maximum(i % JOINT_TILES - 1, 0)


_CTX_ROWS = pl.BlockSpec((ROW_TILE, D), lambda i: (_ctx_tile(i), 0))
_LATENT_ROWS = pl.BlockSpec((ROW_TILE, D), lambda i: (_latent_tile(i), 0))


def _joint_rows(ctx_ref, lat_ref):
    is_ctx = _is_ctx_tile(pl.program_id(0))
    return jnp.where(is_ctx, ctx_ref[...], lat_ref[...])


def _modulate_kernel(ctx_ref, lat_ref, m_ref, o_ref):
    m = m_ref[0]
    o_ref[...] = (_joint_rows(ctx_ref, lat_ref) * (1.0 + m[1:2]) + m[0:1]).astype(o_ref.dtype)


def _modulate(ctx2d, x2d, mt):
    m = ctx2d.shape[0] + x2d.shape[0]
    return pl.pallas_call(
        _modulate_kernel,
        grid=(m // ROW_TILE,),
        in_specs=[_CTX_ROWS, _LATENT_ROWS,
                  pl.BlockSpec((1, 6, D), lambda i: (_group_joint(i), 0, 0))],
        out_specs=pl.BlockSpec((ROW_TILE, D), lambda i: (i, 0)),
        out_shape=jax.ShapeDtypeStruct((m, D), bf16),
        compiler_params=_cparams(1),
        name="modulate",
    )(ctx2d, x2d, mt)


def _erf(x):
    return lax.erf(x)


def _mm_kernel(x_ref, w_ref, o_ref, wb_ref, *, act):
    @pl.when(pl.program_id(1) == 0)
    def _():
        wb_ref[...] = w_ref[...].astype(bf16)

    acc = _dot(x_ref[...], wb_ref[...])
    if act == "gelu":
        acc = 0.5 * acc * (1.0 + _erf(acc * (2.0 ** -0.5)))
    o_ref[...] = acc.astype(o_ref.dtype)


def _matmul(x, w, *, tm, tn, act=None, name):
    m, k = x.shape
    n = w.shape[1]
    return pl.pallas_call(
        functools.partial(_mm_kernel, act=act),
        grid=(n // tn, m // tm),
        in_specs=[pl.BlockSpec((tm, k), lambda j, i: (i, 0)),
                  pl.BlockSpec((k, tn), lambda j, i: (0, j))],
        out_specs=pl.BlockSpec((tm, tn), lambda j, i: (i, j)),
        out_shape=jax.ShapeDtypeStruct((m, n), bf16),
        scratch_shapes=[pltpu.VMEM((k, tn), bf16)],
        compiler_params=_cparams(2),
        name=name,
    )(x, w)


def _conv_qkv_kernel(xm_ref, cw_ref, cb_ref, wq_ref, wk_ref, wv_ref, wg_ref, bg_ref,
                     xc_ref, q_ref, k_ref, v_ref, g_ref):
    ct = pl.program_id(1)
    a = xm_ref[0].astype(f32)
    c = a.shape[1]
    r = lax.broadcasted_iota(jnp.int32, (U, c), 0)
    latent = r >= LC
    p = r - LC
    col = jnp.where(latent, p & (GRID_W - 1), r)
    last = jnp.where(latent, GRID_W - 1, LC - 1)
    a_l = jnp.where(col > 0, pltpu.roll(a, 1, 0), 0.0)
    a_r = jnp.where(col < last, pltpu.roll(a, U - 1, 0), 0.0)
    w = cw_ref[...]
    rows = [w[3 * i:3 * i + 1] * a_l + w[3 * i + 1:3 * i + 2] * a + w[3 * i + 2:3 * i + 3] * a_r
            for i in range(3)]
    up = jnp.where(p >= GRID_W, pltpu.roll(rows[0], GRID_W, 0), 0.0)
    down = jnp.where(latent & (p < SEQ - GRID_W), pltpu.roll(rows[2], U - GRID_W, 0), 0.0)
    pre = rows[1] + up + down + cb_ref[...]
    xc = (pre * _sigmoid(pre)).astype(bf16)
    xc_ref[0] = xc
    xm = xm_ref[0]
    q = _dot(xc, _diag_tile(wq_ref)).astype(bf16)
    k = _dot(xc, _diag_tile(wk_ref)).astype(bf16)
    v = _dot(xm, _diag_tile(wv_ref)).astype(bf16)
    q_ref[0] = q
    k_ref[0] = k
    v_ref[0] = v
    g = (_dot(q, wg_ref[0].astype(bf16)) + _dot(k, wg_ref[1].astype(bf16))
         + _dot(v, wg_ref[2].astype(bf16)))

    @pl.when(ct == 0)
    def _():
        g_ref[0] = g + bg_ref[...]

    @pl.when(ct > 0)
    def _():
        g_ref[0] += g


def _diag_tile(ref):
    r = lax.broadcasted_iota(jnp.int32, ref.shape, 0) // QKV_BLOCK
    c = lax.broadcasted_iota(jnp.int32, ref.shape, 1) // QKV_BLOCK
    return jnp.where(r == c, ref[...], 0.0).astype(bf16)


def _block_rows_tiled(w):
    return jnp.tile(w.reshape(E, QKV_BLOCK), (1, BD_TILE // QKV_BLOCK))


def _conv_qkv(xz, conv_w, conv_b, w_q, w_k, w_v, w_gate, b_gate):
    c = BD_TILE
    n_gate = 4 * NH
    act = jax.ShapeDtypeStruct((BATCH, U, E), bf16)
    tile = pl.BlockSpec((1, U, c), lambda b, t: (b, 0, t))
    bd = pl.BlockSpec((c, c), lambda b, t: (t, 0))
    return pl.pallas_call(
        _conv_qkv_kernel,
        grid=(BATCH, E // c),
        in_specs=[tile,
                  pl.BlockSpec((9, c), lambda b, t: (0, t)),
                  pl.BlockSpec((1, c), lambda b, t: (0, t)),
                  bd, bd, bd,
                  pl.BlockSpec((3, c, n_gate), lambda b, t: (0, t, 0)),
                  pl.BlockSpec((1, n_gate), lambda b, t: (0, 0))],
        out_specs=[tile, tile, tile, tile,
                   pl.BlockSpec((1, U, n_gate), lambda b, t: (b, 0, 0))],
        out_shape=[act, act, act, act, jax.ShapeDtypeStruct((BATCH, U, n_gate), f32)],
        compiler_params=_cparams(2),
        name="conv_qkv_gates",
    )(xz.reshape(BATCH, U, 2 * E), conv_w.reshape(9, E), conv_b.reshape(1, E),
      _block_rows_tiled(w_q), _block_rows_tiled(w_k), _block_rows_tiled(w_v),
      w_gate.reshape(3, E, n_gate), b_gate.reshape(1, n_gate))


def _log_sigmoid(x):
    return jnp.minimum(x, 0.0) - jnp.log1p(jnp.exp(-jnp.abs(x)))


def _mlstm_kernel(*refs, direction, finish):
    if finish:
        (q_ref, k_ref, v_ref, gc_ref, gr_ref, hf_ref, xc_ref, z_ref, nw_ref, sk_ref,
         o_ref, ct_ref, ctb_ref, n_ref, m_ref) = refs
    else:
        q_ref, k_ref, v_ref, gc_ref, gr_ref, o_ref, ct_ref, ctb_ref, n_ref, m_ref = refs
    L = MLSTM_CHUNK
    h = pl.program_id(1)
    step = pl.program_id(2)

    @pl.when(step == 0)
    def _():
        ct_ref[...] = jnp.zeros_like(ct_ref)
        ctb_ref[...] = jnp.zeros_like(ctb_ref)
        n_ref[...] = jnp.zeros_like(n_ref)
        m_ref[...] = jnp.full_like(m_ref, -jnp.inf)

    i_idx = 2 * direction * NH + h
    f_idx = (2 * direction + 1) * NH + h
    gc = gc_ref[0]
    gr = gr_ref[0]
    lane = lax.broadcasted_iota(jnp.int32, gc.shape, 1)
    sub = lax.broadcasted_iota(jnp.int32, gr.shape, 0)
    ig_col = jnp.sum(jnp.where(lane == i_idx, gc, 0.0), axis=1, keepdims=True)
    f_col = jnp.sum(jnp.where(lane == f_idx, gc, 0.0), axis=1, keepdims=True)
    ig_row = jnp.sum(jnp.where(sub == i_idx, gr, 0.0), axis=0, keepdims=True)
    f_row = jnp.sum(jnp.where(sub == f_idx, gr, 0.0), axis=0, keepdims=True)
    lf_col = _log_sigmoid(f_col)
    lf_row = _log_sigmoid(f_row)

    t_i = lax.broadcasted_iota(jnp.int32, (L, L), 0)
    s_i = lax.broadcasted_iota(jnp.int32, (L, L), 1)
    seen = (s_i <= t_i) if direction == 0 else (s_i >= t_i)
    b_col = jnp.sum(jnp.where(seen, lf_row, 0.0), axis=1, keepdims=True)
    seen_t = (t_i <= s_i) if direction == 0 else (t_i >= s_i)
    b_row = jnp.sum(jnp.where(seen_t, lf_col, 0.0), axis=0, keepdims=True)
    b_end = jnp.sum(lf_row, axis=1, keepdims=True)

    m_prev = m_ref[:, 0:1]
    log_d = jnp.where(seen, b_col - b_row + ig_row, -jnp.inf)
    g_col = b_col + m_prev
    m_t = jnp.maximum(g_col, jnp.max(log_d, axis=1, keepdims=True))
    dw = jnp.exp(log_d - m_t)
    inter = jnp.exp(g_col - m_t)

    q = q_ref[0]
    k = k_ref[0] * (DH ** -0.5)
    v = v_ref[0]
    s = lax.dot_general(q, k, (((1,), (1,)), ((), ())), preferred_element_type=f32) * dw
    num = _dot(s.astype(bf16), v) + inter * _dot(q, ctb_ref[...])
    qn = jnp.sum(q.astype(f32) * n_ref[...], axis=1, keepdims=True)
    den = jnp.sum(s, axis=1, keepdims=True) + inter * qn
    hout = num / jnp.maximum(jnp.abs(den), jnp.exp(-m_t))

    w_end = b_end - b_col + ig_col
    m_new = jnp.maximum(b_end + m_prev, jnp.max(w_end, axis=0, keepdims=True))
    decay = jnp.exp(b_end + m_prev - m_new)
    kw = k.astype(f32) * jnp.exp(w_end - m_new)
    upd = lax.dot_general(kw.astype(bf16), v, (((0,), (0,)), ((), ())), preferred_element_type=f32)
    c_new = decay * ct_ref[...] + upd
    ct_ref[...] = c_new
    ctb_ref[...] = c_new.astype(bf16)
    n_ref[...] = decay * n_ref[...] + jnp.sum(kw, axis=0, keepdims=True)
    m_ref[...] = jnp.broadcast_to(m_new, m_ref.shape)

    if not finish:
        o_ref[0] = hout.astype(o_ref.dtype)
    else:
        hs = hout + hf_ref[0].astype(f32)
        mu = jnp.mean(hs, axis=1, keepdims=True)
        xc_ = hs - mu
        var = jnp.mean(xc_ * xc_, axis=1, keepdims=True)
        hn = xc_ * lax.rsqrt(var + LN_EPS)
        z = z_ref[0].astype(f32)
        o_ref[0] = ((hn * nw_ref[...] + sk_ref[...] * xc_ref[0].astype(f32))
                    * (z * _sigmoid(z))).astype(o_ref.dtype)


def _mlstm(q, k, v, gates, gates_t, direction, finish_args=None):
    L = MLSTM_CHUNK
    n_chunks = U // L

    if direction == 0:
        def chunk(s):
            return s
    else:
        def chunk(s):
            return jnp.where(s == 0, 0, n_chunks - s)

    tile = pl.BlockSpec((1, L, DH), lambda b, h, s: (b, chunk(s), h))
    in_specs = [tile, tile, tile,
                pl.BlockSpec((1, L, 4 * NH), lambda b, h, s: (b, chunk(s), 0)),
                pl.BlockSpec((1, 4 * NH, L), lambda b, h, s: (b, 0, chunk(s)))]
    args = [q, k, v, gates, gates_t]
    finish = finish_args is not None
    if finish:
        h_fwd, xc, xz, norm_w, skip = finish_args
        vec = pl.BlockSpec((1, DH), lambda b, h, s: (0, h))
        in_specs += [tile, tile,
                     pl.BlockSpec((1, L, DH), lambda b, h, s: (b, chunk(s), NH + h)),
                     vec, vec]
        args += [h_fwd, xc, xz.reshape(BATCH, U, 2 * E), norm_w.reshape(1, E), skip.reshape(1, E)]
    return pl.pallas_call(
        functools.partial(_mlstm_kernel, direction=direction, finish=finish),
        grid=(BATCH, NH, n_chunks),
        in_specs=in_specs,
        out_specs=tile,
        out_shape=jax.ShapeDtypeStruct((BATCH, U, E), bf16),
        scratch_shapes=[pltpu.VMEM((DH, DH), f32), pltpu.VMEM((DH, DH), bf16),
                        pltpu.VMEM((1, DH), f32), pltpu.VMEM((1, 128), f32)],
        compiler_params=_cparams(3),
        name="mlstm_bwd_finish" if finish else "mlstm_fwd",
    )(*args)


def _ln_rows(r, g, b):
    mu = jnp.mean(r, axis=1, keepdims=True)
    rc = r - mu
    var = jnp.mean(rc * rc, axis=1, keepdims=True)
    return rc * lax.rsqrt(var + LN_EPS) * g + b


def _ln_router_kernel(*refs, joint):
    hs_refs, refs = refs[:1 + joint], refs[1 + joint:]
    (o_ref, m_ref, g_ref, b_ref, rw_ref, rb_ref,
     hs1_ref, tok_ref, gate_ref, eid_ref, rank_ref, cnt_ref, carry_ref) = refs

    @pl.when(pl.program_id(0) == 0)
    def _():
        carry_ref[...] = jnp.zeros_like(carry_ref)

    m = m_ref[0]
    hs = _joint_rows(*hs_refs) if joint else hs_refs[0][...]
    r = ALPHA * hs + m[2:3] * o_ref[...].astype(f32)
    hs1 = _ln_rows(r, g_ref[...], b_ref[...])
    hs1_ref[...] = hs1
    tok = hs1 * (1.0 + m[4:5]) + m[3:4]
    t_hi = tok.astype(bf16)
    _store_tile_rows(tok_ref, _pack_halves(tok))
    t_lo = (tok - t_hi.astype(f32)).astype(bf16)
    w = rw_ref[...]
    w_hi = w.astype(bf16)
    w_lo = (w - w_hi.astype(f32)).astype(bf16)
    logits = _dot(t_hi, w_hi) + _dot(t_hi, w_lo) + _dot(t_lo, w_hi) + rb_ref[...]

    lane = lax.broadcasted_iota(jnp.int32, logits.shape, 1).astype(f32)
    left = logits
    vals, ids = [], []
    for _ in range(TOP_K):
        v = jnp.max(left, axis=1, keepdims=True)
        e = jnp.min(jnp.where(left == v, lane, float(NE)), axis=1, keepdims=True)
        vals.append(v)
        ids.append(e)
        left = jnp.where(lane == e, -jnp.inf, left)
    exps = [jnp.exp(v - vals[0]) for v in vals]
    den = exps[0]
    for x in exps[1:]:
        den = den + x
    gate_ref[...] = jnp.concatenate([x / den for x in exps], axis=1)
    eid_ref[...] = jnp.concatenate(ids, axis=1).astype(jnp.int32)

    member = (lane == ids[0]).astype(f32)
    for e in ids[1:]:
        member = member + (lane == e).astype(f32)
    rows = logits.shape[0]
    earlier = (lax.broadcasted_iota(jnp.int32, (rows, rows), 1)
               < lax.broadcasted_iota(jnp.int32, (rows, rows), 0)).astype(bf16)
    before = _dot(earlier, member.astype(bf16)) + carry_ref[...]
    rank_ref[...] = jnp.concatenate(
        [jnp.sum(jnp.where(lane == e, before, 0.0), axis=1, keepdims=True) for e in ids],
        axis=1).astype(jnp.int32)
    total = carry_ref[...] + jnp.sum(member, axis=0, keepdims=True)
    carry_ref[...] = total
    cnt_ref[...] = total


def _ln_router(hs_parts, o, mt, group, ln_g, ln_b, r_w, r_b):
    m = o.shape[0]
    joint = len(hs_parts) == 2
    row = pl.BlockSpec((ROW_TILE, D), lambda i: (i, 0))
    vec = pl.BlockSpec((1, D), lambda i: (0, 0))
    topk = pl.BlockSpec((ROW_TILE, TOP_K), lambda i: (i, 0))
    return pl.pallas_call(
        functools.partial(_ln_router_kernel, joint=joint),
        grid=(m // ROW_TILE,),
        in_specs=([_CTX_ROWS, _LATENT_ROWS] if joint else [row]) + [
                  row,
                  pl.BlockSpec((1, 6, D), lambda i: (group(i), 0, 0)),
                  vec, vec,
                  pl.BlockSpec((D, NE), lambda i: (0, 0)),
                  pl.BlockSpec((1, NE), lambda i: (0, 0))],
        out_specs=[row, pl.BlockSpec((ROW_TILE * TILE_ROWS, LANES), lambda i: (i, 0)),
                   topk, topk, topk, pl.BlockSpec((1, NE), lambda i: (0, 0))],
        out_shape=[jax.ShapeDtypeStruct((m, D), f32),
                   jax.ShapeDtypeStruct((m * TILE_ROWS, LANES), jnp.uint32),
                   jax.ShapeDtypeStruct((m, TOP_K), f32),
                   jax.ShapeDtypeStruct((m, TOP_K), jnp.int32),
                   jax.ShapeDtypeStruct((m, TOP_K), jnp.int32),
                   jax.ShapeDtypeStruct((1, NE), f32)],
        scratch_shapes=[pltpu.VMEM((1, NE), f32)],
        compiler_params=_cparams(1),
        name="ln_router",
    )(*hs_parts, o, mt, ln_g.reshape(1, D), ln_b.reshape(1, D), r_w, r_b.reshape(1, NE))


def _first_of_expert(be_ref, i):
    return jnp.logical_or(i == 0, be_ref[i] != be_ref[jnp.maximum(i - 1, 0)])


ROW_STEPS = 8


def _for_real_rows(valid, rows, compute):
    q = MOE_TM // ROW_STEPS
    for step in range(1, ROW_STEPS + 1):
        covers = rows <= step * q
        if step > 1:
            covers = jnp.logical_and(covers, rows > (step - 1) * q)
        pl.when(jnp.logical_and(valid, covers))(functools.partial(compute, step * q))


def _moe_up_kernel(be_ref, nu_ref, nxt_ref, rows_ref, x_ref, w_hbm, bg_ref, bl_ref, o_ref, stage_ref,
                   cache_ref, sem, *, layer, tf):
    j = pl.program_id(0)
    i = pl.program_id(1)
    valid = i < nu_ref[0]

    def tile_copy(e, jj, half):
        col = pl.multiple_of(half * F + jj * tf, tf)
        return pltpu.make_async_copy(w_hbm.at[layer, e, :, pl.ds(col, tf)], stage_ref.at[half], sem.at[half])

    def start(e, jj):
        tile_copy(e, jj, 0).start()
        tile_copy(e, jj, 1).start()

    @pl.when(jnp.logical_and(j == 0, i == 0))
    def _():
        start(be_ref[0], 0)

    @pl.when(jnp.logical_and(valid, _first_of_expert(be_ref, i)))
    def _():
        tile_copy(0, 0, 0).wait()
        tile_copy(0, 0, 1).wait()
        cache_ref[...] = stage_ref[...].astype(bf16)
        nxt = nxt_ref[i]

        @pl.when(nxt >= 0)
        def _():
            start(nxt, j)

        @pl.when(jnp.logical_and(nxt < 0, j + 1 < pl.num_programs(0)))
        def _():
            start(be_ref[0], j + 1)

    def compute(rows):
        x = _unpack_halves(_load_tile_rows(x_ref, rows)).astype(bf16)
        glu = jnp.minimum(_dot(x, cache_ref[0]) + bg_ref[0, 0], SWIGLU_LIMIT)
        lin = jnp.clip(_dot(x, cache_ref[1]) + bl_ref[0, 0], -SWIGLU_LIMIT, SWIGLU_LIMIT)
        o_ref[pl.ds(0, rows), :] = (glu * _sigmoid(SWIGLU_ALPHA * glu) * (lin + 1.0)).astype(o_ref.dtype)
        if rows < MOE_TM:
            o_ref[pl.ds(rows, MOE_TM - rows), :] = jnp.zeros((MOE_TM - rows, tf), o_ref.dtype)

    _for_real_rows(valid, rows_ref[i], compute)

    @pl.when(jnp.logical_not(valid))
    def _():
        o_ref[...] = jnp.zeros_like(o_ref)


def _moe_down_kernel(be_ref, nu_ref, nxt_ref, rows_ref, a_ref, w_hbm, b_ref, o_ref, stage_ref, cache_ref, sem,
                     *, layer):
    i = pl.program_id(0)
    valid = i < nu_ref[0]

    def expert_copy(e):
        return pltpu.make_async_copy(w_hbm.at[layer, e], stage_ref, sem)

    @pl.when(i == 0)
    def _():
        expert_copy(be_ref[0]).start()

    @pl.when(jnp.logical_and(valid, _first_of_expert(be_ref, i)))
    def _():
        expert_copy(0).wait()
        cache_ref[...] = stage_ref[...].astype(bf16)
        nxt = nxt_ref[i]

        @pl.when(nxt >= 0)
        def _():
            expert_copy(nxt).start()

    def compute(rows):
        y = _dot(a_ref[pl.ds(0, rows), :], cache_ref[...]) + b_ref[0, 0]
        _store_tile_rows(o_ref, _pack_halves(y))
        if rows < MOE_TM:
            o_ref[pl.ds(rows * TILE_ROWS, (MOE_TM - rows) * TILE_ROWS), :] = jnp.zeros(
                ((MOE_TM - rows) * TILE_ROWS, LANES), o_ref.dtype)

    _for_real_rows(valid, rows_ref[i], compute)

    @pl.when(jnp.logical_not(valid))
    def _():
        o_ref[...] = jnp.zeros_like(o_ref)


def _moe_experts(layer, xs, block_e, n_used, next_e, block_rows, w1, b1, w2, b2):
    ns = xs.shape[0] // TILE_ROWS
    nb = ns // MOE_TM
    tf = 1024
    lin0 = F // tf
    b1 = b1.reshape(DEPTH, NE, 1, 2 * F)
    b2 = b2.reshape(DEPTH, NE, 1, D)

    def bspec(col0):
        return pl.BlockSpec((1, 1, 1, tf), lambda j, i, be, nu, nx, br: (layer, be[i], 0, col0 + j))

    act = pl.pallas_call(
        functools.partial(_moe_up_kernel, layer=layer, tf=tf),
        grid_spec=pltpu.PrefetchScalarGridSpec(
            num_scalar_prefetch=4,
            grid=(F // tf, nb),
            in_specs=[pl.BlockSpec((MOE_TM * TILE_ROWS, LANES),
                                   lambda j, i, be, nu, nx, br: (jnp.minimum(i, nu[0] - 1), 0)),
                      pl.BlockSpec(memory_space=pl.ANY), bspec(0), bspec(lin0)],
            out_specs=pl.BlockSpec((MOE_TM, tf), lambda j, i, be, nu, nx, br: (i, j)),
            scratch_shapes=[pltpu.VMEM((2, D, tf), f32), pltpu.VMEM((2, D, tf), bf16),
                            pltpu.SemaphoreType.DMA((2,))]),
        out_shape=jax.ShapeDtypeStruct((ns, F), bf16),
        compiler_params=_cparams(2),
        name="moe_up",
    )(block_e, n_used, next_e, block_rows, xs, w1, b1, b1)

    return pl.pallas_call(
        functools.partial(_moe_down_kernel, layer=layer),
        grid_spec=pltpu.PrefetchScalarGridSpec(
            num_scalar_prefetch=4,
            grid=(nb,),
            in_specs=[pl.BlockSpec((MOE_TM, F), lambda i, be, nu, nx, br: (jnp.minimum(i, nu[0] - 1), 0)),
                      pl.BlockSpec(memory_space=pl.ANY),
                      pl.BlockSpec((1, 1, 1, D), lambda i, be, nu, nx, br: (layer, be[i], 0, 0))],
            out_specs=pl.BlockSpec((MOE_TM * TILE_ROWS, LANES), lambda i, be, nu, nx, br: (i, 0)),
            scratch_shapes=[pltpu.VMEM((F, D), f32), pltpu.VMEM((F, D), bf16), pltpu.SemaphoreType.DMA]),
        out_shape=jax.ShapeDtypeStruct((ns * TILE_ROWS, LANES), jnp.uint32),
        compiler_params=_cparams(1),
        name="moe_down",
    )(block_e, n_used, next_e, block_rows, act, w2, b2)


GATHER_ROWS = 2048


def _gather_kernel(idx_ref, src_ref, out_ref, sem):
    n = GATHER_ROWS
    t = TILE_ROWS

    def issue(r, carry):
        src_row = pl.multiple_of(idx_ref[0, 0, r] * t, t)
        dst_row = pl.multiple_of(r * t, t)
        pltpu.make_async_copy(src_ref.at[pl.ds(src_row, t)], out_ref.at[pl.ds(dst_row, t)], sem).start()
        return carry

    lax.fori_loop(0, n, issue, 0, unroll=8)
    pltpu.make_async_copy(src_ref.at[pl.ds(0, n * t)], out_ref, sem).wait()


def _gather_rows(src, idx):
    n = idx.shape[0]
    nblk = n // GATHER_ROWS
    return pl.pallas_call(
        _gather_kernel,
        grid=(nblk,),
        in_specs=[pl.BlockSpec((1, 1, GATHER_ROWS), lambda i: (i, 0, 0), memory_space=pltpu.SMEM),
                  pl.BlockSpec(memory_space=pl.ANY)],
        out_specs=pl.BlockSpec((GATHER_ROWS * TILE_ROWS, LANES), lambda i: (i, 0)),
        out_shape=jax.ShapeDtypeStruct((n * TILE_ROWS, LANES), src.dtype),
        scratch_shapes=[pltpu.SemaphoreType.DMA],
        compiler_params=pltpu.CompilerParams(dimension_semantics=("arbitrary",),
                                             vmem_limit_bytes=VMEM_LIMIT,
                                             disable_bounds_checks=True),
        name="gather_rows",
    )(idx.reshape(nblk, 1, GATHER_ROWS), src)


SCATTER_TOKENS = 512


def _scatter_kernel(dest_ref, src_ref, init_ref, out_ref, sem):
    del init_ref
    n = SCATTER_TOKENS
    t = TILE_ROWS

    def issue(a, carry):
        src_row = pl.multiple_of(lax.shift_right_logical(a, TOP_K.bit_length() - 1) * t, t)
        dst_row = pl.multiple_of(dest_ref[0, 0, a] * t, t)
        pltpu.make_async_copy(src_ref.at[pl.ds(src_row, t)], out_ref.at[pl.ds(dst_row, t)], sem).start()
        return carry

    lax.fori_loop(0, n * TOP_K, issue, 0, unroll=8)
    for _ in range(TOP_K):
        pltpu.make_async_copy(src_ref, out_ref.at[pl.ds(0, n * t)], sem).wait()


def _scatter_rows(src, dest, n_slots):
    n = src.shape[0] // TILE_ROWS
    nblk = n // SCATTER_TOKENS
    init = jnp.zeros((n_slots * TILE_ROWS, LANES), src.dtype)
    return pl.pallas_call(
        _scatter_kernel,
        grid=(nblk,),
        in_specs=[pl.BlockSpec((1, 1, SCATTER_TOKENS * TOP_K), lambda i: (i, 0, 0), memory_space=pltpu.SMEM),
                  pl.BlockSpec((SCATTER_TOKENS * TILE_ROWS, LANES), lambda i: (i, 0)),
                  pl.BlockSpec(memory_space=pl.ANY)],
        out_specs=pl.BlockSpec(memory_space=pl.ANY),
        out_shape=jax.ShapeDtypeStruct(init.shape, src.dtype),
        input_output_aliases={2: 0},
        scratch_shapes=[pltpu.SemaphoreType.DMA],
        compiler_params=pltpu.CompilerParams(dimension_semantics=("arbitrary",),
                                             vmem_limit_bytes=VMEM_LIMIT,
                                             disable_bounds_checks=True),
        name="scatter_rows",
    )(dest.reshape(nblk, 1, SCATTER_TOKENS * TOP_K), src, init)


def _slot_layout(eid, rank, counts):
    t = eid.shape[0]
    nb = -(-(t * TOP_K) // MOE_TM) + NE
    counts = counts.reshape(NE).astype(jnp.int32)
    experts = jnp.arange(NE, dtype=jnp.int32)
    pcounts = (counts + MOE_TM - 1) // MOE_TM * MOE_TM
    pends = jnp.cumsum(pcounts)
    pstarts = pends - pcounts
    dest = rank + jnp.sum(jnp.where(eid[:, :, None] == experts, pstarts, 0), axis=-1)
    n_used = pends[-1] // MOE_TM
    blk = jnp.arange(nb, dtype=jnp.int32)
    block_e = jnp.minimum(jnp.sum(pends[None, :] <= (blk * MOE_TM)[:, None], axis=1), NE - 1)
    of_block = block_e[:, None] == experts

    def per_block(table):
        return jnp.sum(jnp.where(of_block, table, 0), axis=1)

    block_rows = jnp.clip(per_block(counts) - (blk * MOE_TM - per_block(pstarts)), 0, MOE_TM)
    block_rows = jnp.where(blk < n_used, block_rows, 0)
    group_end = per_block(pends) // MOE_TM
    next_e = jnp.sum(jnp.where(group_end[:, None] == blk, block_e, 0), axis=1)
    next_e = jnp.where(group_end < n_used, next_e, -1)
    last_e = jnp.sum(jnp.where(blk == n_used - 1, block_e, 0))
    block_e = jnp.where(blk < n_used, block_e, last_e)
    i32 = jnp.int32
    return (dest.astype(i32), block_e.astype(i32), n_used.astype(i32).reshape(1), next_e.astype(i32),
            block_rows.astype(i32))


def _combine_kernel(hs_ref, y_ref, gate_ref, m_ref, g_ref, b_ref, *rest, joint):
    if joint:
        mn_ref, hs2_ref, nxt_ref, ctx_ref = rest
    else:
        (hs2_ref,) = rest
    m = m_ref[0]
    gate = gate_ref[...]
    y = gate[:, 0:1] * _unpack_halves(_load_tile_rows(y_ref.at[0], ROW_TILE))
    for k in range(1, TOP_K):
        y = y + gate[:, k:k + 1] * _unpack_halves(_load_tile_rows(y_ref.at[k], ROW_TILE))
    hs2 = _ln_rows(ALPHA * hs_ref[...] + m[5:6] * y, g_ref[...], b_ref[...])
    if not joint:
        hs2_ref[...] = hs2
        return
    is_ctx = _is_ctx_tile(pl.program_id(0))

    @pl.when(is_ctx)
    def _():
        ctx_ref[...] = hs2

    @pl.when(jnp.logical_not(is_ctx))
    def _():
        mn = mn_ref[0]
        hs2_ref[...] = hs2
        nxt_ref[...] = (hs2 * (1.0 + mn[1:2]) + mn[0:1]).astype(nxt_ref.dtype)


def _combine(hs1, yg, gate, mt, group, ln_g, ln_b, mt_next=None):
    m = hs1.shape[0]
    row = pl.BlockSpec((ROW_TILE, D), lambda i: (i, 0))
    vec = pl.BlockSpec((1, D), lambda i: (0, 0))
    mod = pl.BlockSpec((1, 6, D), lambda i: (group(i), 0, 0))
    in_specs = [row, pl.BlockSpec((TOP_K, ROW_TILE * TILE_ROWS, LANES), lambda i: (0, i, 0)),
                pl.BlockSpec((ROW_TILE, TOP_K), lambda i: (i, 0)), mod, vec, vec]
    args = [hs1, yg, gate, mt, ln_g.reshape(1, D), ln_b.reshape(1, D)]
    joint = mt_next is not None
    if joint:
        in_specs.append(mod)
        args.append(mt_next)
        out_specs = [_LATENT_ROWS, _LATENT_ROWS, _CTX_ROWS]
        out_shape = [jax.ShapeDtypeStruct((BATCH * SEQ, D), f32), jax.ShapeDtypeStruct((BATCH * SEQ, D), bf16),
                     jax.ShapeDtypeStruct((BATCH * LC, D), f32)]
    else:
        out_specs = [row]
        out_shape = [jax.ShapeDtypeStruct((m, D), f32)]
    return pl.pallas_call(
        functools.partial(_combine_kernel, joint=joint),
        grid=(m // ROW_TILE,),
        in_specs=in_specs, out_specs=out_specs, out_shape=out_shape,
        compiler_params=_cparams(1),
        name="moe_combine_ln",
    )(*args)


def _moe_layer(layer, hs_parts, o, mt, group, ln1_g, ln1_b, ln2_g, ln2_b, r_w, r_b, w1, b1, w2, b2,
               mt_next=None):
    t = o.shape[0]
    hs1, tok, gate, eid, rank, counts = _ln_router(hs_parts, o, mt, group, ln1_g[layer], ln1_b[layer],
                                                   r_w[layer], r_b[layer])
    dest, block_e, n_used, next_e, block_rows = _slot_layout(eid, rank, counts)
    xs = _scatter_rows(tok, dest.reshape(-1), block_e.shape[0] * MOE_TM)
    y = _moe_experts(layer, xs, block_e, n_used, next_e, block_rows, w1, b1, w2, b2)
    yg = _gather_rows(y, dest.T.reshape(-1)).reshape(TOP_K, t * TILE_ROWS, LANES)
    return _combine(hs1, yg, gate, mt, group, ln2_g[layer], ln2_b[layer], mt_next)


def _sgu_kernel(g_ref, v_ref, lg_ref, lb_ref, ws_ref, bs_ref, o_ref):
    v = v_ref[...].astype(f32)
    vn = _ln_rows(v, lg_ref[...], lb_ref[...]).astype(bf16)
    gw = E // SGU_GROUPS
    for g in range(SGU_GROUPS):
        mixed = _dot(ws_ref[g].astype(bf16), vn[:, g * gw:(g + 1) * gw]) + bs_ref[:, g:g + 1]
        o_ref[:, g * gw:(g + 1) * gw] = (g_ref[:, g * gw:(g + 1) * gw].astype(f32) * mixed).astype(o_ref.dtype)


def _sgu(uv, ln_g, ln_b, w_s, b_s):
    m = uv.shape[0]
    c = SGU_CHUNK
    vec = pl.BlockSpec((1, E), lambda i: (0, 0))
    return pl.pallas_call(
        _sgu_kernel,
        grid=(m // c,),
        in_specs=[pl.BlockSpec((c, E), lambda i: (i, 0)),
                  pl.BlockSpec((c, E), lambda i: (i, 1)),
                  vec, vec,
                  pl.BlockSpec((SGU_GROUPS, c, c), lambda i: (0, 0, 0)),
                  pl.BlockSpec((c, SGU_GROUPS), lambda i: (0, 0))],
        out_specs=pl.BlockSpec((c, E), lambda i: (i, 0)),
        out_shape=jax.ShapeDtypeStruct((m, E), bf16),
        compiler_params=_cparams(1),
        name="sgu",
    )(uv, uv, ln_g.reshape(1, E), ln_b.reshape(1, E), w_s, b_s.T)


def kernel(x, c, ctx, c_ctx, mod_w, mod_b, ln1_g, ln1_b, ln2_g, ln2_b, a_w_in, a_conv_w, a_conv_b, a_w_q, a_w_k, a_w_v, a_w_gate, a_b_gate, a_norm_w, a_skip, a_w_out, b_w_in, b_ln_g, b_ln_b, b_w_s, b_b_s, b_w_out, r_w, r_b, e_w1, e_b1, e_w2, e_b2):
    cvec = jnp.concatenate([c, c_ctx[None, :], jnp.zeros((8 - BATCH - 1, D), f32)], axis=0)
    mods = _mod_rows(cvec, mod_w, mod_b).reshape(DEPTH, 8, 6, D)
    rows = [r for b in range(BATCH) for r in (BATCH, b)]
    mt = [jnp.stack([mods[l, r] for r in rows], axis=0) for l in range(DEPTH)]

    hs = (ctx.reshape(BATCH * LC, D), x.reshape(BATCH * SEQ, D))
    hx = _modulate(*hs, mt[0])
    xz = _matmul(hx, a_w_in[0], tm=1024, tn=1024, name="mlstm_in_proj")
    xc, q, k, v, gates = _conv_qkv(xz, a_conv_w[0], a_conv_b[0], a_w_q[0], a_w_k[0], a_w_v[0],
                                   a_w_gate[0], a_b_gate[0])
    gates_t = gates.transpose(0, 2, 1)
    h_fwd = _mlstm(q, k, v, gates, gates_t, 0)
    pre = _mlstm(q, k, v, gates, gates_t, 1, (h_fwd, xc, xz, a_norm_w[0], a_skip[0]))
    o = _matmul(pre.reshape(BATCH * U, E), a_w_out[0], tm=1024, tn=512, name="mlstm_out_proj")
    hs, hx, _ = _moe_layer(0, hs, o, mt[0], _group_joint, ln1_g, ln1_b, ln2_g, ln2_b,
                           r_w, r_b, e_w1, e_b1, e_w2, e_b2, mt_next=mt[1])

    uv = _matmul(hx, b_w_in[0], tm=1024, tn=1024, act="gelu", name="sgu_in_proj")
    gated = _sgu(uv, b_ln_g[0], b_ln_b[0], b_w_s[0], b_b_s[0])
    o = _matmul(gated, b_w_out[0], tm=1024, tn=512, name="sgu_out_proj")
    (out,) = _moe_layer(1, (hs,), o, mt[1], _group_latent, ln1_g, ln1_b, ln2_g, ln2_b,
                        r_w, r_b, e_w1, e_b1, e_w2, e_b2)
    return out.reshape(BATCH, SEQ, D)
```

```python
import functools

import jax
import jax.numpy as jnp
from jax import lax
from jax.experimental import pallas as pl
from jax.experimental.pallas import tpu as pltpu

D = 2048
BATCH = 4
SEQ = 2048
DEPTH = 2
GRID_W = 64
LC = 256
U = LC + SEQ
E = 2 * D
NH = 4
DH = E // NH
QKV_BLOCK = 4
SGU_GROUPS = 8
SGU_CHUNK = 128
NE = 32
TOP_K = 4
F = D
SWIGLU_LIMIT = 7.0
SWIGLU_ALPHA = 1.702
ALPHA = (2 * DEPTH) ** 0.25
LN_EPS = 1e-5

ROW_TILE = 256
MLSTM_CHUNK = 256
MOE_TM = 512
BD_TILE = 256
VMEM_LIMIT = 56 * 1024 * 1024

f32 = jnp.float32
bf16 = jnp.bfloat16


def _cparams(n_axes):
    return pltpu.CompilerParams(dimension_semantics=("arbitrary",) * n_axes,
                                vmem_limit_bytes=VMEM_LIMIT)


def _dot(a, b):
    return jnp.dot(a, b, preferred_element_type=f32)


def _sigmoid(x):
    return 1.0 / (1.0 + jnp.exp(-x))


def _pack_halves(x):
    w = x.shape[1] // 2
    bits = lax.bitcast_convert_type(x.astype(bf16).astype(f32), jnp.uint32)
    return (bits[:, w:] & jnp.uint32(0xFFFF0000)) | (bits[:, :w] >> 16)


def _unpack_halves(p):
    lo = lax.bitcast_convert_type(p << 16, f32)
    hi = lax.bitcast_convert_type(p & jnp.uint32(0xFFFF0000), f32)
    return jnp.concatenate([lo, hi], axis=1)


PACKED = D // 2
LANES = 128
TILE_ROWS = PACKED // LANES


def _store_tile_rows(ref, p):
    r = p.shape[0]
    for s in range(TILE_ROWS):
        ref[pl.ds(s, r, stride=TILE_ROWS), :] = p[:, s * LANES:(s + 1) * LANES]


def _load_tile_rows(ref, r):
    return jnp.concatenate([ref[pl.ds(s, r, stride=TILE_ROWS), :] for s in range(TILE_ROWS)], axis=1)


def _mod_kernel(c_ref, w_ref, b_ref, o_ref):
    c = c_ref[...]
    a = (c * _sigmoid(c)).astype(bf16)
    o_ref[0] = _dot(a, w_ref[0].astype(bf16)) + b_ref[0]


def _mod_rows(cvec, mod_w, mod_b):
    tn = 1024
    return pl.pallas_call(
        _mod_kernel,
        grid=(DEPTH, 6 * D // tn),
        in_specs=[pl.BlockSpec((8, D), lambda l, j: (0, 0)),
                  pl.BlockSpec((1, D, tn), lambda l, j: (l, 0, j)),
                  pl.BlockSpec((1, 1, tn), lambda l, j: (l, 0, j))],
        out_specs=pl.BlockSpec((1, 8, tn), lambda l, j: (l, 0, j)),
        out_shape=jax.ShapeDtypeStruct((DEPTH, 8, 6 * D), f32),
        compiler_params=_cparams(2),
        name="adaln_rows",
    )(cvec, mod_w, mod_b.reshape(DEPTH, 1, 6 * D))


def _group_joint(i):
    tiles = U // ROW_TILE
    return 2 * (i // tiles) + jnp.minimum(i % tiles, 1)


def _group_latent(i):
    return 2 * (i // (SEQ // ROW_TILE)) + 1


JOINT_TILES = U // ROW_TILE


def _is_ctx_tile(i):
    return i % JOINT_TILES == 0


def _ctx_tile(i):
    return i // JOINT_TILES


def _latent_tile(i):
    return (i // JOINT_TILES) * (SEQ // ROW_TILE) + jnp.maximum(i % JOINT_TILES - 1, 0)


_CTX_ROWS = pl.BlockSpec((ROW_TILE, D), lambda i: (_ctx_tile(i), 0))
_LATENT_ROWS = pl.BlockSpec((ROW_TILE, D), lambda i: (_latent_tile(i), 0))


def _joint_rows(ctx_ref, lat_ref):
    is_ctx = _is_ctx_tile(pl.program_id(0))
    return jnp.where(is_ctx, ctx_ref[...], lat_ref[...])


def _modulate_kernel(ctx_ref, lat_ref, m_ref, o_ref):
    m = m_ref[0]
    o_ref[...] = (_joint_rows(ctx_ref, lat_ref) * (1.0 + m[1:2]) + m[0:1]).astype(o_ref.dtype)


def _modulate(ctx2d, x2d, mt):
    m = ctx2d.shape[0] + x2d.shape[0]
    return pl.pallas_call(
        _modulate_kernel,
        grid=(m // ROW_TILE,),
        in_specs=[_CTX_ROWS, _LATENT_ROWS,
                  pl.BlockSpec((1, 6, D), lambda i: (_group_joint(i), 0, 0))],
        out_specs=pl.BlockSpec((ROW_TILE, D), lambda i: (i, 0)),
        out_shape=jax.ShapeDtypeStruct((m, D), bf16),
        compiler_params=_cparams(1),
        name="modulate",
    )(ctx2d, x2d, mt)


def _erf(x):
    return lax.erf(x)


def _mm_kernel(x_ref, w_ref, o_ref, wb_ref, *, act):
    @pl.when(pl.program_id(1) == 0)
    def _():
        wb_ref[...] = w_ref[...].astype(bf16)

    acc = _dot(x_ref[...], wb_ref[...])
    if act == "gelu":
        acc = 0.5 * acc * (1.0 + _erf(acc * (2.0 ** -0.5)))
    o_ref[...] = acc.astype(o_ref.dtype)


def _matmul(x, w, *, tm, tn, act=None, name):
    m, k = x.shape
    n = w.shape[1]
    return pl.pallas_call(
        functools.partial(_mm_kernel, act=act),
        grid=(n // tn, m // tm),
        in_specs=[pl.BlockSpec((tm, k), lambda j, i: (i, 0)),
                  pl.BlockSpec((k, tn), lambda j, i: (0, j))],
        out_specs=pl.BlockSpec((tm, tn), lambda j, i: (i, j)),
        out_shape=jax.ShapeDtypeStruct((m, n), bf16),
        scratch_shapes=[pltpu.VMEM((k, tn), bf16)],
        compiler_params=_cparams(2),
        name=name,
    )(x, w)


def _conv_qkv_kernel(xm_ref, cw_ref, cb_ref, wq_ref, wk_ref, wv_ref, wg_ref, bg_ref,
                     xc_ref, q_ref, k_ref, v_ref, g_ref):
    ct = pl.program_id(1)
    a = xm_ref[0].astype(f32)
    c = a.shape[1]
    r = lax.broadcasted_iota(jnp.int32, (U, c), 0)
    latent = r >= LC
    p = r - LC
    col = jnp.where(latent, p & (GRID_W - 1), r)
    last = jnp.where(latent, GRID_W - 1, LC - 1)
    a_l = jnp.where(col > 0, pltpu.roll(a, 1, 0), 0.0)
    a_r = jnp.where(col < last, pltpu.roll(a, U - 1, 0), 0.0)
    w = cw_ref[...]
    rows = [w[3 * i:3 * i + 1] * a_l + w[3 * i + 1:3 * i + 2] * a + w[3 * i + 2:3 * i + 3] * a_r
            for i in range(3)]
    up = jnp.where(p >= GRID_W, pltpu.roll(rows[0], GRID_W, 0), 0.0)
    down = jnp.where(latent & (p < SEQ - GRID_W), pltpu.roll(rows[2], U - GRID_W, 0), 0.0)
    pre = rows[1] + up + down + cb_ref[...]
    xc = (pre * _sigmoid(pre)).astype(bf16)
    xc_ref[0] = xc
    xm = xm_ref[0]
    q = _dot(xc, _diag_tile(wq_ref)).astype(bf16)
    k = _dot(xc, _diag_tile(wk_ref)).astype(bf16)
    v = _dot(xm, _diag_tile(wv_ref)).astype(bf16)
    q_ref[0] = q
    k_ref[0] = k
    v_ref[0] = v
    g = (_dot(q, wg_ref[0].astype(bf16)) + _dot(k, wg_ref[1].astype(bf16))
         + _dot(v, wg_ref[2].astype(bf16)))

    @pl.when(ct == 0)
    def _():
        g_ref[0] = g + bg_ref[...]

    @pl.when(ct > 0)
    def _():
        g_ref[0] += g


def _diag_tile(ref):
    r = lax.broadcasted_iota(jnp.int32, ref.shape, 0) // QKV_BLOCK
    c = lax.broadcasted_iota(jnp.int32, ref.shape, 1) // QKV_BLOCK
    return jnp.where(r == c, ref[...], 0.0).astype(bf16)


def _block_rows_tiled(w):
    return jnp.tile(w.reshape(E, QKV_BLOCK), (1, BD_TILE // QKV_BLOCK))


def _conv_qkv(xz, conv_w, conv_b, w_q, w_k, w_v, w_gate, b_gate):
    c = BD_TILE
    n_gate = 4 * NH
    act = jax.ShapeDtypeStruct((BATCH, U, E), bf16)
    tile = pl.BlockSpec((1, U, c), lambda b, t: (b, 0, t))
    bd = pl.BlockSpec((c, c), lambda b, t: (t, 0))
    return pl.pallas_call(
        _conv_qkv_kernel,
        grid=(BATCH, E // c),
        in_specs=[tile,
                  pl.BlockSpec((9, c), lambda b, t: (0, t)),
                  pl.BlockSpec((1, c), lambda b, t: (0, t)),
                  bd, bd, bd,
                  pl.BlockSpec((3, c, n_gate), lambda b, t: (0, t, 0)),
                  pl.BlockSpec((1, n_gate), lambda b, t: (0, 0))],
        out_specs=[tile, tile, tile, tile,
                   pl.BlockSpec((1, U, n_gate), lambda b, t: (b, 0, 0))],
        out_shape=[act, act, act, act, jax.ShapeDtypeStruct((BATCH, U, n_gate), f32)],
        compiler_params=_cparams(2),
        name="conv_qkv_gates",
    )(xz.reshape(BATCH, U, 2 * E), conv_w.reshape(9, E), conv_b.reshape(1, E),
      _block_rows_tiled(w_q), _block_rows_tiled(w_k), _block_rows_tiled(w_v),
      w_gate.reshape(3, E, n_gate), b_gate.reshape(1, n_gate))


def _log_sigmoid(x):
    return jnp.minimum(x, 0.0) - jnp.log1p(jnp.exp(-jnp.abs(x)))


def _mlstm_kernel(*refs, direction, finish):
    if finish:
        (q_ref, k_ref, v_ref, gc_ref, gr_ref, hf_ref, xc_ref, z_ref, nw_ref, sk_ref,
         o_ref, ct_ref, ctb_ref, n_ref, m_ref) = refs
    else:
        q_ref, k_ref, v_ref, gc_ref, gr_ref, o_ref, ct_ref, ctb_ref, n_ref, m_ref = refs
    L = MLSTM_CHUNK
    h = pl.program_id(1)
    step = pl.program_id(2)

    @pl.when(step == 0)
    def _():
        ct_ref[...] = jnp.zeros_like(ct_ref)
        ctb_ref[...] = jnp.zeros_like(ctb_ref)
        n_ref[...] = jnp.zeros_like(n_ref)
        m_ref[...] = jnp.full_like(m_ref, -jnp.inf)

    i_idx = 2 * direction * NH + h
    f_idx = (2 * direction + 1) * NH + h
    gc = gc_ref[0]
    gr = gr_ref[0]
    lane = lax.broadcasted_iota(jnp.int32, gc.shape, 1)
    sub = lax.broadcasted_iota(jnp.int32, gr.shape, 0)
    ig_col = jnp.sum(jnp.where(lane == i_idx, gc, 0.0), axis=1, keepdims=True)
    f_col = jnp.sum(jnp.where(lane == f_idx, gc, 0.0), axis=1, keepdims=True)
    ig_row = jnp.sum(jnp.where(sub == i_idx, gr, 0.0), axis=0, keepdims=True)
    f_row = jnp.sum(jnp.where(sub == f_idx, gr, 0.0), axis=0, keepdims=True)
    lf_col = _log_sigmoid(f_col)
    lf_row = _log_sigmoid(f_row)

    t_i = lax.broadcasted_iota(jnp.int32, (L, L), 0)
    s_i = lax.broadcasted_iota(jnp.int32, (L, L), 1)
    seen = (s_i <= t_i) if direction == 0 else (s_i >= t_i)
    b_col = jnp.sum(jnp.where(seen, lf_row, 0.0), axis=1, keepdims=True)
    seen_t = (t_i <= s_i) if direction == 0 else (t_i >= s_i)
    b_row = jnp.sum(jnp.where(seen_t, lf_col, 0.0), axis=0, keepdims=True)
    b_end = jnp.sum(lf_row, axis=1, keepdims=True)

    m_prev = m_ref[:, 0:1]
    log_d = jnp.where(seen, b_col - b_row + ig_row, -jnp.inf)
    g_col = b_col + m_prev
    m_t = jnp.maximum(g_col, jnp.max(log_d, axis=1, keepdims=True))
    dw = jnp.exp(log_d - m_t)
    inter = jnp.exp(g_col - m_t)

    q = q_ref[0]
    k = k_ref[0] * (DH ** -0.5)
    v = v_ref[0]
    s = lax.dot_general(q, k, (((1,), (1,)), ((), ())), preferred_element_type=f32) * dw
    num = _dot(s.astype(bf16), v) + inter * _dot(q, ctb_ref[...])
    qn = jnp.sum(q.astype(f32) * n_ref[...], axis=1, keepdims=True)
    den = jnp.sum(s, axis=1, keepdims=True) + inter * qn
    hout = num / jnp.maximum(jnp.abs(den), jnp.exp(-m_t))

    w_end = b_end - b_col + ig_col
    m_new = jnp.maximum(b_end + m_prev, jnp.max(w_end, axis=0, keepdims=True))
    decay = jnp.exp(b_end + m_prev - m_new)
    kw = k.astype(f32) * jnp.exp(w_end - m_new)
    upd = lax.dot_general(kw.astype(bf16), v, (((0,), (0,)), ((), ())), preferred_element_type=f32)
    c_new = decay * ct_ref[...] + upd
    ct_ref[...] = c_new
    ctb_ref[...] = c_new.astype(bf16)
    n_ref[...] = decay * n_ref[...] + jnp.sum(kw, axis=0, keepdims=True)
    m_ref[...] = jnp.broadcast_to(m_new, m_ref.shape)

    if not finish:
        o_ref[0] = hout.astype(o_ref.dtype)
    else:
        hs = hout + hf_ref[0].astype(f32)
        mu = jnp.mean(hs, axis=1, keepdims=True)
        xc_ = hs - mu
        var = jnp.mean(xc_ * xc_, axis=1, keepdims=True)
        hn = xc_ * lax.rsqrt(var + LN_EPS)
        z = z_ref[0].astype(f32)
        o_ref[0] = ((hn * nw_ref[...] + sk_ref[...] * xc_ref[0].astype(f32))
                    * (z * _sigmoid(z))).astype(o_ref.dtype)


def _mlstm(q, k, v, gates, gates_t, direction, finish_args=None):
    L = MLSTM_CHUNK
    n_chunks = U // L

    if direction == 0:
        def chunk(s):
            return s
    else:
        def chunk(s):
            return jnp.where(s == 0, 0, n_chunks - s)

    tile = pl.BlockSpec((1, L, DH), lambda b, h, s: (b, chunk(s), h))
    in_specs = [tile, tile, tile,
                pl.BlockSpec((1, L, 4 * NH), lambda b, h, s: (b, chunk(s), 0)),
                pl.BlockSpec((1, 4 * NH, L), lambda b, h, s: (b, 0, chunk(s)))]
    args = [q, k, v, gates, gates_t]
    finish = finish_args is not None
    if finish:
        h_fwd, xc, xz, norm_w, skip = finish_args
        vec = pl.BlockSpec((1, DH), lambda b, h, s: (0, h))
        in_specs += [tile, tile,
                     pl.BlockSpec((1, L, DH), lambda b, h, s: (b, chunk(s), NH + h)),
                     vec, vec]
        args += [h_fwd, xc, xz.reshape(BATCH, U, 2 * E), norm_w.reshape(1, E), skip.reshape(1, E)]
    return pl.pallas_call(
        functools.partial(_mlstm_kernel, direction=direction, finish=finish),
        grid=(BATCH, NH, n_chunks),
        in_specs=in_specs,
        out_specs=tile,
        out_shape=jax.ShapeDtypeStruct((BATCH, U, E), bf16),
        scratch_shapes=[pltpu.VMEM((DH, DH), f32), pltpu.VMEM((DH, DH), bf16),
                        pltpu.VMEM((1, DH), f32), pltpu.VMEM((1, 128), f32)],
        compiler_params=_cparams(3),
        name="mlstm_bwd_finish" if finish else "mlstm_fwd",
    )(*args)


def _ln_rows(r, g, b):
    mu = jnp.mean(r, axis=1, keepdims=True)
    rc = r - mu
    var = jnp.mean(rc * rc, axis=1, keepdims=True)
    return rc * lax.rsqrt(var + LN_EPS) * g + b


def _ln_router_kernel(*refs, joint):
    hs_refs, refs = refs[:1 + joint], refs[1 + joint:]
    (o_ref, m_ref, g_ref, b_ref, rw_ref, rb_ref,
     hs1_ref, tok_ref, gate_ref, eid_ref, rank_ref, cnt_ref, carry_ref) = refs

    @pl.when(pl.program_id(0) == 0)
    def _():
        carry_ref[...] = jnp.zeros_like(carry_ref)

    m = m_ref[0]
    hs = _joint_rows(*hs_refs) if joint else hs_refs[0][...]
    r = ALPHA * hs + m[2:3] * o_ref[...].astype(f32)
    hs1 = _ln_rows(r, g_ref[...], b_ref[...])
    hs1_ref[...] = hs1
    tok = hs1 * (1.0 + m[4:5]) + m[3:4]
    t_hi = tok.astype(bf16)
    _store_tile_rows(tok_ref, _pack_halves(tok))
    t_lo = (tok - t_hi.astype(f32)).astype(bf16)
    w = rw_ref[...]
    w_hi = w.astype(bf16)
    w_lo = (w - w_hi.astype(f32)).astype(bf16)
    logits = _dot(t_hi, w_hi) + _dot(t_hi, w_lo) + _dot(t_lo, w_hi) + rb_ref[...]

    lane = lax.broadcasted_iota(jnp.int32, logits.shape, 1).astype(f32)
    left = logits
    vals, ids = [], []
    for _ in range(TOP_K):
        v = jnp.max(left, axis=1, keepdims=True)
        e = jnp.min(jnp.where(left == v, lane, float(NE)), axis=1, keepdims=True)
        vals.append(v)
        ids.append(e)
        left = jnp.where(lane == e, -jnp.inf, left)
    exps = [jnp.exp(v - vals[0]) for v in vals]
    den = exps[0]
    for x in exps[1:]:
        den = den + x
    gate_ref[...] = jnp.concatenate([x / den for x in exps], axis=1)
    eid_ref[...] = jnp.concatenate(ids, axis=1).astype(jnp.int32)

    member = (lane == ids[0]).astype(f32)
    for e in ids[1:]:
        member = member + (lane == e).astype(f32)
    rows = logits.shape[0]
    earlier = (lax.broadcasted_iota(jnp.int32, (rows, rows), 1)
               < lax.broadcasted_iota(jnp.int32, (rows, rows), 0)).astype(bf16)
    before = _dot(earlier, member.astype(bf16)) + carry_ref[...]
    rank_ref[...] = jnp.concatenate(
        [jnp.sum(jnp.where(lane == e, before, 0.0), axis=1, keepdims=True) for e in ids],
        axis=1).astype(jnp.int32)
    total = carry_ref[...] + jnp.sum(member, axis=0, keepdims=True)
    carry_ref[...] = total
    cnt_ref[...] = total


def _ln_router(hs_parts, o, mt, group, ln_g, ln_b, r_w, r_b):
    m = o.shape[0]
    joint = len(hs_parts) == 2
    row = pl.BlockSpec((ROW_TILE, D), lambda i: (i, 0))
    vec = pl.BlockSpec((1, D), lambda i: (0, 0))
    topk = pl.BlockSpec((ROW_TILE, TOP_K), lambda i: (i, 0))
    return pl.pallas_call(
        functools.partial(_ln_router_kernel, joint=joint),
        grid=(m // ROW_TILE,),
        in_specs=([_CTX_ROWS, _LATENT_ROWS] if joint else [row]) + [
                  row,
                  pl.BlockSpec((1, 6, D), lambda i: (group(i), 0, 0)),
                  vec, vec,
                  pl.BlockSpec((D, NE), lambda i: (0, 0)),
                  pl.BlockSpec((1, NE), lambda i: (0, 0))],
        out_specs=[row, pl.BlockSpec((ROW_TILE * TILE_ROWS, LANES), lambda i: (i, 0)),
                   topk, topk, topk, pl.BlockSpec((1, NE), lambda i: (0, 0))],
        out_shape=[jax.ShapeDtypeStruct((m, D), f32),
                   jax.ShapeDtypeStruct((m * TILE_ROWS, LANES), jnp.uint32),
                   jax.ShapeDtypeStruct((m, TOP_K), f32),
                   jax.ShapeDtypeStruct((m, TOP_K), jnp.int32),
                   jax.ShapeDtypeStruct((m, TOP_K), jnp.int32),
                   jax.ShapeDtypeStruct((1, NE), f32)],
        scratch_shapes=[pltpu.VMEM((1, NE), f32)],
        compiler_params=_cparams(1),
        name="ln_router",
    )(*hs_parts, o, mt, ln_g.reshape(1, D), ln_b.reshape(1, D), r_w, r_b.reshape(1, NE))


def _first_of_expert(be_ref, i):
    return jnp.logical_or(i == 0, be_ref[i] != be_ref[jnp.maximum(i - 1, 0)])


ROW_STEPS = 4


def _for_real_rows(valid, rows, compute):
    q = MOE_TM // ROW_STEPS
    for step in range(1, ROW_STEPS + 1):
        covers = rows <= step * q
        if step > 1:
            covers = jnp.logical_and(covers, rows > (step - 1) * q)
        pl.when(jnp.logical_and(valid, covers))(functools.partial(compute, step * q))


def _moe_up_kernel(be_ref, nu_ref, nxt_ref, rows_ref, x_ref, w_hbm, bg_ref, bl_ref, o_ref, stage_ref,
                   cache_ref, sem, *, layer, tf):
    j = pl.program_id(0)
    i = pl.program_id(1)
    valid = i < nu_ref[0]

    def tile_copy(e, jj, half):
        col = pl.multiple_of(half * F + jj * tf, tf)
        return pltpu.make_async_copy(w_hbm.at[layer, e, :, pl.ds(col, tf)], stage_ref.at[half], sem.at[half])

    def start(e, jj):
        tile_copy(e, jj, 0).start()
        tile_copy(e, jj, 1).start()

    @pl.when(jnp.logical_and(j == 0, i == 0))
    def _():
        start(be_ref[0], 0)

    @pl.when(jnp.logical_and(valid, _first_of_expert(be_ref, i)))
    def _():
        tile_copy(0, 0, 0).wait()
        tile_copy(0, 0, 1).wait()
        cache_ref[...] = stage_ref[...].astype(bf16)
        nxt = nxt_ref[i]

        @pl.when(nxt >= 0)
        def _():
            start(nxt, j)

        @pl.when(jnp.logical_and(nxt < 0, j + 1 < pl.num_programs(0)))
        def _():
            start(be_ref[0], j + 1)

    def compute(rows):
        x = _unpack_halves(_load_tile_rows(x_ref, rows)).astype(bf16)
        glu = jnp.minimum(_dot(x, cache_ref[0]) + bg_ref[0, 0], SWIGLU_LIMIT)
        lin = jnp.clip(_dot(x, cache_ref[1]) + bl_ref[0, 0], -SWIGLU_LIMIT, SWIGLU_LIMIT)
        o_ref[pl.ds(0, rows), :] = (glu * _sigmoid(SWIGLU_ALPHA * glu) * (lin + 1.0)).astype(o_ref.dtype)
        if rows < MOE_TM:
            o_ref[pl.ds(rows, MOE_TM - rows), :] = jnp.zeros((MOE_TM - rows, tf), o_ref.dtype)

    _for_real_rows(valid, rows_ref[i], compute)

    @pl.when(jnp.logical_not(valid))
    def _():
        o_ref[...] = jnp.zeros_like(o_ref)


def _moe_down_kernel(be_ref, nu_ref, nxt_ref, rows_ref, a_ref, w_hbm, b_ref, o_ref, stage_ref, cache_ref, sem,
                     *, layer):
    i = pl.program_id(0)
    valid = i < nu_ref[0]

    def expert_copy(e):
        return pltpu.make_async_copy(w_hbm.at[layer, e], stage_ref, sem)

    @pl.when(i == 0)
    def _():
        expert_copy(be_ref[0]).start()

    @pl.when(jnp.logical_and(valid, _first_of_expert(be_ref, i)))
    def _():
        expert_copy(0).wait()
        cache_ref[...] = stage_ref[...].astype(bf16)
        nxt = nxt_ref[i]

        @pl.when(nxt >= 0)
        def _():
            expert_copy(nxt).start()

    def compute(rows):
        y = _dot(a_ref[pl.ds(0, rows), :], cache_ref[...]) + b_ref[0, 0]
        _store_tile_rows(o_ref, _pack_halves(y))
        if rows < MOE_TM:
            o_ref[pl.ds(rows * TILE_ROWS, (MOE_TM - rows) * TILE_ROWS), :] = jnp.zeros(
                ((MOE_TM - rows) * TILE_ROWS, LANES), o_ref.dtype)

    _for_real_rows(valid, rows_ref[i], compute)

    @pl.when(jnp.logical_not(valid))
    def _():
        o_ref[...] = jnp.zeros_like(o_ref)


def _moe_experts(layer, xs, block_e, n_used, next_e, block_rows, w1, b1, w2, b2):
    ns = xs.shape[0] // TILE_ROWS
    nb = ns // MOE_TM
    tf = 1024
    lin0 = F // tf
    b1 = b1.reshape(DEPTH, NE, 1, 2 * F)
    b2 = b2.reshape(DEPTH, NE, 1, D)

    def bspec(col0):
        return pl.BlockSpec((1, 1, 1, tf), lambda j, i, be, nu, nx, br: (layer, be[i], 0, col0 + j))

    act = pl.pallas_call(
        functools.partial(_moe_up_kernel, layer=layer, tf=tf),
        grid_spec=pltpu.PrefetchScalarGridSpec(
            num_scalar_prefetch=4,
            grid=(F // tf, nb),
            in_specs=[pl.BlockSpec((MOE_TM * TILE_ROWS, LANES),
                                   lambda j, i, be, nu, nx, br: (jnp.minimum(i, nu[0] - 1), 0)),
                      pl.BlockSpec(memory_space=pl.ANY), bspec(0), bspec(lin0)],
            out_specs=pl.BlockSpec((MOE_TM, tf), lambda j, i, be, nu, nx, br: (i, j)),
            scratch_shapes=[pltpu.VMEM((2, D, tf), f32), pltpu.VMEM((2, D, tf), bf16),
                            pltpu.SemaphoreType.DMA((2,))]),
        out_shape=jax.ShapeDtypeStruct((ns, F), bf16),
        compiler_params=_cparams(2),
        name="moe_up",
    )(block_e, n_used, next_e, block_rows, xs, w1, b1, b1)

    return pl.pallas_call(
        functools.partial(_moe_down_kernel, layer=layer),
        grid_spec=pltpu.PrefetchScalarGridSpec(
            num_scalar_prefetch=4,
            grid=(nb,),
            in_specs=[pl.BlockSpec((MOE_TM, F), lambda i, be, nu, nx, br: (jnp.minimum(i, nu[0] - 1), 0)),
                      pl.BlockSpec(memory_space=pl.ANY),
                      pl.BlockSpec((1, 1, 1, D), lambda i, be, nu, nx, br: (layer, be[i], 0, 0))],
            out_specs=pl.BlockSpec((MOE_TM * TILE_ROWS, LANES), lambda i, be, nu, nx, br: (i, 0)),
            scratch_shapes=[pltpu.VMEM((F, D), f32), pltpu.VMEM((F, D), bf16), pltpu.SemaphoreType.DMA]),
        out_shape=jax.ShapeDtypeStruct((ns * TILE_ROWS, LANES), jnp.uint32),
        compiler_params=_cparams(1),
        name="moe_down",
    )(block_e, n_used, next_e, block_rows, act, w2, b2)


GATHER_ROWS = 4096


def _gather_kernel(idx_ref, src_ref, out_ref, sem):
    n = GATHER_ROWS
    t = TILE_ROWS

    def issue(r, carry):
        src_row = pl.multiple_of(idx_ref[0, 0, r] * t, t)
        dst_row = pl.multiple_of(r * t, t)
        pltpu.make_async_copy(src_ref.at[pl.ds(src_row, t)], out_ref.at[pl.ds(dst_row, t)], sem).start()
        return carry

    lax.fori_loop(0, n, issue, 0, unroll=8)
    pltpu.make_async_copy(src_ref.at[pl.ds(0, n * t)], out_ref, sem).wait()


def _gather_rows(src, idx):
    n = idx.shape[0]
    nblk = n // GATHER_ROWS
    return pl.pallas_call(
        _gather_kernel,
        grid=(nblk,),
        in_specs=[pl.BlockSpec((1, 1, GATHER_ROWS), lambda i: (i, 0, 0), memory_space=pltpu.SMEM),
                  pl.BlockSpec(memory_space=pl.ANY)],
        out_specs=pl.BlockSpec((GATHER_ROWS * TILE_ROWS, LANES), lambda i: (i, 0)),
        out_shape=jax.ShapeDtypeStruct((n * TILE_ROWS, LANES), src.dtype),
        scratch_shapes=[pltpu.SemaphoreType.DMA],
        compiler_params=pltpu.CompilerParams(dimension_semantics=("arbitrary",),
                                             vmem_limit_bytes=VMEM_LIMIT,
                                             disable_bounds_checks=True),
        name="gather_rows",
    )(idx.reshape(nblk, 1, GATHER_ROWS), src)


SCATTER_TOKENS = 1024


def _scatter_kernel(dest_ref, src_ref, init_ref, out_ref, sem):
    del init_ref
    n = SCATTER_TOKENS
    t = TILE_ROWS

    def issue(a, carry):
        src_row = pl.multiple_of(lax.shift_right_logical(a, TOP_K.bit_length() - 1) * t, t)
        dst_row = pl.multiple_of(dest_ref[0, 0, a] * t, t)
        pltpu.make_async_copy(src_ref.at[pl.ds(src_row, t)], out_ref.at[pl.ds(dst_row, t)], sem).start()
        return carry

    lax.fori_loop(0, n * TOP_K, issue, 0, unroll=8)
    for _ in range(TOP_K):
        pltpu.make_async_copy(src_ref, out_ref.at[pl.ds(0, n * t)], sem).wait()


def _scatter_rows(src, dest, init):
    n = src.shape[0] // TILE_ROWS
    nblk = n // SCATTER_TOKENS
    return pl.pallas_call(
        _scatter_kernel,
        grid=(nblk,),
        in_specs=[pl.BlockSpec((1, 1, SCATTER_TOKENS * TOP_K), lambda i: (i, 0, 0), memory_space=pltpu.SMEM),
                  pl.BlockSpec((SCATTER_TOKENS * TILE_ROWS, LANES), lambda i: (i, 0)),
                  pl.BlockSpec(memory_space=pl.ANY)],
        out_specs=pl.BlockSpec(memory_space=pl.ANY),
        out_shape=jax.ShapeDtypeStruct(init.shape, src.dtype),
        input_output_aliases={2: 0},
        scratch_shapes=[pltpu.SemaphoreType.DMA],
        compiler_params=pltpu.CompilerParams(dimension_semantics=("arbitrary",),
                                             vmem_limit_bytes=VMEM_LIMIT,
                                             disable_bounds_checks=True),
        name="scatter_rows",
    )(dest.reshape(nblk, 1, SCATTER_TOKENS * TOP_K), src, init)


MOE_BLOCKS = -(-(BATCH * U * TOP_K) // MOE_TM) + NE


def _slot_layout(eid, rank, counts):
    nb = MOE_BLOCKS
    counts = counts.reshape(NE).astype(jnp.int32)
    experts = jnp.arange(NE, dtype=jnp.int32)
    pcounts = (counts + MOE_TM - 1) // MOE_TM * MOE_TM
    pends = jnp.cumsum(pcounts)
    pstarts = pends - pcounts
    dest = rank + jnp.sum(jnp.where(eid[:, :, None] == experts, pstarts, 0), axis=-1)
    n_used = pends[-1] // MOE_TM
    blk = jnp.arange(nb, dtype=jnp.int32)
    block_e = jnp.minimum(jnp.sum(pends[None, :] <= (blk * MOE_TM)[:, None], axis=1), NE - 1)
    of_block = block_e[:, None] == experts

    def per_block(table):
        return jnp.sum(jnp.where(of_block, table, 0), axis=1)

    block_rows = jnp.clip(per_block(counts) - (blk * MOE_TM - per_block(pstarts)), 0, MOE_TM)
    block_rows = jnp.where(blk < n_used, block_rows, 0)
    group_end = per_block(pends) // MOE_TM
    next_e = jnp.sum(jnp.where(group_end[:, None] == blk, block_e, 0), axis=1)
    next_e = jnp.where(group_end < n_used, next_e, -1)
    last_e = jnp.sum(jnp.where(blk == n_used - 1, block_e, 0))
    block_e = jnp.where(blk < n_used, block_e, last_e)
    i32 = jnp.int32
    return (dest.astype(i32), block_e.astype(i32), n_used.astype(i32).reshape(1), next_e.astype(i32),
            block_rows.astype(i32))


def _combine_kernel(hs_ref, y_ref, gate_ref, m_ref, g_ref, b_ref, *rest, joint):
    if joint:
        mn_ref, hs2_ref, nxt_ref, ctx_ref = rest
    else:
        (hs2_ref,) = rest
    m = m_ref[0]
    gate = gate_ref[...]
    y = gate[:, 0:1] * _unpack_halves(_load_tile_rows(y_ref.at[0], ROW_TILE))
    for k in range(1, TOP_K):
        y = y + gate[:, k:k + 1] * _unpack_halves(_load_tile_rows(y_ref.at[k], ROW_TILE))
    hs2 = _ln_rows(ALPHA * hs_ref[...] + m[5:6] * y, g_ref[...], b_ref[...])
    if not joint:
        hs2_ref[...] = hs2
        return
    is_ctx = _is_ctx_tile(pl.program_id(0))

    @pl.when(is_ctx)
    def _():
        ctx_ref[...] = hs2

    @pl.when(jnp.logical_not(is_ctx))
    def _():
        mn = mn_ref[0]
        hs2_ref[...] = hs2
        nxt_ref[...] = (hs2 * (1.0 + mn[1:2]) + mn[0:1]).astype(nxt_ref.dtype)


def _combine(hs1, yg, gate, mt, group, ln_g, ln_b, mt_next=None):
    m = hs1.shape[0]
    row = pl.BlockSpec((ROW_TILE, D), lambda i: (i, 0))
    vec = pl.BlockSpec((1, D), lambda i: (0, 0))
    mod = pl.BlockSpec((1, 6, D), lambda i: (group(i), 0, 0))
    in_specs = [row, pl.BlockSpec((TOP_K, ROW_TILE * TILE_ROWS, LANES), lambda i: (0, i, 0)),
                pl.BlockSpec((ROW_TILE, TOP_K), lambda i: (i, 0)), mod, vec, vec]
    args = [hs1, yg, gate, mt, ln_g.reshape(1, D), ln_b.reshape(1, D)]
    joint = mt_next is not None
    if joint:
        in_specs.append(mod)
        args.append(mt_next)
        out_specs = [_LATENT_ROWS, _LATENT_ROWS, _CTX_ROWS]
        out_shape = [jax.ShapeDtypeStruct((BATCH * SEQ, D), f32), jax.ShapeDtypeStruct((BATCH * SEQ, D), bf16),
                     jax.ShapeDtypeStruct((BATCH * LC, D), f32)]
    else:
        out_specs = [row]
        out_shape = [jax.ShapeDtypeStruct((m, D), f32)]
    return pl.pallas_call(
        functools.partial(_combine_kernel, joint=joint),
        grid=(m // ROW_TILE,),
        in_specs=in_specs, out_specs=out_specs, out_shape=out_shape,
        compiler_params=_cparams(1),
        name="moe_combine_ln",
    )(*args)


def _moe_layer(layer, hs_parts, o, mt, group, ln1_g, ln1_b, ln2_g, ln2_b, r_w, r_b, w1, b1, w2, b2,
               slots, mt_next=None):
    t = o.shape[0]
    hs1, tok, gate, eid, rank, counts = _ln_router(hs_parts, o, mt, group, ln1_g[layer], ln1_b[layer],
                                                   r_w[layer], r_b[layer])
    dest, block_e, n_used, next_e, block_rows = _slot_layout(eid, rank, counts)
    xs = _scatter_rows(tok, dest.reshape(-1), slots)
    y = _moe_experts(layer, xs, block_e, n_used, next_e, block_rows, w1, b1, w2, b2)
    yg = _gather_rows(y, dest.T.reshape(-1)).reshape(TOP_K, t * TILE_ROWS, LANES)
    return _combine(hs1, yg, gate, mt, group, ln2_g[layer], ln2_b[layer], mt_next), xs


def _sgu_kernel(g_ref, v_ref, lg_ref, lb_ref, ws_ref, bs_ref, o_ref):
    v = v_ref[...].astype(f32)
    vn = _ln_rows(v, lg_ref[...], lb_ref[...]).astype(bf16)
    gw = E // SGU_GROUPS
    for g in range(SGU_GROUPS):
        mixed = _dot(ws_ref[g].astype(bf16), vn[:, g * gw:(g + 1) * gw]) + bs_ref[:, g:g + 1]
        o_ref[:, g * gw:(g + 1) * gw] = (g_ref[:, g * gw:(g + 1) * gw].astype(f32) * mixed).astype(o_ref.dtype)


def _sgu(uv, ln_g, ln_b, w_s, b_s):
    m = uv.shape[0]
    c = SGU_CHUNK
    vec = pl.BlockSpec((1, E), lambda i: (0, 0))
    return pl.pallas_call(
        _sgu_kernel,
        grid=(m // c,),
        in_specs=[pl.BlockSpec((c, E), lambda i: (i, 0)),
                  pl.BlockSpec((c, E), lambda i: (i, 1)),
                  vec, vec,
                  pl.BlockSpec((SGU_GROUPS, c, c), lambda i: (0, 0, 0)),
                  pl.BlockSpec((c, SGU_GROUPS), lambda i: (0, 0))],
        out_specs=pl.BlockSpec((c, E), lambda i: (i, 0)),
        out_shape=jax.ShapeDtypeStruct((m, E), bf16),
        compiler_params=_cparams(1),
        name="sgu",
    )(uv, uv, ln_g.reshape(1, E), ln_b.reshape(1, E), w_s, b_s.T)


def kernel(x, c, ctx, c_ctx, mod_w, mod_b, ln1_g, ln1_b, ln2_g, ln2_b, a_w_in, a_conv_w, a_conv_b, a_w_q, a_w_k, a_w_v, a_w_gate, a_b_gate, a_norm_w, a_skip, a_w_out, b_w_in, b_ln_g, b_ln_b, b_w_s, b_b_s, b_w_out, r_w, r_b, e_w1, e_b1, e_w2, e_b2):
    cvec = jnp.concatenate([c, c_ctx[None, :], jnp.zeros((8 - BATCH - 1, D), f32)], axis=0)
    mods = _mod_rows(cvec, mod_w, mod_b).reshape(DEPTH, 8, 6, D)
    rows = [r for b in range(BATCH) for r in (BATCH, b)]
    mt = [jnp.stack([mods[l, r] for r in rows], axis=0) for l in range(DEPTH)]

    hs = (ctx.reshape(BATCH * LC, D), x.reshape(BATCH * SEQ, D))
    hx = _modulate(*hs, mt[0])
    xz = _matmul(hx, a_w_in[0], tm=1024, tn=1024, name="mlstm_in_proj")
    xc, q, k, v, gates = _conv_qkv(xz, a_conv_w[0], a_conv_b[0], a_w_q[0], a_w_k[0], a_w_v[0],
                                   a_w_gate[0], a_b_gate[0])
    gates_t = gates.transpose(0, 2, 1)
    h_fwd = _mlstm(q, k, v, gates, gates_t, 0)
    pre = _mlstm(q, k, v, gates, gates_t, 1, (h_fwd, xc, xz, a_norm_w[0], a_skip[0]))
    o = _matmul(pre.reshape(BATCH * U, E), a_w_out[0], tm=1024, tn=512, name="mlstm_out_proj")
    slots = jnp.zeros((MOE_BLOCKS * MOE_TM * TILE_ROWS, LANES), jnp.uint32)
    (hs, hx, _), slots = _moe_layer(0, hs, o, mt[0], _group_joint, ln1_g, ln1_b, ln2_g, ln2_b,
                                    r_w, r_b, e_w1, e_b1, e_w2, e_b2, slots, mt_next=mt[1])

    uv = _matmul(hx, b_w_in[0], tm=1024, tn=1024, act="gelu", name="sgu_in_proj")
    gated = _sgu(uv, b_ln_g[0], b_ln_b[0], b_w_s[0], b_b_s[0])
    o = _matmul(gated, b_w_out[0], tm=1024, tn=512, name="sgu_out_proj")
    (out,), _ = _moe_layer(1, (hs,), o, mt[1], _group_latent, ln1_g, ln1_b, ln2_g, ln2_b,
                           r_w, r_b, e_w1, e_b1, e_w2, e_b2, slots)
    return out.reshape(BATCH, SEQ, D)
```

```python
import functools

import jax
import jax.numpy as jnp
from jax import lax
from jax.experimental import pallas as pl
from jax.experimental.pallas import tpu as pltpu

D = 2048
BATCH = 4
SEQ = 2048
DEPTH = 2
GRID_W = 64
LC = 256
U = LC + SEQ
E = 2 * D
NH = 4
DH = E // NH
QKV_BLOCK = 4
SGU_GROUPS = 8
SGU_CHUNK = 128
NE = 32
TOP_K = 4
F = D
SWIGLU_LIMIT = 7.0
SWIGLU_ALPHA = 1.702
ALPHA = (2 * DEPTH) ** 0.25
LN_EPS = 1e-5

ROW_TILE = 256
MLSTM_CHUNK = 256
MOE_TM = 512
BD_TILE = 256
VMEM_LIMIT = 56 * 1024 * 1024

f32 = jnp.float32
bf16 = jnp.bfloat16


def _cparams(n_axes):
    return pltpu.CompilerParams(dimension_semantics=("arbitrary",) * n_axes,
                                vmem_limit_bytes=VMEM_LIMIT)


def _dot(a, b):
    return jnp.dot(a, b, preferred_element_type=f32)


def _sigmoid(x):
    return 1.0 / (1.0 + jnp.exp(-x))


def _pack_halves(x):
    w = x.shape[1] // 2
    bits = lax.bitcast_convert_type(x.astype(bf16).astype(f32), jnp.uint32)
    return (bits[:, w:] & jnp.uint32(0xFFFF0000)) | (bits[:, :w] >> 16)


def _unpack_halves(p):
    lo = lax.bitcast_convert_type(p << 16, f32)
    hi = lax.bitcast_convert_type(p & jnp.uint32(0xFFFF0000), f32)
    return jnp.concatenate([lo, hi], axis=1)


PACKED = D // 2
LANES = 128
TILE_ROWS = PACKED // LANES


def _store_tile_rows(ref, p):
    r = p.shape[0]
    for s in range(TILE_ROWS):
        ref[pl.ds(s, r, stride=TILE_ROWS), :] = p[:, s * LANES:(s + 1) * LANES]


def _load_tile_rows(ref, r):
    return jnp.concatenate([ref[pl.ds(s, r, stride=TILE_ROWS), :] for s in range(TILE_ROWS)], axis=1)


def _mod_kernel(c_ref, w_ref, b_ref, o_ref):
    c = c_ref[...]
    a = (c * _sigmoid(c)).astype(bf16)
    o_ref[0] = _dot(a, w_ref[0].astype(bf16)) + b_ref[0]


def _mod_rows(cvec, mod_w, mod_b):
    tn = 1024
    return pl.pallas_call(
        _mod_kernel,
        grid=(DEPTH, 6 * D // tn),
        in_specs=[pl.BlockSpec((8, D), lambda l, j: (0, 0)),
                  pl.BlockSpec((1, D, tn), lambda l, j: (l, 0, j)),
                  pl.BlockSpec((1, 1, tn), lambda l, j: (l, 0, j))],
        out_specs=pl.BlockSpec((1, 8, tn), lambda l, j: (l, 0, j)),
        out_shape=jax.ShapeDtypeStruct((DEPTH, 8, 6 * D), f32),
        compiler_params=_cparams(2),
        name="adaln_rows",
    )(cvec, mod_w, mod_b.reshape(DEPTH, 1, 6 * D))


def _group_joint(i):
    tiles = U // ROW_TILE
    return 2 * (i // tiles) + jnp.minimum(i % tiles, 1)


def _group_latent(i):
    return 2 * (i // (SEQ // ROW_TILE)) + 1


JOINT_TILES = U // ROW_TILE


def _is_ctx_tile(i):
    return i % JOINT_TILES == 0


def _ctx_tile(i):
    return i // JOINT_TILES


def _latent_tile(i):
    return (i // JOINT_TILES) * (SEQ // ROW_TILE) + jnp.maximum(i % JOINT_TILES - 1, 0)


_CTX_ROWS = pl.BlockSpec((ROW_TILE, D), lambda i: (_ctx_tile(i), 0))
_LATENT_ROWS = pl.BlockSpec((ROW_TILE, D), lambda i: (_latent_tile(i), 0))


def _joint_rows(ctx_ref, lat_ref):
    is_ctx = _is_ctx_tile(pl.program_id(0))
    return jnp.where(is_ctx, ctx_ref[...], lat_ref[...])


def _modulate_kernel(ctx_ref, lat_ref, m_ref, o_ref):
    m = m_ref[0]
    o_ref[...] = (_joint_rows(ctx_ref, lat_ref) * (1.0 + m[1:2]) + m[0:1]).astype(o_ref.dtype)


def _modulate(ctx2d, x2d, mt):
    m = ctx2d.shape[0] + x2d.shape[0]
    return pl.pallas_call(
        _modulate_kernel,
        grid=(m // ROW_TILE,),
        in_specs=[_CTX_ROWS, _LATENT_ROWS,
                  pl.BlockSpec((1, 6, D), lambda i: (_group_joint(i), 0, 0))],
        out_specs=pl.BlockSpec((ROW_TILE, D), lambda i: (i, 0)),
        out_shape=jax.ShapeDtypeStruct((m, D), bf16),
        compiler_params=_cparams(1),
        name="modulate",
    )(ctx2d, x2d, mt)


def _erf(x):
    return lax.erf(x)


def _mm_kernel(x_ref, w_ref, o_ref, wb_ref, *, act):
    @pl.when(pl.program_id(1) == 0)
    def _():
        wb_ref[...] = w_ref[...].astype(bf16)

    acc = _dot(x_ref[...], wb_ref[...])
    if act == "gelu":
        acc = 0.5 * acc * (1.0 + _erf(acc * (2.0 ** -0.5)))
    o_ref[...] = acc.astype(o_ref.dtype)


def _matmul(x, w, *, tm, tn, act=None, name):
    m, k = x.shape
    n = w.shape[1]
    return pl.pallas_call(
        functools.partial(_mm_kernel, act=act),
        grid=(n // tn, m // tm),
        in_specs=[pl.BlockSpec((tm, k), lambda j, i: (i, 0)),
                  pl.BlockSpec((k, tn), lambda j, i: (0, j))],
        out_specs=pl.BlockSpec((tm, tn), lambda j, i: (i, j)),
        out_shape=jax.ShapeDtypeStruct((m, n), bf16),
        scratch_shapes=[pltpu.VMEM((k, tn), bf16)],
        compiler_params=_cparams(2),
        name=name,
    )(x, w)


def _conv_qkv_kernel(xm_ref, cw_ref, cb_ref, wq_ref, wk_ref, wv_ref, wg_ref, bg_ref,
                     xc_ref, q_ref, k_ref, v_ref, g_ref):
    ct = pl.program_id(1)
    a = xm_ref[0].astype(f32)
    c = a.shape[1]
    r = lax.broadcasted_iota(jnp.int32, (U, c), 0)
    latent = r >= LC
    p = r - LC
    col = jnp.where(latent, p & (GRID_W - 1), r)
    last = jnp.where(latent, GRID_W - 1, LC - 1)
    a_l = jnp.where(col > 0, pltpu.roll(a, 1, 0), 0.0)
    a_r = jnp.where(col < last, pltpu.roll(a, U - 1, 0), 0.0)
    w = cw_ref[...]
    rows = [w[3 * i:3 * i + 1] * a_l + w[3 * i + 1:3 * i + 2] * a + w[3 * i + 2:3 * i + 3] * a_r
            for i in range(3)]
    up = jnp.where(p >= GRID_W, pltpu.roll(rows[0], GRID_W, 0), 0.0)
    down = jnp.where(latent & (p < SEQ - GRID_W), pltpu.roll(rows[2], U - GRID_W, 0), 0.0)
    pre = rows[1] + up + down + cb_ref[...]
    xc = (pre * _sigmoid(pre)).astype(bf16)
    xc_ref[0] = xc
    xm = xm_ref[0]
    q = _dot(xc, _diag_tile(wq_ref)).astype(bf16)
    k = _dot(xc, _diag_tile(wk_ref)).astype(bf16)
    v = _dot(xm, _diag_tile(wv_ref)).astype(bf16)
    q_ref[0] = q
    k_ref[0] = k
    v_ref[0] = v
    g = (_dot(q, wg_ref[0].astype(bf16)) + _dot(k, wg_ref[1].astype(bf16))
         + _dot(v, wg_ref[2].astype(bf16)))

    @pl.when(ct == 0)
    def _():
        g_ref[0] = g + bg_ref[...]

    @pl.when(ct > 0)
    def _():
        g_ref[0] += g


def _diag_tile(ref):
    r = lax.broadcasted_iota(jnp.int32, ref.shape, 0) // QKV_BLOCK
    c = lax.broadcasted_iota(jnp.int32, ref.shape, 1) // QKV_BLOCK
    return jnp.where(r == c, ref[...], 0.0).astype(bf16)


def _block_rows_tiled(w):
    return jnp.tile(w.reshape(E, QKV_BLOCK), (1, BD_TILE // QKV_BLOCK))


def _conv_qkv(xz, conv_w, conv_b, w_q, w_k, w_v, w_gate, b_gate):
    c = BD_TILE
    n_gate = 4 * NH
    act = jax.ShapeDtypeStruct((BATCH, U, E), bf16)
    tile = pl.BlockSpec((1, U, c), lambda b, t: (b, 0, t))
    bd = pl.BlockSpec((c, c), lambda b, t: (t, 0))
    return pl.pallas_call(
        _conv_qkv_kernel,
        grid=(BATCH, E // c),
        in_specs=[tile,
                  pl.BlockSpec((9, c), lambda b, t: (0, t)),
                  pl.BlockSpec((1, c), lambda b, t: (0, t)),
                  bd, bd, bd,
                  pl.BlockSpec((3, c, n_gate), lambda b, t: (0, t, 0)),
                  pl.BlockSpec((1, n_gate), lambda b, t: (0, 0))],
        out_specs=[tile, tile, tile, tile,
                   pl.BlockSpec((1, U, n_gate), lambda b, t: (b, 0, 0))],
        out_shape=[act, act, act, act, jax.ShapeDtypeStruct((BATCH, U, n_gate), f32)],
        compiler_params=_cparams(2),
        name="conv_qkv_gates",
    )(xz.reshape(BATCH, U, 2 * E), conv_w.reshape(9, E), conv_b.reshape(1, E),
      _block_rows_tiled(w_q), _block_rows_tiled(w_k), _block_rows_tiled(w_v),
      w_gate.reshape(3, E, n_gate), b_gate.reshape(1, n_gate))


def _log_sigmoid(x):
    return jnp.minimum(x, 0.0) - jnp.log1p(jnp.exp(-jnp.abs(x)))


def _mlstm_kernel(*refs, direction, finish):
    if finish:
        (q_ref, k_ref, v_ref, gc_ref, gr_ref, hf_ref, xc_ref, z_ref, nw_ref, sk_ref,
         o_ref, ct_ref, ctb_ref, n_ref, m_ref) = refs
    else:
        q_ref, k_ref, v_ref, gc_ref, gr_ref, o_ref, ct_ref, ctb_ref, n_ref, m_ref = refs
    L = MLSTM_CHUNK
    h = pl.program_id(1)
    step = pl.program_id(2)

    @pl.when(step == 0)
    def _():
        ct_ref[...] = jnp.zeros_like(ct_ref)
        ctb_ref[...] = jnp.zeros_like(ctb_ref)
        n_ref[...] = jnp.zeros_like(n_ref)
        m_ref[...] = jnp.full_like(m_ref, -jnp.inf)

    i_idx = 2 * direction * NH + h
    f_idx = (2 * direction + 1) * NH + h
    gc = gc_ref[0]
    gr = gr_ref[0]
    lane = lax.broadcasted_iota(jnp.int32, gc.shape, 1)
    sub = lax.broadcasted_iota(jnp.int32, gr.shape, 0)
    ig_col = jnp.sum(jnp.where(lane == i_idx, gc, 0.0), axis=1, keepdims=True)
    f_col = jnp.sum(jnp.where(lane == f_idx, gc, 0.0), axis=1, keepdims=True)
    ig_row = jnp.sum(jnp.where(sub == i_idx, gr, 0.0), axis=0, keepdims=True)
    f_row = jnp.sum(jnp.where(sub == f_idx, gr, 0.0), axis=0, keepdims=True)
    lf_col = _log_sigmoid(f_col)
    lf_row = _log_sigmoid(f_row)

    t_i = lax.broadcasted_iota(jnp.int32, (L, L), 0)
    s_i = lax.broadcasted_iota(jnp.int32, (L, L), 1)
    seen = (s_i <= t_i) if direction == 0 else (s_i >= t_i)
    b_col = jnp.sum(jnp.where(seen, lf_row, 0.0), axis=1, keepdims=True)
    seen_t = (t_i <= s_i) if direction == 0 else (t_i >= s_i)
    b_row = jnp.sum(jnp.where(seen_t, lf_col, 0.0), axis=0, keepdims=True)
    b_end = jnp.sum(lf_row, axis=1, keepdims=True)

    m_prev = m_ref[:, 0:1]
    log_d = jnp.where(seen, b_col - b_row + ig_row, -jnp.inf)
    g_col = b_col + m_prev
    m_t = jnp.maximum(g_col, jnp.max(log_d, axis=1, keepdims=True))
    dw = jnp.exp(log_d - m_t)
    inter = jnp.exp(g_col - m_t)

    q = q_ref[0]
    k = k_ref[0] * (DH ** -0.5)
    v = v_ref[0]
    s = lax.dot_general(q, k, (((1,), (1,)), ((), ())), preferred_element_type=f32) * dw
    num = _dot(s.astype(bf16), v) + inter * _dot(q, ctb_ref[...])
    qn = jnp.sum(q.astype(f32) * n_ref[...], axis=1, keepdims=True)
    den = jnp.sum(s, axis=1, keepdims=True) + inter * qn
    hout = num / jnp.maximum(jnp.abs(den), jnp.exp(-m_t))

    w_end = b_end - b_col + ig_col
    m_new = jnp.maximum(b_end + m_prev, jnp.max(w_end, axis=0, keepdims=True))
    decay = jnp.exp(b_end + m_prev - m_new)
    kw = k.astype(f32) * jnp.exp(w_end - m_new)
    upd = lax.dot_general(kw.astype(bf16), v, (((0,), (0,)), ((), ())), preferred_element_type=f32)
    c_new = decay * ct_ref[...] + upd
    ct_ref[...] = c_new
    ctb_ref[...] = c_new.astype(bf16)
    n_ref[...] = decay * n_ref[...] + jnp.sum(kw, axis=0, keepdims=True)
    m_ref[...] = jnp.broadcast_to(m_new, m_ref.shape)

    if not finish:
        o_ref[0] = hout.astype(o_ref.dtype)
    else:
        hs = hout + hf_ref[0].astype(f32)
        mu = jnp.mean(hs, axis=1, keepdims=True)
        xc_ = hs - mu
        var = jnp.mean(xc_ * xc_, axis=1, keepdims=True)
        hn = xc_ * lax.rsqrt(var + LN_EPS)
        z = z_ref[0].astype(f32)
        o_ref[0] = ((hn * nw_ref[...] + sk_ref[...] * xc_ref[0].astype(f32))
                    * (z * _sigmoid(z))).astype(o_ref.dtype)


def _mlstm(q, k, v, gates, gates_t, direction, finish_args=None):
    L = MLSTM_CHUNK
    n_chunks = U // L

    if direction == 0:
        def chunk(s):
            return s
    else:
        def chunk(s):
            return jnp.where(s == 0, 0, n_chunks - s)

    tile = pl.BlockSpec((1, L, DH), lambda b, h, s: (b, chunk(s), h))
    in_specs = [tile, tile, tile,
                pl.BlockSpec((1, L, 4 * NH), lambda b, h, s: (b, chunk(s), 0)),
                pl.BlockSpec((1, 4 * NH, L), lambda b, h, s: (b, 0, chunk(s)))]
    args = [q, k, v, gates, gates_t]
    finish = finish_args is not None
    if finish:
        h_fwd, xc, xz, norm_w, skip = finish_args
        vec = pl.BlockSpec((1, DH), lambda b, h, s: (0, h))
        in_specs += [tile, tile,
                     pl.BlockSpec((1, L, DH), lambda b, h, s: (b, chunk(s), NH + h)),
                     vec, vec]
        args += [h_fwd, xc, xz.reshape(BATCH, U, 2 * E), norm_w.reshape(1, E), skip.reshape(1, E)]
    return pl.pallas_call(
        functools.partial(_mlstm_kernel, direction=direction, finish=finish),
        grid=(BATCH, NH, n_chunks),
        in_specs=in_specs,
        out_specs=tile,
        out_shape=jax.ShapeDtypeStruct((BATCH, U, E), bf16),
        scratch_shapes=[pltpu.VMEM((DH, DH), f32), pltpu.VMEM((DH, DH), bf16),
                        pltpu.VMEM((1, DH), f32), pltpu.VMEM((1, 128), f32)],
        compiler_params=_cparams(3),
        name="mlstm_bwd_finish" if finish else "mlstm_fwd",
    )(*args)


def _ln_rows(r, g, b):
    mu = jnp.mean(r, axis=1, keepdims=True)
    rc = r - mu
    var = jnp.mean(rc * rc, axis=1, keepdims=True)
    return rc * lax.rsqrt(var + LN_EPS) * g + b


def _ln_router_kernel(*refs, joint):
    hs_refs, refs = refs[:1 + joint], refs[1 + joint:]
    (o_ref, m_ref, g_ref, b_ref, rw_ref, rb_ref,
     hs1_ref, tok_ref, gate_ref, eid_ref, rank_ref, cnt_ref, carry_ref) = refs

    @pl.when(pl.program_id(0) == 0)
    def _():
        carry_ref[...] = jnp.zeros_like(carry_ref)

    m = m_ref[0]
    hs = _joint_rows(*hs_refs) if joint else hs_refs[0][...]
    r = ALPHA * hs + m[2:3] * o_ref[...].astype(f32)
    hs1 = _ln_rows(r, g_ref[...], b_ref[...])
    hs1_ref[...] = hs1
    tok = hs1 * (1.0 + m[4:5]) + m[3:4]
    t_hi = tok.astype(bf16)
    _store_tile_rows(tok_ref, _pack_halves(tok))
    t_lo = (tok - t_hi.astype(f32)).astype(bf16)
    w = rw_ref[...]
    w_hi = w.astype(bf16)
    w_lo = (w - w_hi.astype(f32)).astype(bf16)
    logits = _dot(t_hi, w_hi) + _dot(t_hi, w_lo) + _dot(t_lo, w_hi) + rb_ref[...]

    lane = lax.broadcasted_iota(jnp.int32, logits.shape, 1).astype(f32)
    left = logits
    vals, ids = [], []
    for _ in range(TOP_K):
        v = jnp.max(left, axis=1, keepdims=True)
        e = jnp.min(jnp.where(left == v, lane, float(NE)), axis=1, keepdims=True)
        vals.append(v)
        ids.append(e)
        left = jnp.where(lane == e, -jnp.inf, left)
    exps = [jnp.exp(v - vals[0]) for v in vals]
    den = exps[0]
    for x in exps[1:]:
        den = den + x
    gate_ref[...] = jnp.concatenate([x / den for x in exps], axis=1)
    eid_ref[...] = jnp.concatenate(ids, axis=1).astype(jnp.int32)

    member = (lane == ids[0]).astype(f32)
    for e in ids[1:]:
        member = member + (lane == e).astype(f32)
    rows = logits.shape[0]
    earlier = (lax.broadcasted_iota(jnp.int32, (rows, rows), 1)
               < lax.broadcasted_iota(jnp.int32, (rows, rows), 0)).astype(bf16)
    before = _dot(earlier, member.astype(bf16)) + carry_ref[...]
    rank_ref[...] = jnp.concatenate(
        [jnp.sum(jnp.where(lane == e, before, 0.0), axis=1, keepdims=True) for e in ids],
        axis=1).astype(jnp.int32)
    total = carry_ref[...] + jnp.sum(member, axis=0, keepdims=True)
    carry_ref[...] = total
    cnt_ref[...] = total


def _ln_router(hs_parts, o, mt, group, ln_g, ln_b, r_w, r_b):
    m = o.shape[0]
    joint = len(hs_parts) == 2
    row = pl.BlockSpec((ROW_TILE, D), lambda i: (i, 0))
    vec = pl.BlockSpec((1, D), lambda i: (0, 0))
    topk = pl.BlockSpec((ROW_TILE, TOP_K), lambda i: (i, 0))
    return pl.pallas_call(
        functools.partial(_ln_router_kernel, joint=joint),
        grid=(m // ROW_TILE,),
        in_specs=([_CTX_ROWS, _LATENT_ROWS] if joint else [row]) + [
                  row,
                  pl.BlockSpec((1, 6, D), lambda i: (group(i), 0, 0)),
                  vec, vec,
                  pl.BlockSpec((D, NE), lambda i: (0, 0)),
                  pl.BlockSpec((1, NE), lambda i: (0, 0))],
        out_specs=[row, pl.BlockSpec((ROW_TILE * TILE_ROWS, LANES), lambda i: (i, 0)),
                   topk, topk, topk, pl.BlockSpec((1, NE), lambda i: (0, 0))],
        out_shape=[jax.ShapeDtypeStruct((m, D), f32),
                   jax.ShapeDtypeStruct((m * TILE_ROWS, LANES), jnp.uint32),
                   jax.ShapeDtypeStruct((m, TOP_K), f32),
                   jax.ShapeDtypeStruct((m, TOP_K), jnp.int32),
                   jax.ShapeDtypeStruct((m, TOP_K), jnp.int32),
                   jax.ShapeDtypeStruct((1, NE), f32)],
        scratch_shapes=[pltpu.VMEM((1, NE), f32)],
        compiler_params=_cparams(1),
        name="ln_router",
    )(*hs_parts, o, mt, ln_g.reshape(1, D), ln_b.reshape(1, D), r_w, r_b.reshape(1, NE))


def _first_of_expert(be_ref, i):
    return jnp.logical_or(i == 0, be_ref[i] != be_ref[jnp.maximum(i - 1, 0)])


ROW_STEPS = 8


def _for_real_rows(valid, rows, compute):
    q = MOE_TM // ROW_STEPS
    for step in range(1, ROW_STEPS + 1):
        covers = rows <= step * q
        if step > 1:
            covers = jnp.logical_and(covers, rows > (step - 1) * q)
        pl.when(jnp.logical_and(valid, covers))(functools.partial(compute, step * q))


def _moe_up_kernel(be_ref, nu_ref, nxt_ref, rows_ref, x_ref, w_hbm, bg_ref, bl_ref, o_ref, stage_ref,
                   cache_ref, sem, *, layer, tf):
    j = pl.program_id(0)
    i = pl.program_id(1)
    valid = i < nu_ref[0]

    def tile_copy(e, jj, half):
        col = pl.multiple_of(half * F + jj * tf, tf)
        return pltpu.make_async_copy(w_hbm.at[layer, e, :, pl.ds(col, tf)], stage_ref.at[half], sem.at[half])

    def start(e, jj):
        tile_copy(e, jj, 0).start()
        tile_copy(e, jj, 1).start()

    @pl.when(jnp.logical_and(j == 0, i == 0))
    def _():
        start(be_ref[0], 0)

    @pl.when(jnp.logical_and(valid, _first_of_expert(be_ref, i)))
    def _():
        tile_copy(0, 0, 0).wait()
        tile_copy(0, 0, 1).wait()
        cache_ref[...] = stage_ref[...].astype(bf16)
        nxt = nxt_ref[i]

        @pl.when(nxt >= 0)
        def _():
            start(nxt, j)

        @pl.when(jnp.logical_and(nxt < 0, j + 1 < pl.num_programs(0)))
        def _():
            start(be_ref[0], j + 1)

    def compute(rows):
        x = _unpack_halves(_load_tile_rows(x_ref, rows)).astype(bf16)
        glu = jnp.minimum(_dot(x, cache_ref[0]) + bg_ref[0, 0], SWIGLU_LIMIT)
        lin = jnp.clip(_dot(x, cache_ref[1]) + bl_ref[0, 0], -SWIGLU_LIMIT, SWIGLU_LIMIT)
        o_ref[pl.ds(0, rows), :] = (glu * _sigmoid(SWIGLU_ALPHA * glu) * (lin + 1.0)).astype(o_ref.dtype)
        if rows < MOE_TM:
            o_ref[pl.ds(rows, MOE_TM - rows), :] = jnp.zeros((MOE_TM - rows, tf), o_ref.dtype)

    _for_real_rows(valid, rows_ref[i], compute)

    @pl.when(jnp.logical_not(valid))
    def _():
        o_ref[...] = jnp.zeros_like(o_ref)


def _moe_down_kernel(be_ref, nu_ref, nxt_ref, rows_ref, a_ref, w_hbm, b_ref, o_ref, stage_ref, cache_ref, sem,
                     *, layer):
    i = pl.program_id(0)
    valid = i < nu_ref[0]

    def expert_copy(e):
        return pltpu.make_async_copy(w_hbm.at[layer, e], stage_ref, sem)

    @pl.when(i == 0)
    def _():
        expert_copy(be_ref[0]).start()

    @pl.when(jnp.logical_and(valid, _first_of_expert(be_ref, i)))
    def _():
        expert_copy(0).wait()
        cache_ref[...] = stage_ref[...].astype(bf16)
        nxt = nxt_ref[i]

        @pl.when(nxt >= 0)
        def _():
            expert_copy(nxt).start()

    def compute(rows):
        y = _dot(a_ref[pl.ds(0, rows), :], cache_ref[...]) + b_ref[0, 0]
        _store_tile_rows(o_ref, _pack_halves(y))
        if rows < MOE_TM:
            o_ref[pl.ds(rows * TILE_ROWS, (MOE_TM - rows) * TILE_ROWS), :] = jnp.zeros(
                ((MOE_TM - rows) * TILE_ROWS, LANES), o_ref.dtype)

    _for_real_rows(valid, rows_ref[i], compute)

    @pl.when(jnp.logical_not(valid))
    def _():
        o_ref[...] = jnp.zeros_like(o_ref)


def _moe_experts(layer, xs, block_e, n_used, next_e, block_rows, w1, b1, w2, b2):
    ns = xs.shape[0] // TILE_ROWS
    nb = ns // MOE_TM
    tf = 1024
    lin0 = F // tf
    b1 = b1.reshape(DEPTH, NE, 1, 2 * F)
    b2 = b2.reshape(DEPTH, NE, 1, D)

    def bspec(col0):
        return pl.BlockSpec((1, 1, 1, tf), lambda j, i, be, nu, nx, br: (layer, be[i], 0, col0 + j))

    act = pl.pallas_call(
        functools.partial(_moe_up_kernel, layer=layer, tf=tf),
        grid_spec=pltpu.PrefetchScalarGridSpec(
            num_scalar_prefetch=4,
            grid=(F // tf, nb),
            in_specs=[pl.BlockSpec((MOE_TM * TILE_ROWS, LANES),
                                   lambda j, i, be, nu, nx, br: (jnp.minimum(i, nu[0] - 1), 0)),
                      pl.BlockSpec(memory_space=pl.ANY), bspec(0), bspec(lin0)],
            out_specs=pl.BlockSpec((MOE_TM, tf), lambda j, i, be, nu, nx, br: (i, j)),
            scratch_shapes=[pltpu.VMEM((2, D, tf), f32), pltpu.VMEM((2, D, tf), bf16),
                            pltpu.SemaphoreType.DMA((2,))]),
        out_shape=jax.ShapeDtypeStruct((ns, F), bf16),
        compiler_params=_cparams(2),
        name="moe_up",
    )(block_e, n_used, next_e, block_rows, xs, w1, b1, b1)

    return pl.pallas_call(
        functools.partial(_moe_down_kernel, layer=layer),
        grid_spec=pltpu.PrefetchScalarGridSpec(
            num_scalar_prefetch=4,
            grid=(nb,),
            in_specs=[pl.BlockSpec((MOE_TM, F), lambda i, be, nu, nx, br: (jnp.minimum(i, nu[0] - 1), 0)),
                      pl.BlockSpec(memory_space=pl.ANY),
                      pl.BlockSpec((1, 1, 1, D), lambda i, be, nu, nx, br: (layer, be[i], 0, 0))],
            out_specs=pl.BlockSpec((MOE_TM * TILE_ROWS, LANES), lambda i, be, nu, nx, br: (i, 0)),
            scratch_shapes=[pltpu.VMEM((F, D), f32), pltpu.VMEM((F, D), bf16), pltpu.SemaphoreType.DMA]),
        out_shape=jax.ShapeDtypeStruct((ns * TILE_ROWS, LANES), jnp.uint32),
        compiler_params=_cparams(1),
        name="moe_down",
    )(block_e, n_used, next_e, block_rows, act, w2, b2)


GATHER_ROWS = 4096


def _gather_kernel(idx_ref, src_ref, out_ref, sem):
    n = GATHER_ROWS
    t = TILE_ROWS

    def issue(r, carry):
        src_row = pl.multiple_of(idx_ref[0, 0, r] * t, t)
        dst_row = pl.multiple_of(r * t, t)
        pltpu.make_async_copy(src_ref.at[pl.ds(src_row, t)], out_ref.at[pl.ds(dst_row, t)], sem).start()
        return carry

    lax.fori_loop(0, n, issue, 0, unroll=8)
    pltpu.make_async_copy(src_ref.at[pl.ds(0, n * t)], out_ref, sem).wait()


def _gather_rows(src, idx):
    n = idx.shape[0]
    nblk = n // GATHER_ROWS
    return pl.pallas_call(
        _gather_kernel,
        grid=(nblk,),
        in_specs=[pl.BlockSpec((1, 1, GATHER_ROWS), lambda i: (i, 0, 0), memory_space=pltpu.SMEM),
                  pl.BlockSpec(memory_space=pl.ANY)],
        out_specs=pl.BlockSpec((GATHER_ROWS * TILE_ROWS, LANES), lambda i: (i, 0)),
        out_shape=jax.ShapeDtypeStruct((n * TILE_ROWS, LANES), src.dtype),
        scratch_shapes=[pltpu.SemaphoreType.DMA],
        compiler_params=pltpu.CompilerParams(dimension_semantics=("arbitrary",),
                                             vmem_limit_bytes=VMEM_LIMIT,
                                             disable_bounds_checks=True),
        name="gather_rows",
    )(idx.reshape(nblk, 1, GATHER_ROWS), src)


SCATTER_TOKENS = 1024


def _scatter_kernel(dest_ref, src_ref, init_ref, out_ref, sem):
    del init_ref
    n = SCATTER_TOKENS
    t = TILE_ROWS

    def issue(a, carry):
        src_row = pl.multiple_of(lax.shift_right_logical(a, TOP_K.bit_length() - 1) * t, t)
        dst_row = pl.multiple_of(dest_ref[0, 0, a] * t, t)
        pltpu.make_async_copy(src_ref.at[pl.ds(src_row, t)], out_ref.at[pl.ds(dst_row, t)], sem).start()
        return carry

    lax.fori_loop(0, n * TOP_K, issue, 0, unroll=8)
    for _ in range(TOP_K):
        pltpu.make_async_copy(src_ref, out_ref.at[pl.ds(0, n * t)], sem).wait()


def _scatter_rows(src, dest, init):
    n = src.shape[0] // TILE_ROWS
    nblk = n // SCATTER_TOKENS
    return pl.pallas_call(
        _scatter_kernel,
        grid=(nblk,),
        in_specs=[pl.BlockSpec((1, 1, SCATTER_TOKENS * TOP_K), lambda i: (i, 0, 0), memory_space=pltpu.SMEM),
                  pl.BlockSpec((SCATTER_TOKENS * TILE_ROWS, LANES), lambda i: (i, 0)),
                  pl.BlockSpec(memory_space=pl.ANY)],
        out_specs=pl.BlockSpec(memory_space=pl.ANY),
        out_shape=jax.ShapeDtypeStruct(init.shape, src.dtype),
        input_output_aliases={2: 0},
        scratch_shapes=[pltpu.SemaphoreType.DMA],
        compiler_params=pltpu.CompilerParams(dimension_semantics=("arbitrary",),
                                             vmem_limit_bytes=VMEM_LIMIT,
                                             disable_bounds_checks=True),
        name="scatter_rows",
    )(dest.reshape(nblk, 1, SCATTER_TOKENS * TOP_K), src, init)


MOE_BLOCKS = -(-(BATCH * U * TOP_K) // MOE_TM) + NE


def _slot_layout(eid, rank, counts):
    nb = MOE_BLOCKS
    counts = counts.reshape(NE).astype(jnp.int32)
    experts = jnp.arange(NE, dtype=jnp.int32)
    pcounts = (counts + MOE_TM - 1) // MOE_TM * MOE_TM
    pends = jnp.cumsum(pcounts)
    pstarts = pends - pcounts
    dest = rank + jnp.sum(jnp.where(eid[:, :, None] == experts, pstarts, 0), axis=-1)
    n_used = pends[-1] // MOE_TM
    blk = jnp.arange(nb, dtype=jnp.int32)
    block_e = jnp.minimum(jnp.sum(pends[None, :] <= (blk * MOE_TM)[:, None], axis=1), NE - 1)
    of_block = block_e[:, None] == experts

    def per_block(table):
        return jnp.sum(jnp.where(of_block, table, 0), axis=1)

    block_rows = jnp.clip(per_block(counts) - (blk * MOE_TM - per_block(pstarts)), 0, MOE_TM)
    block_rows = jnp.where(blk < n_used, block_rows, 0)
    group_end = per_block(pends) // MOE_TM
    next_e = jnp.sum(jnp.where(group_end[:, None] == blk, block_e, 0), axis=1)
    next_e = jnp.where(group_end < n_used, next_e, -1)
    last_e = jnp.sum(jnp.where(blk == n_used - 1, block_e, 0))
    block_e = jnp.where(blk < n_used, block_e, last_e)
    i32 = jnp.int32
    return (dest.astype(i32), block_e.astype(i32), n_used.astype(i32).reshape(1), next_e.astype(i32),
            block_rows.astype(i32))


def _combine_kernel(hs_ref, y_ref, gate_ref, m_ref, g_ref, b_ref, *rest, joint):
    if joint:
        mn_ref, hs2_ref, nxt_ref, ctx_ref = rest
    else:
        (hs2_ref,) = rest
    m = m_ref[0]
    gate = gate_ref[...]
    y = gate[:, 0:1] * _unpack_halves(_load_tile_rows(y_ref.at[0], ROW_TILE))
    for k in range(1, TOP_K):
        y = y + gate[:, k:k + 1] * _unpack_halves(_load_tile_rows(y_ref.at[k], ROW_TILE))
    hs2 = _ln_rows(ALPHA * hs_ref[...] + m[5:6] * y, g_ref[...], b_ref[...])
    if not joint:
        hs2_ref[...] = hs2
        return
    is_ctx = _is_ctx_tile(pl.program_id(0))

    @pl.when(is_ctx)
    def _():
        ctx_ref[...] = hs2

    @pl.when(jnp.logical_not(is_ctx))
    def _():
        mn = mn_ref[0]
        hs2_ref[...] = hs2
        nxt_ref[...] = (hs2 * (1.0 + mn[1:2]) + mn[0:1]).astype(nxt_ref.dtype)


def _combine(hs1, yg, gate, mt, group, ln_g, ln_b, mt_next=None):
    m = hs1.shape[0]
    row = pl.BlockSpec((ROW_TILE, D), lambda i: (i, 0))
    vec = pl.BlockSpec((1, D), lambda i: (0, 0))
    mod = pl.BlockSpec((1, 6, D), lambda i: (group(i), 0, 0))
    in_specs = [row, pl.BlockSpec((TOP_K, ROW_TILE * TILE_ROWS, LANES), lambda i: (0, i, 0)),
                pl.BlockSpec((ROW_TILE, TOP_K), lambda i: (i, 0)), mod, vec, vec]
    args = [hs1, yg, gate, mt, ln_g.reshape(1, D), ln_b.reshape(1, D)]
    joint = mt_next is not None
    if joint:
        in_specs.append(mod)
        args.append(mt_next)
        out_specs = [_LATENT_ROWS, _LATENT_ROWS, _CTX_ROWS]
        out_shape = [jax.ShapeDtypeStruct((BATCH * SEQ, D), f32), jax.ShapeDtypeStruct((BATCH * SEQ, D), bf16),
                     jax.ShapeDtypeStruct((BATCH * LC, D), f32)]
    else:
        out_specs = [row]
        out_shape = [jax.ShapeDtypeStruct((m, D), f32)]
    return pl.pallas_call(
        functools.partial(_combine_kernel, joint=joint),
        grid=(m // ROW_TILE,),
        in_specs=in_specs, out_specs=out_specs, out_shape=out_shape,
        compiler_params=_cparams(1),
        name="moe_combine_ln",
    )(*args)


def _moe_layer(layer, hs_parts, o, mt, group, ln1_g, ln1_b, ln2_g, ln2_b, r_w, r_b, w1, b1, w2, b2,
               slots, mt_next=None):
    t = o.shape[0]
    hs1, tok, gate, eid, rank, counts = _ln_router(hs_parts, o, mt, group, ln1_g[layer], ln1_b[layer],
                                                   r_w[layer], r_b[layer])
    dest, block_e, n_used, next_e, block_rows = _slot_layout(eid, rank, counts)
    xs = _scatter_rows(tok, dest.reshape(-1), slots)
    y = _moe_experts(layer, xs, block_e, n_used, next_e, block_rows, w1, b1, w2, b2)
    yg = _gather_rows(y, dest.T.reshape(-1)).reshape(TOP_K, t * TILE_ROWS, LANES)
    return _combine(hs1, yg, gate, mt, group, ln2_g[layer], ln2_b[layer], mt_next), xs


def _sgu_kernel(g_ref, v_ref, lg_ref, lb_ref, ws_ref, bs_ref, o_ref):
    v = v_ref[...].astype(f32)
    vn = _ln_rows(v, lg_ref[...], lb_ref[...]).astype(bf16)
    gw = E // SGU_GROUPS
    for g in range(SGU_GROUPS):
        mixed = _dot(ws_ref[g].astype(bf16), vn[:, g * gw:(g + 1) * gw]) + bs_ref[:, g:g + 1]
        o_ref[:, g * gw:(g + 1) * gw] = (g_ref[:, g * gw:(g + 1) * gw].astype(f32) * mixed).astype(o_ref.dtype)


def _sgu(uv, ln_g, ln_b, w_s, b_s):
    m = uv.shape[0]
    c = SGU_CHUNK
    vec = pl.BlockSpec((1, E), lambda i: (0, 0))
    return pl.pallas_call(
        _sgu_kernel,
        grid=(m // c,),
        in_specs=[pl.BlockSpec((c, E), lambda i: (i, 0)),
                  pl.BlockSpec((c, E), lambda i: (i, 1)),
                  vec, vec,
                  pl.BlockSpec((SGU_GROUPS, c, c), lambda i: (0, 0, 0)),
                  pl.BlockSpec((c, SGU_GROUPS), lambda i: (0, 0))],
        out_specs=pl.BlockSpec((c, E), lambda i: (i, 0)),
        out_shape=jax.ShapeDtypeStruct((m, E), bf16),
        compiler_params=_cparams(1),
        name="sgu",
    )(uv, uv, ln_g.reshape(1, E), ln_b.reshape(1, E), w_s, b_s.T)


def kernel(x, c, ctx, c_ctx, mod_w, mod_b, ln1_g, ln1_b, ln2_g, ln2_b, a_w_in, a_conv_w, a_conv_b, a_w_q, a_w_k, a_w_v, a_w_gate, a_b_gate, a_norm_w, a_skip, a_w_out, b_w_in, b_ln_g, b_ln_b, b_w_s, b_b_s, b_w_out, r_w, r_b, e_w1, e_b1, e_w2, e_b2):
    cvec = jnp.concatenate([c, c_ctx[None, :], jnp.zeros((8 - BATCH - 1, D), f32)], axis=0)
    mods = _mod_rows(cvec, mod_w, mod_b).reshape(DEPTH, 8, 6, D)
    rows = [r for b in range(BATCH) for r in (BATCH, b)]
    mt = [jnp.stack([mods[l, r] for r in rows], axis=0) for l in range(DEPTH)]

    hs = (ctx.reshape(BATCH * LC, D), x.reshape(BATCH * SEQ, D))
    hx = _modulate(*hs, mt[0])
    xz = _matmul(hx, a_w_in[0], tm=1024, tn=1024, name="mlstm_in_proj")
    xc, q, k, v, gates = _conv_qkv(xz, a_conv_w[0], a_conv_b[0], a_w_q[0], a_w_k[0], a_w_v[0],
                                   a_w_gate[0], a_b_gate[0])
    gates_t = gates.transpose(0, 2, 1)
    h_fwd = _mlstm(q, k, v, gates, gates_t, 0)
    pre = _mlstm(q, k, v, gates, gates_t, 1, (h_fwd, xc, xz, a_norm_w[0], a_skip[0]))
    o = _matmul(pre.reshape(BATCH * U, E), a_w_out[0], tm=1024, tn=512, name="mlstm_out_proj")
    slots = jnp.zeros((MOE_BLOCKS * MOE_TM * TILE_ROWS, LANES), jnp.uint32)
    (hs, hx, _), slots = _moe_layer(0, hs, o, mt[0], _group_joint, ln1_g, ln1_b, ln2_g, ln2_b,
                                    r_w, r_b, e_w1, e_b1, e_w2, e_b2, slots, mt_next=mt[1])

    uv = _matmul(hx, b_w_in[0], tm=1024, tn=1024, act="gelu", name="sgu_in_proj")
    gated = _sgu(uv, b_ln_g[0], b_ln_b[0], b_w_s[0], b_b_s[0])
    o = _matmul(gated, b_w_out[0], tm=1024, tn=512, name="sgu_out_proj")
    (out,), _ = _moe_layer(1, (hs,), o, mt[1], _group_latent, ln1_g, ln1_b, ln2_g, ln2_b,
                           r_w, r_b, e_w1, e_b1, e_w2, e_b2, slots)
    return out.reshape(BATCH, SEQ, D)
```

```python
import functools

import jax
import jax.numpy as jnp
from jax import lax
from jax.experimental import pallas as pl
from jax.experimental.pallas import tpu as pltpu

D = 2048
BATCH = 4
SEQ = 2048
DEPTH = 2
GRID_W = 64
LC = 256
U = LC + SEQ
E = 2 * D
NH = 4
DH = E // NH
QKV_BLOCK = 4
SGU_GROUPS = 8
SGU_CHUNK = 128
NE = 32
TOP_K = 4
F = D
SWIGLU_LIMIT = 7.0
SWIGLU_ALPHA = 1.702
ALPHA = (2 * DEPTH) ** 0.25
LN_EPS = 1e-5

ROW_TILE = 256
MLSTM_CHUNK = 256
MOE_TM = 512
BD_TILE = 256
VMEM_LIMIT = 56 * 1024 * 1024

f32 = jnp.float32
bf16 = jnp.bfloat16


def _cparams(n_axes):
    return pltpu.CompilerParams(dimension_semantics=("arbitrary",) * n_axes,
                                vmem_limit_bytes=VMEM_LIMIT)


def _dot(a, b):
    return jnp.dot(a, b, preferred_element_type=f32)


def _sigmoid(x):
    return 1.0 / (1.0 + jnp.exp(-x))


def _pack_halves(x):
    w = x.shape[1] // 2
    bits = lax.bitcast_convert_type(x.astype(bf16).astype(f32), jnp.uint32)
    return (bits[:, w:] & jnp.uint32(0xFFFF0000)) | (bits[:, :w] >> 16)


def _unpack_halves(p):
    lo = lax.bitcast_convert_type(p << 16, f32)
    hi = lax.bitcast_convert_type(p & jnp.uint32(0xFFFF0000), f32)
    return jnp.concatenate([lo, hi], axis=1)


PACKED = D // 2
LANES = 128
TILE_ROWS = PACKED // LANES


def _store_tile_rows(ref, p):
    r = p.shape[0]
    for s in range(TILE_ROWS):
        ref[pl.ds(s, r, stride=TILE_ROWS), :] = p[:, s * LANES:(s + 1) * LANES]


def _load_tile_rows(ref, r):
    return jnp.concatenate([ref[pl.ds(s, r, stride=TILE_ROWS), :] for s in range(TILE_ROWS)], axis=1)


def _mod_kernel(c_ref, w_ref, b_ref, o_ref):
    c = c_ref[...]
    a = (c * _sigmoid(c)).astype(bf16)
    o_ref[0] = _dot(a, w_ref[0].astype(bf16)) + b_ref[0]


def _mod_rows(cvec, mod_w, mod_b):
    tn = 1024
    return pl.pallas_call(
        _mod_kernel,
        grid=(DEPTH, 6 * D // tn),
        in_specs=[pl.BlockSpec((8, D), lambda l, j: (0, 0)),
                  pl.BlockSpec((1, D, tn), lambda l, j: (l, 0, j)),
                  pl.BlockSpec((1, 1, tn), lambda l, j: (l, 0, j))],
        out_specs=pl.BlockSpec((1, 8, tn), lambda l, j: (l, 0, j)),
        out_shape=jax.ShapeDtypeStruct((DEPTH, 8, 6 * D), f32),
        compiler_params=_cparams(2),
        name="adaln_rows",
    )(cvec, mod_w, mod_b.reshape(DEPTH, 1, 6 * D))


def _group_joint(i):
    tiles = U // ROW_TILE
    return 2 * (i // tiles) + jnp.minimum(i % tiles, 1)


def _group_latent(i):
    return 2 * (i // (SEQ // ROW_TILE)) + 1


JOINT_TILES = U // ROW_TILE


def _is_ctx_tile(i):
    return i % JOINT_TILES == 0


def _ctx_tile(i):
    return i // JOINT_TILES


def _latent_tile(i):
    return (i // JOINT_TILES) * (SEQ // ROW_TILE) + jnp.maximum(i % JOINT_TILES - 1, 0)


_CTX_ROWS = pl.BlockSpec((ROW_TILE, D), lambda i: (_ctx_tile(i), 0))
_LATENT_ROWS = pl.BlockSpec((ROW_TILE, D), lambda i: (_latent_tile(i), 0))


def _joint_rows(ctx_ref, lat_ref):
    is_ctx = _is_ctx_tile(pl.program_id(0))
    return jnp.where(is_ctx, ctx_ref[...], lat_ref[...])


def _modulate_kernel(ctx_ref, lat_ref, m_ref, o_ref):
    m = m_ref[0]
    o_ref[...] = (_joint_rows(ctx_ref, lat_ref) * (1.0 + m[1:2]) + m[0:1]).astype(o_ref.dtype)


def _modulate(ctx2d, x2d, mt):
    m = ctx2d.shape[0] + x2d.shape[0]
    return pl.pallas_call(
        _modulate_kernel,
        grid=(m // ROW_TILE,),
        in_specs=[_CTX_ROWS, _LATENT_ROWS,
                  pl.BlockSpec((1, 6, D), lambda i: (_group_joint(i), 0, 0))],
        out_specs=pl.BlockSpec((ROW_TILE, D), lambda i: (i, 0)),
        out_shape=jax.ShapeDtypeStruct((m, D), bf16),
        compiler_params=_cparams(1),
        name="modulate",
    )(ctx2d, x2d, mt)


def _erf(x):
    return lax.erf(x)


def _mm_kernel(x_ref, w_ref, o_ref, wb_ref, *, act):
    @pl.when(pl.program_id(1) == 0)
    def _():
        wb_ref[...] = w_ref[...].astype(bf16)

    acc = _dot(x_ref[...], wb_ref[...])
    if act == "gelu":
        acc = 0.5 * acc * (1.0 + _erf(acc * (2.0 ** -0.5)))
    o_ref[...] = acc.astype(o_ref.dtype)


def _matmul(x, w, *, tm, tn, act=None, name):
    m, k = x.shape
    n = w.shape[1]
    return pl.pallas_call(
        functools.partial(_mm_kernel, act=act),
        grid=(n // tn, m // tm),
        in_specs=[pl.BlockSpec((tm, k), lambda j, i: (i, 0)),
                  pl.BlockSpec((k, tn), lambda j, i: (0, j))],
        out_specs=pl.BlockSpec((tm, tn), lambda j, i: (i, j)),
        out_shape=jax.ShapeDtypeStruct((m, n), bf16),
        scratch_shapes=[pltpu.VMEM((k, tn), bf16)],
        compiler_params=_cparams(2),
        name=name,
    )(x, w)


def _conv_qkv_kernel(xm_ref, cw_ref, cb_ref, wq_ref, wk_ref, wv_ref, wg_ref, bg_ref,
                     xc_ref, q_ref, k_ref, v_ref, g_ref):
    ct = pl.program_id(1)
    a = xm_ref[0].astype(f32)
    c = a.shape[1]
    r = lax.broadcasted_iota(jnp.int32, (U, c), 0)
    latent = r >= LC
    p = r - LC
    col = jnp.where(latent, p & (GRID_W - 1), r)
    last = jnp.where(latent, GRID_W - 1, LC - 1)
    a_l = jnp.where(col > 0, pltpu.roll(a, 1, 0), 0.0)
    a_r = jnp.where(col < last, pltpu.roll(a, U - 1, 0), 0.0)
    w = cw_ref[...]
    rows = [w[3 * i:3 * i + 1] * a_l + w[3 * i + 1:3 * i + 2] * a + w[3 * i + 2:3 * i + 3] * a_r
            for i in range(3)]
    up = jnp.where(p >= GRID_W, pltpu.roll(rows[0], GRID_W, 0), 0.0)
    down = jnp.where(latent & (p < SEQ - GRID_W), pltpu.roll(rows[2], U - GRID_W, 0), 0.0)
    pre = rows[1] + up + down + cb_ref[...]
    xc = (pre * _sigmoid(pre)).astype(bf16)
    xc_ref[0] = xc
    xm = xm_ref[0]
    q = _dot(xc, _diag_tile(wq_ref)).astype(bf16)
    k = _dot(xc, _diag_tile(wk_ref)).astype(bf16)
    v = _dot(xm, _diag_tile(wv_ref)).astype(bf16)
    q_ref[0] = q
    k_ref[0] = k
    v_ref[0] = v
    g = (_dot(q, wg_ref[0].astype(bf16)) + _dot(k, wg_ref[1].astype(bf16))
         + _dot(v, wg_ref[2].astype(bf16)))

    @pl.when(ct == 0)
    def _():
        g_ref[0] = g + bg_ref[...]

    @pl.when(ct > 0)
    def _():
        g_ref[0] += g


def _diag_tile(ref):
    r = lax.broadcasted_iota(jnp.int32, ref.shape, 0) // QKV_BLOCK
    c = lax.broadcasted_iota(jnp.int32, ref.shape, 1) // QKV_BLOCK
    return jnp.where(r == c, ref[...], 0.0).astype(bf16)


def _block_rows_tiled(w):
    return jnp.tile(w.reshape(E, QKV_BLOCK), (1, BD_TILE // QKV_BLOCK))


def _conv_qkv(xz, conv_w, conv_b, w_q, w_k, w_v, w_gate, b_gate):
    c = BD_TILE
    n_gate = 4 * NH
    act = jax.ShapeDtypeStruct((BATCH, U, E), bf16)
    tile = pl.BlockSpec((1, U, c), lambda b, t: (b, 0, t))
    bd = pl.BlockSpec((c, c), lambda b, t: (t, 0))
    return pl.pallas_call(
        _conv_qkv_kernel,
        grid=(BATCH, E // c),
        in_specs=[tile,
                  pl.BlockSpec((9, c), lambda b, t: (0, t)),
                  pl.BlockSpec((1, c), lambda b, t: (0, t)),
                  bd, bd, bd,
                  pl.BlockSpec((3, c, n_gate), lambda b, t: (0, t, 0)),
                  pl.BlockSpec((1, n_gate), lambda b, t: (0, 0))],
        out_specs=[tile, tile, tile, tile,
                   pl.BlockSpec((1, U, n_gate), lambda b, t: (b, 0, 0))],
        out_shape=[act, act, act, act, jax.ShapeDtypeStruct((BATCH, U, n_gate), f32)],
        compiler_params=_cparams(2),
        name="conv_qkv_gates",
    )(xz.reshape(BATCH, U, 2 * E), conv_w.reshape(9, E), conv_b.reshape(1, E),
      _block_rows_tiled(w_q), _block_rows_tiled(w_k), _block_rows_tiled(w_v),
      w_gate.reshape(3, E, n_gate), b_gate.reshape(1, n_gate))


def _log_sigmoid(x):
    return jnp.minimum(x, 0.0) - jnp.log1p(jnp.exp(-jnp.abs(x)))


def _mlstm_kernel(*refs, direction, finish):
    if finish:
        (q_ref, k_ref, v_ref, gc_ref, gr_ref, hf_ref, xc_ref, z_ref, nw_ref, sk_ref,
         o_ref, ct_ref, ctb_ref, n_ref, m_ref) = refs
    else:
        q_ref, k_ref, v_ref, gc_ref, gr_ref, o_ref, ct_ref, ctb_ref, n_ref, m_ref = refs
    L = MLSTM_CHUNK
    h = pl.program_id(1)
    step = pl.program_id(2)

    @pl.when(step == 0)
    def _():
        ct_ref[...] = jnp.zeros_like(ct_ref)
        ctb_ref[...] = jnp.zeros_like(ctb_ref)
        n_ref[...] = jnp.zeros_like(n_ref)
        m_ref[...] = jnp.full_like(m_ref, -jnp.inf)

    i_idx = 2 * direction * NH + h
    f_idx = (2 * direction + 1) * NH + h
    gc = gc_ref[0]
    gr = gr_ref[0]
    lane = lax.broadcasted_iota(jnp.int32, gc.shape, 1)
    sub = lax.broadcasted_iota(jnp.int32, gr.shape, 0)
    ig_col = jnp.sum(jnp.where(lane == i_idx, gc, 0.0), axis=1, keepdims=True)
    f_col = jnp.sum(jnp.where(lane == f_idx, gc, 0.0), axis=1, keepdims=True)
    ig_row = jnp.sum(jnp.where(sub == i_idx, gr, 0.0), axis=0, keepdims=True)
    f_row = jnp.sum(jnp.where(sub == f_idx, gr, 0.0), axis=0, keepdims=True)
    lf_col = _log_sigmoid(f_col)
    lf_row = _log_sigmoid(f_row)

    t_i = lax.broadcasted_iota(jnp.int32, (L, L), 0)
    s_i = lax.broadcasted_iota(jnp.int32, (L, L), 1)
    seen = (s_i <= t_i) if direction == 0 else (s_i >= t_i)
    b_col = jnp.sum(jnp.where(seen, lf_row, 0.0), axis=1, keepdims=True)
    seen_t = (t_i <= s_i) if direction == 0 else (t_i >= s_i)
    b_row = jnp.sum(jnp.where(seen_t, lf_col, 0.0), axis=0, keepdims=True)
    b_end = jnp.sum(lf_row, axis=1, keepdims=True)

    m_prev = m_ref[:, 0:1]
    log_d = jnp.where(seen, b_col - b_row + ig_row, -jnp.inf)
    g_col = b_col + m_prev
    m_t = jnp.maximum(g_col, jnp.max(log_d, axis=1, keepdims=True))
    dw = jnp.exp(log_d - m_t)
    inter = jnp.exp(g_col - m_t)

    q = q_ref[0]
    k = k_ref[0] * (DH ** -0.5)
    v = v_ref[0]
    s = lax.dot_general(q, k, (((1,), (1,)), ((), ())), preferred_element_type=f32) * dw
    num = _dot(s.astype(bf16), v) + inter * _dot(q, ctb_ref[...])
    qn = jnp.sum(q.astype(f32) * n_ref[...], axis=1, keepdims=True)
    den = jnp.sum(s, axis=1, keepdims=True) + inter * qn
    hout = num / jnp.maximum(jnp.abs(den), jnp.exp(-m_t))

    w_end = b_end - b_col + ig_col
    m_new = jnp.maximum(b_end + m_prev, jnp.max(w_end, axis=0, keepdims=True))
    decay = jnp.exp(b_end + m_prev - m_new)
    kw = k.astype(f32) * jnp.exp(w_end - m_new)
    upd = lax.dot_general(kw.astype(bf16), v, (((0,), (0,)), ((), ())), preferred_element_type=f32)
    c_new = decay * ct_ref[...] + upd
    ct_ref[...] = c_new
    ctb_ref[...] = c_new.astype(bf16)
    n_ref[...] = decay * n_ref[...] + jnp.sum(kw, axis=0, keepdims=True)
    m_ref[...] = jnp.broadcast_to(m_new, m_ref.shape)

    if not finish:
        o_ref[0] = hout.astype(o_ref.dtype)
    else:
        hs = hout + hf_ref[0].astype(f32)
        mu = jnp.mean(hs, axis=1, keepdims=True)
        xc_ = hs - mu
        var = jnp.mean(xc_ * xc_, axis=1, keepdims=True)
        hn = xc_ * lax.rsqrt(var + LN_EPS)
        z = z_ref[0].astype(f32)
        o_ref[0] = ((hn * nw_ref[...] + sk_ref[...] * xc_ref[0].astype(f32))
                    * (z * _sigmoid(z))).astype(o_ref.dtype)


def _mlstm(q, k, v, gates, gates_t, direction, finish_args=None):
    L = MLSTM_CHUNK
    n_chunks = U // L

    if direction == 0:
        def chunk(s):
            return s
    else:
        def chunk(s):
            return jnp.where(s == 0, 0, n_chunks - s)

    tile = pl.BlockSpec((1, L, DH), lambda b, h, s: (b, chunk(s), h))
    in_specs = [tile, tile, tile,
                pl.BlockSpec((1, L, 4 * NH), lambda b, h, s: (b, chunk(s), 0)),
                pl.BlockSpec((1, 4 * NH, L), lambda b, h, s: (b, 0, chunk(s)))]
    args = [q, k, v, gates, gates_t]
    finish = finish_args is not None
    if finish:
        h_fwd, xc, xz, norm_w, skip = finish_args
        vec = pl.BlockSpec((1, DH), lambda b, h, s: (0, h))
        in_specs += [tile, tile,
                     pl.BlockSpec((1, L, DH), lambda b, h, s: (b, chunk(s), NH + h)),
                     vec, vec]
        args += [h_fwd, xc, xz.reshape(BATCH, U, 2 * E), norm_w.reshape(1, E), skip.reshape(1, E)]
    return pl.pallas_call(
        functools.partial(_mlstm_kernel, direction=direction, finish=finish),
        grid=(BATCH, NH, n_chunks),
        in_specs=in_specs,
        out_specs=tile,
        out_shape=jax.ShapeDtypeStruct((BATCH, U, E), bf16),
        scratch_shapes=[pltpu.VMEM((DH, DH), f32), pltpu.VMEM((DH, DH), bf16),
                        pltpu.VMEM((1, DH), f32), pltpu.VMEM((1, 128), f32)],
        compiler_params=_cparams(3),
        name="mlstm_bwd_finish" if finish else "mlstm_fwd",
    )(*args)


def _ln_rows(r, g, b):
    mu = jnp.mean(r, axis=1, keepdims=True)
    rc = r - mu
    var = jnp.mean(rc * rc, axis=1, keepdims=True)
    return rc * lax.rsqrt(var + LN_EPS) * g + b


def _ln_router_kernel(*refs, joint):
    hs_refs, refs = refs[:1 + joint], refs[1 + joint:]
    (o_ref, m_ref, g_ref, b_ref, rw_ref, rb_ref,
     hs1_ref, tok_ref, gate_ref, eid_ref, rank_ref, cnt_ref, carry_ref) = refs

    @pl.when(pl.program_id(0) == 0)
    def _():
        carry_ref[...] = jnp.zeros_like(carry_ref)

    m = m_ref[0]
    hs = _joint_rows(*hs_refs) if joint else hs_refs[0][...]
    r = ALPHA * hs + m[2:3] * o_ref[...].astype(f32)
    hs1 = _ln_rows(r, g_ref[...], b_ref[...])
    hs1_ref[...] = hs1
    tok = hs1 * (1.0 + m[4:5]) + m[3:4]
    t_hi = tok.astype(bf16)
    _store_tile_rows(tok_ref, _pack_halves(tok))
    t_lo = (tok - t_hi.astype(f32)).astype(bf16)
    w = rw_ref[...]
    w_hi = w.astype(bf16)
    w_lo = (w - w_hi.astype(f32)).astype(bf16)
    logits = _dot(t_hi, w_hi) + _dot(t_hi, w_lo) + _dot(t_lo, w_hi) + rb_ref[...]

    lane = lax.broadcasted_iota(jnp.int32, logits.shape, 1).astype(f32)
    left = logits
    vals, ids = [], []
    for _ in range(TOP_K):
        v = jnp.max(left, axis=1, keepdims=True)
        e = jnp.min(jnp.where(left == v, lane, float(NE)), axis=1, keepdims=True)
        vals.append(v)
        ids.append(e)
        left = jnp.where(lane == e, -jnp.inf, left)
    exps = [jnp.exp(v - vals[0]) for v in vals]
    den = exps[0]
    for x in exps[1:]:
        den = den + x
    gate_ref[...] = jnp.concatenate([x / den for x in exps], axis=1)
    eid_ref[...] = jnp.concatenate(ids, axis=1).astype(jnp.int32)

    member = (lane == ids[0]).astype(f32)
    for e in ids[1:]:
        member = member + (lane == e).astype(f32)
    rows = logits.shape[0]
    earlier = (lax.broadcasted_iota(jnp.int32, (rows, rows), 1)
               < lax.broadcasted_iota(jnp.int32, (rows, rows), 0)).astype(bf16)
    before = _dot(earlier, member.astype(bf16)) + carry_ref[...]
    rank_ref[...] = jnp.concatenate(
        [jnp.sum(jnp.where(lane == e, before, 0.0), axis=1, keepdims=True) for e in ids],
        axis=1).astype(jnp.int32)
    total = carry_ref[...] + jnp.sum(member, axis=0, keepdims=True)
    carry_ref[...] = total
    cnt_ref[...] = total


def _ln_router(hs_parts, o, mt, group, ln_g, ln_b, r_w, r_b):
    m = o.shape[0]
    joint = len(hs_parts) == 2
    row = pl.BlockSpec((ROW_TILE, D), lambda i: (i, 0))
    vec = pl.BlockSpec((1, D), lambda i: (0, 0))
    topk = pl.BlockSpec((ROW_TILE, TOP_K), lambda i: (i, 0))
    return pl.pallas_call(
        functools.partial(_ln_router_kernel, joint=joint),
        grid=(m // ROW_TILE,),
        in_specs=([_CTX_ROWS, _LATENT_ROWS] if joint else [row]) + [
                  row,
                  pl.BlockSpec((1, 6, D), lambda i: (group(i), 0, 0)),
                  vec, vec,
                  pl.BlockSpec((D, NE), lambda i: (0, 0)),
                  pl.BlockSpec((1, NE), lambda i: (0, 0))],
        out_specs=[row, pl.BlockSpec((ROW_TILE * TILE_ROWS, LANES), lambda i: (i, 0)),
                   topk, topk, topk, pl.BlockSpec((1, NE), lambda i: (0, 0))],
        out_shape=[jax.ShapeDtypeStruct((m, D), f32),
                   jax.ShapeDtypeStruct((m * TILE_ROWS, LANES), jnp.uint32),
                   jax.ShapeDtypeStruct((m, TOP_K), f32),
                   jax.ShapeDtypeStruct((m, TOP_K), jnp.int32),
                   jax.ShapeDtypeStruct((m, TOP_K), jnp.int32),
                   jax.ShapeDtypeStruct((1, NE), f32)],
        scratch_shapes=[pltpu.VMEM((1, NE), f32)],
        compiler_params=_cparams(1),
        name="ln_router",
    )(*hs_parts, o, mt, ln_g.reshape(1, D), ln_b.reshape(1, D), r_w, r_b.reshape(1, NE))


def _first_of_expert(be_ref, i):
    return jnp.logical_or(i == 0, be_ref[i] != be_ref[jnp.maximum(i - 1, 0)])


ROW_STEPS = 8


def _for_real_rows(valid, rows, compute):
    q = MOE_TM // ROW_STEPS
    for step in range(1, ROW_STEPS + 1):
        covers = rows <= step * q
        if step > 1:
            covers = jnp.logical_and(covers, rows > (step - 1) * q)
        pl.when(jnp.logical_and(valid, covers))(functools.partial(compute, step * q))


def _moe_up_kernel(be_ref, nu_ref, nxt_ref, rows_ref, x_ref, w_hbm, bg_ref, bl_ref, o_ref, stage_ref,
                   cache_ref, sem, *, layer, tf):
    j = pl.program_id(0)
    i = pl.program_id(1)
    valid = i < nu_ref[0]

    def tile_copy(e, jj, half):
        col = pl.multiple_of(half * F + jj * tf, tf)
        return pltpu.make_async_copy(w_hbm.at[layer, e, :, pl.ds(col, tf)], stage_ref.at[half], sem.at[half])

    def start(e, jj):
        tile_copy(e, jj, 0).start()
        tile_copy(e, jj, 1).start()

    @pl.when(jnp.logical_and(j == 0, i == 0))
    def _():
        start(be_ref[0], 0)

    @pl.when(jnp.logical_and(valid, _first_of_expert(be_ref, i)))
    def _():
        tile_copy(0, 0, 0).wait()
        tile_copy(0, 0, 1).wait()
        cache_ref[...] = stage_ref[...].astype(bf16)
        nxt = nxt_ref[i]

        @pl.when(nxt >= 0)
        def _():
            start(nxt, j)

        @pl.when(jnp.logical_and(nxt < 0, j + 1 < pl.num_programs(0)))
        def _():
            start(be_ref[0], j + 1)

    def compute(rows):
        x = _unpack_halves(_load_tile_rows(x_ref, rows)).astype(bf16)
        glu = jnp.minimum(_dot(x, cache_ref[0]) + bg_ref[0, 0], SWIGLU_LIMIT)
        lin = jnp.clip(_dot(x, cache_ref[1]) + bl_ref[0, 0], -SWIGLU_LIMIT, SWIGLU_LIMIT)
        o_ref[pl.ds(0, rows), :] = (glu * _sigmoid(SWIGLU_ALPHA * glu) * (lin + 1.0)).astype(o_ref.dtype)
        if rows < MOE_TM:
            o_ref[pl.ds(rows, MOE_TM - rows), :] = jnp.zeros((MOE_TM - rows, tf), o_ref.dtype)

    _for_real_rows(valid, rows_ref[i], compute)

    @pl.when(jnp.logical_not(valid))
    def _():
        o_ref[...] = jnp.zeros_like(o_ref)


def _moe_down_kernel(be_ref, nu_ref, nxt_ref, rows_ref, a_ref, w_hbm, b_ref, o_ref, stage_ref, cache_ref, sem,
                     *, layer):
    i = pl.program_id(0)
    valid = i < nu_ref[0]

    def expert_copy(e):
        return pltpu.make_async_copy(w_hbm.at[layer, e], stage_ref, sem)

    @pl.when(i == 0)
    def _():
        expert_copy(be_ref[0]).start()

    @pl.when(jnp.logical_and(valid, _first_of_expert(be_ref, i)))
    def _():
        expert_copy(0).wait()
        cache_ref[...] = stage_ref[...].astype(bf16)
        nxt = nxt_ref[i]

        @pl.when(nxt >= 0)
        def _():
            expert_copy(nxt).start()

    def compute(rows):
        y = _dot(a_ref[pl.ds(0, rows), :], cache_ref[...]) + b_ref[0, 0]
        _store_tile_rows(o_ref, _pack_halves(y))
        if rows < MOE_TM:
            o_ref[pl.ds(rows * TILE_ROWS, (MOE_TM - rows) * TILE_ROWS), :] = jnp.zeros(
                ((MOE_TM - rows) * TILE_ROWS, LANES), o_ref.dtype)

    _for_real_rows(valid, rows_ref[i], compute)

    @pl.when(jnp.logical_not(valid))
    def _():
        o_ref[...] = jnp.zeros_like(o_ref)


def _moe_experts(layer, xs, block_e, n_used, next_e, block_rows, w1, b1, w2, b2):
    ns = xs.shape[0] // TILE_ROWS
    nb = ns // MOE_TM
    tf = 1024
    lin0 = F // tf
    b1 = b1.reshape(DEPTH, NE, 1, 2 * F)
    b2 = b2.reshape(DEPTH, NE, 1, D)

    def bspec(col0):
        return pl.BlockSpec((1, 1, 1, tf), lambda j, i, be, nu, nx, br: (layer, be[i], 0, col0 + j))

    act = pl.pallas_call(
        functools.partial(_moe_up_kernel, layer=layer, tf=tf),
        grid_spec=pltpu.PrefetchScalarGridSpec(
            num_scalar_prefetch=4,
            grid=(F // tf, nb),
            in_specs=[pl.BlockSpec((MOE_TM * TILE_ROWS, LANES),
                                   lambda j, i, be, nu, nx, br: (jnp.minimum(i, nu[0] - 1), 0)),
                      pl.BlockSpec(memory_space=pl.ANY), bspec(0), bspec(lin0)],
            out_specs=pl.BlockSpec((MOE_TM, tf), lambda j, i, be, nu, nx, br: (i, j)),
            scratch_shapes=[pltpu.VMEM((2, D, tf), f32), pltpu.VMEM((2, D, tf), bf16),
                            pltpu.SemaphoreType.DMA((2,))]),
        out_shape=jax.ShapeDtypeStruct((ns, F), bf16),
        compiler_params=_cparams(2),
        name="moe_up",
    )(block_e, n_used, next_e, block_rows, xs, w1, b1, b1)

    return pl.pallas_call(
        functools.partial(_moe_down_kernel, layer=layer),
        grid_spec=pltpu.PrefetchScalarGridSpec(
            num_scalar_prefetch=4,
            grid=(nb,),
            in_specs=[pl.BlockSpec((MOE_TM, F), lambda i, be, nu, nx, br: (jnp.minimum(i, nu[0] - 1), 0)),
                      pl.BlockSpec(memory_space=pl.ANY),
                      pl.BlockSpec((1, 1, 1, D), lambda i, be, nu, nx, br: (layer, be[i], 0, 0))],
            out_specs=pl.BlockSpec((MOE_TM * TILE_ROWS, LANES), lambda i, be, nu, nx, br: (i, 0)),
            scratch_shapes=[pltpu.VMEM((F, D), f32), pltpu.VMEM((F, D), bf16), pltpu.SemaphoreType.DMA]),
        out_shape=jax.ShapeDtypeStruct((ns * TILE_ROWS, LANES), jnp.uint32),
        compiler_params=_cparams(1),
        name="moe_down",
    )(block_e, n_used, next_e, block_rows, act, w2, b2)


GATHER_ROWS = 4096


def _gather_kernel(idx_ref, src_ref, out_ref, sem):
    n = GATHER_ROWS
    t = TILE_ROWS

    def issue(r, carry):
        src_row = pl.multiple_of(idx_ref[0, 0, r] * t, t)
        dst_row = pl.multiple_of(r * t, t)
        pltpu.make_async_copy(src_ref.at[pl.ds(src_row, t)], out_ref.at[pl.ds(dst_row, t)], sem).start()
        return carry

    lax.fori_loop(0, n, issue, 0, unroll=16)
    pltpu.make_async_copy(src_ref.at[pl.ds(0, n * t)], out_ref, sem).wait()


def _gather_rows(src, idx):
    n = idx.shape[0]
    nblk = n // GATHER_ROWS
    return pl.pallas_call(
        _gather_kernel,
        grid=(nblk,),
        in_specs=[pl.BlockSpec((1, 1, GATHER_ROWS), lambda i: (i, 0, 0), memory_space=pltpu.SMEM),
                  pl.BlockSpec(memory_space=pl.ANY)],
        out_specs=pl.BlockSpec((GATHER_ROWS * TILE_ROWS, LANES), lambda i: (i, 0)),
        out_shape=jax.ShapeDtypeStruct((n * TILE_ROWS, LANES), src.dtype),
        scratch_shapes=[pltpu.SemaphoreType.DMA],
        compiler_params=pltpu.CompilerParams(dimension_semantics=("arbitrary",),
                                             vmem_limit_bytes=VMEM_LIMIT,
                                             disable_bounds_checks=True),
        name="gather_rows",
    )(idx.reshape(nblk, 1, GATHER_ROWS), src)


SCATTER_TOKENS = 1024


def _scatter_kernel(dest_ref, src_ref, init_ref, out_ref, sem):
    del init_ref
    n = SCATTER_TOKENS
    t = TILE_ROWS

    def issue(a, carry):
        src_row = pl.multiple_of(lax.shift_right_logical(a, TOP_K.bit_length() - 1) * t, t)
        dst_row = pl.multiple_of(dest_ref[0, 0, a] * t, t)
        pltpu.make_async_copy(src_ref.at[pl.ds(src_row, t)], out_ref.at[pl.ds(dst_row, t)], sem).start()
        return carry

    lax.fori_loop(0, n * TOP_K, issue, 0, unroll=16)
    for _ in range(TOP_K):
        pltpu.make_async_copy(src_ref, out_ref.at[pl.ds(0, n * t)], sem).wait()


def _scatter_rows(src, dest, init):
    n = src.shape[0] // TILE_ROWS
    nblk = n // SCATTER_TOKENS
    return pl.pallas_call(
        _scatter_kernel,
        grid=(nblk,),
        in_specs=[pl.BlockSpec((1, 1, SCATTER_TOKENS * TOP_K), lambda i: (i, 0, 0), memory_space=pltpu.SMEM),
                  pl.BlockSpec((SCATTER_TOKENS * TILE_ROWS, LANES), lambda i: (i, 0)),
                  pl.BlockSpec(memory_space=pl.ANY)],
        out_specs=pl.BlockSpec(memory_space=pl.ANY),
        out_shape=jax.ShapeDtypeStruct(init.shape, src.dtype),
        input_output_aliases={2: 0},
        scratch_shapes=[pltpu.SemaphoreType.DMA],
        compiler_params=pltpu.CompilerParams(dimension_semantics=("arbitrary",),
                                             vmem_limit_bytes=VMEM_LIMIT,
                                             disable_bounds_checks=True),
        name="scatter_rows",
    )(dest.reshape(nblk, 1, SCATTER_TOKENS * TOP_K), src, init)


MOE_BLOCKS = -(-(BATCH * U * TOP_K) // MOE_TM) + NE


def _slot_layout(eid, rank, counts):
    nb = MOE_BLOCKS
    counts = counts.reshape(NE).astype(jnp.int32)
    experts = jnp.arange(NE, dtype=jnp.int32)
    pcounts = (counts + MOE_TM - 1) // MOE_TM * MOE_TM
    pends = jnp.cumsum(pcounts)
    pstarts = pends - pcounts
    dest = rank + jnp.sum(jnp.where(eid[:, :, None] == experts, pstarts, 0), axis=-1)
    n_used = pends[-1] // MOE_TM
    blk = jnp.arange(nb, dtype=jnp.int32)
    block_e = jnp.minimum(jnp.sum(pends[None, :] <= (blk * MOE_TM)[:, None], axis=1), NE - 1)
    of_block = block_e[:, None] == experts

    def per_block(table):
        return jnp.sum(jnp.where(of_block, table, 0), axis=1)

    block_rows = jnp.clip(per_block(counts) - (blk * MOE_TM - per_block(pstarts)), 0, MOE_TM)
    block_rows = jnp.where(blk < n_used, block_rows, 0)
    group_end = per_block(pends) // MOE_TM
    next_e = jnp.sum(jnp.where(group_end[:, None] == blk, block_e, 0), axis=1)
    next_e = jnp.where(group_end < n_used, next_e, -1)
    last_e = jnp.sum(jnp.where(blk == n_used - 1, block_e, 0))
    block_e = jnp.where(blk < n_used, block_e, last_e)
    i32 = jnp.int32
    return (dest.astype(i32), block_e.astype(i32), n_used.astype(i32).reshape(1), next_e.astype(i32),
            block_rows.astype(i32))


def _combine_kernel(hs_ref, y_ref, gate_ref, m_ref, g_ref, b_ref, *rest, joint):
    if joint:
        mn_ref, hs2_ref, nxt_ref, ctx_ref = rest
    else:
        (hs2_ref,) = rest
    m = m_ref[0]
    gate = gate_ref[...]
    y = gate[:, 0:1] * _unpack_halves(_load_tile_rows(y_ref.at[0], ROW_TILE))
    for k in range(1, TOP_K):
        y = y + gate[:, k:k + 1] * _unpack_halves(_load_tile_rows(y_ref.at[k], ROW_TILE))
    hs2 = _ln_rows(ALPHA * hs_ref[...] + m[5:6] * y, g_ref[...], b_ref[...])
    if not joint:
        hs2_ref[...] = hs2
        return
    is_ctx = _is_ctx_tile(pl.program_id(0))

    @pl.when(is_ctx)
    def _():
        ctx_ref[...] = hs2

    @pl.when(jnp.logical_not(is_ctx))
    def _():
        mn = mn_ref[0]
        hs2_ref[...] = hs2
        nxt_ref[...] = (hs2 * (1.0 + mn[1:2]) + mn[0:1]).astype(nxt_ref.dtype)


def _combine(hs1, yg, gate, mt, group, ln_g, ln_b, mt_next=None):
    m = hs1.shape[0]
    row = pl.BlockSpec((ROW_TILE, D), lambda i: (i, 0))
    vec = pl.BlockSpec((1, D), lambda i: (0, 0))
    mod = pl.BlockSpec((1, 6, D), lambda i: (group(i), 0, 0))
    in_specs = [row, pl.BlockSpec((TOP_K, ROW_TILE * TILE_ROWS, LANES), lambda i: (0, i, 0)),
                pl.BlockSpec((ROW_TILE, TOP_K), lambda i: (i, 0)), mod, vec, vec]
    args = [hs1, yg, gate, mt, ln_g.reshape(1, D), ln_b.reshape(1, D)]
    joint = mt_next is not None
    if joint:
        in_specs.append(mod)
        args.append(mt_next)
        out_specs = [_LATENT_ROWS, _LATENT_ROWS, _CTX_ROWS]
        out_shape = [jax.ShapeDtypeStruct((BATCH * SEQ, D), f32), jax.ShapeDtypeStruct((BATCH * SEQ, D), bf16),
                     jax.ShapeDtypeStruct((BATCH * LC, D), f32)]
    else:
        out_specs = [row]
        out_shape = [jax.ShapeDtypeStruct((m, D), f32)]
    return pl.pallas_call(
        functools.partial(_combine_kernel, joint=joint),
        grid=(m // ROW_TILE,),
        in_specs=in_specs, out_specs=out_specs, out_shape=out_shape,
        compiler_params=_cparams(1),
        name="moe_combine_ln",
    )(*args)


def _moe_layer(layer, hs_parts, o, mt, group, ln1_g, ln1_b, ln2_g, ln2_b, r_w, r_b, w1, b1, w2, b2,
               slots, mt_next=None):
    t = o.shape[0]
    hs1, tok, gate, eid, rank, counts = _ln_router(hs_parts, o, mt, group, ln1_g[layer], ln1_b[layer],
                                                   r_w[layer], r_b[layer])
    dest, block_e, n_used, next_e, block_rows = _slot_layout(eid, rank, counts)
    xs = _scatter_rows(tok, dest.reshape(-1), slots)
    y = _moe_experts(layer, xs, block_e, n_used, next_e, block_rows, w1, b1, w2, b2)
    yg = _gather_rows(y, dest.T.reshape(-1)).reshape(TOP_K, t * TILE_ROWS, LANES)
    return _combine(hs1, yg, gate, mt, group, ln2_g[layer], ln2_b[layer], mt_next), xs


def _sgu_kernel(g_ref, v_ref, lg_ref, lb_ref, ws_ref, bs_ref, o_ref):
    v = v_ref[...].astype(f32)
    vn = _ln_rows(v, lg_ref[...], lb_ref[...]).astype(bf16)
    gw = E // SGU_GROUPS
    for g in range(SGU_GROUPS):
        mixed = _dot(ws_ref[g].astype(bf16), vn[:, g * gw:(g + 1) * gw]) + bs_ref[:, g:g + 1]
        o_ref[:, g * gw:(g + 1) * gw] = (g_ref[:, g * gw:(g + 1) * gw].astype(f32) * mixed).astype(o_ref.dtype)


def _sgu(uv, ln_g, ln_b, w_s, b_s):
    m = uv.shape[0]
    c = SGU_CHUNK
    vec = pl.BlockSpec((1, E), lambda i: (0, 0))
    return pl.pallas_call(
        _sgu_kernel,
        grid=(m // c,),
        in_specs=[pl.BlockSpec((c, E), lambda i: (i, 0)),
                  pl.BlockSpec((c, E), lambda i: (i, 1)),
                  vec, vec,
                  pl.BlockSpec((SGU_GROUPS, c, c), lambda i: (0, 0, 0)),
                  pl.BlockSpec((c, SGU_GROUPS), lambda i: (0, 0))],
        out_specs=pl.BlockSpec((c, E), lambda i: (i, 0)),
        out_shape=jax.ShapeDtypeStruct((m, E), bf16),
        compiler_params=_cparams(1),
        name="sgu",
    )(uv, uv, ln_g.reshape(1, E), ln_b.reshape(1, E), w_s, b_s.T)


def kernel(x, c, ctx, c_ctx, mod_w, mod_b, ln1_g, ln1_b, ln2_g, ln2_b, a_w_in, a_conv_w, a_conv_b, a_w_q, a_w_k, a_w_v, a_w_gate, a_b_gate, a_norm_w, a_skip, a_w_out, b_w_in, b_ln_g, b_ln_b, b_w_s, b_b_s, b_w_out, r_w, r_b, e_w1, e_b1, e_w2, e_b2):
    cvec = jnp.concatenate([c, c_ctx[None, :], jnp.zeros((8 - BATCH - 1, D), f32)], axis=0)
    mods = _mod_rows(cvec, mod_w, mod_b).reshape(DEPTH, 8, 6, D)
    rows = [r for b in range(BATCH) for r in (BATCH, b)]
    mt = [jnp.stack([mods[l, r] for r in rows], axis=0) for l in range(DEPTH)]

    hs = (ctx.reshape(BATCH * LC, D), x.reshape(BATCH * SEQ, D))
    hx = _modulate(*hs, mt[0])
    xz = _matmul(hx, a_w_in[0], tm=1536, tn=1024, name="mlstm_in_proj")
    xc, q, k, v, gates = _conv_qkv(xz, a_conv_w[0], a_conv_b[0], a_w_q[0], a_w_k[0], a_w_v[0],
                                   a_w_gate[0], a_b_gate[0])
    gates_t = gates.transpose(0, 2, 1)
    h_fwd = _mlstm(q, k, v, gates, gates_t, 0)
    pre = _mlstm(q, k, v, gates, gates_t, 1, (h_fwd, xc, xz, a_norm_w[0], a_skip[0]))
    o = _matmul(pre.reshape(BATCH * U, E), a_w_out[0], tm=1024, tn=512, name="mlstm_out_proj")
    slots = jnp.zeros((MOE_BLOCKS * MOE_TM * TILE_ROWS, LANES), jnp.uint32)
    (hs, hx, _), slots = _moe_layer(0, hs, o, mt[0], _group_joint, ln1_g, ln1_b, ln2_g, ln2_b,
                                    r_w, r_b, e_w1, e_b1, e_w2, e_b2, slots, mt_next=mt[1])

    uv = _matmul(hx, b_w_in[0], tm=1024, tn=1024, act="gelu", name="sgu_in_proj")
    gated = _sgu(uv, b_ln_g[0], b_ln_b[0], b_w_s[0], b_b_s[0])
    o = _matmul(gated, b_w_out[0], tm=1024, tn=512, name="sgu_out_proj")
    (out,), _ = _moe_layer(1, (hs,), o, mt[1], _group_latent, ln1_g, ln1_b, ln2_g, ln2_b,
                           r_w, r_b, e_w1, e_b1, e_w2, e_b2, slots)
    return out.reshape(BATCH, SEQ, D)
```

```python
import functools

import jax
import jax.numpy as jnp
from jax import lax
from jax.experimental import pallas as pl
from jax.experimental.pallas import tpu as pltpu

D = 2048
BATCH = 4
SEQ = 2048
DEPTH = 2
GRID_W = 64
LC = 256
U = LC + SEQ
E = 2 * D
NH = 4
DH = E // NH
QKV_BLOCK = 4
SGU_GROUPS = 8
SGU_CHUNK = 128
NE = 32
TOP_K = 4
F = D
SWIGLU_LIMIT = 7.0
SWIGLU_ALPHA = 1.702
ALPHA = (2 * DEPTH) ** 0.25
LN_EPS = 1e-5

ROW_TILE = 256
MLSTM_CHUNK = 256
MOE_TM = 512
BD_TILE = 256
VMEM_LIMIT = 56 * 1024 * 1024

f32 = jnp.float32
bf16 = jnp.bfloat16


def _cparams(n_axes):
    return pltpu.CompilerParams(dimension_semantics=("arbitrary",) * n_axes,
                                vmem_limit_bytes=VMEM_LIMIT)


def _dot(a, b):
    return jnp.dot(a, b, preferred_element_type=f32)


def _sigmoid(x):
    return 1.0 / (1.0 + jnp.exp(-x))


def _pack_halves(x):
    w = x.shape[1] // 2
    bits = lax.bitcast_convert_type(x.astype(bf16).astype(f32), jnp.uint32)
    return (bits[:, w:] & jnp.uint32(0xFFFF0000)) | (bits[:, :w] >> 16)


def _unpack_halves(p):
    lo = lax.bitcast_convert_type(p << 16, f32)
    hi = lax.bitcast_convert_type(p & jnp.uint32(0xFFFF0000), f32)
    return jnp.concatenate([lo, hi], axis=1)


PACKED = D // 2
LANES = 128
TILE_ROWS = PACKED // LANES


def _store_tile_rows(ref, p):
    r = p.shape[0]
    for s in range(TILE_ROWS):
        ref[pl.ds(s, r, stride=TILE_ROWS), :] = p[:, s * LANES:(s + 1) * LANES]


def _load_tile_rows(ref, r):
    return jnp.concatenate([ref[pl.ds(s, r, stride=TILE_ROWS), :] for s in range(TILE_ROWS)], axis=1)


def _mod_kernel(c_ref, w_ref, b_ref, o_ref):
    c = c_ref[...]
    a = (c * _sigmoid(c)).astype(bf16)
    o_ref[0] = _dot(a, w_ref[0].astype(bf16)) + b_ref[0]


def _mod_rows(cvec, mod_w, mod_b):
    tn = 1024
    return pl.pallas_call(
        _mod_kernel,
        grid=(DEPTH, 6 * D // tn),
        in_specs=[pl.BlockSpec((8, D), lambda l, j: (0, 0)),
                  pl.BlockSpec((1, D, tn), lambda l, j: (l, 0, j)),
                  pl.BlockSpec((1, 1, tn), lambda l, j: (l, 0, j))],
        out_specs=pl.BlockSpec((1, 8, tn), lambda l, j: (l, 0, j)),
        out_shape=jax.ShapeDtypeStruct((DEPTH, 8, 6 * D), f32),
        compiler_params=_cparams(2),
        name="adaln_rows",
    )(cvec, mod_w, mod_b.reshape(DEPTH, 1, 6 * D))


def _group_joint(i):
    tiles = U // ROW_TILE
    return 2 * (i // tiles) + jnp.minimum(i % tiles, 1)


def _group_latent(i):
    return 2 * (i // (SEQ // ROW_TILE)) + 1


JOINT_TILES = U // ROW_TILE


def _is_ctx_tile(i):
    return i % JOINT_TILES == 0


def _ctx_tile(i):
    return i // JOINT_TILES


def _latent_tile(i):
    return (i // JOINT_TILES) * (SEQ // ROW_TILE) + jnp.maximum(i % JOINT_TILES - 1, 0)


_CTX_ROWS = pl.BlockSpec((ROW_TILE, D), lambda i: (_ctx_tile(i), 0))
_LATENT_ROWS = pl.BlockSpec((ROW_TILE, D), lambda i: (_latent_tile(i), 0))


def _joint_rows(ctx_ref, lat_ref):
    is_ctx = _is_ctx_tile(pl.program_id(0))
    return jnp.where(is_ctx, ctx_ref[...], lat_ref[...])


def _modulate_kernel(ctx_ref, lat_ref, m_ref, o_ref):
    m = m_ref[0]
    o_ref[...] = (_joint_rows(ctx_ref, lat_ref) * (1.0 + m[1:2]) + m[0:1]).astype(o_ref.dtype)


def _modulate(ctx2d, x2d, mt):
    m = ctx2d.shape[0] + x2d.shape[0]
    return pl.pallas_call(
        _modulate_kernel,
        grid=(m // ROW_TILE,),
        in_specs=[_CTX_ROWS, _LATENT_ROWS,
                  pl.BlockSpec((1, 6, D), lambda i: (_group_joint(i), 0, 0))],
        out_specs=pl.BlockSpec((ROW_TILE, D), lambda i: (i, 0)),
        out_shape=jax.ShapeDtypeStruct((m, D), bf16),
        compiler_params=_cparams(1),
        name="modulate",
    )(ctx2d, x2d, mt)


def _erf(x):
    return lax.erf(x)


def _mm_kernel(x_ref, w_ref, o_ref, wb_ref, *, act):
    def compute():
        acc = _dot(x_ref[...], wb_ref[...])
        if act == "gelu":
            acc = 0.5 * acc * (1.0 + _erf(acc * (2.0 ** -0.5)))
        o_ref[...] = acc.astype(o_ref.dtype)

    @pl.when(pl.program_id(1) == 0)
    def _():
        wb_ref[...] = w_ref[...].astype(bf16)
        compute()

    @pl.when(pl.program_id(1) > 0)
    def _():
        compute()


def _matmul(x, w, *, tm, tn, act=None, name):
    m, k = x.shape
    n = w.shape[1]
    return pl.pallas_call(
        functools.partial(_mm_kernel, act=act),
        grid=(n // tn, m // tm),
        in_specs=[pl.BlockSpec((tm, k), lambda j, i: (i, 0)),
                  pl.BlockSpec((k, tn), lambda j, i: (0, j))],
        out_specs=pl.BlockSpec((tm, tn), lambda j, i: (i, j)),
        out_shape=jax.ShapeDtypeStruct((m, n), bf16),
        scratch_shapes=[pltpu.VMEM((k, tn), bf16)],
        compiler_params=_cparams(2),
        name=name,
    )(x, w)


def _conv_qkv_kernel(xm_ref, cw_ref, cb_ref, wq_ref, wk_ref, wv_ref, wg_ref, bg_ref,
                     xc_ref, q_ref, k_ref, v_ref, g_ref):
    ct = pl.program_id(1)
    a = xm_ref[0].astype(f32)
    c = a.shape[1]
    r = lax.broadcasted_iota(jnp.int32, (U, c), 0)
    latent = r >= LC
    p = r - LC
    col = jnp.where(latent, p & (GRID_W - 1), r)
    last = jnp.where(latent, GRID_W - 1, LC - 1)
    a_l = jnp.where(col > 0, pltpu.roll(a, 1, 0), 0.0)
    a_r = jnp.where(col < last, pltpu.roll(a, U - 1, 0), 0.0)
    w = cw_ref[...]
    rows = [w[3 * i:3 * i + 1] * a_l + w[3 * i + 1:3 * i + 2] * a + w[3 * i + 2:3 * i + 3] * a_r
            for i in range(3)]
    up = jnp.where(p >= GRID_W, pltpu.roll(rows[0], GRID_W, 0), 0.0)
    down = jnp.where(latent & (p < SEQ - GRID_W), pltpu.roll(rows[2], U - GRID_W, 0), 0.0)
    pre = rows[1] + up + down + cb_ref[...]
    xc = (pre * _sigmoid(pre)).astype(bf16)
    xc_ref[0] = xc
    xm = xm_ref[0]
    q = _dot(xc, _diag_tile(wq_ref)).astype(bf16)
    k = _dot(xc, _diag_tile(wk_ref)).astype(bf16)
    v = _dot(xm, _diag_tile(wv_ref)).astype(bf16)
    q_ref[0] = q
    k_ref[0] = k
    v_ref[0] = v
    g = (_dot(q, wg_ref[0].astype(bf16)) + _dot(k, wg_ref[1].astype(bf16))
         + _dot(v, wg_ref[2].astype(bf16)))

    @pl.when(ct == 0)
    def _():
        g_ref[0] = g + bg_ref[...]

    @pl.when(ct > 0)
    def _():
        g_ref[0] += g


def _diag_tile(ref):
    r = lax.broadcasted_iota(jnp.int32, ref.shape, 0) // QKV_BLOCK
    c = lax.broadcasted_iota(jnp.int32, ref.shape, 1) // QKV_BLOCK
    return jnp.where(r == c, ref[...], 0.0).astype(bf16)


def _block_rows_tiled(w):
    return jnp.tile(w.reshape(E, QKV_BLOCK), (1, BD_TILE // QKV_BLOCK))


def _conv_qkv(xz, conv_w, conv_b, w_q, w_k, w_v, w_gate, b_gate):
    c = BD_TILE
    n_gate = 4 * NH
    act = jax.ShapeDtypeStruct((BATCH, U, E), bf16)
    tile = pl.BlockSpec((1, U, c), lambda b, t: (b, 0, t))
    bd = pl.BlockSpec((c, c), lambda b, t: (t, 0))
    return pl.pallas_call(
        _conv_qkv_kernel,
        grid=(BATCH, E // c),
        in_specs=[tile,
                  pl.BlockSpec((9, c), lambda b, t: (0, t)),
                  pl.BlockSpec((1, c), lambda b, t: (0, t)),
                  bd, bd, bd,
                  pl.BlockSpec((3, c, n_gate), lambda b, t: (0, t, 0)),
                  pl.BlockSpec((1, n_gate), lambda b, t: (0, 0))],
        out_specs=[tile, tile, tile, tile,
                   pl.BlockSpec((1, U, n_gate), lambda b, t: (b, 0, 0))],
        out_shape=[act, act, act, act, jax.ShapeDtypeStruct((BATCH, U, n_gate), f32)],
        compiler_params=_cparams(2),
        name="conv_qkv_gates",
    )(xz.reshape(BATCH, U, 2 * E), conv_w.reshape(9, E), conv_b.reshape(1, E),
      _block_rows_tiled(w_q), _block_rows_tiled(w_k), _block_rows_tiled(w_v),
      w_gate.reshape(3, E, n_gate), b_gate.reshape(1, n_gate))


def _log_sigmoid(x):
    return jnp.minimum(x, 0.0) - jnp.log1p(jnp.exp(-jnp.abs(x)))


def _mlstm_kernel(*refs, direction, finish):
    if finish:
        (q_ref, k_ref, v_ref, gc_ref, gr_ref, hf_ref, xc_ref, z_ref, nw_ref, sk_ref,
         o_ref, ct_ref, ctb_ref, n_ref, m_ref) = refs
    else:
        q_ref, k_ref, v_ref, gc_ref, gr_ref, o_ref, ct_ref, ctb_ref, n_ref, m_ref = refs
    L = MLSTM_CHUNK
    h = pl.program_id(1)
    step = pl.program_id(2)

    @pl.when(step == 0)
    def _():
        ct_ref[...] = jnp.zeros_like(ct_ref)
        ctb_ref[...] = jnp.zeros_like(ctb_ref)
        n_ref[...] = jnp.zeros_like(n_ref)
        m_ref[...] = jnp.full_like(m_ref, -jnp.inf)

    i_idx = 2 * direction * NH + h
    f_idx = (2 * direction + 1) * NH + h
    gc = gc_ref[0]
    gr = gr_ref[0]
    lane = lax.broadcasted_iota(jnp.int32, gc.shape, 1)
    sub = lax.broadcasted_iota(jnp.int32, gr.shape, 0)
    ig_col = jnp.sum(jnp.where(lane == i_idx, gc, 0.0), axis=1, keepdims=True)
    f_col = jnp.sum(jnp.where(lane == f_idx, gc, 0.0), axis=1, keepdims=True)
    ig_row = jnp.sum(jnp.where(sub == i_idx, gr, 0.0), axis=0, keepdims=True)
    f_row = jnp.sum(jnp.where(sub == f_idx, gr, 0.0), axis=0, keepdims=True)
    lf_col = _log_sigmoid(f_col)
    lf_row = _log_sigmoid(f_row)

    t_i = lax.broadcasted_iota(jnp.int32, (L, L), 0)
    s_i = lax.broadcasted_iota(jnp.int32, (L, L), 1)
    seen = (s_i <= t_i) if direction == 0 else (s_i >= t_i)
    b_col = jnp.sum(jnp.where(seen, lf_row, 0.0), axis=1, keepdims=True)
    seen_t = (t_i <= s_i) if direction == 0 else (t_i >= s_i)
    b_row = jnp.sum(jnp.where(seen_t, lf_col, 0.0), axis=0, keepdims=True)
    b_end = jnp.sum(lf_row, axis=1, keepdims=True)

    m_prev = m_ref[:, 0:1]
    log_d = jnp.where(seen, b_col - b_row + ig_row, -jnp.inf)
    g_col = b_col + m_prev
    m_t = jnp.maximum(g_col, jnp.max(log_d, axis=1, keepdims=True))
    dw = jnp.exp(log_d - m_t)
    inter = jnp.exp(g_col - m_t)

    q = q_ref[0]
    k = k_ref[0] * (DH ** -0.5)
    v = v_ref[0]
    s = lax.dot_general(q, k, (((1,), (1,)), ((), ())), preferred_element_type=f32) * dw
    num = _dot(s.astype(bf16), v) + inter * _dot(q, ctb_ref[...])
    qn = jnp.sum(q.astype(f32) * n_ref[...], axis=1, keepdims=True)
    den = jnp.sum(s, axis=1, keepdims=True) + inter * qn
    hout = num / jnp.maximum(jnp.abs(den), jnp.exp(-m_t))

    w_end = b_end - b_col + ig_col
    m_new = jnp.maximum(b_end + m_prev, jnp.max(w_end, axis=0, keepdims=True))
    decay = jnp.exp(b_end + m_prev - m_new)
    kw = k.astype(f32) * jnp.exp(w_end - m_new)
    upd = lax.dot_general(kw.astype(bf16), v, (((0,), (0,)), ((), ())), preferred_element_type=f32)
    c_new = decay * ct_ref[...] + upd
    ct_ref[...] = c_new
    ctb_ref[...] = c_new.astype(bf16)
    n_ref[...] = decay * n_ref[...] + jnp.sum(kw, axis=0, keepdims=True)
    m_ref[...] = jnp.broadcast_to(m_new, m_ref.shape)

    if not finish:
        o_ref[0] = hout.astype(o_ref.dtype)
    else:
        hs = hout + hf_ref[0].astype(f32)
        mu = jnp.mean(hs, axis=1, keepdims=True)
        xc_ = hs - mu
        var = jnp.mean(xc_ * xc_, axis=1, keepdims=True)
        hn = xc_ * lax.rsqrt(var + LN_EPS)
        z = z_ref[0].astype(f32)
        o_ref[0] = ((hn * nw_ref[...] + sk_ref[...] * xc_ref[0].astype(f32))
                    * (z * _sigmoid(z))).astype(o_ref.dtype)


def _mlstm(q, k, v, gates, gates_t, direction, finish_args=None):
    L = MLSTM_CHUNK
    n_chunks = U // L

    if direction == 0:
        def chunk(s):
            return s
    else:
        def chunk(s):
            return jnp.where(s == 0, 0, n_chunks - s)

    tile = pl.BlockSpec((1, L, DH), lambda b, h, s: (b, chunk(s), h))
    in_specs = [tile, tile, tile,
                pl.BlockSpec((1, L, 4 * NH), lambda b, h, s: (b, chunk(s), 0)),
                pl.BlockSpec((1, 4 * NH, L), lambda b, h, s: (b, 0, chunk(s)))]
    args = [q, k, v, gates, gates_t]
    finish = finish_args is not None
    if finish:
        h_fwd, xc, xz, norm_w, skip = finish_args
        vec = pl.BlockSpec((1, DH), lambda b, h, s: (0, h))
        in_specs += [tile, tile,
                     pl.BlockSpec((1, L, DH), lambda b, h, s: (b, chunk(s), NH + h)),
                     vec, vec]
        args += [h_fwd, xc, xz.reshape(BATCH, U, 2 * E), norm_w.reshape(1, E), skip.reshape(1, E)]
    return pl.pallas_call(
        functools.partial(_mlstm_kernel, direction=direction, finish=finish),
        grid=(BATCH, NH, n_chunks),
        in_specs=in_specs,
        out_specs=tile,
        out_shape=jax.ShapeDtypeStruct((BATCH, U, E), bf16),
        scratch_shapes=[pltpu.VMEM((DH, DH), f32), pltpu.VMEM((DH, DH), bf16),
                        pltpu.VMEM((1, DH), f32), pltpu.VMEM((1, 128), f32)],
        compiler_params=_cparams(3),
        name="mlstm_bwd_finish" if finish else "mlstm_fwd",
    )(*args)


def _ln_rows(r, g, b):
    mu = jnp.mean(r, axis=1, keepdims=True)
    rc = r - mu
    var = jnp.mean(rc * rc, axis=1, keepdims=True)
    return rc * lax.rsqrt(var + LN_EPS) * g + b


def _ln_router_kernel(*refs, joint):
    hs_refs, refs = refs[:1 + joint], refs[1 + joint:]
    (o_ref, m_ref, g_ref, b_ref, rw_ref, rb_ref,
     hs1_ref, tok_ref, gate_ref, eid_ref, rank_ref, cnt_ref, carry_ref) = refs

    @pl.when(pl.program_id(0) == 0)
    def _():
        carry_ref[...] = jnp.zeros_like(carry_ref)

    m = m_ref[0]
    hs = _joint_rows(*hs_refs) if joint else hs_refs[0][...]
    r = ALPHA * hs + m[2:3] * o_ref[...].astype(f32)
    hs1 = _ln_rows(r, g_ref[...], b_ref[...])
    hs1_ref[...] = hs1
    tok = hs1 * (1.0 + m[4:5]) + m[3:4]
    t_hi = tok.astype(bf16)
    _store_tile_rows(tok_ref, _pack_halves(tok))
    t_lo = (tok - t_hi.astype(f32)).astype(bf16)
    w = rw_ref[...]
    w_hi = w.astype(bf16)
    w_lo = (w - w_hi.astype(f32)).astype(bf16)
    logits = _dot(t_hi, w_hi) + _dot(t_hi, w_lo) + _dot(t_lo, w_hi) + rb_ref[...]

    lane = lax.broadcasted_iota(jnp.int32, logits.shape, 1).astype(f32)
    left = logits
    vals, ids = [], []
    for _ in range(TOP_K):
        v = jnp.max(left, axis=1, keepdims=True)
        e = jnp.min(jnp.where(left == v, lane, float(NE)), axis=1, keepdims=True)
        vals.append(v)
        ids.append(e)
        left = jnp.where(lane == e, -jnp.inf, left)
    exps = [jnp.exp(v - vals[0]) for v in vals]
    den = exps[0]
    for x in exps[1:]:
        den = den + x
    gate_ref[...] = jnp.concatenate([x / den for x in exps], axis=1)
    eid_ref[...] = jnp.concatenate(ids, axis=1).astype(jnp.int32)

    member = (lane == ids[0]).astype(f32)
    for e in ids[1:]:
        member = member + (lane == e).astype(f32)
    rows = logits.shape[0]
    earlier = (lax.broadcasted_iota(jnp.int32, (rows, rows), 1)
               < lax.broadcasted_iota(jnp.int32, (rows, rows), 0)).astype(bf16)
    before = _dot(earlier, member.astype(bf16)) + carry_ref[...]
    rank_ref[...] = jnp.concatenate(
        [jnp.sum(jnp.where(lane == e, before, 0.0), axis=1, keepdims=True) for e in ids],
        axis=1).astype(jnp.int32)
    total = carry_ref[...] + jnp.sum(member, axis=0, keepdims=True)
    carry_ref[...] = total
    cnt_ref[...] = total


def _ln_router(hs_parts, o, mt, group, ln_g, ln_b, r_w, r_b):
    m = o.shape[0]
    joint = len(hs_parts) == 2
    row = pl.BlockSpec((ROW_TILE, D), lambda i: (i, 0))
    vec = pl.BlockSpec((1, D), lambda i: (0, 0))
    topk = pl.BlockSpec((ROW_TILE, TOP_K), lambda i: (i, 0))
    return pl.pallas_call(
        functools.partial(_ln_router_kernel, joint=joint),
        grid=(m // ROW_TILE,),
        in_specs=([_CTX_ROWS, _LATENT_ROWS] if joint else [row]) + [
                  row,
                  pl.BlockSpec((1, 6, D), lambda i: (group(i), 0, 0)),
                  vec, vec,
                  pl.BlockSpec((D, NE), lambda i: (0, 0)),
                  pl.BlockSpec((1, NE), lambda i: (0, 0))],
        out_specs=[row, pl.BlockSpec((ROW_TILE * TILE_ROWS, LANES), lambda i: (i, 0)),
                   topk, topk, topk, pl.BlockSpec((1, NE), lambda i: (0, 0))],
        out_shape=[jax.ShapeDtypeStruct((m, D), f32),
                   jax.ShapeDtypeStruct((m * TILE_ROWS, LANES), jnp.uint32),
                   jax.ShapeDtypeStruct((m, TOP_K), f32),
                   jax.ShapeDtypeStruct((m, TOP_K), jnp.int32),
                   jax.ShapeDtypeStruct((m, TOP_K), jnp.int32),
                   jax.ShapeDtypeStruct((1, NE), f32)],
        scratch_shapes=[pltpu.VMEM((1, NE), f32)],
        compiler_params=_cparams(1),
        name="ln_router",
    )(*hs_parts, o, mt, ln_g.reshape(1, D), ln_b.reshape(1, D), r_w, r_b.reshape(1, NE))


def _first_of_expert(be_ref, i):
    return jnp.logical_or(i == 0, be_ref[i] != be_ref[jnp.maximum(i - 1, 0)])


ROW_STEPS = 8


def _for_real_rows(valid, rows, compute):
    q = MOE_TM // ROW_STEPS
    for step in range(1, ROW_STEPS + 1):
        covers = rows <= step * q
        if step > 1:
            covers = jnp.logical_and(covers, rows > (step - 1) * q)
        pl.when(jnp.logical_and(valid, covers))(functools.partial(compute, step * q))


def _moe_up_kernel(be_ref, nu_ref, nxt_ref, rows_ref, x_ref, w_hbm, bg_ref, bl_ref, o_ref, stage_ref,
                   cache_ref, sem, *, layer, tf):
    j = pl.program_id(0)
    i = pl.program_id(1)
    valid = i < nu_ref[0]

    def tile_copy(e, jj, half):
        col = pl.multiple_of(half * F + jj * tf, tf)
        return pltpu.make_async_copy(w_hbm.at[layer, e, :, pl.ds(col, tf)], stage_ref.at[half], sem.at[half])

    def start(e, jj):
        tile_copy(e, jj, 0).start()
        tile_copy(e, jj, 1).start()

    @pl.when(jnp.logical_and(j == 0, i == 0))
    def _():
        start(be_ref[0], 0)

    @pl.when(jnp.logical_and(valid, _first_of_expert(be_ref, i)))
    def _():
        tile_copy(0, 0, 0).wait()
        tile_copy(0, 0, 1).wait()
        cache_ref[...] = stage_ref[...].astype(bf16)
        nxt = nxt_ref[i]

        @pl.when(nxt >= 0)
        def _():
            start(nxt, j)

        @pl.when(jnp.logical_and(nxt < 0, j + 1 < pl.num_programs(0)))
        def _():
            start(be_ref[0], j + 1)

    def compute(rows):
        x = _unpack_halves(_load_tile_rows(x_ref, rows)).astype(bf16)
        glu = jnp.minimum(_dot(x, cache_ref[0]) + bg_ref[0, 0], SWIGLU_LIMIT)
        lin = jnp.clip(_dot(x, cache_ref[1]) + bl_ref[0, 0], -SWIGLU_LIMIT, SWIGLU_LIMIT)
        o_ref[pl.ds(0, rows), :] = (glu * _sigmoid(SWIGLU_ALPHA * glu) * (lin + 1.0)).astype(o_ref.dtype)
        if rows < MOE_TM:
            o_ref[pl.ds(rows, MOE_TM - rows), :] = jnp.zeros((MOE_TM - rows, tf), o_ref.dtype)

    _for_real_rows(valid, rows_ref[i], compute)

    @pl.when(jnp.logical_not(valid))
    def _():
        o_ref[...] = jnp.zeros_like(o_ref)


def _moe_down_kernel(be_ref, nu_ref, nxt_ref, rows_ref, a_ref, w_hbm, b_ref, o_ref, stage_ref, cache_ref, sem,
                     *, layer):
    i = pl.program_id(0)
    valid = i < nu_ref[0]

    def expert_copy(e):
        return pltpu.make_async_copy(w_hbm.at[layer, e], stage_ref, sem)

    @pl.when(i == 0)
    def _():
        expert_copy(be_ref[0]).start()

    @pl.when(jnp.logical_and(valid, _first_of_expert(be_ref, i)))
    def _():
        expert_copy(0).wait()
        cache_ref[...] = stage_ref[...].astype(bf16)
        nxt = nxt_ref[i]

        @pl.when(nxt >= 0)
        def _():
            expert_copy(nxt).start()

    def compute(rows):
        y = _dot(a_ref[pl.ds(0, rows), :], cache_ref[...]) + b_ref[0, 0]
        _store_tile_rows(o_ref, _pack_halves(y))
        if rows < MOE_TM:
            o_ref[pl.ds(rows * TILE_ROWS, (MOE_TM - rows) * TILE_ROWS), :] = jnp.zeros(
                ((MOE_TM - rows) * TILE_ROWS, LANES), o_ref.dtype)

    _for_real_rows(valid, rows_ref[i], compute)

    @pl.when(jnp.logical_not(valid))
    def _():
        o_ref[...] = jnp.zeros_like(o_ref)


def _moe_experts(layer, xs, block_e, n_used, next_e, block_rows, w1, b1, w2, b2):
    ns = xs.shape[0] // TILE_ROWS
    nb = ns // MOE_TM
    tf = 1024
    lin0 = F // tf
    b1 = b1.reshape(DEPTH, NE, 1, 2 * F)
    b2 = b2.reshape(DEPTH, NE, 1, D)

    def bspec(col0):
        return pl.BlockSpec((1, 1, 1, tf), lambda j, i, be, nu, nx, br: (layer, be[i], 0, col0 + j))

    act = pl.pallas_call(
        functools.partial(_moe_up_kernel, layer=layer, tf=tf),
        grid_spec=pltpu.PrefetchScalarGridSpec(
            num_scalar_prefetch=4,
            grid=(F // tf, nb),
            in_specs=[pl.BlockSpec((MOE_TM * TILE_ROWS, LANES),
                                   lambda j, i, be, nu, nx, br: (jnp.minimum(i, nu[0] - 1), 0)),
                      pl.BlockSpec(memory_space=pl.ANY), bspec(0), bspec(lin0)],
            out_specs=pl.BlockSpec((MOE_TM, tf), lambda j, i, be, nu, nx, br: (i, j)),
            scratch_shapes=[pltpu.VMEM((2, D, tf), f32), pltpu.VMEM((2, D, tf), bf16),
                            pltpu.SemaphoreType.DMA((2,))]),
        out_shape=jax.ShapeDtypeStruct((ns, F), bf16),
        compiler_params=_cparams(2),
        name="moe_up",
    )(block_e, n_used, next_e, block_rows, xs, w1, b1, b1)

    return pl.pallas_call(
        functools.partial(_moe_down_kernel, layer=layer),
        grid_spec=pltpu.PrefetchScalarGridSpec(
            num_scalar_prefetch=4,
            grid=(nb,),
            in_specs=[pl.BlockSpec((MOE_TM, F), lambda i, be, nu, nx, br: (jnp.minimum(i, nu[0] - 1), 0)),
                      pl.BlockSpec(memory_space=pl.ANY),
                      pl.BlockSpec((1, 1, 1, D), lambda i, be, nu, nx, br: (layer, be[i], 0, 0))],
            out_specs=pl.BlockSpec((MOE_TM * TILE_ROWS, LANES), lambda i, be, nu, nx, br: (i, 0)),
            scratch_shapes=[pltpu.VMEM((F, D), f32), pltpu.VMEM((F, D), bf16), pltpu.SemaphoreType.DMA]),
        out_shape=jax.ShapeDtypeStruct((ns * TILE_ROWS, LANES), jnp.uint32),
        compiler_params=_cparams(1),
        name="moe_down",
    )(block_e, n_used, next_e, block_rows, act, w2, b2)


GATHER_ROWS = 4096


def _gather_kernel(idx_ref, src_ref, out_ref, sem):
    n = GATHER_ROWS
    t = TILE_ROWS

    def issue(r, carry):
        src_row = pl.multiple_of(idx_ref[0, 0, r] * t, t)
        dst_row = pl.multiple_of(r * t, t)
        pltpu.make_async_copy(src_ref.at[pl.ds(src_row, t)], out_ref.at[pl.ds(dst_row, t)], sem).start()
        return carry

    lax.fori_loop(0, n, issue, 0, unroll=8)
    pltpu.make_async_copy(src_ref.at[pl.ds(0, n * t)], out_ref, sem).wait()


def _gather_rows(src, idx):
    n = idx.shape[0]
    nblk = n // GATHER_ROWS
    return pl.pallas_call(
        _gather_kernel,
        grid=(nblk,),
        in_specs=[pl.BlockSpec((1, 1, GATHER_ROWS), lambda i: (i, 0, 0), memory_space=pltpu.SMEM),
                  pl.BlockSpec(memory_space=pl.ANY)],
        out_specs=pl.BlockSpec((GATHER_ROWS * TILE_ROWS, LANES), lambda i: (i, 0)),
        out_shape=jax.ShapeDtypeStruct((n * TILE_ROWS, LANES), src.dtype),
        scratch_shapes=[pltpu.SemaphoreType.DMA],
        compiler_params=pltpu.CompilerParams(dimension_semantics=("arbitrary",),
                                             vmem_limit_bytes=VMEM_LIMIT,
                                             disable_bounds_checks=True),
        name="gather_rows",
    )(idx.reshape(nblk, 1, GATHER_ROWS), src)


SCATTER_TOKENS = 1024


def _scatter_kernel(dest_ref, src_ref, init_ref, out_ref, sem):
    del init_ref
    n = SCATTER_TOKENS
    t = TILE_ROWS

    def issue(a, carry):
        src_row = pl.multiple_of(lax.shift_right_logical(a, TOP_K.bit_length() - 1) * t, t)
        dst_row = pl.multiple_of(dest_ref[0, 0, a] * t, t)
        pltpu.make_async_copy(src_ref.at[pl.ds(src_row, t)], out_ref.at[pl.ds(dst_row, t)], sem).start()
        return carry

    lax.fori_loop(0, n * TOP_K, issue, 0, unroll=8)
    for _ in range(TOP_K):
        pltpu.make_async_copy(src_ref, out_ref.at[pl.ds(0, n * t)], sem).wait()


def _scatter_rows(src, dest, init):
    n = src.shape[0] // TILE_ROWS
    nblk = n // SCATTER_TOKENS
    return pl.pallas_call(
        _scatter_kernel,
        grid=(nblk,),
        in_specs=[pl.BlockSpec((1, 1, SCATTER_TOKENS * TOP_K), lambda i: (i, 0, 0), memory_space=pltpu.SMEM),
                  pl.BlockSpec((SCATTER_TOKENS * TILE_ROWS, LANES), lambda i: (i, 0)),
                  pl.BlockSpec(memory_space=pl.ANY)],
        out_specs=pl.BlockSpec(memory_space=pl.ANY),
        out_shape=jax.ShapeDtypeStruct(init.shape, src.dtype),
        input_output_aliases={2: 0},
        scratch_shapes=[pltpu.SemaphoreType.DMA],
        compiler_params=pltpu.CompilerParams(dimension_semantics=("arbitrary",),
                                             vmem_limit_bytes=VMEM_LIMIT,
                                             disable_bounds_checks=True),
        name="scatter_rows",
    )(dest.reshape(nblk, 1, SCATTER_TOKENS * TOP_K), src, init)


MOE_BLOCKS = -(-(BATCH * U * TOP_K) // MOE_TM) + NE


def _slot_layout(eid, rank, counts):
    nb = MOE_BLOCKS
    counts = counts.reshape(NE).astype(jnp.int32)
    experts = jnp.arange(NE, dtype=jnp.int32)
    pcounts = (counts + MOE_TM - 1) // MOE_TM * MOE_TM
    pends = jnp.cumsum(pcounts)
    pstarts = pends - pcounts
    dest = rank + jnp.sum(jnp.where(eid[:, :, None] == experts, pstarts, 0), axis=-1)
    n_used = pends[-1] // MOE_TM
    blk = jnp.arange(nb, dtype=jnp.int32)
    block_e = jnp.minimum(jnp.sum(pends[None, :] <= (blk * MOE_TM)[:, None], axis=1), NE - 1)
    of_block = block_e[:, None] == experts

    def per_block(table):
        return jnp.sum(jnp.where(of_block, table, 0), axis=1)

    block_rows = jnp.clip(per_block(counts) - (blk * MOE_TM - per_block(pstarts)), 0, MOE_TM)
    block_rows = jnp.where(blk < n_used, block_rows, 0)
    group_end = per_block(pends) // MOE_TM
    next_e = jnp.sum(jnp.where(group_end[:, None] == blk, block_e, 0), axis=1)
    next_e = jnp.where(group_end < n_used, next_e, -1)
    last_e = jnp.sum(jnp.where(blk == n_used - 1, block_e, 0))
    block_e = jnp.where(blk < n_used, block_e, last_e)
    i32 = jnp.int32
    return (dest.astype(i32), block_e.astype(i32), n_used.astype(i32).reshape(1), next_e.astype(i32),
            block_rows.astype(i32))


def _combine_kernel(hs_ref, y_ref, gate_ref, m_ref, g_ref, b_ref, *rest, joint):
    if joint:
        mn_ref, hs2_ref, nxt_ref, ctx_ref = rest
    else:
        (hs2_ref,) = rest
    m = m_ref[0]
    gate = gate_ref[...]
    y = gate[:, 0:1] * _unpack_halves(_load_tile_rows(y_ref.at[0], ROW_TILE))
    for k in range(1, TOP_K):
        y = y + gate[:, k:k + 1] * _unpack_halves(_load_tile_rows(y_ref.at[k], ROW_TILE))
    hs2 = _ln_rows(ALPHA * hs_ref[...] + m[5:6] * y, g_ref[...], b_ref[...])
    if not joint:
        hs2_ref[...] = hs2
        return
    is_ctx = _is_ctx_tile(pl.program_id(0))

    @pl.when(is_ctx)
    def _():
        ctx_ref[...] = hs2

    @pl.when(jnp.logical_not(is_ctx))
    def _():
        mn = mn_ref[0]
        hs2_ref[...] = hs2
        nxt_ref[...] = (hs2 * (1.0 + mn[1:2]) + mn[0:1]).astype(nxt_ref.dtype)


def _combine(hs1, yg, gate, mt, group, ln_g, ln_b, mt_next=None):
    m = hs1.shape[0]
    row = pl.BlockSpec((ROW_TILE, D), lambda i: (i, 0))
    vec = pl.BlockSpec((1, D), lambda i: (0, 0))
    mod = pl.BlockSpec((1, 6, D), lambda i: (group(i), 0, 0))
    in_specs = [row, pl.BlockSpec((TOP_K, ROW_TILE * TILE_ROWS, LANES), lambda i: (0, i, 0)),
                pl.BlockSpec((ROW_TILE, TOP_K), lambda i: (i, 0)), mod, vec, vec]
    args = [hs1, yg, gate, mt, ln_g.reshape(1, D), ln_b.reshape(1, D)]
    joint = mt_next is not None
    if joint:
        in_specs.append(mod)
        args.append(mt_next)
        out_specs = [_LATENT_ROWS, _LATENT_ROWS, _CTX_ROWS]
        out_shape = [jax.ShapeDtypeStruct((BATCH * SEQ, D), f32), jax.ShapeDtypeStruct((BATCH * SEQ, D), bf16),
                     jax.ShapeDtypeStruct((BATCH * LC, D), f32)]
    else:
        out_specs = [row]
        out_shape = [jax.ShapeDtypeStruct((m, D), f32)]
    return pl.pallas_call(
        functools.partial(_combine_kernel, joint=joint),
        grid=(m // ROW_TILE,),
        in_specs=in_specs, out_specs=out_specs, out_shape=out_shape,
        compiler_params=_cparams(1),
        name="moe_combine_ln",
    )(*args)


def _moe_layer(layer, hs_parts, o, mt, group, ln1_g, ln1_b, ln2_g, ln2_b, r_w, r_b, w1, b1, w2, b2,
               slots, mt_next=None):
    t = o.shape[0]
    hs1, tok, gate, eid, rank, counts = _ln_router(hs_parts, o, mt, group, ln1_g[layer], ln1_b[layer],
                                                   r_w[layer], r_b[layer])
    dest, block_e, n_used, next_e, block_rows = _slot_layout(eid, rank, counts)
    xs = _scatter_rows(tok, dest.reshape(-1), slots)
    y = _moe_experts(layer, xs, block_e, n_used, next_e, block_rows, w1, b1, w2, b2)
    yg = _gather_rows(y, dest.T.reshape(-1)).reshape(TOP_K, t * TILE_ROWS, LANES)
    return _combine(hs1, yg, gate, mt, group, ln2_g[layer], ln2_b[layer], mt_next), xs


def _sgu_kernel(g_ref, v_ref, lg_ref, lb_ref, ws_ref, bs_ref, o_ref):
    v = v_ref[...].astype(f32)
    vn = _ln_rows(v, lg_ref[...], lb_ref[...]).astype(bf16)
    gw = E // SGU_GROUPS
    for g in range(SGU_GROUPS):
        mixed = _dot(ws_ref[g].astype(bf16), vn[:, g * gw:(g + 1) * gw]) + bs_ref[:, g:g + 1]
        o_ref[:, g * gw:(g + 1) * gw] = (g_ref[:, g * gw:(g + 1) * gw].astype(f32) * mixed).astype(o_ref.dtype)


def _sgu(uv, ln_g, ln_b, w_s, b_s):
    m = uv.shape[0]
    c = SGU_CHUNK
    vec = pl.BlockSpec((1, E), lambda i: (0, 0))
    return pl.pallas_call(
        _sgu_kernel,
        grid=(m // c,),
        in_specs=[pl.BlockSpec((c, E), lambda i: (i, 0)),
                  pl.BlockSpec((c, E), lambda i: (i, 1)),
                  vec, vec,
                  pl.BlockSpec((SGU_GROUPS, c, c), lambda i: (0, 0, 0)),
                  pl.BlockSpec((c, SGU_GROUPS), lambda i: (0, 0))],
        out_specs=pl.BlockSpec((c, E), lambda i: (i, 0)),
        out_shape=jax.ShapeDtypeStruct((m, E), bf16),
        compiler_params=_cparams(1),
        name="sgu",
    )(uv, uv, ln_g.reshape(1, E), ln_b.reshape(1, E), w_s, b_s.T)


def kernel(x, c, ctx, c_ctx, mod_w, mod_b, ln1_g, ln1_b, ln2_g, ln2_b, a_w_in, a_conv_w, a_conv_b, a_w_q, a_w_k, a_w_v, a_w_gate, a_b_gate, a_norm_w, a_skip, a_w_out, b_w_in, b_ln_g, b_ln_b, b_w_s, b_b_s, b_w_out, r_w, r_b, e_w1, e_b1, e_w2, e_b2):
    cvec = jnp.concatenate([c, c_ctx[None, :], jnp.zeros((8 - BATCH - 1, D), f32)], axis=0)
    mods = _mod_rows(cvec, mod_w, mod_b).reshape(DEPTH, 8, 6, D)
    rows = [r for b in range(BATCH) for r in (BATCH, b)]
    mt = [jnp.stack([mods[l, r] for r in rows], axis=0) for l in range(DEPTH)]

    hs = (ctx.reshape(BATCH * LC, D), x.reshape(BATCH * SEQ, D))
    hx = _modulate(*hs, mt[0])
    xz = _matmul(hx, a_w_in[0], tm=1024, tn=1024, name="mlstm_in_proj")
    xc, q, k, v, gates = _conv_qkv(xz, a_conv_w[0], a_conv_b[0], a_w_q[0], a_w_k[0], a_w_v[0],
                                   a_w_gate[0], a_b_gate[0])
    gates_t = gates.transpose(0, 2, 1)
    h_fwd = _mlstm(q, k, v, gates, gates_t, 0)
    pre = _mlstm(q, k, v, gates, gates_t, 1, (h_fwd, xc, xz, a_norm_w[0], a_skip[0]))
    o = _matmul(pre.reshape(BATCH * U, E), a_w_out[0], tm=1024, tn=512, name="mlstm_out_proj")
    slots = jnp.zeros((MOE_BLOCKS * MOE_TM * TILE_ROWS, LANES), jnp.uint32)
    (hs, hx, _), slots = _moe_layer(0, hs, o, mt[0], _group_joint, ln1_g, ln1_b, ln2_g, ln2_b,
                                    r_w, r_b, e_w1, e_b1, e_w2, e_b2, slots, mt_next=mt[1])

    uv = _matmul(hx, b_w_in[0], tm=1024, tn=1024, act="gelu", name="sgu_in_proj")
    gated = _sgu(uv, b_ln_g[0], b_ln_b[0], b_w_s[0], b_b_s[0])
    o = _matmul(gated, b_w_out[0], tm=1024, tn=512, name="sgu_out_proj")
    (out,), _ = _moe_layer(1, (hs,), o, mt[1], _group_latent, ln1_g, ln1_b, ln2_g, ln2_b,
                           r_w, r_b, e_w1, e_b1, e_w2, e_b2, slots)
    return out.reshape(BATCH, SEQ, D)
```

```python
import functools

import jax
import jax.numpy as jnp
from jax import lax
from jax.experimental import pallas as pl
from jax.experimental.pallas import tpu as pltpu

D = 2048
BATCH = 4
SEQ = 2048
DEPTH = 2
GRID_W = 64
LC = 256
U = LC + SEQ
E = 2 * D
NH = 4
DH = E // NH
QKV_BLOCK = 4
SGU_GROUPS = 8
SGU_CHUNK = 128
NE = 32
TOP_K = 4
F = D
SWIGLU_LIMIT = 7.0
SWIGLU_ALPHA = 1.702
ALPHA = (2 * DEPTH) ** 0.25
LN_EPS = 1e-5

ROW_TILE = 256
MLSTM_CHUNK = 256
MOE_TM = 512
BD_TILE = 256
VMEM_LIMIT = 56 * 1024 * 1024

f32 = jnp.float32
bf16 = jnp.bfloat16


def _cparams(n_axes):
    return pltpu.CompilerParams(dimension_semantics=("arbitrary",) * n_axes,
                                vmem_limit_bytes=VMEM_LIMIT)


def _dot(a, b):
    return jnp.dot(a, b, preferred_element_type=f32)


def _sigmoid(x):
    return 1.0 / (1.0 + jnp.exp(-x))


def _pack_halves(x):
    w = x.shape[1] // 2
    bits = lax.bitcast_convert_type(x.astype(bf16).astype(f32), jnp.uint32)
    return (bits[:, w:] & jnp.uint32(0xFFFF0000)) | (bits[:, :w] >> 16)


def _unpack_halves(p):
    lo = lax.bitcast_convert_type(p << 16, f32)
    hi = lax.bitcast_convert_type(p & jnp.uint32(0xFFFF0000), f32)
    return jnp.concatenate([lo, hi], axis=1)


PACKED = D // 2
LANES = 128
TILE_ROWS = PACKED // LANES


def _store_tile_rows(ref, p):
    r = p.shape[0]
    for s in range(TILE_ROWS):
        ref[pl.ds(s, r, stride=TILE_ROWS), :] = p[:, s * LANES:(s + 1) * LANES]


def _load_tile_rows(ref, r):
    return jnp.concatenate([ref[pl.ds(s, r, stride=TILE_ROWS), :] for s in range(TILE_ROWS)], axis=1)


def _mod_kernel(c_ref, w_ref, b_ref, o_ref):
    c = c_ref[...]
    a = (c * _sigmoid(c)).astype(bf16)
    o_ref[0] = _dot(a, w_ref[0].astype(bf16)) + b_ref[0]


def _mod_rows(cvec, mod_w, mod_b):
    tn = 1024
    return pl.pallas_call(
        _mod_kernel,
        grid=(DEPTH, 6 * D // tn),
        in_specs=[pl.BlockSpec((8, D), lambda l, j: (0, 0)),
                  pl.BlockSpec((1, D, tn), lambda l, j: (l, 0, j)),
                  pl.BlockSpec((1, 1, tn), lambda l, j: (l, 0, j))],
        out_specs=pl.BlockSpec((1, 8, tn), lambda l, j: (l, 0, j)),
        out_shape=jax.ShapeDtypeStruct((DEPTH, 8, 6 * D), f32),
        compiler_params=_cparams(2),
        name="adaln_rows",
    )(cvec, mod_w, mod_b.reshape(DEPTH, 1, 6 * D))


def _group_joint(i):
    tiles = U // ROW_TILE
    return 2 * (i // tiles) + jnp.minimum(i % tiles, 1)


def _group_latent(i):
    return 2 * (i // (SEQ // ROW_TILE)) + 1


JOINT_TILES = U // ROW_TILE


def _is_ctx_tile(i):
    return i % JOINT_TILES == 0


def _ctx_tile(i):
    return i // JOINT_TILES


def _latent_tile(i):
    return (i // JOINT_TILES) * (SEQ // ROW_TILE) + jnp.maximum(i % JOINT_TILES - 1, 0)


_CTX_ROWS = pl.BlockSpec((ROW_TILE, D), lambda i: (_ctx_tile(i), 0))
_LATENT_ROWS = pl.BlockSpec((ROW_TILE, D), lambda i: (_latent_tile(i), 0))


def _joint_rows(ctx_ref, lat_ref):
    is_ctx = _is_ctx_tile(pl.program_id(0))
    return jnp.where(is_ctx, ctx_ref[...], lat_ref[...])


def _modulate_kernel(ctx_ref, lat_ref, m_ref, o_ref):
    m = m_ref[0]
    o_ref[...] = (_joint_rows(ctx_ref, lat_ref) * (1.0 + m[1:2]) + m[0:1]).astype(o_ref.dtype)


def _modulate(ctx2d, x2d, mt):
    m = ctx2d.shape[0] + x2d.shape[0]
    return pl.pallas_call(
        _modulate_kernel,
        grid=(m // ROW_TILE,),
        in_specs=[_CTX_ROWS, _LATENT_ROWS,
                  pl.BlockSpec((1, 6, D), lambda i: (_group_joint(i), 0, 0))],
        out_specs=pl.BlockSpec((ROW_TILE, D), lambda i: (i, 0)),
        out_shape=jax.ShapeDtypeStruct((m, D), bf16),
        compiler_params=_cparams(1),
        name="modulate",
    )(ctx2d, x2d, mt)


def _erf(x):
    return lax.erf(x)


def _mm_kernel(x_ref, w_ref, o_ref, wb_ref, *, act):
    @pl.when(pl.program_id(1) == 0)
    def _():
        wb_ref[...] = w_ref[...].astype(bf16)

    acc = _dot(x_ref[...], wb_ref[...])
    if act == "gelu":
        acc = 0.5 * acc * (1.0 + _erf(acc * (2.0 ** -0.5)))
    o_ref[...] = acc.astype(o_ref.dtype)


def _matmul(x, w, *, tm, tn, act=None, name):
    m, k = x.shape
    n = w.shape[1]
    return pl.pallas_call(
        functools.partial(_mm_kernel, act=act),
        grid=(n // tn, m // tm),
        in_specs=[pl.BlockSpec((tm, k), lambda j, i: (i, 0)),
                  pl.BlockSpec((k, tn), lambda j, i: (0, j))],
        out_specs=pl.BlockSpec((tm, tn), lambda j, i: (i, j)),
        out_shape=jax.ShapeDtypeStruct((m, n), bf16),
        scratch_shapes=[pltpu.VMEM((k, tn), bf16)],
        compiler_params=_cparams(2),
        name=name,
    )(x, w)


def _conv_qkv_kernel(xm_ref, cw_ref, cb_ref, wq_ref, wk_ref, wv_ref, wg_ref, bg_ref,
                     xc_ref, q_ref, k_ref, v_ref, g_ref):
    ct = pl.program_id(1)
    a = xm_ref[0].astype(f32)
    c = a.shape[1]
    r = lax.broadcasted_iota(jnp.int32, (U, c), 0)
    latent = r >= LC
    p = r - LC
    col = jnp.where(latent, p & (GRID_W - 1), r)
    last = jnp.where(latent, GRID_W - 1, LC - 1)
    a_l = jnp.where(col > 0, pltpu.roll(a, 1, 0), 0.0)
    a_r = jnp.where(col < last, pltpu.roll(a, U - 1, 0), 0.0)
    w = cw_ref[...]
    rows = [w[3 * i:3 * i + 1] * a_l + w[3 * i + 1:3 * i + 2] * a + w[3 * i + 2:3 * i + 3] * a_r
            for i in range(3)]
    up = jnp.where(p >= GRID_W, pltpu.roll(rows[0], GRID_W, 0), 0.0)
    down = jnp.where(latent & (p < SEQ - GRID_W), pltpu.roll(rows[2], U - GRID_W, 0), 0.0)
    pre = rows[1] + up + down + cb_ref[...]
    xc = (pre * _sigmoid(pre)).astype(bf16)
    xc_ref[0] = xc
    xm = xm_ref[0]
    q = _dot(xc, _diag_tile(wq_ref)).astype(bf16)
    k = _dot(xc, _diag_tile(wk_ref)).astype(bf16)
    v = _dot(xm, _diag_tile(wv_ref)).astype(bf16)
    q_ref[0] = q
    k_ref[0] = k
    v_ref[0] = v
    g = (_dot(q, wg_ref[0].astype(bf16)) + _dot(k, wg_ref[1].astype(bf16))
         + _dot(v, wg_ref[2].astype(bf16)))

    @pl.when(ct == 0)
    def _():
        g_ref[0] = g + bg_ref[...]

    @pl.when(ct > 0)
    def _():
        g_ref[0] += g


def _diag_tile(ref):
    r = lax.broadcasted_iota(jnp.int32, ref.shape, 0) // QKV_BLOCK
    c = lax.broadcasted_iota(jnp.int32, ref.shape, 1) // QKV_BLOCK
    return jnp.where(r == c, ref[...], 0.0).astype(bf16)


def _block_rows_tiled(w):
    return jnp.tile(w.reshape(E, QKV_BLOCK), (1, BD_TILE // QKV_BLOCK))


def _conv_qkv(xz, conv_w, conv_b, w_q, w_k, w_v, w_gate, b_gate):
    c = BD_TILE
    n_gate = 4 * NH
    act = jax.ShapeDtypeStruct((BATCH, U, E), bf16)
    tile = pl.BlockSpec((1, U, c), lambda b, t: (b, 0, t))
    bd = pl.BlockSpec((c, c), lambda b, t: (t, 0))
    return pl.pallas_call(
        _conv_qkv_kernel,
        grid=(BATCH, E // c),
        in_specs=[tile,
                  pl.BlockSpec((9, c), lambda b, t: (0, t)),
                  pl.BlockSpec((1, c), lambda b, t: (0, t)),
                  bd, bd, bd,
                  pl.BlockSpec((3, c, n_gate), lambda b, t: (0, t, 0)),
                  pl.BlockSpec((1, n_gate), lambda b, t: (0, 0))],
        out_specs=[tile, tile, tile, tile,
                   pl.BlockSpec((1, U, n_gate), lambda b, t: (b, 0, 0))],
        out_shape=[act, act, act, act, jax.ShapeDtypeStruct((BATCH, U, n_gate), f32)],
        compiler_params=_cparams(2),
        name="conv_qkv_gates",
    )(xz.reshape(BATCH, U, 2 * E), conv_w.reshape(9, E), conv_b.reshape(1, E),
      _block_rows_tiled(w_q), _block_rows_tiled(w_k), _block_rows_tiled(w_v),
      w_gate.reshape(3, E, n_gate), b_gate.reshape(1, n_gate))


def _log_sigmoid(x):
    return jnp.minimum(x, 0.0) - jnp.log1p(jnp.exp(-jnp.abs(x)))


def _mlstm_kernel(*refs, direction, finish):
    if finish:
        (q_ref, k_ref, v_ref, gc_ref, gr_ref, hf_ref, xc_ref, z_ref, nw_ref, sk_ref,
         o_ref, ct_ref, ctb_ref, n_ref, m_ref) = refs
    else:
        q_ref, k_ref, v_ref, gc_ref, gr_ref, o_ref, ct_ref, ctb_ref, n_ref, m_ref = refs
    L = MLSTM_CHUNK
    h = pl.program_id(1)
    step = pl.program_id(2)

    @pl.when(step == 0)
    def _():
        ct_ref[...] = jnp.zeros_like(ct_ref)
        ctb_ref[...] = jnp.zeros_like(ctb_ref)
        n_ref[...] = jnp.zeros_like(n_ref)
        m_ref[...] = jnp.full_like(m_ref, -jnp.inf)

    i_idx = 2 * direction * NH + h
    f_idx = (2 * direction + 1) * NH + h
    gc = gc_ref[0]
    gr = gr_ref[0]
    lane = lax.broadcasted_iota(jnp.int32, gc.shape, 1)
    sub = lax.broadcasted_iota(jnp.int32, gr.shape, 0)
    ig_col = jnp.sum(jnp.where(lane == i_idx, gc, 0.0), axis=1, keepdims=True)
    f_col = jnp.sum(jnp.where(lane == f_idx, gc, 0.0), axis=1, keepdims=True)
    ig_row = jnp.sum(jnp.where(sub == i_idx, gr, 0.0), axis=0, keepdims=True)
    f_row = jnp.sum(jnp.where(sub == f_idx, gr, 0.0), axis=0, keepdims=True)
    lf_col = _log_sigmoid(f_col)
    lf_row = _log_sigmoid(f_row)

    t_i = lax.broadcasted_iota(jnp.int32, (L, L), 0)
    s_i = lax.broadcasted_iota(jnp.int32, (L, L), 1)
    seen = (s_i <= t_i) if direction == 0 else (s_i >= t_i)
    b_col = jnp.sum(jnp.where(seen, lf_row, 0.0), axis=1, keepdims=True)
    seen_t = (t_i <= s_i) if direction == 0 else (t_i >= s_i)
    b_row = jnp.sum(jnp.where(seen_t, lf_col, 0.0), axis=0, keepdims=True)
    b_end = jnp.sum(lf_row, axis=1, keepdims=True)

    m_prev = m_ref[:, 0:1]
    log_d = jnp.where(seen, b_col - b_row + ig_row, -jnp.inf)
    g_col = b_col + m_prev
    m_t = jnp.maximum(g_col, jnp.max(log_d, axis=1, keepdims=True))
    dw = jnp.exp(log_d - m_t)
    inter = jnp.exp(g_col - m_t)

    q = q_ref[0]
    k = k_ref[0] * (DH ** -0.5)
    v = v_ref[0]
    s = lax.dot_general(q, k, (((1,), (1,)), ((), ())), preferred_element_type=f32) * dw
    num = _dot(s.astype(bf16), v) + inter * _dot(q, ctb_ref[...])
    qn = jnp.sum(q.astype(f32) * n_ref[...], axis=1, keepdims=True)
    den = jnp.sum(s, axis=1, keepdims=True) + inter * qn
    hout = num / jnp.maximum(jnp.abs(den), jnp.exp(-m_t))

    w_end = b_end - b_col + ig_col
    m_new = jnp.maximum(b_end + m_prev, jnp.max(w_end, axis=0, keepdims=True))
    decay = jnp.exp(b_end + m_prev - m_new)
    kw = k.astype(f32) * jnp.exp(w_end - m_new)
    upd = lax.dot_general(kw.astype(bf16), v, (((0,), (0,)), ((), ())), preferred_element_type=f32)
    c_new = decay * ct_ref[...] + upd
    ct_ref[...] = c_new
    ctb_ref[...] = c_new.astype(bf16)
    n_ref[...] = decay * n_ref[...] + jnp.sum(kw, axis=0, keepdims=True)
    m_ref[...] = jnp.broadcast_to(m_new, m_ref.shape)

    if not finish:
        o_ref[0] = hout.astype(o_ref.dtype)
    else:
        hs = hout + hf_ref[0].astype(f32)
        mu = jnp.mean(hs, axis=1, keepdims=True)
        xc_ = hs - mu
        var = jnp.mean(xc_ * xc_, axis=1, keepdims=True)
        hn = xc_ * lax.rsqrt(var + LN_EPS)
        z = z_ref[0].astype(f32)
        o_ref[0] = ((hn * nw_ref[...] + sk_ref[...] * xc_ref[0].astype(f32))
                    * (z * _sigmoid(z))).astype(o_ref.dtype)


def _mlstm(q, k, v, gates, gates_t, direction, finish_args=None):
    L = MLSTM_CHUNK
    n_chunks = U // L

    if direction == 0:
        def chunk(s):
            return s
    else:
        def chunk(s):
            return jnp.where(s == 0, 0, n_chunks - s)

    tile = pl.BlockSpec((1, L, DH), lambda b, h, s: (b, chunk(s), h))
    in_specs = [tile, tile, tile,
                pl.BlockSpec((1, L, 4 * NH), lambda b, h, s: (b, chunk(s), 0)),
                pl.BlockSpec((1, 4 * NH, L), lambda b, h, s: (b, 0, chunk(s)))]
    args = [q, k, v, gates, gates_t]
    finish = finish_args is not None
    if finish:
        h_fwd, xc, xz, norm_w, skip = finish_args
        vec = pl.BlockSpec((1, DH), lambda b, h, s: (0, h))
        in_specs += [tile, tile,
                     pl.BlockSpec((1, L, DH), lambda b, h, s: (b, chunk(s), NH + h)),
                     vec, vec]
        args += [h_fwd, xc, xz.reshape(BATCH, U, 2 * E), norm_w.reshape(1, E), skip.reshape(1, E)]
    return pl.pallas_call(
        functools.partial(_mlstm_kernel, direction=direction, finish=finish),
        grid=(BATCH, NH, n_chunks),
        in_specs=in_specs,
        out_specs=tile,
        out_shape=jax.ShapeDtypeStruct((BATCH, U, E), bf16),
        scratch_shapes=[pltpu.VMEM((DH, DH), f32), pltpu.VMEM((DH, DH), bf16),
                        pltpu.VMEM((1, DH), f32), pltpu.VMEM((1, 128), f32)],
        compiler_params=_cparams(3),
        name="mlstm_bwd_finish" if finish else "mlstm_fwd",
    )(*args)


def _ln_rows(r, g, b):
    mu = jnp.mean(r, axis=1, keepdims=True)
    rc = r - mu
    var = jnp.mean(rc * rc, axis=1, keepdims=True)
    return rc * lax.rsqrt(var + LN_EPS) * g + b


def _ln_router_kernel(*refs, joint):
    hs_refs, refs = refs[:1 + joint], refs[1 + joint:]
    (o_ref, m_ref, g_ref, b_ref, rw_ref, rb_ref,
     hs1_ref, tok_ref, gate_ref, eid_ref, rank_ref, cnt_ref, carry_ref) = refs

    @pl.when(pl.program_id(0) == 0)
    def _():
        carry_ref[...] = jnp.zeros_like(carry_ref)

    m = m_ref[0]
    hs = _joint_rows(*hs_refs) if joint else hs_refs[0][...]
    r = ALPHA * hs + m[2:3] * o_ref[...].astype(f32)
    hs1 = _ln_rows(r, g_ref[...], b_ref[...])
    hs1_ref[...] = hs1
    tok = hs1 * (1.0 + m[4:5]) + m[3:4]
    t_hi = tok.astype(bf16)
    _store_tile_rows(tok_ref, _pack_halves(tok))
    t_lo = (tok - t_hi.astype(f32)).astype(bf16)
    w = rw_ref[...]
    w_hi = w.astype(bf16)
    w_lo = (w - w_hi.astype(f32)).astype(bf16)
    logits = _dot(t_hi, w_hi) + _dot(t_hi, w_lo) + _dot(t_lo, w_hi) + rb_ref[...]

    lane = lax.broadcasted_iota(jnp.int32, logits.shape, 1).astype(f32)
    left = logits
    vals, ids = [], []
    for _ in range(TOP_K):
        v = jnp.max(left, axis=1, keepdims=True)
        e = jnp.min(jnp.where(left == v, lane, float(NE)), axis=1, keepdims=True)
        vals.append(v)
        ids.append(e)
        left = jnp.where(lane == e, -jnp.inf, left)
    exps = [jnp.exp(v - vals[0]) for v in vals]
    den = exps[0]
    for x in exps[1:]:
        den = den + x
    gate_ref[...] = jnp.concatenate([x / den for x in exps], axis=1)
    eid_ref[...] = jnp.concatenate(ids, axis=1).astype(jnp.int32)

    member = (lane == ids[0]).astype(f32)
    for e in ids[1:]:
        member = member + (lane == e).astype(f32)
    rows = logits.shape[0]
    earlier = (lax.broadcasted_iota(jnp.int32, (rows, rows), 1)
               < lax.broadcasted_iota(jnp.int32, (rows, rows), 0)).astype(bf16)
    before = _dot(earlier, member.astype(bf16)) + carry_ref[...]
    rank_ref[...] = jnp.concatenate(
        [jnp.sum(jnp.where(lane == e, before, 0.0), axis=1, keepdims=True) for e in ids],
        axis=1).astype(jnp.int32)
    total = carry_ref[...] + jnp.sum(member, axis=0, keepdims=True)
    carry_ref[...] = total
    cnt_ref[...] = total


def _ln_router(hs_parts, o, mt, group, ln_g, ln_b, r_w, r_b):
    m = o.shape[0]
    joint = len(hs_parts) == 2
    row = pl.BlockSpec((ROW_TILE, D), lambda i: (i, 0))
    vec = pl.BlockSpec((1, D), lambda i: (0, 0))
    topk = pl.BlockSpec((ROW_TILE, TOP_K), lambda i: (i, 0))
    return pl.pallas_call(
        functools.partial(_ln_router_kernel, joint=joint),
        grid=(m // ROW_TILE,),
        in_specs=([_CTX_ROWS, _LATENT_ROWS] if joint else [row]) + [
                  row,
                  pl.BlockSpec((1, 6, D), lambda i: (group(i), 0, 0)),
                  vec, vec,
                  pl.BlockSpec((D, NE), lambda i: (0, 0)),
                  pl.BlockSpec((1, NE), lambda i: (0, 0))],
        out_specs=[row, pl.BlockSpec((ROW_TILE * TILE_ROWS, LANES), lambda i: (i, 0)),
                   topk, topk, topk, pl.BlockSpec((1, NE), lambda i: (0, 0))],
        out_shape=[jax.ShapeDtypeStruct((m, D), f32),
                   jax.ShapeDtypeStruct((m * TILE_ROWS, LANES), jnp.uint32),
                   jax.ShapeDtypeStruct((m, TOP_K), f32),
                   jax.ShapeDtypeStruct((m, TOP_K), jnp.int32),
                   jax.ShapeDtypeStruct((m, TOP_K), jnp.int32),
                   jax.ShapeDtypeStruct((1, NE), f32)],
        scratch_shapes=[pltpu.VMEM((1, NE), f32)],
        compiler_params=_cparams(1),
        name="ln_router",
    )(*hs_parts, o, mt, ln_g.reshape(1, D), ln_b.reshape(1, D), r_w, r_b.reshape(1, NE))


def _first_of_expert(be_ref, i):
    return jnp.logical_or(i == 0, be_ref[i] != be_ref[jnp.maximum(i - 1, 0)])


ROW_STEPS = 8


def _for_real_rows(valid, rows, compute):
    q = MOE_TM // ROW_STEPS
    for step in range(1, ROW_STEPS + 1):
        covers = rows <= step * q
        if step > 1:
            covers = jnp.logical_and(covers, rows > (step - 1) * q)
        pl.when(jnp.logical_and(valid, covers))(functools.partial(compute, step * q))


def _moe_up_kernel(be_ref, nu_ref, nxt_ref, rows_ref, x_ref, w_hbm, bg_ref, bl_ref, o_ref, stage_ref,
                   cache_ref, sem, *, layer, tf):
    j = pl.program_id(0)
    i = pl.program_id(1)
    valid = i < nu_ref[0]

    def tile_copy(e, jj, half):
        col = pl.multiple_of(half * F + jj * tf, tf)
        return pltpu.make_async_copy(w_hbm.at[layer, e, :, pl.ds(col, tf)], stage_ref.at[half], sem.at[half])

    def start(e, jj):
        tile_copy(e, jj, 0).start()
        tile_copy(e, jj, 1).start()

    @pl.when(jnp.logical_and(j == 0, i == 0))
    def _():
        start(be_ref[0], 0)

    @pl.when(jnp.logical_and(valid, _first_of_expert(be_ref, i)))
    def _():
        tile_copy(0, 0, 0).wait()
        tile_copy(0, 0, 1).wait()
        cache_ref[...] = stage_ref[...].astype(bf16)
        nxt = nxt_ref[i]

        @pl.when(nxt >= 0)
        def _():
            start(nxt, j)

        @pl.when(jnp.logical_and(nxt < 0, j + 1 < pl.num_programs(0)))
        def _():
            start(be_ref[0], j + 1)

    def compute(rows):
        x = _unpack_halves(_load_tile_rows(x_ref, rows)).astype(bf16)
        glu = jnp.minimum(_dot(x, cache_ref[0]) + bg_ref[0, 0], SWIGLU_LIMIT)
        lin = jnp.clip(_dot(x, cache_ref[1]) + bl_ref[0, 0], -SWIGLU_LIMIT, SWIGLU_LIMIT)
        o_ref[pl.ds(0, rows), :] = (glu * _sigmoid(SWIGLU_ALPHA * glu) * (lin + 1.0)).astype(o_ref.dtype)
        if rows < MOE_TM:
            o_ref[pl.ds(rows, MOE_TM - rows), :] = jnp.zeros((MOE_TM - rows, tf), o_ref.dtype)

    _for_real_rows(valid, rows_ref[i], compute)

    @pl.when(jnp.logical_not(valid))
    def _():
        o_ref[...] = jnp.zeros_like(o_ref)


def _moe_down_kernel(be_ref, nu_ref, nxt_ref, rows_ref, a_ref, w_hbm, b_ref, o_ref, stage_ref, cache_ref, sem,
                     *, layer):
    i = pl.program_id(0)
    valid = i < nu_ref[0]

    def expert_copy(e):
        return pltpu.make_async_copy(w_hbm.at[layer, e], stage_ref, sem)

    @pl.when(i == 0)
    def _():
        expert_copy(be_ref[0]).start()

    @pl.when(jnp.logical_and(valid, _first_of_expert(be_ref, i)))
    def _():
        expert_copy(0).wait()
        cache_ref[...] = stage_ref[...].astype(bf16)
        nxt = nxt_ref[i]

        @pl.when(nxt >= 0)
        def _():
            expert_copy(nxt).start()

    def compute(rows):
        y = _dot(a_ref[pl.ds(0, rows), :], cache_ref[...]) + b_ref[0, 0]
        _store_tile_rows(o_ref, _pack_halves(y))
        if rows < MOE_TM:
            o_ref[pl.ds(rows * TILE_ROWS, (MOE_TM - rows) * TILE_ROWS), :] = jnp.zeros(
                ((MOE_TM - rows) * TILE_ROWS, LANES), o_ref.dtype)

    _for_real_rows(valid, rows_ref[i], compute)

    @pl.when(jnp.logical_not(valid))
    def _():
        o_ref[...] = jnp.zeros_like(o_ref)


def _moe_experts(layer, xs, block_e, n_used, next_e, block_rows, w1, b1, w2, b2):
    ns = xs.shape[0] // TILE_ROWS
    nb = ns // MOE_TM
    tf = 1024
    lin0 = F // tf
    b1 = b1.reshape(DEPTH, NE, 1, 2 * F)
    b2 = b2.reshape(DEPTH, NE, 1, D)

    def bspec(col0):
        return pl.BlockSpec((1, 1, 1, tf), lambda j, i, be, nu, nx, br: (layer, be[i], 0, col0 + j))

    act = pl.pallas_call(
        functools.partial(_moe_up_kernel, layer=layer, tf=tf),
        grid_spec=pltpu.PrefetchScalarGridSpec(
            num_scalar_prefetch=4,
            grid=(F // tf, nb),
            in_specs=[pl.BlockSpec((MOE_TM * TILE_ROWS, LANES),
                                   lambda j, i, be, nu, nx, br: (jnp.minimum(i, nu[0] - 1), 0)),
                      pl.BlockSpec(memory_space=pl.ANY), bspec(0), bspec(lin0)],
            out_specs=pl.BlockSpec((MOE_TM, tf), lambda j, i, be, nu, nx, br: (i, j)),
            scratch_shapes=[pltpu.VMEM((2, D, tf), f32), pltpu.VMEM((2, D, tf), bf16),
                            pltpu.SemaphoreType.DMA((2,))]),
        out_shape=jax.ShapeDtypeStruct((ns, F), bf16),
        compiler_params=_cparams(2),
        name="moe_up",
    )(block_e, n_used, next_e, block_rows, xs, w1, b1, b1)

    return pl.pallas_call(
        functools.partial(_moe_down_kernel, layer=layer),
        grid_spec=pltpu.PrefetchScalarGridSpec(
            num_scalar_prefetch=4,
            grid=(nb,),
            in_specs=[pl.BlockSpec((MOE_TM, F), lambda i, be, nu, nx, br: (jnp.minimum(i, nu[0] - 1), 0)),
                      pl.BlockSpec(memory_space=pl.ANY),
                      pl.BlockSpec((1, 1, 1, D), lambda i, be, nu, nx, br: (layer, be[i], 0, 0))],
            out_specs=pl.BlockSpec((MOE_TM * TILE_ROWS, LANES), lambda i, be, nu, nx, br: (i, 0)),
            scratch_shapes=[pltpu.VMEM((F, D), f32), pltpu.VMEM((F, D), bf16), pltpu.SemaphoreType.DMA]),
        out_shape=jax.ShapeDtypeStruct((ns * TILE_ROWS, LANES), jnp.uint32),
        compiler_params=_cparams(1),
        name="moe_down",
    )(block_e, n_used, next_e, block_rows, act, w2, b2)


GATHER_ROWS = 4096


def _gather_kernel(idx_ref, src_ref, out_ref, sem):
    n = GATHER_ROWS
    t = TILE_ROWS

    def issue(pair, carry):
        for priority in range(2):
            r = 2 * pair + priority
            src_row = pl.multiple_of(idx_ref[0, 0, r] * t, t)
            dst_row = pl.multiple_of(r * t, t)
            pltpu.make_async_copy(src_ref.at[pl.ds(src_row, t)], out_ref.at[pl.ds(dst_row, t)],
                                  sem).start(priority=priority)
        return carry

    lax.fori_loop(0, n // 2, issue, 0, unroll=4)
    pltpu.make_async_copy(src_ref.at[pl.ds(0, n * t)], out_ref, sem).wait()


def _gather_rows(src, idx):
    n = idx.shape[0]
    nblk = n // GATHER_ROWS
    return pl.pallas_call(
        _gather_kernel,
        grid=(nblk,),
        in_specs=[pl.BlockSpec((1, 1, GATHER_ROWS), lambda i: (i, 0, 0), memory_space=pltpu.SMEM),
                  pl.BlockSpec(memory_space=pl.ANY)],
        out_specs=pl.BlockSpec((GATHER_ROWS * TILE_ROWS, LANES), lambda i: (i, 0)),
        out_shape=jax.ShapeDtypeStruct((n * TILE_ROWS, LANES), src.dtype),
        scratch_shapes=[pltpu.SemaphoreType.DMA],
        compiler_params=pltpu.CompilerParams(dimension_semantics=("arbitrary",),
                                             vmem_limit_bytes=VMEM_LIMIT,
                                             disable_bounds_checks=True),
        name="gather_rows",
    )(idx.reshape(nblk, 1, GATHER_ROWS), src)


SCATTER_TOKENS = 1024


def _scatter_kernel(dest_ref, src_ref, init_ref, out_ref, sem):
    del init_ref
    n = SCATTER_TOKENS
    t = TILE_ROWS

    def issue(pair, carry):
        for priority in range(2):
            a = 2 * pair + priority
            src_row = pl.multiple_of(lax.shift_right_logical(a, TOP_K.bit_length() - 1) * t, t)
            dst_row = pl.multiple_of(dest_ref[0, 0, a] * t, t)
            pltpu.make_async_copy(src_ref.at[pl.ds(src_row, t)], out_ref.at[pl.ds(dst_row, t)],
                                  sem).start(priority=priority)
        return carry

    lax.fori_loop(0, n * TOP_K // 2, issue, 0, unroll=4)
    for _ in range(TOP_K):
        pltpu.make_async_copy(src_ref, out_ref.at[pl.ds(0, n * t)], sem).wait()


def _scatter_rows(src, dest, init):
    n = src.shape[0] // TILE_ROWS
    nblk = n // SCATTER_TOKENS
    return pl.pallas_call(
        _scatter_kernel,
        grid=(nblk,),
        in_specs=[pl.BlockSpec((1, 1, SCATTER_TOKENS * TOP_K), lambda i: (i, 0, 0), memory_space=pltpu.SMEM),
                  pl.BlockSpec((SCATTER_TOKENS * TILE_ROWS, LANES), lambda i: (i, 0)),
                  pl.BlockSpec(memory_space=pl.ANY)],
        out_specs=pl.BlockSpec(memory_space=pl.ANY),
        out_shape=jax.ShapeDtypeStruct(init.shape, src.dtype),
        input_output_aliases={2: 0},
        scratch_shapes=[pltpu.SemaphoreType.DMA],
        compiler_params=pltpu.CompilerParams(dimension_semantics=("arbitrary",),
                                             vmem_limit_bytes=VMEM_LIMIT,
                                             disable_bounds_checks=True),
        name="scatter_rows",
    )(dest.reshape(nblk, 1, SCATTER_TOKENS * TOP_K), src, init)


MOE_BLOCKS = -(-(BATCH * U * TOP_K) // MOE_TM) + NE


def _slot_layout(eid, rank, counts):
    nb = MOE_BLOCKS
    counts = counts.reshape(NE).astype(jnp.int32)
    experts = jnp.arange(NE, dtype=jnp.int32)
    pcounts = (counts + MOE_TM - 1) // MOE_TM * MOE_TM
    pends = jnp.cumsum(pcounts)
    pstarts = pends - pcounts
    dest = rank + jnp.sum(jnp.where(eid[:, :, None] == experts, pstarts, 0), axis=-1)
    n_used = pends[-1] // MOE_TM
    blk = jnp.arange(nb, dtype=jnp.int32)
    block_e = jnp.minimum(jnp.sum(pends[None, :] <= (blk * MOE_TM)[:, None], axis=1), NE - 1)
    of_block = block_e[:, None] == experts

    def per_block(table):
        return jnp.sum(jnp.where(of_block, table, 0), axis=1)

    block_rows = jnp.clip(per_block(counts) - (blk * MOE_TM - per_block(pstarts)), 0, MOE_TM)
    block_rows = jnp.where(blk < n_used, block_rows, 0)
    group_end = per_block(pends) // MOE_TM
    next_e = jnp.sum(jnp.where(group_end[:, None] == blk, block_e, 0), axis=1)
    next_e = jnp.where(group_end < n_used, next_e, -1)
    last_e = jnp.sum(jnp.where(blk == n_used - 1, block_e, 0))
    block_e = jnp.where(blk < n_used, block_e, last_e)
    i32 = jnp.int32
    return (dest.astype(i32), block_e.astype(i32), n_used.astype(i32).reshape(1), next_e.astype(i32),
            block_rows.astype(i32))


def _combine_kernel(hs_ref, y_ref, gate_ref, m_ref, g_ref, b_ref, *rest, joint):
    if joint:
        mn_ref, hs2_ref, nxt_ref, ctx_ref = rest
    else:
        (hs2_ref,) = rest
    m = m_ref[0]
    gate = gate_ref[...]
    y = gate[:, 0:1] * _unpack_halves(_load_tile_rows(y_ref.at[0], ROW_TILE))
    for k in range(1, TOP_K):
        y = y + gate[:, k:k + 1] * _unpack_halves(_load_tile_rows(y_ref.at[k], ROW_TILE))
    hs2 = _ln_rows(ALPHA * hs_ref[...] + m[5:6] * y, g_ref[...], b_ref[...])
    if not joint:
        hs2_ref[...] = hs2
        return
    is_ctx = _is_ctx_tile(pl.program_id(0))

    @pl.when(is_ctx)
    def _():
        ctx_ref[...] = hs2

    @pl.when(jnp.logical_not(is_ctx))
    def _():
        mn = mn_ref[0]
        hs2_ref[...] = hs2
        nxt_ref[...] = (hs2 * (1.0 + mn[1:2]) + mn[0:1]).astype(nxt_ref.dtype)


def _combine(hs1, yg, gate, mt, group, ln_g, ln_b, mt_next=None):
    m = hs1.shape[0]
    row = pl.BlockSpec((ROW_TILE, D), lambda i: (i, 0))
    vec = pl.BlockSpec((1, D), lambda i: (0, 0))
    mod = pl.BlockSpec((1, 6, D), lambda i: (group(i), 0, 0))
    in_specs = [row, pl.BlockSpec((TOP_K, ROW_TILE * TILE_ROWS, LANES), lambda i: (0, i, 0)),
                pl.BlockSpec((ROW_TILE, TOP_K), lambda i: (i, 0)), mod, vec, vec]
    args = [hs1, yg, gate, mt, ln_g.reshape(1, D), ln_b.reshape(1, D)]
    joint = mt_next is not None
    if joint:
        in_specs.append(mod)
        args.append(mt_next)
        out_specs = [_LATENT_ROWS, _LATENT_ROWS, _CTX_ROWS]
        out_shape = [jax.ShapeDtypeStruct((BATCH * SEQ, D), f32), jax.ShapeDtypeStruct((BATCH * SEQ, D), bf16),
                     jax.ShapeDtypeStruct((BATCH * LC, D), f32)]
    else:
        out_specs = [row]
        out_shape = [jax.ShapeDtypeStruct((m, D), f32)]
    return pl.pallas_call(
        functools.partial(_combine_kernel, joint=joint),
        grid=(m // ROW_TILE,),
        in_specs=in_specs, out_specs=out_specs, out_shape=out_shape,
        compiler_params=_cparams(1),
        name="moe_combine_ln",
    )(*args)


def _moe_layer(layer, hs_parts, o, mt, group, ln1_g, ln1_b, ln2_g, ln2_b, r_w, r_b, w1, b1, w2, b2,
               slots, mt_next=None):
    t = o.shape[0]
    hs1, tok, gate, eid, rank, counts = _ln_router(hs_parts, o, mt, group, ln1_g[layer], ln1_b[layer],
                                                   r_w[layer], r_b[layer])
    dest, block_e, n_used, next_e, block_rows = _slot_layout(eid, rank, counts)
    xs = _scatter_rows(tok, dest.reshape(-1), slots)
    y = _moe_experts(layer, xs, block_e, n_used, next_e, block_rows, w1, b1, w2, b2)
    yg = _gather_rows(y, dest.T.reshape(-1)).reshape(TOP_K, t * TILE_ROWS, LANES)
    return _combine(hs1, yg, gate, mt, group, ln2_g[layer], ln2_b[layer], mt_next), xs


def _sgu_kernel(g_ref, v_ref, lg_ref, lb_ref, ws_ref, bs_ref, o_ref):
    v = v_ref[...].astype(f32)
    vn = _ln_rows(v, lg_ref[...], lb_ref[...]).astype(bf16)
    gw = E // SGU_GROUPS
    for g in range(SGU_GROUPS):
        mixed = _dot(ws_ref[g].astype(bf16), vn[:, g * gw:(g + 1) * gw]) + bs_ref[:, g:g + 1]
        o_ref[:, g * gw:(g + 1) * gw] = (g_ref[:, g * gw:(g + 1) * gw].astype(f32) * mixed).astype(o_ref.dtype)


def _sgu(uv, ln_g, ln_b, w_s, b_s):
    m = uv.shape[0]
    c = SGU_CHUNK
    vec = pl.BlockSpec((1, E), lambda i: (0, 0))
    return pl.pallas_call(
        _sgu_kernel,
        grid=(m // c,),
        in_specs=[pl.BlockSpec((c, E), lambda i: (i, 0)),
                  pl.BlockSpec((c, E), lambda i: (i, 1)),
                  vec, vec,
                  pl.BlockSpec((SGU_GROUPS, c, c), lambda i: (0, 0, 0)),
                  pl.BlockSpec((c, SGU_GROUPS), lambda i: (0, 0))],
        out_specs=pl.BlockSpec((c, E), lambda i: (i, 0)),
        out_shape=jax.ShapeDtypeStruct((m, E), bf16),
        compiler_params=_cparams(1),
        name="sgu",
    )(uv, uv, ln_g.reshape(1, E), ln_b.reshape(1, E), w_s, b_s.T)


def kernel(x, c, ctx, c_ctx, mod_w, mod_b, ln1_g, ln1_b, ln2_g, ln2_b, a_w_in, a_conv_w, a_conv_b, a_w_q, a_w_k, a_w_v, a_w_gate, a_b_gate, a_norm_w, a_skip, a_w_out, b_w_in, b_ln_g, b_ln_b, b_w_s, b_b_s, b_w_out, r_w, r_b, e_w1, e_b1, e_w2, e_b2):
    cvec = jnp.concatenate([c, c_ctx[None, :], jnp.zeros((8 - BATCH - 1, D), f32)], axis=0)
    mods = _mod_rows(cvec, mod_w, mod_b).reshape(DEPTH, 8, 6, D)
    rows = [r for b in range(BATCH) for r in (BATCH, b)]
    mt = [jnp.stack([mods[l, r] for r in rows], axis=0) for l in range(DEPTH)]

    hs = (ctx.reshape(BATCH * LC, D), x.reshape(BATCH * SEQ, D))
    hx = _modulate(*hs, mt[0])
    xz = _matmul(hx, a_w_in[0], tm=1024, tn=1024, name="mlstm_in_proj")
    xc, q, k, v, gates = _conv_qkv(xz, a_conv_w[0], a_conv_b[0], a_w_q[0], a_w_k[0], a_w_v[0],
                                   a_w_gate[0], a_b_gate[0])
    gates_t = gates.transpose(0, 2, 1)
    h_fwd = _mlstm(q, k, v, gates, gates_t, 0)
    pre = _mlstm(q, k, v, gates, gates_t, 1, (h_fwd, xc, xz, a_norm_w[0], a_skip[0]))
    o = _matmul(pre.reshape(BATCH * U, E), a_w_out[0], tm=1024, tn=512, name="mlstm_out_proj")
    slots = jnp.zeros((MOE_BLOCKS * MOE_TM * TILE_ROWS, LANES), jnp.uint32)
    (hs, hx, _), slots = _moe_layer(0, hs, o, mt[0], _group_joint, ln1_g, ln1_b, ln2_g, ln2_b,
                                    r_w, r_b, e_w1, e_b1, e_w2, e_b2, slots, mt_next=mt[1])

    uv = _matmul(hx, b_w_in[0], tm=1024, tn=1024, act="gelu", name="sgu_in_proj")
    gated = _sgu(uv, b_ln_g[0], b_ln_b[0], b_w_s[0], b_b_s[0])
    o = _matmul(gated, b_w_out[0], tm=1024, tn=512, name="sgu_out_proj")
    (out,), _ = _moe_layer(1, (hs,), o, mt[1], _group_latent, ln1_g, ln1_b, ln2_g, ln2_b,
                           r_w, r_b, e_w1, e_b1, e_w2, e_b2, slots)
    return out.reshape(BATCH, SEQ, D)
```
